```python
import math
import jax, jax.numpy as jnp
from jax import lax
import numpy as np

D_MODEL = 4096
BATCH = 4
SEQ = 2048
DEPTH = 1
DEC_BATCH = 32
DEC_SEQ = 1
PAST_LEN = 8192
PAGE_SIZE = 128

N_HEADS = 16
HEAD_DIM = 128
N_KV_HEADS = 4
ATTN_WIDTH = N_HEADS * HEAD_DIM
KV_WIDTH = N_KV_HEADS * HEAD_DIM
IDX_HEADS = 16
IDX_DIM = 64
TOPK_MAX = 256
Q_BLOCK = 64
NUM_BUCKETS = 32
MAX_DISTANCE = 128
CHUNK = 128
GMLP_GROUPS = 16
GMLP_CH = 128
GMLP_WIDTH = GMLP_GROUPS * GMLP_CH
MEM_LEN = 256
X_HEADS = 4
X_HEAD_DIM = 128
X_WIDTH = X_HEADS * X_HEAD_DIM
N_KEYS = 128
N_EXPERTS = N_KEYS * N_KEYS
PEER_HEADS = 8
PEER_QDIM = 256
PEER_HALF = PEER_QDIM // 2
PEER_TOPK = 16
PEER_BLOCK = 64
EPS = 1e-6

IN_SPLITS = (ATTN_WIDTH, KV_WIDTH, KV_WIDTH, IDX_HEADS * IDX_DIM, IDX_DIM, IDX_HEADS,
             GMLP_WIDTH, GMLP_WIDTH, ATTN_WIDTH, GMLP_WIDTH)
IN_WIDTH = sum(IN_SPLITS)
MIX_WIDTH = ATTN_WIDTH + GMLP_WIDTH

kernel_name = "dsa_gmlp_peer_gated_hybrid_step"


def rmsnorm(x, g):
    xf = x.astype(jnp.float32)
    y = xf * lax.rsqrt(jnp.mean(xf * xf, axis=-1, keepdims=True) + EPS)
    return (y * g.astype(jnp.float32)).astype(x.dtype)


def t5_bucket(dist):
    n = jnp.maximum(dist, 0)
    max_exact = NUM_BUCKETS // 2
    nf = jnp.maximum(n, 1).astype(jnp.float32)
    large = max_exact + (jnp.log(nf / max_exact) / math.log(MAX_DISTANCE / max_exact)
                         * (NUM_BUCKETS - max_exact)).astype(jnp.int32)
    large = jnp.minimum(large, NUM_BUCKETS - 1)
    return jnp.where(n < max_exact, n, large)


def indexer_scores(qi, wi, ki):
    r = jax.nn.relu(jnp.einsum('bthd,bsd->bths', qi, ki).astype(jnp.float32) * IDX_DIM ** -0.5)
    return jnp.einsum('bth,bths->bts', wi.astype(jnp.float32) * IDX_HEADS ** -0.5, r)


def sparse_attend(q, kg, vg, sel, tpos, rel_bias):
    b, t = q.shape[:2]
    n_sel = sel.shape[-1]
    grp = N_HEADS // N_KV_HEADS
    qg = q.reshape(b, t, N_KV_HEADS, grp, HEAD_DIM)
    logits = jnp.einsum('btkgd,btnkd->btkgn', qg, kg).astype(jnp.float32) * HEAD_DIM ** -0.5
    dist = tpos[None, :, None] - sel
    bias = rel_bias[t5_bucket(dist)].astype(jnp.float32)
    bias = bias.reshape(b, t, n_sel, N_KV_HEADS, grp).transpose(0, 1, 3, 4, 2)
    valid = (dist >= 0)[:, :, None, None, :]
    logits = jnp.where(valid, logits + bias, -jnp.inf)
    p = jax.nn.softmax(logits, axis=-1).astype(vg.dtype)
    o = jnp.einsum('btkgn,btnkd->btkgd', p, vg)
    return o.reshape(b, t, ATTN_WIDTH)


_gather_rows = jax.vmap(lambda rows, idx: rows[idx])


def dsa_prompt(q, k, v, qi, ki, wi, rel_bias):
    b, s = q.shape[:2]
    k_top = min(TOPK_MAX, s // 4)
    spos = jnp.arange(s)

    def block(i):
        t0 = i * Q_BLOCK
        sl = lambda a: lax.dynamic_slice_in_dim(a, t0, Q_BLOCK, axis=1)
        qb, qib, wib = sl(q), sl(qi), sl(wi)
        tpos = t0 + jnp.arange(Q_BLOCK)
        score = indexer_scores(qib, wib, ki)
        score = jnp.where(spos[None, None, :] <= tpos[None, :, None], score, -jnp.inf)
        _, sel = lax.top_k(score, k_top)
        return sparse_attend(qb, _gather_rows(k, sel), _gather_rows(v, sel), sel, tpos, rel_bias)

    out = lax.map(block, jnp.arange(s // Q_BLOCK))
    return out.transpose(1, 0, 2, 3).reshape(b, s, ATTN_WIDTH)


def dsa_sample(q, k_new, v_new, qi, ki_new, wi, cache_k, cache_v, cache_kidx, page_table, rel_bias):
    db, t1 = q.shape[:2]
    past = page_table.shape[1] * PAGE_SIZE
    total = past + t1
    k_top = min(TOPK_MAX, total // 4)
    ki_past = cache_kidx[page_table].reshape(db, past, IDX_DIM)
    ki_all = jnp.concatenate([ki_past, ki_new.astype(ki_past.dtype)], axis=1)
    tpos = past + jnp.arange(t1)
    score = indexer_scores(qi, wi, ki_all)
    score = jnp.where(jnp.arange(total)[None, None, :] <= tpos[None, :, None], score, -jnp.inf)
    _, sel = lax.top_k(score, k_top)
    is_new = (sel >= past)[..., None, None]
    sel_p = jnp.minimum(sel, past - 1)
    phys = jax.vmap(lambda pt, si: pt[si // PAGE_SIZE])(page_table, sel_p)
    off = sel_p % PAGE_SIZE
    sel_n = jnp.clip(sel - past, 0, t1 - 1)
    kg = jnp.where(is_new, _gather_rows(k_new, sel_n).astype(cache_k.dtype), cache_k[phys, off])
    vg = jnp.where(is_new, _gather_rows(v_new, sel_n).astype(cache_v.dtype), cache_v[phys, off])
    return sparse_attend(q, kg, vg, sel, tpos, rel_bias)


def gmlp_mix(u, vn, w_s, b_s):
    b, length = u.shape[:2]
    lp = -(-length // CHUNK) * CHUNK
    vp = jnp.pad(vn, ((0, 0), (0, lp - length), (0, 0))).reshape(b, lp // CHUNK, CHUNK, GMLP_GROUPS, GMLP_CH)
    w = jnp.tril(w_s)
    mixed = jnp.einsum('gts,bnsgc->bntgc', w, vp) + b_s.T[:, :, None]
    mixed = mixed.reshape(b, lp, GMLP_WIDTH)[:, :length]
    return u * mixed


def cross_attend(q, mk, mv):
    b, t = q.shape[:2]
    logits = jnp.einsum('bthd,bmhd->bhtm', q, mk).astype(jnp.float32) * X_HEAD_DIM ** -0.5
    p = jax.nn.softmax(logits, axis=-1).astype(mv.dtype)
    return jnp.einsum('bhtm,bmhd->bthd', p, mv).reshape(b, t, X_WIDTH)


def peer(x, w_pq, sub_keys, u_tab, v_tab):
    xf = x.reshape(-1, D_MODEL)
    n_tok = xf.shape[0]
    tp = -(-n_tok // PEER_BLOCK) * PEER_BLOCK
    xb = jnp.pad(xf, ((0, tp - n_tok), (0, 0))).reshape(tp // PEER_BLOCK, PEER_BLOCK, D_MODEL)

    def block(xt):
        q = (xt @ w_pq).reshape(PEER_BLOCK, PEER_HEADS, 2, PEER_HALF)
        s = jnp.einsum('thpd,pnd->thpn', q, sub_keys).astype(jnp.float32)
        s1, i1 = lax.top_k(s[:, :, 0], PEER_TOPK)
        s2, i2 = lax.top_k(s[:, :, 1], PEER_TOPK)
        cand = (s1[..., :, None] + s2[..., None, :]).reshape(PEER_BLOCK, PEER_HEADS, PEER_TOPK * PEER_TOPK)
        sc, ci = lax.top_k(cand, PEER_TOPK)
        e = (jnp.take_along_axis(i1, ci // PEER_TOPK, axis=-1) * N_KEYS
             + jnp.take_along_axis(i2, ci % PEER_TOPK, axis=-1))
        g = jax.nn.softmax(sc, axis=-1)
        h = jax.nn.gelu(jnp.einsum('thkd,td->thk', u_tab[e], xt).astype(jnp.float32))
        return jnp.einsum('thk,thkd->td', (g * h).astype(xt.dtype), v_tab[e])

    out = lax.map(block, xb).reshape(tp, D_MODEL)[:n_tok]
    return out.reshape(x.shape)


def token_mixer_inputs(h, g_in, w_in, g_sgu):
    lead = h.shape[:-1]
    z = rmsnorm(h, g_in) @ w_in
    offsets = np.cumsum(IN_SPLITS)[:-1].tolist()
    q, k, v, qi, ki, wi, zu, zv, ga, gb = jnp.split(z, offsets, axis=-1)
    q = q.reshape(*lead, N_HEADS, HEAD_DIM)
    k = k.reshape(*lead, N_KV_HEADS, HEAD_DIM)
    v = v.reshape(*lead, N_KV_HEADS, HEAD_DIM)
    qi = qi.reshape(*lead, IDX_HEADS, IDX_DIM)
    u = jax.nn.gelu(zu)
    vn = rmsnorm(jax.nn.gelu(zv), g_sgu)
    return q, k, v, qi, ki, wi, u, vn, ga, gb


def merge_branches(h, att, sgu, ga, gb, w_out):
    m = jnp.concatenate([jax.nn.sigmoid(ga) * att, jax.nn.sigmoid(gb) * sgu], axis=-1)
    return h + m @ w_out


def cross_and_channel(h, mk, mv, g_x, w_xq, w_xo, g_ffn, w_pq, sub_keys, u_tab, v_tab):
    lead = h.shape[:-1]
    q = (rmsnorm(h, g_x) @ w_xq).reshape(*lead, X_HEADS, X_HEAD_DIM)
    h = h + cross_attend(q, mk, mv) @ w_xo
    return h + peer(rmsnorm(h, g_ffn), w_pq, sub_keys, u_tab, v_tab)


def setup_inputs(seed: int = 0) -> dict:
    key = jax.random.key(seed)
    ks = jax.random.split(key, 32)
    n_pages = PAST_LEN // PAGE_SIZE
    n_used = DEC_BATCH * n_pages
    n_pool = n_used + n_used // 4
    nrm = lambda k, shape, scale=1.0: jax.random.normal(k, shape, jnp.float32) * scale
    gain = lambda k, shape: 1.0 + 0.05 * jax.random.normal(k, shape, jnp.float32)
    page_table = jax.random.permutation(ks[0], n_pool)[:n_used].reshape(DEC_BATCH, n_pages).astype(jnp.int32)
    return {
        "x_prompt": nrm(ks[1], (BATCH, SEQ, D_MODEL)),
        "x_sample": nrm(ks[2], (DEC_BATCH, DEC_SEQ, D_MODEL)),
        "cache_k": nrm(ks[3], (DEPTH, n_pool, PAGE_SIZE, N_KV_HEADS, HEAD_DIM)),
        "cache_v": nrm(ks[4], (DEPTH, n_pool, PAGE_SIZE, N_KV_HEADS, HEAD_DIM)),
        "cache_kidx": nrm(ks[5], (DEPTH, n_pool, PAGE_SIZE, IDX_DIM)),
        "cache_mem_k": nrm(ks[6], (DEPTH, DEC_BATCH, MEM_LEN, X_HEADS, X_HEAD_DIM)),
        "cache_mem_v": nrm(ks[7], (DEPTH, DEC_BATCH, MEM_LEN, X_HEADS, X_HEAD_DIM)),
        "page_table": page_table,
        "mem_prompt": nrm(ks[8], (BATCH, MEM_LEN, D_MODEL)),
        "rel_bias": nrm(ks[9], (NUM_BUCKETS, N_HEADS), 0.5),
        "g_in": gain(ks[10], (DEPTH, D_MODEL)),
        "w_in": nrm(ks[11], (DEPTH, D_MODEL, IN_WIDTH), D_MODEL ** -0.5),
        "g_sgu": gain(ks[12], (DEPTH, GMLP_WIDTH)),
        "w_s": nrm(ks[13], (DEPTH, GMLP_GROUPS, CHUNK, CHUNK), CHUNK ** -0.5),
        "b_s": 1.0 + 0.1 * jax.random.normal(ks[14], (DEPTH, GMLP_GROUPS, CHUNK), jnp.float32),
        "w_out": nrm(ks[15], (DEPTH, MIX_WIDTH, D_MODEL), MIX_WIDTH ** -0.5),
        "g_x": gain(ks[16], (DEPTH, D_MODEL)),
        "w_xq": nrm(ks[17], (DEPTH, D_MODEL, X_WIDTH), D_MODEL ** -0.5),
        "w_xk": nrm(ks[18], (DEPTH, D_MODEL, X_WIDTH), D_MODEL ** -0.5),
        "w_xv": nrm(ks[19], (DEPTH, D_MODEL, X_WIDTH), D_MODEL ** -0.5),
        "w_xo": nrm(ks[20], (DEPTH, X_WIDTH, D_MODEL), X_WIDTH ** -0.5),
        "g_ffn": gain(ks[21], (DEPTH, D_MODEL)),
        "w_pq": nrm(ks[22], (DEPTH, D_MODEL, PEER_HEADS * PEER_QDIM), D_MODEL ** -0.5),
        "sub_keys": nrm(ks[23], (DEPTH, 2, N_KEYS, PEER_HALF), PEER_HALF ** -0.5),
        "peer_u": nrm(ks[24], (DEPTH, N_EXPERTS, D_MODEL), D_MODEL ** -0.5),
        "peer_v": nrm(ks[25], (DEPTH, N_EXPERTS, D_MODEL), PEER_HEADS ** -0.5),
        "g_final": gain(ks[26], (D_MODEL,)),
    }


def reference(x_prompt, x_sample, cache_k, cache_v, cache_kidx, cache_mem_k, cache_mem_v, page_table,
              mem_prompt, rel_bias, g_in, w_in, g_sgu, w_s, b_s, w_out, g_x, w_xq, w_xk, w_xv, w_xo,
              g_ffn, w_pq, sub_keys, peer_u, peer_v, g_final):
    hp, hs = x_prompt, x_sample
    b = x_prompt.shape[0]
    kp_l, vp_l, ip_l, mkp_l, mvp_l = [], [], [], [], []
    ks_l, vs_l, is_l, sv_l = [], [], [], []
    for l in range(DEPTH):
        q, k, v, qi, ki, wi, u, vn, ga, gb = token_mixer_inputs(hp, g_in[l], w_in[l], g_sgu[l])
        att = dsa_prompt(q, k, v, qi, ki, wi, rel_bias)
        sgu = gmlp_mix(u, vn, w_s[l], b_s[l])
        hp = merge_branches(hp, att, sgu, ga, gb, w_out[l])
        mk = (mem_prompt @ w_xk[l]).reshape(b, MEM_LEN, X_HEADS, X_HEAD_DIM)
        mv = (mem_prompt @ w_xv[l]).reshape(b, MEM_LEN, X_HEADS, X_HEAD_DIM)
        hp = cross_and_channel(hp, mk, mv, g_x[l], w_xq[l], w_xo[l], g_ffn[l], w_pq[l], sub_keys[l],
                               peer_u[l], peer_v[l])
        kp_l.append(k); vp_l.append(v); ip_l.append(ki); mkp_l.append(mk); mvp_l.append(mv)
        q, k, v, qi, ki, wi, u, vn, ga, gb = token_mixer_inputs(hs, g_in[l], w_in[l], g_sgu[l])
        att = dsa_sample(q, k, v, qi, ki, wi, cache_k[l], cache_v[l], cache_kidx[l], page_table, rel_bias)
        sgu = gmlp_mix(u, vn, w_s[l], b_s[l])
        hs = merge_branches(hs, att, sgu, ga, gb, w_out[l])
        hs = cross_and_channel(hs, cache_mem_k[l], cache_mem_v[l], g_x[l], w_xq[l], w_xo[l], g_ffn[l],
                               w_pq[l], sub_keys[l], peer_u[l], peer_v[l])
        ks_l.append(k); vs_l.append(v); is_l.append(ki); sv_l.append(vn)
    y_prompt = rmsnorm(hp, g_final)
    y_sample = rmsnorm(hs, g_final)
    return (y_prompt, y_sample, jnp.stack(kp_l), jnp.stack(vp_l), jnp.stack(ip_l), jnp.stack(mkp_l),
            jnp.stack(mvp_l), jnp.stack(ks_l), jnp.stack(vs_l), jnp.stack(is_l), jnp.stack(sv_l))
```

```python
import functools
import math

import jax
import jax.numpy as jnp
from jax import lax
from jax.experimental import pallas as pl
from jax.experimental.pallas import tpu as pltpu

F32 = jnp.float32
BF16 = jnp.bfloat16
I32 = jnp.int32

N_HEADS = 16
HEAD_DIM = 128
N_KV_HEADS = 4
GQA = N_HEADS // N_KV_HEADS
IDX_HEADS = 16
IDX_DIM = 64
TOPK_MAX = 256
NUM_BUCKETS = 32
MAX_DISTANCE = 128
CHUNK = 128
GMLP_GROUPS = 16
X_HEADS = 4
N_KEYS = 128
PEER_HEADS = 8
PEER_TOPK = 16
EPS = 1e-6

LANES = 128
SUBLANES = 8
VMEM_LIMIT_BYTES = 56 * 1024 * 1024

NEG = -1e30
INT_MIN = -(2 ** 31)


def _dot_nt(a, b):
    return lax.dot_general(a, b, (((1,), (1,)), ((), ())), preferred_element_type=F32)


def _dot(a, b):
    return jnp.dot(a, b, preferred_element_type=F32)


def _params(*sem):
    return pltpu.CompilerParams(dimension_semantics=sem, vmem_limit_bytes=VMEM_LIMIT_BYTES)


def _sortable(x):
    bits = pltpu.bitcast(x, I32)
    return bits ^ (jnp.right_shift(bits, 31) & 0x7FFFFFFF)


def _t5_bucket(dist):
    n = jnp.maximum(dist, 0)
    max_exact = NUM_BUCKETS // 2
    nf = jnp.maximum(n, 1).astype(F32)
    large = max_exact + (jnp.log(nf / max_exact) / math.log(MAX_DISTANCE / max_exact)
                         * (NUM_BUCKETS - max_exact)).astype(I32)
    large = jnp.minimum(large, NUM_BUCKETS - 1)
    return jnp.where(n < max_exact, n, large)


def _kth_largest_key(count_ge, shape, k):
    def body(it, t):
        cand = t | jnp.left_shift(jnp.int32(1), 31 - it)
        cnt = count_ge(cand ^ INT_MIN)
        return jnp.where(cnt >= k, cand, t)

    t = lax.fori_loop(0, 32, body, jnp.zeros(shape, I32))
    return t ^ INT_MIN


def _proj_kernel(x_ref, g_ref, w_ref, o_ref, xn_ref, *, norm):
    @pl.when(pl.program_id(1) == 0)
    def _():
        x = x_ref[...]
        if norm:
            x = x * lax.rsqrt(jnp.mean(x * x, axis=-1, keepdims=True) + EPS) * g_ref[...]
        xn_ref[...] = x.astype(BF16)

    o_ref[...] = _dot(xn_ref[...], w_ref[...])


def _proj(x, g, w, *, norm, tm, tn, name):
    m, k = x.shape
    n = w.shape[1]
    tm, tn = min(tm, m), min(tn, n)
    return pl.pallas_call(
        functools.partial(_proj_kernel, norm=norm),
        grid=(m // tm, n // tn),
        in_specs=[pl.BlockSpec((tm, k), lambda i, j: (i, 0)),
                  pl.BlockSpec((1, k), lambda i, j: (0, 0)),
                  pl.BlockSpec((k, tn), lambda i, j: (0, j))],
        out_specs=pl.BlockSpec((tm, tn), lambda i, j: (i, j)),
        out_shape=jax.ShapeDtypeStruct((m, n), F32),
        scratch_shapes=[pltpu.VMEM((tm, k), BF16)],
        compiler_params=_params("arbitrary", "arbitrary"),
        name=name,
    )(x, g.reshape(1, k), w)


def _bias_tiles_kernel(rb_ref, o_ref):
    tq = lax.broadcasted_iota(I32, (LANES, LANES), 0)
    c = lax.broadcasted_iota(I32, (LANES, LANES), 1)
    for part, off in enumerate((2 * LANES, LANES, 0)):
        dist = tq - c + off
        bucket = _t5_bucket(dist)
        for h in range(N_HEADS):
            tile = lax.fori_loop(
                0, NUM_BUCKETS, lambda b, acc: jnp.where(bucket == b, rb_ref[b, h], acc),
                jnp.zeros((LANES, LANES), F32))
            if off == 0:
                tile = jnp.where(dist < 0, NEG, tile)
            o_ref[part, h] = tile


def _bias_tiles(rel_bias):
    return pl.pallas_call(
        _bias_tiles_kernel,
        in_specs=[pl.BlockSpec(memory_space=pltpu.SMEM)],
        out_specs=pl.BlockSpec(memory_space=pltpu.VMEM),
        out_shape=jax.ShapeDtypeStruct((3, N_HEADS, LANES, LANES), F32),
        name="t5_bias_tiles",
    )(rel_bias)


def _dsa_prompt_kernel(q_ref, k_ref, v_ref, qi_ref, zsk_ref, zsq_ref, bt_ref, o_ref,
                       kbf, vbf, kilo, kihi, key3, qs, m_s, l_s, acc_s, *, topk):
    i = pl.program_id(1)
    tq = q_ref.shape[1]
    s = k_ref.shape[1]
    nkb = s // LANES
    cw = 2 * LANES

    @pl.when(i == 0)
    def _():
        kbf[...] = k_ref[0].astype(BF16)
        vbf[...] = v_ref[0].astype(BF16)
        zs = zsk_ref[0]
        lane = lax.broadcasted_iota(I32, zs.shape, 1)
        kilo[...] = jnp.where(lane < IDX_DIM, zs, 0.0).astype(BF16)
        kihi[...] = jnp.where(lane >= IDX_DIM, pltpu.roll(zs, IDX_DIM, axis=1), 0.0).astype(BF16)

    qi = qi_ref[0].astype(BF16)
    coef = zsq_ref[0][:, IDX_DIM:IDX_DIM + IDX_HEADS] * (IDX_HEADS ** -0.5 * IDX_DIM ** -0.5)
    t_pos = i * tq + lax.broadcasted_iota(I32, (tq, cw), 0)
    t_last = i * tq + tq - 1
    for c in range(s // cw):
        @pl.when(c * cw <= t_last)
        def _():
            klo = kilo[c * cw:(c + 1) * cw, :]
            khi = kihi[c * cw:(c + 1) * cw, :]
            acc = jnp.zeros((tq, cw), F32)
            for p in range(IDX_HEADS // 2):
                a = qi[:, p * LANES:(p + 1) * LANES]
                acc = acc + coef[:, 2 * p:2 * p + 1] * jnp.maximum(_dot_nt(a, klo), 0.0)
                acc = acc + coef[:, 2 * p + 1:2 * p + 2] * jnp.maximum(_dot_nt(a, khi), 0.0)
            s_pos = c * cw + lax.broadcasted_iota(I32, (tq, cw), 1)
            key = _sortable(jnp.where(s_pos <= t_pos, acc, -jnp.inf))
            for u in range(cw // LANES):
                key3[c * (cw // LANES) + u] = key[:, u * LANES:(u + 1) * LANES]

        @pl.when(c * cw > t_last)
        def _():
            for u in range(cw // LANES):
                key3[c * (cw // LANES) + u] = jnp.full((tq, LANES), INT_MIN, I32)

    def count_ge(cs):
        ge = jnp.where(key3[...] >= cs[None], 1.0, 0.0)
        return jnp.sum(jnp.sum(ge, axis=0), axis=1, keepdims=True)

    thr = _kth_largest_key(count_ge, (tq, 1), float(topk))
    thr4 = jnp.concatenate([thr] * GQA, axis=0)

    qb = q_ref[0].astype(BF16)
    for kh in range(N_KV_HEADS):
        qs[kh] = jnp.concatenate(
            [qb[:, (GQA * kh + g) * HEAD_DIM:(GQA * kh + g + 1) * HEAD_DIM] for g in range(GQA)], axis=0)
    m_s[...] = jnp.full(m_s.shape, NEG, F32)
    l_s[...] = jnp.zeros(l_s.shape, F32)
    acc_s[...] = jnp.zeros(acc_s.shape, F32)
    scale = HEAD_DIM ** -0.5

    def fbody(j, carry):
        part = jnp.clip(j - i + 2, 0, 2)
        key4 = jnp.concatenate([key3[j]] * GQA, axis=0)
        mask = key4 >= thr4
        r0 = pl.multiple_of(j * LANES, LANES)
        for kh in range(N_KV_HEADS):
            kj = kbf[pl.ds(r0, LANES), kh * HEAD_DIM:(kh + 1) * HEAD_DIM]
            vj = vbf[pl.ds(r0, LANES), kh * HEAD_DIM:(kh + 1) * HEAD_DIM]
            bias = bt_ref[part, GQA * kh:GQA * (kh + 1)].reshape(GQA * tq, LANES)
            lg = jnp.where(mask, _dot_nt(qs[kh], kj) * scale + bias, NEG)
            m_old = m_s[kh]
            m_new = jnp.maximum(m_old, jnp.max(lg, axis=1, keepdims=True))
            alpha = jnp.exp(m_old - m_new)
            p = jnp.where(mask, jnp.exp(lg - m_new), 0.0)
            l_s[kh] = alpha * l_s[kh] + jnp.sum(p, axis=1, keepdims=True)
            acc_s[kh] = alpha * acc_s[kh] + _dot(p.astype(BF16), vj)
            m_s[kh] = m_new
        return carry

    lax.fori_loop(0, i + 1, fbody, 0)
    for kh in range(N_KV_HEADS):
        out = acc_s[kh] / l_s[kh]
        for g in range(GQA):
            h = GQA * kh + g
            o_ref[0, :, h * HEAD_DIM:(h + 1) * HEAD_DIM] = out[g * tq:(g + 1) * tq]


def _dsa_prompt(z, zs, btiles, *, batch, seq, topk):
    tq = LANES
    aw = N_HEADS * HEAD_DIM
    kvw = N_KV_HEADS * HEAD_DIM
    qiw = IDX_HEADS * IDX_DIM
    return pl.pallas_call(
        functools.partial(_dsa_prompt_kernel, topk=topk),
        grid=(batch, seq // tq),
        in_specs=[
            pl.BlockSpec((1, tq, aw), lambda b, i: (b, i, 0)),
            pl.BlockSpec((1, seq, kvw), lambda b, i: (b, 0, aw // kvw)),
            pl.BlockSpec((1, seq, kvw), lambda b, i: (b, 0, aw // kvw + 1)),
            pl.BlockSpec((1, tq, qiw), lambda b, i: (b, i, (aw + 2 * kvw) // qiw)),
            pl.BlockSpec((1, seq, LANES), lambda b, i: (b, 0, 0)),
            pl.BlockSpec((1, tq, LANES), lambda b, i: (b, i, 0)),
            pl.BlockSpec((3, N_HEADS, LANES, LANES), lambda b, i: (0, 0, 0, 0)),
        ],
        out_specs=pl.BlockSpec((1, tq, aw), lambda b, i: (b, i, 0)),
        out_shape=jax.ShapeDtypeStruct((batch, seq, aw), F32),
        scratch_shapes=[
            pltpu.VMEM((seq, kvw), BF16), pltpu.VMEM((seq, kvw), BF16),
            pltpu.VMEM((seq, LANES), BF16), pltpu.VMEM((seq, LANES), BF16),
            pltpu.VMEM((seq // LANES, tq, LANES), I32),
            pltpu.VMEM((N_KV_HEADS, GQA * tq, HEAD_DIM), BF16),
            pltpu.VMEM((N_KV_HEADS, GQA * tq, LANES), F32),
            pltpu.VMEM((N_KV_HEADS, GQA * tq, LANES), F32),
            pltpu.VMEM((N_KV_HEADS, GQA * tq, HEAD_DIM), F32),
        ],
        compiler_params=_params("arbitrary", "arbitrary"),
        name="dsa_prompt",
    )(z, z, z, z, zs, zs, btiles)


def _dsa_sample_score_kernel(pt_ref, qi_ref, wi_ref, kinew_ref, kidx_ref, o_ref, *, n_pages):
    j = pl.program_id(1)
    is_new = j == n_pages
    ki = jnp.where(is_new, jnp.broadcast_to(kinew_ref[0], kidx_ref.shape[1:]), kidx_ref[0])
    r = jnp.maximum(_dot_nt(qi_ref[0].astype(BF16), ki.astype(BF16)), 0.0)
    sc = jnp.sum(r * (wi_ref[0] * (IDX_HEADS ** -0.5 * IDX_DIM ** -0.5)), axis=0, keepdims=True)
    lane = lax.broadcasted_iota(I32, sc.shape, 1)
    o_ref[0] = jnp.where(jnp.logical_and(is_new, lane >= 1), -jnp.inf, sc)


def _dsa_sample_scores(page_table, qi3, wi3, ki_new, cache_kidx):
    db, n_pages = page_table.shape
    page = cache_kidx.shape[1]
    grid_spec = pltpu.PrefetchScalarGridSpec(
        num_scalar_prefetch=1,
        grid=(db, n_pages + 1),
        in_specs=[
            pl.BlockSpec((1, IDX_HEADS, IDX_DIM), lambda b, j, pt: (b, 0, 0)),
            pl.BlockSpec((1, IDX_HEADS, 1), lambda b, j, pt: (b, 0, 0)),
            pl.BlockSpec((1, 1, IDX_DIM), lambda b, j, pt: (b, 0, 0)),
            pl.BlockSpec((1, page, IDX_DIM), lambda b, j, pt: (pt[b, jnp.minimum(j, n_pages - 1)], 0, 0)),
        ],
        out_specs=pl.BlockSpec((1, 1, page), lambda b, j, pt: (b, 0, j)),
    )
    return pl.pallas_call(
        functools.partial(_dsa_sample_score_kernel, n_pages=n_pages),
        grid_spec=grid_spec,
        out_shape=jax.ShapeDtypeStruct((db, 1, (n_pages + 1) * page), F32),
        compiler_params=_params("arbitrary", "arbitrary"),
        name="dsa_sample_scores",
    )(page_table, qi3, wi3, ki_new, cache_kidx)


def _dsa_sample_thr_kernel(s_ref, o_ref, *, topk):
    key = _sortable(s_ref[...])

    def count_ge(cs):
        return jnp.sum(jnp.where(key >= cs, 1.0, 0.0), axis=1, keepdims=True)

    thr = _kth_largest_key(count_ge, (key.shape[0], 1), float(topk))
    o_ref[...] = jnp.broadcast_to(thr, o_ref.shape)


def _dsa_sample_thr(scores, *, topk):
    db = scores.shape[0]
    return pl.pallas_call(
        functools.partial(_dsa_sample_thr_kernel, topk=topk),
        out_shape=jax.ShapeDtypeStruct((db, LANES), I32),
        name="dsa_sample_threshold",
    )(scores)


def _dsa_sample_attn_kernel(pt_ref, q_ref, knew_ref, vnew_ref, s_ref, thr_ref, rbt_ref, k_ref, v_ref, o_ref,
                            m_s, l_s, acc_s, *, n_pages, past):
    j = pl.program_id(1)
    page = k_ref.shape[1]

    @pl.when(j == 0)
    def _():
        m_s[...] = jnp.full(m_s.shape, NEG, F32)
        l_s[...] = jnp.zeros(l_s.shape, F32)
        acc_s[...] = jnp.zeros(acc_s.shape, F32)

    is_new = j == n_pages
    kp = jnp.where(is_new, jnp.broadcast_to(knew_ref[0], k_ref.shape[1:]), k_ref[0]).astype(BF16)
    vp = jnp.where(is_new, jnp.broadcast_to(vnew_ref[0], v_ref.shape[1:]), v_ref[0]).astype(BF16)
    mask = _sortable(s_ref[0]) >= thr_ref[0]
    pos = j * page + lax.broadcasted_iota(I32, (1, page), 1)
    bucket = _t5_bucket(past - pos)
    bias = jnp.zeros((N_HEADS, page), F32)
    for b in range(NUM_BUCKETS):
        bias = jnp.where(bucket == b, rbt_ref[:, b:b + 1], bias)
    q = q_ref[0].astype(BF16)
    kv_of_row = jnp.right_shift(lax.broadcasted_iota(I32, (N_HEADS, page), 0), GQA.bit_length() - 1)
    lg = jnp.zeros((N_HEADS, page), F32)
    for kh in range(N_KV_HEADS):
        lg = jnp.where(kv_of_row == kh, _dot_nt(q, kp[:, kh * HEAD_DIM:(kh + 1) * HEAD_DIM]), lg)
    lg = jnp.where(mask, lg * HEAD_DIM ** -0.5 + bias, NEG)
    m_old = m_s[...]
    m_new = jnp.maximum(m_old, jnp.max(lg, axis=1, keepdims=True))
    alpha = jnp.exp(m_old - m_new)
    p = jnp.where(mask, jnp.exp(lg - m_new), 0.0)
    l_s[...] = alpha * l_s[...] + jnp.sum(p, axis=1, keepdims=True)
    pv = _dot(p.astype(BF16), vp)
    upd = jnp.zeros((N_HEADS, HEAD_DIM), F32)
    for kh in range(N_KV_HEADS):
        upd = jnp.where(kv_of_row == kh, pv[:, kh * HEAD_DIM:(kh + 1) * HEAD_DIM], upd)
    acc_s[...] = alpha * acc_s[...] + upd
    m_s[...] = m_new

    @pl.when(is_new)
    def _():
        o_ref[0] = acc_s[...] / l_s[...]


def _dsa_sample_attn(page_table, q3, k_new, v_new, scores3, thr3, rb_t, cache_k, cache_v):
    db, n_pages = page_table.shape
    page = cache_k.shape[1]
    kvw = cache_k.shape[2]
    past = n_pages * page
    pg = lambda b, j, pt: (pt[b, jnp.minimum(j, n_pages - 1)], 0, 0)
    row = lambda b, j, pt: (b, 0, 0)
    grid_spec = pltpu.PrefetchScalarGridSpec(
        num_scalar_prefetch=1,
        grid=(db, n_pages + 1),
        in_specs=[
            pl.BlockSpec((1, N_HEADS, HEAD_DIM), row),
            pl.BlockSpec((1, 1, kvw), row),
            pl.BlockSpec((1, 1, kvw), row),
            pl.BlockSpec((1, 1, page), lambda b, j, pt: (b, 0, j)),
            pl.BlockSpec((1, 1, LANES), row),
            pl.BlockSpec((N_HEADS, NUM_BUCKETS), lambda b, j, pt: (0, 0)),
            pl.BlockSpec((1, page, kvw), pg),
            pl.BlockSpec((1, page, kvw), pg),
        ],
        out_specs=pl.BlockSpec((1, N_HEADS, HEAD_DIM), row),
        scratch_shapes=[pltpu.VMEM((N_HEADS, LANES), F32), pltpu.VMEM((N_HEADS, LANES), F32),
                        pltpu.VMEM((N_HEADS, HEAD_DIM), F32)],
    )
    return pl.pallas_call(
        functools.partial(_dsa_sample_attn_kernel, n_pages=n_pages, past=past),
        grid_spec=grid_spec,
        out_shape=jax.ShapeDtypeStruct((db, N_HEADS, HEAD_DIM), F32),
        compiler_params=_params("arbitrary", "arbitrary"),
        name="dsa_sample_attn",
    )(page_table, q3, k_new, v_new, scores3, thr3, rb_t, cache_k, cache_v)


def _mix_kernel(att_ref, zu_ref, zv_ref, ga_ref, gb_ref, gs_ref, ws_ref, bs_ref, h_ref, w_ref,
                o_ref, vn_ref, m_ref, *, first_rows_only):
    aw = att_ref.shape[1]

    @pl.when(pl.program_id(1) == 0)
    def _():
        tm = zu_ref.shape[0]
        v = jax.nn.gelu(zv_ref[...])
        vn = v * lax.rsqrt(jnp.mean(v * v, axis=-1, keepdims=True) + EPS) * gs_ref[...]
        vn_ref[...] = vn
        m_ref[:, :aw] = (jax.nn.sigmoid(ga_ref[...]) * att_ref[...]).astype(BF16)
        if first_rows_only:
            sgu = jax.nn.gelu(zu_ref[...]) * (vn * ws_ref[...] + bs_ref[...])
            m_ref[:, aw:] = (jax.nn.sigmoid(gb_ref[...]) * sgu).astype(BF16)
        else:
            row = lax.broadcasted_iota(I32, (CHUNK, CHUNK), 0)
            col = lax.broadcasted_iota(I32, (CHUNK, CHUNK), 1)
            for g in range(GMLP_GROUPS):
                cs = slice(g * LANES, (g + 1) * LANES)
                wt = jnp.where(col <= row, ws_ref[g], 0.0).astype(BF16)
                bcol = bs_ref[:, g:g + 1]
                for c in range(tm // CHUNK):
                    rs = slice(c * CHUNK, (c + 1) * CHUNK)
                    mixed = _dot(wt, vn_ref[rs, cs].astype(BF16)) + bcol
                    sgu = jax.nn.gelu(zu_ref[rs, cs]) * mixed
                    m_ref[rs, aw + g * LANES:aw + (g + 1) * LANES] = (
                        jax.nn.sigmoid(gb_ref[rs, cs]) * sgu).astype(BF16)

    o_ref[...] = h_ref[...] + _dot(m_ref[...], w_ref[...])


def _mix(att, z, h, g_sgu, ws, bs, w_out, *, first_rows_only, tm, tn, name):
    m, d = h.shape
    aw = att.shape[1]
    gw = g_sgu.shape[0]
    tm, tn = min(tm, m), min(tn, d)
    zb = lambda c: pl.BlockSpec((tm, gw), lambda i, j: (i, c))
    full = lambda a: pl.BlockSpec(a.shape, lambda i, j: (0,) * a.ndim)
    first = z.shape[1] // gw - 4
    return pl.pallas_call(
        functools.partial(_mix_kernel, first_rows_only=first_rows_only),
        grid=(m // tm, d // tn),
        in_specs=[pl.BlockSpec((tm, aw), lambda i, j: (i, 0)),
                  zb(first), zb(first + 1), zb(first + 2), zb(first + 3),
                  pl.BlockSpec((1, gw), lambda i, j: (0, 0)),
                  full(ws), full(bs),
                  pl.BlockSpec((tm, tn), lambda i, j: (i, j)),
                  pl.BlockSpec((aw + gw, tn), lambda i, j: (0, j))],
        out_specs=[pl.BlockSpec((tm, tn), lambda i, j: (i, j)),
                   pl.BlockSpec((tm, gw), lambda i, j: (i, 0))],
        out_shape=[jax.ShapeDtypeStruct((m, d), F32), jax.ShapeDtypeStruct((m, gw), F32)],
        scratch_shapes=[pltpu.VMEM((tm, aw + gw), BF16)],
        compiler_params=_params("arbitrary", "arbitrary"),
        name=name,
    )(att, z, z, z, z, g_sgu.reshape(1, gw), ws, bs, h, w_out)


def _cross_kernel(q_ref, mk_ref, mv_ref, h_ref, wo_ref, o_ref):
    rows = q_ref.shape[1]
    q = q_ref[0]
    if rows < SUBLANES:
        q = jnp.broadcast_to(q, (SUBLANES, q.shape[1]))
    qb = q.astype(BF16)
    mk = mk_ref[0].astype(BF16)
    mv = mv_ref[0].astype(BF16)
    hd = mk.shape[1] // X_HEADS
    outs = []
    for hh in range(X_HEADS):
        sl = slice(hh * hd, (hh + 1) * hd)
        lg = _dot_nt(qb[:, sl], mk[:, sl]) * hd ** -0.5
        e = jnp.exp(lg - jnp.max(lg, axis=1, keepdims=True))
        p = e / jnp.sum(e, axis=1, keepdims=True)
        outs.append(_dot(p.astype(BF16), mv[:, sl]))
    y = _dot(jnp.concatenate(outs, axis=1).astype(BF16), wo_ref[...])
    o_ref[0] = h_ref[0] + y[:rows]


def _cross(q, mk, mv, h, w_xo, *, tq, name):
    b, t, xw = q.shape
    d = h.shape[2]
    mlen = mk.shape[1]
    return pl.pallas_call(
        _cross_kernel,
        grid=(b, t // tq),
        in_specs=[pl.BlockSpec((1, tq, xw), lambda bb, i: (bb, i, 0)),
                  pl.BlockSpec((1, mlen, xw), lambda bb, i: (bb, 0, 0)),
                  pl.BlockSpec((1, mlen, xw), lambda bb, i: (bb, 0, 0)),
                  pl.BlockSpec((1, tq, d), lambda bb, i: (bb, i, 0)),
                  pl.BlockSpec((xw, d), lambda bb, i: (0, 0))],
        out_specs=pl.BlockSpec((1, tq, d), lambda bb, i: (bb, i, 0)),
        out_shape=jax.ShapeDtypeStruct((b, t, d), F32),
        compiler_params=_params("arbitrary", "arbitrary"),
        name=name,
    )(q, mk, mv, h, w_xo)


def _top_rows(s, k, payload=None):
    r = s.shape[0]
    rid = lax.broadcasted_iota(I32, s.shape, 0).astype(F32)
    vals, picks = [], []
    for _ in range(k):
        m = jnp.max(s, axis=0, keepdims=True)
        am = jnp.min(jnp.where(s == m, rid, float(r)), axis=0, keepdims=True)
        hit = rid == am
        vals.append(m)
        picks.append(am if payload is None else jnp.sum(jnp.where(hit, payload, 0.0), axis=0, keepdims=True))
        s = jnp.where(hit, -jnp.inf, s)
    return jnp.concatenate(vals, axis=0), jnp.concatenate(picks, axis=0)


def _peer_route_kernel(h_ref, g_ref, w_ref, sk_ref, ids_ref, gate_ref, xn_ref):
    @pl.when(pl.program_id(1) == 0)
    def _():
        x = h_ref[...]
        xn_ref[...] = (x * lax.rsqrt(jnp.mean(x * x, axis=-1, keepdims=True) + EPS) * g_ref[...]).astype(BF16)

    half = sk_ref.shape[2]
    qt = _dot_nt(w_ref[...], xn_ref[...]).astype(BF16)
    v0, i0 = _top_rows(_dot(sk_ref[0], qt[:half]), PEER_TOPK)
    v1, i1 = _top_rows(_dot(sk_ref[1], qt[half:]), PEER_TOPK)
    cand = jnp.concatenate([v0[a:a + 1] + v1 for a in range(PEER_TOPK)], axis=0)
    eid = jnp.concatenate([i0[a:a + 1] * float(N_KEYS) + i1 for a in range(PEER_TOPK)], axis=0)
    sc, e = _top_rows(cand, PEER_TOPK, payload=eid)
    ex = jnp.exp(sc - sc[0:1])
    gate_ref[...] = ex / jnp.sum(ex, axis=0, keepdims=True)
    ids_ref[...] = e.astype(I32)


def _peer_route(h, g_ffn, w_pq_t, sub_keys, *, tb):
    m, d = h.shape
    qd = w_pq_t.shape[0] // PEER_HEADS
    return pl.pallas_call(
        _peer_route_kernel,
        grid=(m // tb, PEER_HEADS),
        in_specs=[pl.BlockSpec((tb, d), lambda t, hh: (t, 0)),
                  pl.BlockSpec((1, d), lambda t, hh: (0, 0)),
                  pl.BlockSpec((qd, d), lambda t, hh: (hh, 0)),
                  pl.BlockSpec(sub_keys.shape, lambda t, hh: (0, 0, 0))],
        out_specs=[pl.BlockSpec((PEER_TOPK, tb), lambda t, hh: (hh, t)),
                   pl.BlockSpec((PEER_TOPK, tb), lambda t, hh: (hh, t))],
        out_shape=[jax.ShapeDtypeStruct((PEER_HEADS * PEER_TOPK, m), I32),
                   jax.ShapeDtypeStruct((PEER_HEADS * PEER_TOPK, m), F32)],
        scratch_shapes=[pltpu.VMEM((tb, d), BF16)],
        compiler_params=_params("arbitrary", "arbitrary"),
        name="peer_route",
    )(h, g_ffn.reshape(1, d), w_pq_t, sub_keys)


def _peer_gather_kernel(ids_hbm, gate_ref, h_ref, gffn_ref, gfin_ref, u_hbm, v_hbm, o_ref,
                        ids_s, ubuf, vbuf, part_s, sem_ids, sem_u, sem_v, *, n_steps):
    s = pl.program_id(0)
    g_tok = h_ref.shape[0]
    n_e = ubuf.shape[1]
    d = ubuf.shape[2]
    nlb = d // LANES
    cur = lax.rem(s, 2)
    nxt = 1 - cur

    def ids_copy(step, slot):
        return pltpu.make_async_copy(ids_hbm.at[step], ids_s.at[slot], sem_ids.at[slot])

    def row_copies(e, k, slot):
        return (pltpu.make_async_copy(u_hbm.at[pl.ds(e, 1)], ubuf.at[slot, pl.ds(k, 1)], sem_u.at[slot]),
                pltpu.make_async_copy(v_hbm.at[pl.ds(e, 1)], vbuf.at[slot, pl.ds(k, 1)], sem_v.at[slot]))

    def issue(ids_slot, t, slot):
        for k in range(n_e):
            cu, cv = row_copies(ids_s[ids_slot, t, k], k, slot)
            cu.start()
            cv.start()

    def wait(slot):
        for k in range(n_e):
            cu, cv = row_copies(0, k, slot)
            cu.wait()
            cv.wait()

    @pl.when(s == 0)
    def _():
        ids_copy(0, 0).start()
        ids_copy(0, 0).wait()
        issue(0, 0, 0)

    @pl.when(s + 1 < n_steps)
    def _():
        ids_copy(s + 1, nxt).start()

    h = h_ref[...]
    xn = h * lax.rsqrt(jnp.mean(h * h, axis=-1, keepdims=True) + EPS) * gffn_ref[...]
    eye = lax.broadcasted_iota(I32, (n_e, n_e), 0) == lax.broadcasted_iota(I32, (n_e, n_e), 1)

    for t in range(g_tok):
        slot = t % 2
        if t + 1 < g_tok:
            issue(cur, t + 1, 1 - slot)
        else:
            @pl.when(s + 1 < n_steps)
            def _():
                ids_copy(s + 1, nxt).wait()
                issue(nxt, 0, 1 - slot)
        wait(slot)
        x = xn[t:t + 1, :]
        hacc = jnp.zeros((n_e, LANES), F32)
        for cb in range(nlb):
            cs = slice(cb * LANES, (cb + 1) * LANES)
            hacc = hacc + ubuf[slot, :, cs] * x[:, cs]
        act = jax.nn.gelu(jnp.sum(hacc, axis=-1, keepdims=True))
        gcol = jnp.sum(jnp.where(eye, jnp.broadcast_to(gate_ref[t:t + 1, :], (n_e, n_e)), 0.0),
                       axis=-1, keepdims=True)
        w = jnp.broadcast_to(gcol * act, (n_e, LANES))
        for cb in range(nlb):
            cs = slice(cb * LANES, (cb + 1) * LANES)
            prod = vbuf[slot, :, cs] * w
            part_s[t, :, cs] = jnp.sum(prod.reshape(n_e // SUBLANES, SUBLANES, LANES), axis=0)

    y = h + jnp.sum(part_s[...], axis=1)
    o_ref[...] = y * lax.rsqrt(jnp.mean(y * y, axis=-1, keepdims=True) + EPS) * gfin_ref[...]


def _peer_gather(ids, gates, h, g_ffn, g_final, u_tab, v_tab, *, g_tok):
    m, d = h.shape
    n_e = ids.shape[1]
    n_steps = m // g_tok
    ids3 = ids.reshape(n_steps, g_tok, n_e)
    return pl.pallas_call(
        functools.partial(_peer_gather_kernel, n_steps=n_steps),
        grid=(n_steps,),
        in_specs=[pl.BlockSpec(memory_space=pl.ANY),
                  pl.BlockSpec((g_tok, n_e), lambda s: (s, 0)),
                  pl.BlockSpec((g_tok, d), lambda s: (s, 0)),
                  pl.BlockSpec((1, d), lambda s: (0, 0)),
                  pl.BlockSpec((1, d), lambda s: (0, 0)),
                  pl.BlockSpec(memory_space=pl.ANY),
                  pl.BlockSpec(memory_space=pl.ANY)],
        out_specs=pl.BlockSpec((g_tok, d), lambda s: (s, 0)),
        out_shape=jax.ShapeDtypeStruct((m, d), F32),
        scratch_shapes=[pltpu.SMEM((2, g_tok, n_e), I32),
                        pltpu.VMEM((2, n_e, d), F32), pltpu.VMEM((2, n_e, d), F32),
                        pltpu.VMEM((g_tok, SUBLANES, d), F32),
                        pltpu.SemaphoreType.DMA((2,)), pltpu.SemaphoreType.DMA((2,)),
                        pltpu.SemaphoreType.DMA((2,))],
        compiler_params=_params("arbitrary"),
        name="peer_gather",
    )(ids3, gates, h, g_ffn.reshape(1, d), g_final.reshape(1, d), u_tab, v_tab)


def kernel(x_prompt, x_sample, cache_k, cache_v, cache_kidx, cache_mem_k, cache_mem_v, page_table,
           mem_prompt, rel_bias, g_in, w_in, g_sgu, w_s, b_s, w_out, g_x, w_xq, w_xk, w_xv, w_xo,
           g_ffn, w_pq, sub_keys, peer_u, peer_v, g_final):
    batch, seq, d = x_prompt.shape
    db, dseq, _ = x_sample.shape
    depth = w_in.shape[0]
    assert dseq == 1, "sample group is one new token per sequence"
    n_pages = page_table.shape[1]
    page = cache_k.shape[2]
    past = n_pages * page
    aw, kvw, qiw, gw = N_HEADS * HEAD_DIM, N_KV_HEADS * HEAD_DIM, IDX_HEADS * IDX_DIM, GMLP_GROUPS * LANES
    xw = w_xq.shape[2]
    mlen = mem_prompt.shape[1]
    small_lo = aw + 2 * kvw + qiw
    small_hi = small_lo + IDX_DIM + IDX_HEADS
    assert NUM_BUCKETS // 2 + int(math.log((LANES + 1) / (NUM_BUCKETS // 2)) / math.log(
        MAX_DISTANCE / (NUM_BUCKETS // 2)) * (NUM_BUCKETS // 2)) >= NUM_BUCKETS - 1

    hp = x_prompt.reshape(batch * seq, d)
    hs = x_sample.reshape(db, d)
    btiles = _bias_tiles(rel_bias)
    rb_t = rel_bias.T
    outs = {n: [] for n in ("kp", "vp", "ip", "mkp", "mvp", "ks", "vs", "is", "sv")}

    for l in range(depth):
        w_main = jnp.concatenate([w_in[l][:, :small_lo], w_in[l][:, small_hi:]], axis=1).astype(BF16)
        w_small = jnp.pad(w_in[l][:, small_lo:small_hi], ((0, 0), (0, LANES - (small_hi - small_lo)))).astype(BF16)
        w_out_b = w_out[l].astype(BF16)
        w_xq_b, w_xk_b, w_xv_b, w_xo_b = (w.astype(BF16) for w in (w_xq[l], w_xk[l], w_xv[l], w_xo[l]))
        w_pq_t = w_pq[l].T.astype(BF16)
        sk_b = sub_keys[l].astype(BF16)
        ws_first = jnp.repeat(w_s[l][:, 0, 0], LANES).reshape(1, gw)
        bs_first = jnp.repeat(b_s[l][:, 0], LANES).reshape(1, gw)
        ones_d = jnp.ones((d,), F32)

        z = _proj(hp, g_in[l], w_main, norm=True, tm=512, tn=512, name="in_proj_prompt")
        zs = _proj(hp, g_in[l], w_small, norm=True, tm=512, tn=LANES, name="in_proj_idx_prompt")
        att = _dsa_prompt(z.reshape(batch, seq, -1), zs.reshape(batch, seq, LANES), btiles,
                          batch=batch, seq=seq, topk=min(TOPK_MAX, seq // 4))
        hp1, _ = _mix(att.reshape(batch * seq, aw), z, hp, g_sgu[l], w_s[l], b_s[l].T, w_out_b,
                      first_rows_only=False, tm=256, tn=512, name="mix_prompt")
        mem = mem_prompt.reshape(batch * mlen, d)
        mk = _proj(mem, ones_d, w_xk_b, norm=False, tm=256, tn=xw, name="mem_k_proj")
        mv = _proj(mem, ones_d, w_xv_b, norm=False, tm=256, tn=xw, name="mem_v_proj")
        qx = _proj(hp1, g_x[l], w_xq_b, norm=True, tm=512, tn=xw, name="xq_proj_prompt")
        hp2 = _cross(qx.reshape(batch, seq, xw), mk.reshape(batch, mlen, xw), mv.reshape(batch, mlen, xw),
                     hp1.reshape(batch, seq, d), w_xo_b, tq=256, name="cross_prompt").reshape(batch * seq, d)
        outs["kp"].append(z[:, aw:aw + kvw].reshape(batch, seq, N_KV_HEADS, HEAD_DIM))
        outs["vp"].append(z[:, aw + kvw:aw + 2 * kvw].reshape(batch, seq, N_KV_HEADS, HEAD_DIM))
        outs["ip"].append(zs[:, :IDX_DIM].reshape(batch, seq, IDX_DIM))
        outs["mkp"].append(mk.reshape(batch, mlen, X_HEADS, xw // X_HEADS))
        outs["mvp"].append(mv.reshape(batch, mlen, X_HEADS, xw // X_HEADS))

        z_s = _proj(hs, g_in[l], w_main, norm=True, tm=db, tn=512, name="in_proj_sample")
        zs_s = _proj(hs, g_in[l], w_small, norm=True, tm=db, tn=LANES, name="in_proj_idx_sample")
        qi3 = z_s[:, aw + 2 * kvw:aw + 2 * kvw + qiw].reshape(db, IDX_HEADS, IDX_DIM)
        wi3 = zs_s[:, IDX_DIM:IDX_DIM + IDX_HEADS].reshape(db, IDX_HEADS, 1)
        ki_new = zs_s[:, :IDX_DIM].reshape(db, 1, IDX_DIM)
        k_new = z_s[:, aw:aw + kvw].reshape(db, 1, kvw)
        v_new = z_s[:, aw + kvw:aw + 2 * kvw].reshape(db, 1, kvw)
        scores3 = _dsa_sample_scores(page_table, qi3, wi3, ki_new, cache_kidx[l])
        thr = _dsa_sample_thr(scores3.reshape(db, -1), topk=min(TOPK_MAX, (past + dseq) // 4))
        att_s = _dsa_sample_attn(page_table, z_s[:, :aw].reshape(db, N_HEADS, HEAD_DIM), k_new, v_new,
                                 scores3, thr.reshape(db, 1, LANES), rb_t,
                                 cache_k[l].reshape(-1, page, kvw), cache_v[l].reshape(-1, page, kvw))
        hs1, vn_s = _mix(att_s.reshape(db, aw), z_s, hs, g_sgu[l], ws_first, bs_first, w_out_b,
                         first_rows_only=True, tm=db, tn=512, name="mix_sample")
        qx_s = _proj(hs1, g_x[l], w_xq_b, norm=True, tm=db, tn=xw, name="xq_proj_sample")
        hs2 = _cross(qx_s.reshape(db, 1, xw), cache_mem_k[l].reshape(db, mlen, xw),
                     cache_mem_v[l].reshape(db, mlen, xw), hs1.reshape(db, 1, d), w_xo_b,
                     tq=1, name="cross_sample").reshape(db, d)
        outs["ks"].append(k_new.reshape(db, dseq, N_KV_HEADS, HEAD_DIM))
        outs["vs"].append(v_new.reshape(db, dseq, N_KV_HEADS, HEAD_DIM))
        outs["is"].append(ki_new.reshape(db, dseq, IDX_DIM))
        outs["sv"].append(vn_s.reshape(db, dseq, gw))

        assert depth == 1
        tb = 256
        n_tok = batch * seq + db
        n_pad = -(-n_tok // tb) * tb
        h_all = jnp.concatenate([hp2, hs2, jnp.zeros((n_pad - n_tok, d), F32)], axis=0)
        ids_t, gates_t = _peer_route(h_all, g_ffn[l], w_pq_t, sk_b, tb=tb)
        y_all = _peer_gather(ids_t.T, gates_t.T, h_all, g_ffn[l], g_final, peer_u[l], peer_v[l], g_tok=8)

    y_prompt = y_all[:batch * seq].reshape(batch, seq, d)
    y_sample = y_all[batch * seq:n_tok].reshape(db, dseq, d)
    st = lambda n: jnp.stack(outs[n])
    return (y_prompt, y_sample, st("kp"), st("vp"), st("ip"), st("mkp"), st("mvp"),
            st("ks"), st("vs"), st("is"), st("sv"))
```

```python
import functools
import math

import jax
import jax.numpy as jnp
from jax import lax
from jax.experimental import pallas as pl
from jax.experimental.pallas import tpu as pltpu

F32 = jnp.float32
BF16 = jnp.bfloat16
I32 = jnp.int32

N_HEADS = 16
HEAD_DIM = 128
N_KV_HEADS = 4
GQA = N_HEADS // N_KV_HEADS
IDX_HEADS = 16
IDX_DIM = 64
TOPK_MAX = 256
NUM_BUCKETS = 32
MAX_DISTANCE = 128
CHUNK = 128
GMLP_GROUPS = 16
X_HEADS = 4
N_KEYS = 128
PEER_HEADS = 8
PEER_TOPK = 16
EPS = 1e-6

LANES = 128
SUBLANES = 8
VMEM_LIMIT_BYTES = 56 * 1024 * 1024

NEG = -1e30
INT_MIN = -(2 ** 31)


def _dot_nt(a, b):
    return lax.dot_general(a, b, (((1,), (1,)), ((), ())), preferred_element_type=F32)


def _dot(a, b):
    return jnp.dot(a, b, preferred_element_type=F32)


def _params(*sem):
    return pltpu.CompilerParams(dimension_semantics=sem, vmem_limit_bytes=VMEM_LIMIT_BYTES)


def _sortable(x):
    bits = pltpu.bitcast(x, I32)
    return bits ^ (jnp.right_shift(bits, 31) & 0x7FFFFFFF)


def _t5_bucket(dist):
    n = jnp.maximum(dist, 0)
    max_exact = NUM_BUCKETS // 2
    nf = jnp.maximum(n, 1).astype(F32)
    large = max_exact + (jnp.log(nf / max_exact) / math.log(MAX_DISTANCE / max_exact)
                         * (NUM_BUCKETS - max_exact)).astype(I32)
    large = jnp.minimum(large, NUM_BUCKETS - 1)
    return jnp.where(n < max_exact, n, large)


def _kth_largest_key(count_ge, shape, k):
    def body(it, t):
        cand = t | jnp.left_shift(jnp.int32(1), 31 - it)
        cnt = count_ge(cand ^ INT_MIN)
        return jnp.where(cnt >= k, cand, t)

    t = lax.fori_loop(0, 32, body, jnp.zeros(shape, I32))
    return t ^ INT_MIN


def _proj_kernel(x_ref, g_ref, w_ref, o_ref, xn_ref, *, norm):
    @pl.when(pl.program_id(1) == 0)
    def _():
        x = x_ref[...]
        if norm:
            x = x * lax.rsqrt(jnp.mean(x * x, axis=-1, keepdims=True) + EPS) * g_ref[...]
        xn_ref[...] = x.astype(BF16)

    o_ref[...] = _dot(xn_ref[...], w_ref[...])


def _proj(x, g, w, *, norm, tm, tn, name):
    m, k = x.shape
    n = w.shape[1]
    tm, tn = min(tm, m), min(tn, n)
    return pl.pallas_call(
        functools.partial(_proj_kernel, norm=norm),
        grid=(m // tm, n // tn),
        in_specs=[pl.BlockSpec((tm, k), lambda i, j: (i, 0)),
                  pl.BlockSpec((1, k), lambda i, j: (0, 0)),
                  pl.BlockSpec((k, tn), lambda i, j: (0, j))],
        out_specs=pl.BlockSpec((tm, tn), lambda i, j: (i, j)),
        out_shape=jax.ShapeDtypeStruct((m, n), F32),
        scratch_shapes=[pltpu.VMEM((tm, k), BF16)],
        compiler_params=_params("arbitrary", "arbitrary"),
        name=name,
    )(x, g.reshape(1, k), w)


def _bias_tiles_kernel(rb_ref, o_ref):
    tq = lax.broadcasted_iota(I32, (LANES, LANES), 0)
    c = lax.broadcasted_iota(I32, (LANES, LANES), 1)
    for part, off in enumerate((2 * LANES, LANES, 0)):
        dist = tq - c + off
        bucket = _t5_bucket(dist)
        for h in range(N_HEADS):
            tile = lax.fori_loop(
                0, NUM_BUCKETS, lambda b, acc: jnp.where(bucket == b, rb_ref[b, h], acc),
                jnp.zeros((LANES, LANES), F32))
            if off == 0:
                tile = jnp.where(dist < 0, NEG, tile)
            o_ref[part, h] = tile


def _bias_tiles(rel_bias):
    return pl.pallas_call(
        _bias_tiles_kernel,
        in_specs=[pl.BlockSpec(memory_space=pltpu.SMEM)],
        out_specs=pl.BlockSpec(memory_space=pltpu.VMEM),
        out_shape=jax.ShapeDtypeStruct((3, N_HEADS, LANES, LANES), F32),
        name="t5_bias_tiles",
    )(rel_bias)


def _dsa_prompt_kernel(q_ref, k_ref, v_ref, qi_ref, zsk_ref, zsq_ref, bt_ref, o_ref,
                       kbf, vbf, kilo, kihi, key3, qs, m_s, l_s, acc_s, *, topk):
    i = pl.program_id(1)
    tq = q_ref.shape[1]
    s = k_ref.shape[1]
    nkb = s // LANES
    cw = 2 * LANES

    @pl.when(i == 0)
    def _():
        kbf[...] = k_ref[0].astype(BF16)
        vbf[...] = v_ref[0].astype(BF16)
        zs = zsk_ref[0]
        lane = lax.broadcasted_iota(I32, zs.shape, 1)
        kilo[...] = jnp.where(lane < IDX_DIM, zs, 0.0).astype(BF16)
        kihi[...] = jnp.where(lane >= IDX_DIM, pltpu.roll(zs, IDX_DIM, axis=1), 0.0).astype(BF16)

    qi = qi_ref[0].astype(BF16)
    coef = zsq_ref[0][:, IDX_DIM:IDX_DIM + IDX_HEADS] * (IDX_HEADS ** -0.5 * IDX_DIM ** -0.5)
    t_pos = i * tq + lax.broadcasted_iota(I32, (tq, cw), 0)
    t_last = i * tq + tq - 1
    for c in range(s // cw):
        @pl.when(c * cw <= t_last)
        def _():
            klo = kilo[c * cw:(c + 1) * cw, :]
            khi = kihi[c * cw:(c + 1) * cw, :]
            acc = jnp.zeros((tq, cw), F32)
            for p in range(IDX_HEADS // 2):
                a = qi[:, p * LANES:(p + 1) * LANES]
                acc = acc + coef[:, 2 * p:2 * p + 1] * jnp.maximum(_dot_nt(a, klo), 0.0)
                acc = acc + coef[:, 2 * p + 1:2 * p + 2] * jnp.maximum(_dot_nt(a, khi), 0.0)
            s_pos = c * cw + lax.broadcasted_iota(I32, (tq, cw), 1)
            key = _sortable(jnp.where(s_pos <= t_pos, acc, -jnp.inf))
            for u in range(cw // LANES):
                key3[c * (cw // LANES) + u] = key[:, u * LANES:(u + 1) * LANES]

        @pl.when(c * cw > t_last)
        def _():
            for u in range(cw // LANES):
                key3[c * (cw // LANES) + u] = jnp.full((tq, LANES), INT_MIN, I32)

    def count_ge(cs):
        ge = jnp.where(key3[...] >= cs[None], 1.0, 0.0)
        return jnp.sum(jnp.sum(ge, axis=0), axis=1, keepdims=True)

    thr = _kth_largest_key(count_ge, (tq, 1), float(topk))
    thr4 = jnp.concatenate([thr] * GQA, axis=0)

    qb = q_ref[0].astype(BF16)
    for kh in range(N_KV_HEADS):
        qs[kh] = jnp.concatenate(
            [qb[:, (GQA * kh + g) * HEAD_DIM:(GQA * kh + g + 1) * HEAD_DIM] for g in range(GQA)], axis=0)
    m_s[...] = jnp.full(m_s.shape, NEG, F32)
    l_s[...] = jnp.zeros(l_s.shape, F32)
    acc_s[...] = jnp.zeros(acc_s.shape, F32)
    scale = HEAD_DIM ** -0.5

    def fbody(j, carry):
        part = jnp.clip(j - i + 2, 0, 2)
        key4 = jnp.concatenate([key3[j]] * GQA, axis=0)
        mask = key4 >= thr4
        r0 = pl.multiple_of(j * LANES, LANES)
        for kh in range(N_KV_HEADS):
            kj = kbf[pl.ds(r0, LANES), kh * HEAD_DIM:(kh + 1) * HEAD_DIM]
            vj = vbf[pl.ds(r0, LANES), kh * HEAD_DIM:(kh + 1) * HEAD_DIM]
            bias = bt_ref[part, GQA * kh:GQA * (kh + 1)].reshape(GQA * tq, LANES)
            lg = jnp.where(mask, _dot_nt(qs[kh], kj) * scale + bias, NEG)
            m_old = m_s[kh]
            m_new = jnp.maximum(m_old, jnp.max(lg, axis=1, keepdims=True))
            alpha = jnp.exp(m_old - m_new)
            p = jnp.where(mask, jnp.exp(lg - m_new), 0.0)
            l_s[kh] = alpha * l_s[kh] + jnp.sum(p, axis=1, keepdims=True)
            acc_s[kh] = alpha * acc_s[kh] + _dot(p.astype(BF16), vj)
            m_s[kh] = m_new
        return carry

    lax.fori_loop(0, i + 1, fbody, 0)
    for kh in range(N_KV_HEADS):
        out = acc_s[kh] / l_s[kh]
        for g in range(GQA):
            h = GQA * kh + g
            o_ref[0, :, h * HEAD_DIM:(h + 1) * HEAD_DIM] = out[g * tq:(g + 1) * tq]


def _dsa_prompt(z, zs, btiles, *, batch, seq, topk):
    tq = LANES
    aw = N_HEADS * HEAD_DIM
    kvw = N_KV_HEADS * HEAD_DIM
    qiw = IDX_HEADS * IDX_DIM
    return pl.pallas_call(
        functools.partial(_dsa_prompt_kernel, topk=topk),
        grid=(batch, seq // tq),
        in_specs=[
            pl.BlockSpec((1, tq, aw), lambda b, i: (b, i, 0)),
            pl.BlockSpec((1, seq, kvw), lambda b, i: (b, 0, aw // kvw)),
            pl.BlockSpec((1, seq, kvw), lambda b, i: (b, 0, aw // kvw + 1)),
            pl.BlockSpec((1, tq, qiw), lambda b, i: (b, i, (aw + 2 * kvw) // qiw)),
            pl.BlockSpec((1, seq, LANES), lambda b, i: (b, 0, 0)),
            pl.BlockSpec((1, tq, LANES), lambda b, i: (b, i, 0)),
            pl.BlockSpec((3, N_HEADS, LANES, LANES), lambda b, i: (0, 0, 0, 0)),
        ],
        out_specs=pl.BlockSpec((1, tq, aw), lambda b, i: (b, i, 0)),
        out_shape=jax.ShapeDtypeStruct((batch, seq, aw), F32),
        scratch_shapes=[
            pltpu.VMEM((seq, kvw), BF16), pltpu.VMEM((seq, kvw), BF16),
            pltpu.VMEM((seq, LANES), BF16), pltpu.VMEM((seq, LANES), BF16),
            pltpu.VMEM((seq // LANES, tq, LANES), I32),
            pltpu.VMEM((N_KV_HEADS, GQA * tq, HEAD_DIM), BF16),
            pltpu.VMEM((N_KV_HEADS, GQA * tq, LANES), F32),
            pltpu.VMEM((N_KV_HEADS, GQA * tq, LANES), F32),
            pltpu.VMEM((N_KV_HEADS, GQA * tq, HEAD_DIM), F32),
        ],
        compiler_params=_params("arbitrary", "arbitrary"),
        name="dsa_prompt",
    )(z, z, z, z, zs, zs, btiles)


def _dsa_sample_score_kernel(pt_ref, qi_ref, wi_ref, kinew_ref, kidx_ref, o_ref, *, n_pages):
    j = pl.program_id(1)
    is_new = j == n_pages
    ki = jnp.where(is_new, jnp.broadcast_to(kinew_ref[0], kidx_ref.shape[1:]), kidx_ref[0])
    r = jnp.maximum(_dot_nt(qi_ref[0].astype(BF16), ki.astype(BF16)), 0.0)
    sc = jnp.sum(r * (wi_ref[0] * (IDX_HEADS ** -0.5 * IDX_DIM ** -0.5)), axis=0, keepdims=True)
    lane = lax.broadcasted_iota(I32, sc.shape, 1)
    o_ref[0] = jnp.where(jnp.logical_and(is_new, lane >= 1), -jnp.inf, sc)


def _dsa_sample_scores(page_table, qi3, wi3, ki_new, cache_kidx):
    db, n_pages = page_table.shape
    page = cache_kidx.shape[1]
    grid_spec = pltpu.PrefetchScalarGridSpec(
        num_scalar_prefetch=1,
        grid=(db, n_pages + 1),
        in_specs=[
            pl.BlockSpec((1, IDX_HEADS, IDX_DIM), lambda b, j, pt: (b, 0, 0)),
            pl.BlockSpec((1, IDX_HEADS, 1), lambda b, j, pt: (b, 0, 0)),
            pl.BlockSpec((1, 1, IDX_DIM), lambda b, j, pt: (b, 0, 0)),
            pl.BlockSpec((1, page, IDX_DIM), lambda b, j, pt: (pt[b, jnp.minimum(j, n_pages - 1)], 0, 0)),
        ],
        out_specs=pl.BlockSpec((1, 1, page), lambda b, j, pt: (b, 0, j)),
    )
    return pl.pallas_call(
        functools.partial(_dsa_sample_score_kernel, n_pages=n_pages),
        grid_spec=grid_spec,
        out_shape=jax.ShapeDtypeStruct((db, 1, (n_pages + 1) * page), F32),
        compiler_params=_params("arbitrary", "arbitrary"),
        name="dsa_sample_scores",
    )(page_table, qi3, wi3, ki_new, cache_kidx)


def _dsa_sample_thr_kernel(s_ref, o_ref, *, topk):
    key = _sortable(s_ref[...])

    def count_ge(cs):
        return jnp.sum(jnp.where(key >= cs, 1.0, 0.0), axis=1, keepdims=True)

    thr = _kth_largest_key(count_ge, (key.shape[0], 1), float(topk))
    o_ref[...] = jnp.broadcast_to(thr, o_ref.shape)


def _dsa_sample_thr(scores, *, topk):
    db = scores.shape[0]
    return pl.pallas_call(
        functools.partial(_dsa_sample_thr_kernel, topk=topk),
        out_shape=jax.ShapeDtypeStruct((db, LANES), I32),
        name="dsa_sample_threshold",
    )(scores)


def _dsa_sample_attn_kernel(pt_ref, q_ref, knew_ref, vnew_ref, s_ref, thr_ref, rbt_ref, k_ref, v_ref, o_ref,
                            m_s, l_s, acc_s, *, n_pages, past):
    j = pl.program_id(1)
    page = k_ref.shape[1]

    @pl.when(j == 0)
    def _():
        m_s[...] = jnp.full(m_s.shape, NEG, F32)
        l_s[...] = jnp.zeros(l_s.shape, F32)
        acc_s[...] = jnp.zeros(acc_s.shape, F32)

    is_new = j == n_pages
    kp = jnp.where(is_new, jnp.broadcast_to(knew_ref[0], k_ref.shape[1:]), k_ref[0]).astype(BF16)
    vp = jnp.where(is_new, jnp.broadcast_to(vnew_ref[0], v_ref.shape[1:]), v_ref[0]).astype(BF16)
    mask = _sortable(s_ref[0]) >= thr_ref[0]
    pos = j * page + lax.broadcasted_iota(I32, (1, page), 1)
    bucket = _t5_bucket(past - pos)
    bias = jnp.zeros((N_HEADS, page), F32)
    for b in range(NUM_BUCKETS):
        bias = jnp.where(bucket == b, rbt_ref[:, b:b + 1], bias)
    q = q_ref[0].astype(BF16)
    kv_of_row = jnp.right_shift(lax.broadcasted_iota(I32, (N_HEADS, page), 0), GQA.bit_length() - 1)
    lg = jnp.zeros((N_HEADS, page), F32)
    for kh in range(N_KV_HEADS):
        lg = jnp.where(kv_of_row == kh, _dot_nt(q, kp[:, kh * HEAD_DIM:(kh + 1) * HEAD_DIM]), lg)
    lg = jnp.where(mask, lg * HEAD_DIM ** -0.5 + bias, NEG)
    m_old = m_s[...]
    m_new = jnp.maximum(m_old, jnp.max(lg, axis=1, keepdims=True))
    alpha = jnp.exp(m_old - m_new)
    p = jnp.where(mask, jnp.exp(lg - m_new), 0.0)
    l_s[...] = alpha * l_s[...] + jnp.sum(p, axis=1, keepdims=True)
    pv = _dot(p.astype(BF16), vp)
    upd = jnp.zeros((N_HEADS, HEAD_DIM), F32)
    for kh in range(N_KV_HEADS):
        upd = jnp.where(kv_of_row == kh, pv[:, kh * HEAD_DIM:(kh + 1) * HEAD_DIM], upd)
    acc_s[...] = alpha * acc_s[...] + upd
    m_s[...] = m_new

    @pl.when(is_new)
    def _():
        o_ref[0] = acc_s[...] / l_s[...]


def _dsa_sample_attn(page_table, q3, k_new, v_new, scores3, thr3, rb_t, cache_k, cache_v):
    db, n_pages = page_table.shape
    page = cache_k.shape[1]
    kvw = cache_k.shape[2]
    past = n_pages * page
    pg = lambda b, j, pt: (pt[b, jnp.minimum(j, n_pages - 1)], 0, 0)
    row = lambda b, j, pt: (b, 0, 0)
    grid_spec = pltpu.PrefetchScalarGridSpec(
        num_scalar_prefetch=1,
        grid=(db, n_pages + 1),
        in_specs=[
            pl.BlockSpec((1, N_HEADS, HEAD_DIM), row),
            pl.BlockSpec((1, 1, kvw), row),
            pl.BlockSpec((1, 1, kvw), row),
            pl.BlockSpec((1, 1, page), lambda b, j, pt: (b, 0, j)),
            pl.BlockSpec((1, 1, LANES), row),
            pl.BlockSpec((N_HEADS, NUM_BUCKETS), lambda b, j, pt: (0, 0)),
            pl.BlockSpec((1, page, kvw), pg),
            pl.BlockSpec((1, page, kvw), pg),
        ],
        out_specs=pl.BlockSpec((1, N_HEADS, HEAD_DIM), row),
        scratch_shapes=[pltpu.VMEM((N_HEADS, LANES), F32), pltpu.VMEM((N_HEADS, LANES), F32),
                        pltpu.VMEM((N_HEADS, HEAD_DIM), F32)],
    )
    return pl.pallas_call(
        functools.partial(_dsa_sample_attn_kernel, n_pages=n_pages, past=past),
        grid_spec=grid_spec,
        out_shape=jax.ShapeDtypeStruct((db, N_HEADS, HEAD_DIM), F32),
        compiler_params=_params("arbitrary", "arbitrary"),
        name="dsa_sample_attn",
    )(page_table, q3, k_new, v_new, scores3, thr3, rb_t, cache_k, cache_v)


def _mix_kernel(att_ref, zu_ref, zv_ref, ga_ref, gb_ref, gs_ref, ws_ref, bs_ref, h_ref, w_ref,
                o_ref, vn_ref, m_ref, *, first_rows_only):
    aw = att_ref.shape[1]

    @pl.when(pl.program_id(1) == 0)
    def _():
        tm = zu_ref.shape[0]
        v = jax.nn.gelu(zv_ref[...])
        vn = v * lax.rsqrt(jnp.mean(v * v, axis=-1, keepdims=True) + EPS) * gs_ref[...]
        vn_ref[...] = vn
        m_ref[:, :aw] = (jax.nn.sigmoid(ga_ref[...]) * att_ref[...]).astype(BF16)
        if first_rows_only:
            sgu = jax.nn.gelu(zu_ref[...]) * (vn * ws_ref[...] + bs_ref[...])
            m_ref[:, aw:] = (jax.nn.sigmoid(gb_ref[...]) * sgu).astype(BF16)
        else:
            row = lax.broadcasted_iota(I32, (CHUNK, CHUNK), 0)
            col = lax.broadcasted_iota(I32, (CHUNK, CHUNK), 1)
            for g in range(GMLP_GROUPS):
                cs = slice(g * LANES, (g + 1) * LANES)
                wt = jnp.where(col <= row, ws_ref[g], 0.0).astype(BF16)
                bcol = bs_ref[:, g:g + 1]
                for c in range(tm // CHUNK):
                    rs = slice(c * CHUNK, (c + 1) * CHUNK)
                    mixed = _dot(wt, vn_ref[rs, cs].astype(BF16)) + bcol
                    sgu = jax.nn.gelu(zu_ref[rs, cs]) * mixed
                    m_ref[rs, aw + g * LANES:aw + (g + 1) * LANES] = (
                        jax.nn.sigmoid(gb_ref[rs, cs]) * sgu).astype(BF16)

    o_ref[...] = h_ref[...] + _dot(m_ref[...], w_ref[...])


def _mix(att, z, h, g_sgu, ws, bs, w_out, *, first_rows_only, tm, tn, name):
    m, d = h.shape
    aw = att.shape[1]
    gw = g_sgu.shape[0]
    tm, tn = min(tm, m), min(tn, d)
    zb = lambda c: pl.BlockSpec((tm, gw), lambda i, j: (i, c))
    full = lambda a: pl.BlockSpec(a.shape, lambda i, j: (0,) * a.ndim)
    first = z.shape[1] // gw - 4
    return pl.pallas_call(
        functools.partial(_mix_kernel, first_rows_only=first_rows_only),
        grid=(m // tm, d // tn),
        in_specs=[pl.BlockSpec((tm, aw), lambda i, j: (i, 0)),
                  zb(first), zb(first + 1), zb(first + 2), zb(first + 3),
                  pl.BlockSpec((1, gw), lambda i, j: (0, 0)),
                  full(ws), full(bs),
                  pl.BlockSpec((tm, tn), lambda i, j: (i, j)),
                  pl.BlockSpec((aw + gw, tn), lambda i, j: (0, j))],
        out_specs=[pl.BlockSpec((tm, tn), lambda i, j: (i, j)),
                   pl.BlockSpec((tm, gw), lambda i, j: (i, 0))],
        out_shape=[jax.ShapeDtypeStruct((m, d), F32), jax.ShapeDtypeStruct((m, gw), F32)],
        scratch_shapes=[pltpu.VMEM((tm, aw + gw), BF16)],
        compiler_params=_params("arbitrary", "arbitrary"),
        name=name,
    )(att, z, z, z, z, g_sgu.reshape(1, gw), ws, bs, h, w_out)


def _cross_kernel(q_ref, mk_ref, mv_ref, h_ref, wo_ref, o_ref):
    rows = q_ref.shape[1]
    q = q_ref[0]
    if rows < SUBLANES:
        q = jnp.broadcast_to(q, (SUBLANES, q.shape[1]))
    qb = q.astype(BF16)
    mk = mk_ref[0].astype(BF16)
    mv = mv_ref[0].astype(BF16)
    hd = mk.shape[1] // X_HEADS
    outs = []
    for hh in range(X_HEADS):
        sl = slice(hh * hd, (hh + 1) * hd)
        lg = _dot_nt(qb[:, sl], mk[:, sl]) * hd ** -0.5
        e = jnp.exp(lg - jnp.max(lg, axis=1, keepdims=True))
        p = e / jnp.sum(e, axis=1, keepdims=True)
        outs.append(_dot(p.astype(BF16), mv[:, sl]))
    y = _dot(jnp.concatenate(outs, axis=1).astype(BF16), wo_ref[...])
    o_ref[0] = h_ref[0] + y[:rows]


def _cross(q, mk, mv, h, w_xo, *, tq, name):
    b, t, xw = q.shape
    d = h.shape[2]
    mlen = mk.shape[1]
    return pl.pallas_call(
        _cross_kernel,
        grid=(b, t // tq),
        in_specs=[pl.BlockSpec((1, tq, xw), lambda bb, i: (bb, i, 0)),
                  pl.BlockSpec((1, mlen, xw), lambda bb, i: (bb, 0, 0)),
                  pl.BlockSpec((1, mlen, xw), lambda bb, i: (bb, 0, 0)),
                  pl.BlockSpec((1, tq, d), lambda bb, i: (bb, i, 0)),
                  pl.BlockSpec((xw, d), lambda bb, i: (0, 0))],
        out_specs=pl.BlockSpec((1, tq, d), lambda bb, i: (bb, i, 0)),
        out_shape=jax.ShapeDtypeStruct((b, t, d), F32),
        compiler_params=_params("arbitrary", "arbitrary"),
        name=name,
    )(q, mk, mv, h, w_xo)


def _top_rows(s, k, payload=None):
    r = s.shape[0]
    rid = lax.broadcasted_iota(I32, s.shape, 0).astype(F32)
    vals, picks = [], []
    for _ in range(k):
        m = jnp.max(s, axis=0, keepdims=True)
        am = jnp.min(jnp.where(s == m, rid, float(r)), axis=0, keepdims=True)
        hit = rid == am
        vals.append(m)
        picks.append(am if payload is None else jnp.sum(jnp.where(hit, payload, 0.0), axis=0, keepdims=True))
        s = jnp.where(hit, -jnp.inf, s)
    return jnp.concatenate(vals, axis=0), jnp.concatenate(picks, axis=0)


def _peer_route_kernel(h_ref, g_ref, w_ref, sk_ref, ids_ref, gate_ref, xn_ref):
    @pl.when(pl.program_id(1) == 0)
    def _():
        x = h_ref[...]
        xn_ref[...] = (x * lax.rsqrt(jnp.mean(x * x, axis=-1, keepdims=True) + EPS) * g_ref[...]).astype(BF16)

    half = sk_ref.shape[2]
    qt = _dot_nt(w_ref[...], xn_ref[...]).astype(BF16)
    v0, i0 = _top_rows(_dot(sk_ref[0], qt[:half]), PEER_TOPK)
    v1, i1 = _top_rows(_dot(sk_ref[1], qt[half:]), PEER_TOPK)
    cand = jnp.concatenate([v0[a:a + 1] + v1 for a in range(PEER_TOPK)], axis=0)
    eid = jnp.concatenate([i0[a:a + 1] * float(N_KEYS) + i1 for a in range(PEER_TOPK)], axis=0)
    sc, e = _top_rows(cand, PEER_TOPK, payload=eid)
    ex = jnp.exp(sc - sc[0:1])
    gate_ref[...] = ex / jnp.sum(ex, axis=0, keepdims=True)
    ids_ref[...] = e.astype(I32)


def _peer_route(h, g_ffn, w_pq_t, sub_keys, *, tb):
    m, d = h.shape
    qd = w_pq_t.shape[0] // PEER_HEADS
    return pl.pallas_call(
        _peer_route_kernel,
        grid=(m // tb, PEER_HEADS),
        in_specs=[pl.BlockSpec((tb, d), lambda t, hh: (t, 0)),
                  pl.BlockSpec((1, d), lambda t, hh: (0, 0)),
                  pl.BlockSpec((qd, d), lambda t, hh: (hh, 0)),
                  pl.BlockSpec(sub_keys.shape, lambda t, hh: (0, 0, 0))],
        out_specs=[pl.BlockSpec((PEER_TOPK, tb), lambda t, hh: (hh, t)),
                   pl.BlockSpec((PEER_TOPK, tb), lambda t, hh: (hh, t))],
        out_shape=[jax.ShapeDtypeStruct((PEER_HEADS * PEER_TOPK, m), I32),
                   jax.ShapeDtypeStruct((PEER_HEADS * PEER_TOPK, m), F32)],
        scratch_shapes=[pltpu.VMEM((tb, d), BF16)],
        compiler_params=_params("arbitrary", "arbitrary"),
        name="peer_route",
    )(h, g_ffn.reshape(1, d), w_pq_t, sub_keys)


def _pack_tables_kernel(u_ref, v_ref, o_ref):
    nr = o_ref.shape[1] // 2

    def words(x):
        bits = pltpu.bitcast(x.astype(BF16).astype(F32), I32)
        half = bits.shape[1] // 2
        return (bits[:, half:] & jnp.int32(-65536)) | lax.shift_right_logical(bits[:, :half], 16)

    wu = words(u_ref[...])
    wv = words(v_ref[...])
    for r in range(nr):
        o_ref[:, r, :] = wu[:, r * LANES:(r + 1) * LANES]
        o_ref[:, nr + r, :] = wv[:, r * LANES:(r + 1) * LANES]


def _pack_tables(u_tab, v_tab, *, eb):
    n, d = u_tab.shape
    nr = d // (2 * LANES)
    return pl.pallas_call(
        _pack_tables_kernel,
        grid=(n // eb,),
        in_specs=[pl.BlockSpec((eb, d), lambda i: (i, 0)), pl.BlockSpec((eb, d), lambda i: (i, 0))],
        out_specs=pl.BlockSpec((eb, 2 * nr, LANES), lambda i: (i, 0, 0)),
        out_shape=jax.ShapeDtypeStruct((n, 2 * nr, LANES), I32),
        compiler_params=_params("arbitrary"),
        name="peer_pack_tables",
    )(u_tab, v_tab)


def _unpack_words(w):
    return pltpu.bitcast(jnp.left_shift(w, 16), F32), pltpu.bitcast(w & jnp.int32(-65536), F32)


def _peer_gather_kernel(ids_hbm, gate_ref, h_ref, gffn_ref, gfin_ref, tab_hbm, o_ref,
                        ids_s, buf, part_s, sem_ids, sem_rows, *, n_steps):
    s = pl.program_id(0)
    g_tok = h_ref.shape[0]
    nr = buf.shape[1] // 2
    n_e = buf.shape[2]
    cur = lax.rem(s, 2)
    nxt = 1 - cur

    def ids_copy(step, slot):
        return pltpu.make_async_copy(ids_hbm.at[step], ids_s.at[slot], sem_ids.at[slot])

    def row_copy(e, k, slot):
        return pltpu.make_async_copy(tab_hbm.at[e], buf.at[slot, :, k, :], sem_rows.at[slot])

    def issue(ids_slot, t, slot):
        for k in range(n_e):
            row_copy(ids_s[ids_slot, t, k], k, slot).start()

    def wait(slot):
        for k in range(n_e):
            row_copy(0, k, slot).wait()

    @pl.when(s == 0)
    def _():
        ids_copy(0, 0).start()
        ids_copy(0, 0).wait()
        issue(0, 0, 0)

    @pl.when(s + 1 < n_steps)
    def _():
        ids_copy(s + 1, nxt).start()

    h = h_ref[...]
    xn = h * lax.rsqrt(jnp.mean(h * h, axis=-1, keepdims=True) + EPS) * gffn_ref[...]
    eye = lax.broadcasted_iota(I32, (n_e, n_e), 0) == lax.broadcasted_iota(I32, (n_e, n_e), 1)

    for t in range(g_tok):
        slot = t % 2
        if t + 1 < g_tok:
            issue(cur, t + 1, 1 - slot)
        else:
            @pl.when(s + 1 < n_steps)
            def _():
                ids_copy(s + 1, nxt).wait()
                issue(nxt, 0, 1 - slot)
        wait(slot)
        x = xn[t:t + 1, :]
        hacc = jnp.zeros((n_e, LANES), F32)
        for r in range(nr):
            lo, hi = _unpack_words(buf[slot, r])
            hacc = hacc + lo * x[:, r * LANES:(r + 1) * LANES] + hi * x[:, (nr + r) * LANES:(nr + r + 1) * LANES]
        act = jax.nn.gelu(jnp.sum(hacc, axis=-1, keepdims=True))
        gcol = jnp.sum(jnp.where(eye, jnp.broadcast_to(gate_ref[t:t + 1, :], (n_e, n_e)), 0.0),
                       axis=-1, keepdims=True)
        w = jnp.broadcast_to(gcol * act, (n_e, LANES))
        for r in range(nr):
            lo, hi = _unpack_words(buf[slot, nr + r])
            part_s[t, :, r * LANES:(r + 1) * LANES] = jnp.sum(
                (lo * w).reshape(n_e // SUBLANES, SUBLANES, LANES), axis=0)
            part_s[t, :, (nr + r) * LANES:(nr + r + 1) * LANES] = jnp.sum(
                (hi * w).reshape(n_e // SUBLANES, SUBLANES, LANES), axis=0)

    y = h + jnp.sum(part_s[...], axis=1)
    o_ref[...] = y * lax.rsqrt(jnp.mean(y * y, axis=-1, keepdims=True) + EPS) * gfin_ref[...]


def _peer_gather(ids, gates, h, g_ffn, g_final, packed_tab, *, g_tok):
    m, d = h.shape
    n_e = ids.shape[1]
    n_steps = m // g_tok
    n_rows = packed_tab.shape[1]
    ids3 = ids.reshape(n_steps, g_tok, n_e)
    return pl.pallas_call(
        functools.partial(_peer_gather_kernel, n_steps=n_steps),
        grid=(n_steps,),
        in_specs=[pl.BlockSpec(memory_space=pl.ANY),
                  pl.BlockSpec((g_tok, n_e), lambda s: (s, 0)),
                  pl.BlockSpec((g_tok, d), lambda s: (s, 0)),
                  pl.BlockSpec((1, d), lambda s: (0, 0)),
                  pl.BlockSpec((1, d), lambda s: (0, 0)),
                  pl.BlockSpec(memory_space=pl.ANY)],
        out_specs=pl.BlockSpec((g_tok, d), lambda s: (s, 0)),
        out_shape=jax.ShapeDtypeStruct((m, d), F32),
        scratch_shapes=[pltpu.SMEM((2, g_tok, n_e), I32),
                        pltpu.VMEM((2, n_rows, n_e, LANES), I32),
                        pltpu.VMEM((g_tok, SUBLANES, d), F32),
                        pltpu.SemaphoreType.DMA((2,)), pltpu.SemaphoreType.DMA((2,))],
        compiler_params=_params("arbitrary"),
        name="peer_gather",
    )(ids3, gates, h, g_ffn.reshape(1, d), g_final.reshape(1, d), packed_tab)


def kernel(x_prompt, x_sample, cache_k, cache_v, cache_kidx, cache_mem_k, cache_mem_v, page_table,
           mem_prompt, rel_bias, g_in, w_in, g_sgu, w_s, b_s, w_out, g_x, w_xq, w_xk, w_xv, w_xo,
           g_ffn, w_pq, sub_keys, peer_u, peer_v, g_final):
    batch, seq, d = x_prompt.shape
    db, dseq, _ = x_sample.shape
    depth = w_in.shape[0]
    assert dseq == 1, "sample group is one new token per sequence"
    n_pages = page_table.shape[1]
    page = cache_k.shape[2]
    past = n_pages * page
    aw, kvw, qiw, gw = N_HEADS * HEAD_DIM, N_KV_HEADS * HEAD_DIM, IDX_HEADS * IDX_DIM, GMLP_GROUPS * LANES
    xw = w_xq.shape[2]
    mlen = mem_prompt.shape[1]
    small_lo = aw + 2 * kvw + qiw
    small_hi = small_lo + IDX_DIM + IDX_HEADS
    assert NUM_BUCKETS // 2 + int(math.log((LANES + 1) / (NUM_BUCKETS // 2)) / math.log(
        MAX_DISTANCE / (NUM_BUCKETS // 2)) * (NUM_BUCKETS // 2)) >= NUM_BUCKETS - 1

    hp = x_prompt.reshape(batch * seq, d)
    hs = x_sample.reshape(db, d)
    btiles = _bias_tiles(rel_bias)
    rb_t = rel_bias.T
    outs = {n: [] for n in ("kp", "vp", "ip", "mkp", "mvp", "ks", "vs", "is", "sv")}

    for l in range(depth):
        w_main = jnp.concatenate([w_in[l][:, :small_lo], w_in[l][:, small_hi:]], axis=1).astype(BF16)
        w_small = jnp.pad(w_in[l][:, small_lo:small_hi], ((0, 0), (0, LANES - (small_hi - small_lo)))).astype(BF16)
        w_out_b = w_out[l].astype(BF16)
        w_xq_b, w_xk_b, w_xv_b, w_xo_b = (w.astype(BF16) for w in (w_xq[l], w_xk[l], w_xv[l], w_xo[l]))
        w_pq_t = w_pq[l].T.astype(BF16)
        sk_b = sub_keys[l].astype(BF16)
        ws_first = jnp.repeat(w_s[l][:, 0, 0], LANES).reshape(1, gw)
        bs_first = jnp.repeat(b_s[l][:, 0], LANES).reshape(1, gw)
        ones_d = jnp.ones((d,), F32)

        z = _proj(hp, g_in[l], w_main, norm=True, tm=512, tn=512, name="in_proj_prompt")
        zs = _proj(hp, g_in[l], w_small, norm=True, tm=512, tn=LANES, name="in_proj_idx_prompt")
        att = _dsa_prompt(z.reshape(batch, seq, -1), zs.reshape(batch, seq, LANES), btiles,
                          batch=batch, seq=seq, topk=min(TOPK_MAX, seq // 4))
        hp1, _ = _mix(att.reshape(batch * seq, aw), z, hp, g_sgu[l], w_s[l], b_s[l].T, w_out_b,
                      first_rows_only=False, tm=256, tn=512, name="mix_prompt")
        mem = mem_prompt.reshape(batch * mlen, d)
        mk = _proj(mem, ones_d, w_xk_b, norm=False, tm=256, tn=xw, name="mem_k_proj")
        mv = _proj(mem, ones_d, w_xv_b, norm=False, tm=256, tn=xw, name="mem_v_proj")
        qx = _proj(hp1, g_x[l], w_xq_b, norm=True, tm=512, tn=xw, name="xq_proj_prompt")
        hp2 = _cross(qx.reshape(batch, seq, xw), mk.reshape(batch, mlen, xw), mv.reshape(batch, mlen, xw),
                     hp1.reshape(batch, seq, d), w_xo_b, tq=256, name="cross_prompt").reshape(batch * seq, d)
        outs["kp"].append(z[:, aw:aw + kvw].reshape(batch, seq, N_KV_HEADS, HEAD_DIM))
        outs["vp"].append(z[:, aw + kvw:aw + 2 * kvw].reshape(batch, seq, N_KV_HEADS, HEAD_DIM))
        outs["ip"].append(zs[:, :IDX_DIM].reshape(batch, seq, IDX_DIM))
        outs["mkp"].append(mk.reshape(batch, mlen, X_HEADS, xw // X_HEADS))
        outs["mvp"].append(mv.reshape(batch, mlen, X_HEADS, xw // X_HEADS))

        z_s = _proj(hs, g_in[l], w_main, norm=True, tm=db, tn=512, name="in_proj_sample")
        zs_s = _proj(hs, g_in[l], w_small, norm=True, tm=db, tn=LANES, name="in_proj_idx_sample")
        qi3 = z_s[:, aw + 2 * kvw:aw + 2 * kvw + qiw].reshape(db, IDX_HEADS, IDX_DIM)
        wi3 = zs_s[:, IDX_DIM:IDX_DIM + IDX_HEADS].reshape(db, IDX_HEADS, 1)
        ki_new = zs_s[:, :IDX_DIM].reshape(db, 1, IDX_DIM)
        k_new = z_s[:, aw:aw + kvw].reshape(db, 1, kvw)
        v_new = z_s[:, aw + kvw:aw + 2 * kvw].reshape(db, 1, kvw)
        scores3 = _dsa_sample_scores(page_table, qi3, wi3, ki_new, cache_kidx[l])
        thr = _dsa_sample_thr(scores3.reshape(db, -1), topk=min(TOPK_MAX, (past + dseq) // 4))
        att_s = _dsa_sample_attn(page_table, z_s[:, :aw].reshape(db, N_HEADS, HEAD_DIM), k_new, v_new,
                                 scores3, thr.reshape(db, 1, LANES), rb_t,
                                 cache_k[l].reshape(-1, page, kvw), cache_v[l].reshape(-1, page, kvw))
        hs1, vn_s = _mix(att_s.reshape(db, aw), z_s, hs, g_sgu[l], ws_first, bs_first, w_out_b,
                         first_rows_only=True, tm=db, tn=512, name="mix_sample")
        qx_s = _proj(hs1, g_x[l], w_xq_b, norm=True, tm=db, tn=xw, name="xq_proj_sample")
        hs2 = _cross(qx_s.reshape(db, 1, xw), cache_mem_k[l].reshape(db, mlen, xw),
                     cache_mem_v[l].reshape(db, mlen, xw), hs1.reshape(db, 1, d), w_xo_b,
                     tq=1, name="cross_sample").reshape(db, d)
        outs["ks"].append(k_new.reshape(db, dseq, N_KV_HEADS, HEAD_DIM))
        outs["vs"].append(v_new.reshape(db, dseq, N_KV_HEADS, HEAD_DIM))
        outs["is"].append(ki_new.reshape(db, dseq, IDX_DIM))
        outs["sv"].append(vn_s.reshape(db, dseq, gw))

        assert depth == 1
        tb = 256
        n_tok = batch * seq + db
        n_pad = -(-n_tok // tb) * tb
        h_all = jnp.concatenate([hp2, hs2, jnp.zeros((n_pad - n_tok, d), F32)], axis=0)
        ids_t, gates_t = _peer_route(h_all, g_ffn[l], w_pq_t, sk_b, tb=tb)
        packed = _pack_tables(peer_u[l], peer_v[l], eb=256)
        y_all = _peer_gather(ids_t.T, gates_t.T, h_all, g_ffn[l], g_final, packed, g_tok=8)

    y_prompt = y_all[:batch * seq].reshape(batch, seq, d)
    y_sample = y_all[batch * seq:n_tok].reshape(db, dseq, d)
    st = lambda n: jnp.stack(outs[n])
    return (y_prompt, y_sample, st("kp"), st("vp"), st("ip"), st("mkp"), st("mvp"),
            st("ks"), st("vs"), st("is"), st("sv"))
```

```python
import functools
import math

import jax
import jax.numpy as jnp
from jax import lax
from jax.experimental import pallas as pl
from jax.experimental.pallas import tpu as pltpu

F32 = jnp.float32
BF16 = jnp.bfloat16
I32 = jnp.int32

N_HEADS = 16
HEAD_DIM = 128
N_KV_HEADS = 4
GQA = N_HEADS // N_KV_HEADS
IDX_HEADS = 16
IDX_DIM = 64
TOPK_MAX = 256
NUM_BUCKETS = 32
MAX_DISTANCE = 128
CHUNK = 128
GMLP_GROUPS = 16
X_HEADS = 4
N_KEYS = 128
PEER_HEADS = 8
PEER_TOPK = 16
EPS = 1e-6

LANES = 128
SUBLANES = 8
VMEM_LIMIT_BYTES = 56 * 1024 * 1024

NEG = -1e30
INT_MIN = -(2 ** 31)


def _dot_nt(a, b):
    return lax.dot_general(a, b, (((1,), (1,)), ((), ())), preferred_element_type=F32)


def _dot(a, b):
    return jnp.dot(a, b, preferred_element_type=F32)


def _params(*sem):
    return pltpu.CompilerParams(dimension_semantics=sem, vmem_limit_bytes=VMEM_LIMIT_BYTES)


def _sortable(x):
    bits = pltpu.bitcast(x, I32)
    return bits ^ (jnp.right_shift(bits, 31) & 0x7FFFFFFF)


def _t5_bucket(dist):
    n = jnp.maximum(dist, 0)
    max_exact = NUM_BUCKETS // 2
    nf = jnp.maximum(n, 1).astype(F32)
    large = max_exact + (jnp.log(nf / max_exact) / math.log(MAX_DISTANCE / max_exact)
                         * (NUM_BUCKETS - max_exact)).astype(I32)
    large = jnp.minimum(large, NUM_BUCKETS - 1)
    return jnp.where(n < max_exact, n, large)


def _kth_largest_key(count_ge, shape, k):
    def body(it, t):
        cand = t | jnp.left_shift(jnp.int32(1), 31 - it)
        cnt = count_ge(cand ^ INT_MIN)
        return jnp.where(cnt >= k, cand, t)

    t = lax.fori_loop(0, 32, body, jnp.zeros(shape, I32))
    return t ^ INT_MIN


def _proj_kernel(x_ref, g_ref, w_ref, o_ref, xn_ref, *, norm):
    @pl.when(pl.program_id(1) == 0)
    def _():
        x = x_ref[...]
        if norm:
            x = x * lax.rsqrt(jnp.mean(x * x, axis=-1, keepdims=True) + EPS) * g_ref[...]
        xn_ref[...] = x.astype(BF16)

    o_ref[...] = _dot(xn_ref[...], w_ref[...])


def _proj(x, g, w, *, norm, tm, tn, name):
    m, k = x.shape
    n = w.shape[1]
    tm, tn = min(tm, m), min(tn, n)
    return pl.pallas_call(
        functools.partial(_proj_kernel, norm=norm),
        grid=(m // tm, n // tn),
        in_specs=[pl.BlockSpec((tm, k), lambda i, j: (i, 0)),
                  pl.BlockSpec((1, k), lambda i, j: (0, 0)),
                  pl.BlockSpec((k, tn), lambda i, j: (0, j))],
        out_specs=pl.BlockSpec((tm, tn), lambda i, j: (i, j)),
        out_shape=jax.ShapeDtypeStruct((m, n), F32),
        scratch_shapes=[pltpu.VMEM((tm, k), BF16)],
        compiler_params=_params("arbitrary", "arbitrary"),
        name=name,
    )(x, g.reshape(1, k), w)


def _bias_tiles_kernel(rb_ref, o_ref):
    tq = lax.broadcasted_iota(I32, (LANES, LANES), 0)
    c = lax.broadcasted_iota(I32, (LANES, LANES), 1)
    for part, off in enumerate((2 * LANES, LANES, 0)):
        dist = tq - c + off
        bucket = _t5_bucket(dist)
        for h in range(N_HEADS):
            tile = lax.fori_loop(
                0, NUM_BUCKETS, lambda b, acc: jnp.where(bucket == b, rb_ref[b, h], acc),
                jnp.zeros((LANES, LANES), F32))
            if off == 0:
                tile = jnp.where(dist < 0, NEG, tile)
            o_ref[part, h] = tile


def _bias_tiles(rel_bias):
    return pl.pallas_call(
        _bias_tiles_kernel,
        in_specs=[pl.BlockSpec(memory_space=pltpu.SMEM)],
        out_specs=pl.BlockSpec(memory_space=pltpu.VMEM),
        out_shape=jax.ShapeDtypeStruct((3, N_HEADS, LANES, LANES), F32),
        name="t5_bias_tiles",
    )(rel_bias)


def _dsa_prompt_kernel(q_ref, k_ref, v_ref, qi_ref, zsk_ref, zsq_ref, bt_ref, o_ref,
                       kbf, vbf, kilo, kihi, key3, qs, m_s, l_s, acc_s, *, topk):
    i = pl.program_id(1)
    tq = q_ref.shape[1]
    s = k_ref.shape[1]
    nkb = s // LANES
    cw = 2 * LANES

    @pl.when(i == 0)
    def _():
        kbf[...] = k_ref[0].astype(BF16)
        vbf[...] = v_ref[0].astype(BF16)
        zs = zsk_ref[0]
        lane = lax.broadcasted_iota(I32, zs.shape, 1)
        kilo[...] = jnp.where(lane < IDX_DIM, zs, 0.0).astype(BF16)
        kihi[...] = jnp.where(lane >= IDX_DIM, pltpu.roll(zs, IDX_DIM, axis=1), 0.0).astype(BF16)

    qi = qi_ref[0].astype(BF16)
    coef = zsq_ref[0][:, IDX_DIM:IDX_DIM + IDX_HEADS] * (IDX_HEADS ** -0.5 * IDX_DIM ** -0.5)
    t_pos = i * tq + lax.broadcasted_iota(I32, (tq, cw), 0)
    t_last = i * tq + tq - 1
    for c in range(s // cw):
        @pl.when(c * cw <= t_last)
        def _():
            klo = kilo[c * cw:(c + 1) * cw, :]
            khi = kihi[c * cw:(c + 1) * cw, :]
            acc = jnp.zeros((tq, cw), F32)
            for p in range(IDX_HEADS // 2):
                a = qi[:, p * LANES:(p + 1) * LANES]
                acc = acc + coef[:, 2 * p:2 * p + 1] * jnp.maximum(_dot_nt(a, klo), 0.0)
                acc = acc + coef[:, 2 * p + 1:2 * p + 2] * jnp.maximum(_dot_nt(a, khi), 0.0)
            s_pos = c * cw + lax.broadcasted_iota(I32, (tq, cw), 1)
            key = _sortable(jnp.where(s_pos <= t_pos, acc, -jnp.inf))
            for u in range(cw // LANES):
                key3[c * (cw // LANES) + u] = key[:, u * LANES:(u + 1) * LANES]

        @pl.when(c * cw > t_last)
        def _():
            for u in range(cw // LANES):
                key3[c * (cw // LANES) + u] = jnp.full((tq, LANES), INT_MIN, I32)

    def count_ge(cs):
        ge = jnp.where(key3[...] >= cs[None], 1.0, 0.0)
        return jnp.sum(jnp.sum(ge, axis=0), axis=1, keepdims=True)

    thr = _kth_largest_key(count_ge, (tq, 1), float(topk))
    thr4 = jnp.concatenate([thr] * GQA, axis=0)

    qb = q_ref[0].astype(BF16)
    for kh in range(N_KV_HEADS):
        qs[kh] = jnp.concatenate(
            [qb[:, (GQA * kh + g) * HEAD_DIM:(GQA * kh + g + 1) * HEAD_DIM] for g in range(GQA)], axis=0)
    m_s[...] = jnp.full(m_s.shape, NEG, F32)
    l_s[...] = jnp.zeros(l_s.shape, F32)
    acc_s[...] = jnp.zeros(acc_s.shape, F32)
    scale = HEAD_DIM ** -0.5

    def fbody(j, carry):
        part = jnp.clip(j - i + 2, 0, 2)
        key4 = jnp.concatenate([key3[j]] * GQA, axis=0)
        mask = key4 >= thr4
        r0 = pl.multiple_of(j * LANES, LANES)
        for kh in range(N_KV_HEADS):
            kj = kbf[pl.ds(r0, LANES), kh * HEAD_DIM:(kh + 1) * HEAD_DIM]
            vj = vbf[pl.ds(r0, LANES), kh * HEAD_DIM:(kh + 1) * HEAD_DIM]
            bias = bt_ref[part, GQA * kh:GQA * (kh + 1)].reshape(GQA * tq, LANES)
            lg = jnp.where(mask, _dot_nt(qs[kh], kj) * scale + bias, NEG)
            m_old = m_s[kh]
            m_new = jnp.maximum(m_old, jnp.max(lg, axis=1, keepdims=True))
            alpha = jnp.exp(m_old - m_new)
            p = jnp.where(mask, jnp.exp(lg - m_new), 0.0)
            l_s[kh] = alpha * l_s[kh] + jnp.sum(p, axis=1, keepdims=True)
            acc_s[kh] = alpha * acc_s[kh] + _dot(p.astype(BF16), vj)
            m_s[kh] = m_new
        return carry

    lax.fori_loop(0, i + 1, fbody, 0)
    for kh in range(N_KV_HEADS):
        out = acc_s[kh] / l_s[kh]
        for g in range(GQA):
            h = GQA * kh + g
            o_ref[0, :, h * HEAD_DIM:(h + 1) * HEAD_DIM] = out[g * tq:(g + 1) * tq]


def _dsa_prompt(z, zs, btiles, *, batch, seq, topk):
    tq = LANES
    aw = N_HEADS * HEAD_DIM
    kvw = N_KV_HEADS * HEAD_DIM
    qiw = IDX_HEADS * IDX_DIM
    return pl.pallas_call(
        functools.partial(_dsa_prompt_kernel, topk=topk),
        grid=(batch, seq // tq),
        in_specs=[
            pl.BlockSpec((1, tq, aw), lambda b, i: (b, i, 0)),
            pl.BlockSpec((1, seq, kvw), lambda b, i: (b, 0, aw // kvw)),
            pl.BlockSpec((1, seq, kvw), lambda b, i: (b, 0, aw // kvw + 1)),
            pl.BlockSpec((1, tq, qiw), lambda b, i: (b, i, (aw + 2 * kvw) // qiw)),
            pl.BlockSpec((1, seq, LANES), lambda b, i: (b, 0, 0)),
            pl.BlockSpec((1, tq, LANES), lambda b, i: (b, i, 0)),
            pl.BlockSpec((3, N_HEADS, LANES, LANES), lambda b, i: (0, 0, 0, 0)),
        ],
        out_specs=pl.BlockSpec((1, tq, aw), lambda b, i: (b, i, 0)),
        out_shape=jax.ShapeDtypeStruct((batch, seq, aw), F32),
        scratch_shapes=[
            pltpu.VMEM((seq, kvw), BF16), pltpu.VMEM((seq, kvw), BF16),
            pltpu.VMEM((seq, LANES), BF16), pltpu.VMEM((seq, LANES), BF16),
            pltpu.VMEM((seq // LANES, tq, LANES), I32),
            pltpu.VMEM((N_KV_HEADS, GQA * tq, HEAD_DIM), BF16),
            pltpu.VMEM((N_KV_HEADS, GQA * tq, LANES), F32),
            pltpu.VMEM((N_KV_HEADS, GQA * tq, LANES), F32),
            pltpu.VMEM((N_KV_HEADS, GQA * tq, HEAD_DIM), F32),
        ],
        compiler_params=_params("arbitrary", "arbitrary"),
        name="dsa_prompt",
    )(z, z, z, z, zs, zs, btiles)


SAMPLE_PAGES_PER_STEP = 8


def _dsa_sample_score_kernel(pt_ref, qi_ref, wi_ref, kinew_ref, *rest):
    kidx_refs, (o_ref, onew_ref) = rest[:-2], rest[-2:]
    qi = qi_ref[0].astype(BF16)
    coef = wi_ref[0] * (IDX_HEADS ** -0.5 * IDX_DIM ** -0.5)

    def score(ki):
        return jnp.sum(jnp.maximum(_dot_nt(qi, ki.astype(BF16)), 0.0) * coef, axis=0, keepdims=True)

    o_ref[0] = jnp.concatenate([score(r[0]) for r in kidx_refs], axis=1)

    @pl.when(pl.program_id(1) == pl.num_programs(1) - 1)
    def _():
        onew_ref[0] = score(jnp.broadcast_to(kinew_ref[0], (LANES, kinew_ref.shape[2])))


def _dsa_sample_scores(page_table, qi3, wi3, ki_new, cache_kidx):
    db, n_pages = page_table.shape
    page = cache_kidx.shape[1]
    pg = math.gcd(SAMPLE_PAGES_PER_STEP, n_pages)
    row = lambda b, j, pt: (b, 0, 0)
    grid_spec = pltpu.PrefetchScalarGridSpec(
        num_scalar_prefetch=1,
        grid=(db, n_pages // pg),
        in_specs=[pl.BlockSpec((1, IDX_HEADS, IDX_DIM), row),
                  pl.BlockSpec((1, IDX_HEADS, 1), row),
                  pl.BlockSpec((1, 1, IDX_DIM), row)]
        + [pl.BlockSpec((1, page, IDX_DIM), functools.partial(lambda b, j, pt, u: (pt[b, j * pg + u], 0, 0), u=u))
           for u in range(pg)],
        out_specs=[pl.BlockSpec((1, 1, pg * page), lambda b, j, pt: (b, 0, j)),
                   pl.BlockSpec((1, 1, LANES), row)],
    )
    return pl.pallas_call(
        _dsa_sample_score_kernel,
        grid_spec=grid_spec,
        out_shape=[jax.ShapeDtypeStruct((db, 1, n_pages * page), F32),
                   jax.ShapeDtypeStruct((db, 1, LANES), F32)],
        compiler_params=_params("arbitrary", "arbitrary"),
        name="dsa_sample_scores",
    )(page_table, qi3, wi3, ki_new, *([cache_kidx] * pg))


def _dsa_sample_select_kernel(s_ref, snew_ref, tri_ref, pos_ref, new_ref, rank_s, nsel_s, *, topk):
    db, past = s_ref.shape
    k = float(topk)
    key = _sortable(s_ref[...])
    key_new = _sortable(snew_ref[:, 0:1])

    def count_ge(cs):
        return (jnp.sum(jnp.where(key >= cs, 1.0, 0.0), axis=1, keepdims=True)
                + jnp.where(key_new >= cs, 1.0, 0.0))

    thr = _kth_largest_key(count_ge, (db, 1), k)
    gt = key > thr
    eq = key == thr
    need = k - (jnp.sum(jnp.where(gt, 1.0, 0.0), axis=1, keepdims=True) + jnp.where(key_new > thr, 1.0, 0.0))

    tri = tri_ref[...]

    def cumsum(flags):
        out, off = [], jnp.zeros((db, 1), F32)
        for blk in range(past // LANES):
            pc = _dot(flags[:, blk * LANES:(blk + 1) * LANES].astype(BF16), tri) + off
            out.append(pc)
            off = pc[:, LANES - 1:LANES]
        return jnp.concatenate(out, axis=1), off

    eq_rank, eq_total = cumsum(jnp.where(eq, 1.0, 0.0))
    sel = jnp.where(jnp.logical_or(gt, jnp.logical_and(eq, eq_rank <= need)), 1.0, 0.0)
    new_sel = jnp.logical_or(key_new > thr, jnp.logical_and(key_new == thr, eq_total < need))
    new_ref[...] = jnp.broadcast_to(jnp.where(new_sel, 1.0, 0.0), new_ref.shape)
    rank, n_sel = cumsum(sel)
    rank = jnp.where(sel > 0.0, rank, 0.0)
    for b in range(db):
        rank_s[b] = rank[b:b + 1, :]
        nsel_s[b] = jnp.broadcast_to(n_sel[b:b + 1, :], (1, LANES))
    ch = min(8 * LANES, past)
    slot = lax.broadcasted_iota(I32, (topk, ch), 0).astype(F32) + 1.0
    lane = lax.broadcasted_iota(I32, (topk, ch), 1).astype(F32)
    slot_col = lax.broadcasted_iota(I32, (topk, 1), 0).astype(F32)

    def extract(b, carry):
        r = rank_s[b]
        pos = jnp.zeros((topk, 1), F32)
        for c in range(past // ch):
            hit = r[:, c * ch:(c + 1) * ch] == slot
            pos = pos + jnp.sum(jnp.where(hit, lane + float(c * ch), 0.0), axis=1, keepdims=True)
        pos = jnp.where(slot_col < nsel_s[b][:, 0:1], pos, -1.0)
        pos_ref[b] = jnp.broadcast_to(pos, (topk, LANES)).astype(I32)
        return carry

    lax.fori_loop(0, db, extract, 0)


def _dsa_sample_select(scores, snew, *, topk):
    db = scores.shape[0]
    tri = (lax.broadcasted_iota(I32, (LANES, LANES), 0) <= lax.broadcasted_iota(I32, (LANES, LANES), 1)).astype(BF16)
    return pl.pallas_call(
        functools.partial(_dsa_sample_select_kernel, topk=topk),
        out_shape=[jax.ShapeDtypeStruct((db, topk, LANES), I32), jax.ShapeDtypeStruct((db, LANES), F32)],
        scratch_shapes=[pltpu.VMEM((db, 1, scores.shape[1]), F32), pltpu.VMEM((db, 1, LANES), F32)],
        compiler_params=pltpu.CompilerParams(vmem_limit_bytes=VMEM_LIMIT_BYTES),
        name="dsa_sample_select",
    )(scores, snew, tri)


def _dsa_sample_attn_kernel(pos_s, pt_s, q_ref, knew_ref, vnew_ref, posrow_ref, newsel_ref, rbt_ref,
                            ck_hbm, cv_hbm, o_ref, kbuf, vbuf, sem_k, sem_v, *, layer, past, page):
    b = pl.program_id(0)
    topk = kbuf.shape[1]
    slot = lax.rem(b, 2)
    page_shift = page.bit_length() - 1
    unroll = 8

    def row_copies(bb, j, sl):
        p = jnp.maximum(pos_s[bb * topk + j], 0)
        pg = pt_s[bb, jnp.right_shift(p, page_shift)]
        off = p & (page - 1)
        return (pltpu.make_async_copy(ck_hbm.at[layer, pg, off], kbuf.at[sl, j], sem_k.at[sl]),
                pltpu.make_async_copy(cv_hbm.at[layer, pg, off], vbuf.at[sl, j], sem_v.at[sl]))

    def issue(bb, sl):
        def body(i, carry):
            for u in range(unroll):
                ck, cv = row_copies(bb, i * unroll + u, sl)
                ck.start()
                cv.start()
            return carry
        lax.fori_loop(0, topk // unroll, body, 0)

    def wait(sl):
        def body(i, carry):
            for u in range(unroll):
                j = i * unroll + u
                pltpu.make_async_copy(ck_hbm.at[layer, 0, 0], kbuf.at[sl, j], sem_k.at[sl]).wait()
                pltpu.make_async_copy(cv_hbm.at[layer, 0, 0], vbuf.at[sl, j], sem_v.at[sl]).wait()
            return carry
        lax.fori_loop(0, topk // unroll, body, 0)

    @pl.when(b == 0)
    def _():
        issue(0, 0)

    @pl.when(b + 1 < pl.num_programs(0))
    def _():
        issue(b + 1, 1 - slot)

    wait(slot)

    q = q_ref[0].astype(BF16)
    posr = posrow_ref[0]
    valid = posr >= 0
    bucket = _t5_bucket(past - posr)
    bias = jnp.zeros((N_HEADS, topk), F32)
    for bk in range(NUM_BUCKETS):
        bias = jnp.where(bucket == bk, rbt_ref[:, bk:bk + 1], bias)
    shift = GQA.bit_length() - 1
    kv_of_row = jnp.right_shift(lax.broadcasted_iota(I32, (N_HEADS, topk), 0), shift)
    kv_of_row_d = jnp.right_shift(lax.broadcasted_iota(I32, (N_HEADS, HEAD_DIM), 0), shift)
    scale = HEAD_DIM ** -0.5
    qf = q.astype(F32)
    knew = knew_ref[0].astype(BF16).astype(F32)
    vnew = vnew_ref[0].astype(BF16).astype(F32)
    lg = jnp.zeros((N_HEADS, topk), F32)
    lg_new = jnp.zeros((N_HEADS, 1), F32)
    v_new_rows = jnp.zeros((N_HEADS, HEAD_DIM), F32)
    for kh in range(N_KV_HEADS):
        hs = slice(kh * HEAD_DIM, (kh + 1) * HEAD_DIM)
        lg = jnp.where(kv_of_row == kh, _dot_nt(q, kbuf[slot, :, kh, :].astype(BF16)), lg)
        lg_new = jnp.where(kv_of_row[:, 0:1] == kh, jnp.sum(qf * knew[:, hs], axis=1, keepdims=True), lg_new)
        v_new_rows = jnp.where(kv_of_row_d == kh, jnp.broadcast_to(vnew[:, hs], (N_HEADS, HEAD_DIM)), v_new_rows)
    lg = jnp.where(valid, lg * scale + bias, NEG)
    new_sel = newsel_ref[0][:, 0:1] > 0.5
    lg_new = jnp.where(new_sel, lg_new * scale + rbt_ref[:, 0:1], NEG)
    m = jnp.maximum(jnp.max(lg, axis=1, keepdims=True), lg_new)
    p = jnp.where(valid, jnp.exp(lg - m), 0.0)
    p_new = jnp.where(new_sel, jnp.exp(lg_new - m), 0.0)
    denom = jnp.sum(p, axis=1, keepdims=True) + p_new
    pb = (p / denom).astype(BF16)
    out = (p_new / denom).astype(BF16).astype(F32) * v_new_rows
    for kh in range(N_KV_HEADS):
        out = out + jnp.where(kv_of_row_d == kh, _dot(pb, vbuf[slot, :, kh, :].astype(BF16)), 0.0)
    o_ref[0] = out


def _dsa_sample_attn(pos, page_table, q3, k_new, v_new, new_sel, rb_t, cache_k, cache_v, *, layer):
    db, topk = pos.shape
    n_pages = page_table.shape[1]
    page, n_kv, hd = cache_k.shape[2:]
    assert page & (page - 1) == 0
    kvw = n_kv * hd
    row = lambda b, ps, pt: (b, 0, 0)
    grid_spec = pltpu.PrefetchScalarGridSpec(
        num_scalar_prefetch=2,
        grid=(db,),
        in_specs=[
            pl.BlockSpec((1, N_HEADS, HEAD_DIM), row),
            pl.BlockSpec((1, 1, kvw), row),
            pl.BlockSpec((1, 1, kvw), row),
            pl.BlockSpec((1, 1, topk), row),
            pl.BlockSpec((1, 1, LANES), row),
            pl.BlockSpec((N_HEADS, NUM_BUCKETS), lambda b, ps, pt: (0, 0)),
            pl.BlockSpec(memory_space=pl.ANY),
            pl.BlockSpec(memory_space=pl.ANY),
        ],
        out_specs=pl.BlockSpec((1, N_HEADS, HEAD_DIM), row),
        scratch_shapes=[pltpu.VMEM((2, topk, n_kv, hd), F32), pltpu.VMEM((2, topk, n_kv, hd), F32),
                        pltpu.SemaphoreType.DMA((2,)), pltpu.SemaphoreType.DMA((2,))],
    )
    return pl.pallas_call(
        functools.partial(_dsa_sample_attn_kernel, layer=layer, past=n_pages * page, page=page),
        grid_spec=grid_spec,
        out_shape=jax.ShapeDtypeStruct((db, N_HEADS, HEAD_DIM), F32),
        compiler_params=_params("arbitrary"),
        name="dsa_sample_attn",
    )(pos.reshape(-1), page_table, q3, k_new, v_new, pos.reshape(db, 1, topk),
      new_sel.reshape(db, 1, LANES), rb_t, cache_k, cache_v)


def _mix_kernel(att_ref, zu_ref, zv_ref, ga_ref, gb_ref, gs_ref, ws_ref, bs_ref, h_ref, w_ref,
                o_ref, vn_ref, m_ref, *, first_rows_only):
    aw = att_ref.shape[1]

    @pl.when(pl.program_id(1) == 0)
    def _():
        tm = zu_ref.shape[0]
        v = jax.nn.gelu(zv_ref[...])
        vn = v * lax.rsqrt(jnp.mean(v * v, axis=-1, keepdims=True) + EPS) * gs_ref[...]
        vn_ref[...] = vn
        m_ref[:, :aw] = (jax.nn.sigmoid(ga_ref[...]) * att_ref[...]).astype(BF16)
        if first_rows_only:
            sgu = jax.nn.gelu(zu_ref[...]) * (vn * ws_ref[...] + bs_ref[...])
            m_ref[:, aw:] = (jax.nn.sigmoid(gb_ref[...]) * sgu).astype(BF16)
        else:
            row = lax.broadcasted_iota(I32, (CHUNK, CHUNK), 0)
            col = lax.broadcasted_iota(I32, (CHUNK, CHUNK), 1)
            for g in range(GMLP_GROUPS):
                cs = slice(g * LANES, (g + 1) * LANES)
                wt = jnp.where(col <= row, ws_ref[g], 0.0).astype(BF16)
                bcol = bs_ref[:, g:g + 1]
                for c in range(tm // CHUNK):
                    rs = slice(c * CHUNK, (c + 1) * CHUNK)
                    mixed = _dot(wt, vn_ref[rs, cs].astype(BF16)) + bcol
                    sgu = jax.nn.gelu(zu_ref[rs, cs]) * mixed
                    m_ref[rs, aw + g * LANES:aw + (g + 1) * LANES] = (
                        jax.nn.sigmoid(gb_ref[rs, cs]) * sgu).astype(BF16)

    o_ref[...] = h_ref[...] + _dot(m_ref[...], w_ref[...])


def _mix(att, z, h, g_sgu, ws, bs, w_out, *, first_rows_only, tm, tn, name):
    m, d = h.shape
    aw = att.shape[1]
    gw = g_sgu.shape[0]
    tm, tn = min(tm, m), min(tn, d)
    zb = lambda c: pl.BlockSpec((tm, gw), lambda i, j: (i, c))
    full = lambda a: pl.BlockSpec(a.shape, lambda i, j: (0,) * a.ndim)
    first = z.shape[1] // gw - 4
    return pl.pallas_call(
        functools.partial(_mix_kernel, first_rows_only=first_rows_only),
        grid=(m // tm, d // tn),
        in_specs=[pl.BlockSpec((tm, aw), lambda i, j: (i, 0)),
                  zb(first), zb(first + 1), zb(first + 2), zb(first + 3),
                  pl.BlockSpec((1, gw), lambda i, j: (0, 0)),
                  full(ws), full(bs),
                  pl.BlockSpec((tm, tn), lambda i, j: (i, j)),
                  pl.BlockSpec((aw + gw, tn), lambda i, j: (0, j))],
        out_specs=[pl.BlockSpec((tm, tn), lambda i, j: (i, j)),
                   pl.BlockSpec((tm, gw), lambda i, j: (i, 0))],
        out_shape=[jax.ShapeDtypeStruct((m, d), F32), jax.ShapeDtypeStruct((m, gw), F32)],
        scratch_shapes=[pltpu.VMEM((tm, aw + gw), BF16)],
        compiler_params=_params("arbitrary", "arbitrary"),
        name=name,
    )(att, z, z, z, z, g_sgu.reshape(1, gw), ws, bs, h, w_out)


def _cross_kernel(q_ref, mk_ref, mv_ref, h_ref, wo_ref, o_ref):
    rows = q_ref.shape[1]
    q = q_ref[0]
    if rows < SUBLANES:
        q = jnp.broadcast_to(q, (SUBLANES, q.shape[1]))
    qb = q.astype(BF16)
    mk = mk_ref[0].astype(BF16)
    mv = mv_ref[0].astype(BF16)
    hd = mk.shape[1] // X_HEADS
    outs = []
    for hh in range(X_HEADS):
        sl = slice(hh * hd, (hh + 1) * hd)
        lg = _dot_nt(qb[:, sl], mk[:, sl]) * hd ** -0.5
        e = jnp.exp(lg - jnp.max(lg, axis=1, keepdims=True))
        p = e / jnp.sum(e, axis=1, keepdims=True)
        outs.append(_dot(p.astype(BF16), mv[:, sl]))
    y = _dot(jnp.concatenate(outs, axis=1).astype(BF16), wo_ref[...])
    o_ref[0] = h_ref[0] + y[:rows]


def _cross(q, mk, mv, h, w_xo, *, tq, name):
    b, t, xw = q.shape
    d = h.shape[2]
    mlen = mk.shape[1]
    return pl.pallas_call(
        _cross_kernel,
        grid=(b, t // tq),
        in_specs=[pl.BlockSpec((1, tq, xw), lambda bb, i: (bb, i, 0)),
                  pl.BlockSpec((1, mlen, xw), lambda bb, i: (bb, 0, 0)),
                  pl.BlockSpec((1, mlen, xw), lambda bb, i: (bb, 0, 0)),
                  pl.BlockSpec((1, tq, d), lambda bb, i: (bb, i, 0)),
                  pl.BlockSpec((xw, d), lambda bb, i: (0, 0))],
        out_specs=pl.BlockSpec((1, tq, d), lambda bb, i: (bb, i, 0)),
        out_shape=jax.ShapeDtypeStruct((b, t, d), F32),
        compiler_params=_params("arbitrary", "arbitrary"),
        name=name,
    )(q, mk, mv, h, w_xo)


def _top_rows(s, k, payload=None):
    r = s.shape[0]
    rid = lax.broadcasted_iota(I32, s.shape, 0).astype(F32)
    vals, picks = [], []
    for _ in range(k):
        m = jnp.max(s, axis=0, keepdims=True)
        am = jnp.min(jnp.where(s == m, rid, float(r)), axis=0, keepdims=True)
        hit = rid == am
        vals.append(m)
        picks.append(am if payload is None else jnp.sum(jnp.where(hit, payload, 0.0), axis=0, keepdims=True))
        s = jnp.where(hit, -jnp.inf, s)
    return jnp.concatenate(vals, axis=0), jnp.concatenate(picks, axis=0)


def _peer_route_kernel(h_ref, g_ref, w_ref, sk_ref, ids_ref, gate_ref, xn_ref):
    @pl.when(pl.program_id(1) == 0)
    def _():
        x = h_ref[...]
        xn_ref[...] = (x * lax.rsqrt(jnp.mean(x * x, axis=-1, keepdims=True) + EPS) * g_ref[...]).astype(BF16)

    half = sk_ref.shape[2]
    qt = _dot_nt(w_ref[...], xn_ref[...]).astype(BF16)
    v0, i0 = _top_rows(_dot(sk_ref[0], qt[:half]), PEER_TOPK)
    v1, i1 = _top_rows(_dot(sk_ref[1], qt[half:]), PEER_TOPK)
    cand = jnp.concatenate([v0[a:a + 1] + v1 for a in range(PEER_TOPK)], axis=0)
    eid = jnp.concatenate([i0[a:a + 1] * float(N_KEYS) + i1 for a in range(PEER_TOPK)], axis=0)
    sc, e = _top_rows(cand, PEER_TOPK, payload=eid)
    ex = jnp.exp(sc - sc[0:1])
    gate_ref[...] = ex / jnp.sum(ex, axis=0, keepdims=True)
    ids_ref[...] = e.astype(I32)


def _peer_route(h, g_ffn, w_pq_t, sub_keys, *, tb):
    m, d = h.shape
    qd = w_pq_t.shape[0] // PEER_HEADS
    return pl.pallas_call(
        _peer_route_kernel,
        grid=(m // tb, PEER_HEADS),
        in_specs=[pl.BlockSpec((tb, d), lambda t, hh: (t, 0)),
                  pl.BlockSpec((1, d), lambda t, hh: (0, 0)),
                  pl.BlockSpec((qd, d), lambda t, hh: (hh, 0)),
                  pl.BlockSpec(sub_keys.shape, lambda t, hh: (0, 0, 0))],
        out_specs=[pl.BlockSpec((PEER_TOPK, tb), lambda t, hh: (hh, t)),
                   pl.BlockSpec((PEER_TOPK, tb), lambda t, hh: (hh, t))],
        out_shape=[jax.ShapeDtypeStruct((PEER_HEADS * PEER_TOPK, m), I32),
                   jax.ShapeDtypeStruct((PEER_HEADS * PEER_TOPK, m), F32)],
        scratch_shapes=[pltpu.VMEM((tb, d), BF16)],
        compiler_params=_params("arbitrary", "arbitrary"),
        name="peer_route",
    )(h, g_ffn.reshape(1, d), w_pq_t, sub_keys)


def _pack_tables_kernel(u_ref, v_ref, o_ref):
    nr = o_ref.shape[1] // 2

    def words(x):
        bits = pltpu.bitcast(x.astype(BF16).astype(F32), I32)
        half = bits.shape[1] // 2
        return (bits[:, half:] & jnp.int32(-65536)) | lax.shift_right_logical(bits[:, :half], 16)

    wu = words(u_ref[...])
    wv = words(v_ref[...])
    for r in range(nr):
        o_ref[:, r, :] = wu[:, r * LANES:(r + 1) * LANES]
        o_ref[:, nr + r, :] = wv[:, r * LANES:(r + 1) * LANES]


def _pack_tables(u_tab, v_tab, *, eb):
    n, d = u_tab.shape
    nr = d // (2 * LANES)
    return pl.pallas_call(
        _pack_tables_kernel,
        grid=(n // eb,),
        in_specs=[pl.BlockSpec((eb, d), lambda i: (i, 0)), pl.BlockSpec((eb, d), lambda i: (i, 0))],
        out_specs=pl.BlockSpec((eb, 2 * nr, LANES), lambda i: (i, 0, 0)),
        out_shape=jax.ShapeDtypeStruct((n, 2 * nr, LANES), I32),
        compiler_params=_params("arbitrary"),
        name="peer_pack_tables",
    )(u_tab, v_tab)


def _unpack_words(w):
    return pltpu.bitcast(jnp.left_shift(w, 16), F32), pltpu.bitcast(w & jnp.int32(-65536), F32)


def _peer_gather_kernel(ids_hbm, gate_ref, h_ref, gffn_ref, gfin_ref, tab_hbm, o_ref,
                        ids_s, buf, part_s, sem_ids, sem_rows, *, n_steps):
    s = pl.program_id(0)
    g_tok = h_ref.shape[0]
    nr = buf.shape[1] // 2
    n_e = buf.shape[2]
    n_buf = buf.shape[0]
    ahead = n_buf - 1
    assert g_tok % n_buf == 0 and ahead <= g_tok
    cur = lax.rem(s, 2)
    nxt = 1 - cur

    def ids_copy(step, slot):
        return pltpu.make_async_copy(ids_hbm.at[step], ids_s.at[slot], sem_ids.at[slot])

    def row_copy(e, k, slot):
        return pltpu.make_async_copy(tab_hbm.at[e], buf.at[slot, :, k, :], sem_rows.at[slot])

    def issue(ids_slot, t, slot):
        for k in range(n_e):
            row_copy(ids_s[ids_slot, t, k], k, slot).start()

    def wait(slot):
        for k in range(n_e):
            row_copy(0, k, slot).wait()

    @pl.when(s == 0)
    def _():
        ids_copy(0, 0).start()
        ids_copy(0, 0).wait()
        for t in range(ahead):
            issue(0, t, t % n_buf)

    @pl.when(s + 1 < n_steps)
    def _():
        ids_copy(s + 1, nxt).start()

    h = h_ref[...]
    xn = h * lax.rsqrt(jnp.mean(h * h, axis=-1, keepdims=True) + EPS) * gffn_ref[...]
    eye = lax.broadcasted_iota(I32, (n_e, n_e), 0) == lax.broadcasted_iota(I32, (n_e, n_e), 1)

    for t in range(g_tok):
        slot = t % n_buf
        ta = t + ahead
        if ta < g_tok:
            issue(cur, ta, ta % n_buf)
        else:
            @pl.when(s + 1 < n_steps)
            def _():
                if ta == g_tok:
                    ids_copy(s + 1, nxt).wait()
                issue(nxt, ta - g_tok, ta % n_buf)
        wait(slot)
        x = xn[t:t + 1, :]
        hacc = jnp.zeros((n_e, LANES), F32)
        for r in range(nr):
            lo, hi = _unpack_words(buf[slot, r])
            hacc = hacc + lo * x[:, r * LANES:(r + 1) * LANES] + hi * x[:, (nr + r) * LANES:(nr + r + 1) * LANES]
        act = jax.nn.gelu(jnp.sum(hacc, axis=-1, keepdims=True))
        gcol = jnp.sum(jnp.where(eye, jnp.broadcast_to(gate_ref[t:t + 1, :], (n_e, n_e)), 0.0),
                       axis=-1, keepdims=True)
        w = jnp.broadcast_to(gcol * act, (n_e, LANES))
        for r in range(nr):
            lo, hi = _unpack_words(buf[slot, nr + r])
            part_s[t, :, r * LANES:(r + 1) * LANES] = jnp.sum(
                (lo * w).reshape(n_e // SUBLANES, SUBLANES, LANES), axis=0)
            part_s[t, :, (nr + r) * LANES:(nr + r + 1) * LANES] = jnp.sum(
                (hi * w).reshape(n_e // SUBLANES, SUBLANES, LANES), axis=0)

    y = h + jnp.sum(part_s[...], axis=1)
    o_ref[...] = y * lax.rsqrt(jnp.mean(y * y, axis=-1, keepdims=True) + EPS) * gfin_ref[...]


PEER_GATHER_BUFFERS = 4


def _peer_gather(ids, gates, h, g_ffn, g_final, packed_tab, *, g_tok):
    m, d = h.shape
    n_e = ids.shape[1]
    n_steps = m // g_tok
    n_rows = packed_tab.shape[1]
    ids3 = ids.reshape(n_steps, g_tok, n_e)
    return pl.pallas_call(
        functools.partial(_peer_gather_kernel, n_steps=n_steps),
        grid=(n_steps,),
        in_specs=[pl.BlockSpec(memory_space=pl.ANY),
                  pl.BlockSpec((g_tok, n_e), lambda s: (s, 0)),
                  pl.BlockSpec((g_tok, d), lambda s: (s, 0)),
                  pl.BlockSpec((1, d), lambda s: (0, 0)),
                  pl.BlockSpec((1, d), lambda s: (0, 0)),
                  pl.BlockSpec(memory_space=pl.ANY)],
        out_specs=pl.BlockSpec((g_tok, d), lambda s: (s, 0)),
        out_shape=jax.ShapeDtypeStruct((m, d), F32),
        scratch_shapes=[pltpu.SMEM((2, g_tok, n_e), I32),
                        pltpu.VMEM((PEER_GATHER_BUFFERS, n_rows, n_e, LANES), I32),
                        pltpu.VMEM((g_tok, SUBLANES, d), F32),
                        pltpu.SemaphoreType.DMA((2,)), pltpu.SemaphoreType.DMA((PEER_GATHER_BUFFERS,))],
        compiler_params=_params("arbitrary"),
        name="peer_gather",
    )(ids3, gates, h, g_ffn.reshape(1, d), g_final.reshape(1, d), packed_tab)


def kernel(x_prompt, x_sample, cache_k, cache_v, cache_kidx, cache_mem_k, cache_mem_v, page_table,
           mem_prompt, rel_bias, g_in, w_in, g_sgu, w_s, b_s, w_out, g_x, w_xq, w_xk, w_xv, w_xo,
           g_ffn, w_pq, sub_keys, peer_u, peer_v, g_final):
    batch, seq, d = x_prompt.shape
    db, dseq, _ = x_sample.shape
    depth = w_in.shape[0]
    assert dseq == 1, "sample group is one new token per sequence"
    n_pages = page_table.shape[1]
    page = cache_k.shape[2]
    past = n_pages * page
    aw, kvw, qiw, gw = N_HEADS * HEAD_DIM, N_KV_HEADS * HEAD_DIM, IDX_HEADS * IDX_DIM, GMLP_GROUPS * LANES
    xw = w_xq.shape[2]
    mlen = mem_prompt.shape[1]
    small_lo = aw + 2 * kvw + qiw
    small_hi = small_lo + IDX_DIM + IDX_HEADS
    assert NUM_BUCKETS // 2 + int(math.log((LANES + 1) / (NUM_BUCKETS // 2)) / math.log(
        MAX_DISTANCE / (NUM_BUCKETS // 2)) * (NUM_BUCKETS // 2)) >= NUM_BUCKETS - 1

    hp = x_prompt.reshape(batch * seq, d)
    hs = x_sample.reshape(db, d)
    btiles = _bias_tiles(rel_bias)
    rb_t = rel_bias.T
    outs = {n: [] for n in ("kp", "vp", "ip", "mkp", "mvp", "ks", "vs", "is", "sv")}

    for l in range(depth):
        w_main = jnp.concatenate([w_in[l][:, :small_lo], w_in[l][:, small_hi:]], axis=1).astype(BF16)
        w_small = jnp.pad(w_in[l][:, small_lo:small_hi], ((0, 0), (0, LANES - (small_hi - small_lo)))).astype(BF16)
        w_out_b = w_out[l].astype(BF16)
        w_xq_b, w_xk_b, w_xv_b, w_xo_b = (w.astype(BF16) for w in (w_xq[l], w_xk[l], w_xv[l], w_xo[l]))
        w_pq_t = w_pq[l].T.astype(BF16)
        sk_b = sub_keys[l].astype(BF16)
        ws_first = jnp.repeat(w_s[l][:, 0, 0], LANES).reshape(1, gw)
        bs_first = jnp.repeat(b_s[l][:, 0], LANES).reshape(1, gw)
        ones_d = jnp.ones((d,), F32)

        z = _proj(hp, g_in[l], w_main, norm=True, tm=512, tn=512, name="in_proj_prompt")
        zs = _proj(hp, g_in[l], w_small, norm=True, tm=512, tn=LANES, name="in_proj_idx_prompt")
        att = _dsa_prompt(z.reshape(batch, seq, -1), zs.reshape(batch, seq, LANES), btiles,
                          batch=batch, seq=seq, topk=min(TOPK_MAX, seq // 4))
        hp1, _ = _mix(att.reshape(batch * seq, aw), z, hp, g_sgu[l], w_s[l], b_s[l].T, w_out_b,
                      first_rows_only=False, tm=256, tn=512, name="mix_prompt")
        mem = mem_prompt.reshape(batch * mlen, d)
        mk = _proj(mem, ones_d, w_xk_b, norm=False, tm=256, tn=xw, name="mem_k_proj")
        mv = _proj(mem, ones_d, w_xv_b, norm=False, tm=256, tn=xw, name="mem_v_proj")
        qx = _proj(hp1, g_x[l], w_xq_b, norm=True, tm=512, tn=xw, name="xq_proj_prompt")
        hp2 = _cross(qx.reshape(batch, seq, xw), mk.reshape(batch, mlen, xw), mv.reshape(batch, mlen, xw),
                     hp1.reshape(batch, seq, d), w_xo_b, tq=256, name="cross_prompt").reshape(batch * seq, d)
        outs["kp"].append(z[:, aw:aw + kvw].reshape(batch, seq, N_KV_HEADS, HEAD_DIM))
        outs["vp"].append(z[:, aw + kvw:aw + 2 * kvw].reshape(batch, seq, N_KV_HEADS, HEAD_DIM))
        outs["ip"].append(zs[:, :IDX_DIM].reshape(batch, seq, IDX_DIM))
        outs["mkp"].append(mk.reshape(batch, mlen, X_HEADS, xw // X_HEADS))
        outs["mvp"].append(mv.reshape(batch, mlen, X_HEADS, xw // X_HEADS))

        z_s = _proj(hs, g_in[l], w_main, norm=True, tm=db, tn=512, name="in_proj_sample")
        zs_s = _proj(hs, g_in[l], w_small, norm=True, tm=db, tn=LANES, name="in_proj_idx_sample")
        qi3 = z_s[:, aw + 2 * kvw:aw + 2 * kvw + qiw].reshape(db, IDX_HEADS, IDX_DIM)
        wi3 = zs_s[:, IDX_DIM:IDX_DIM + IDX_HEADS].reshape(db, IDX_HEADS, 1)
        ki_new = zs_s[:, :IDX_DIM].reshape(db, 1, IDX_DIM)
        k_new = z_s[:, aw:aw + kvw].reshape(db, 1, kvw)
        v_new = z_s[:, aw + kvw:aw + 2 * kvw].reshape(db, 1, kvw)
        scores3, snew3 = _dsa_sample_scores(page_table, qi3, wi3, ki_new, cache_kidx[l])
        pos3, new_sel = _dsa_sample_select(scores3.reshape(db, past), snew3.reshape(db, LANES),
                                           topk=min(TOPK_MAX, (past + dseq) // 4))
        att_s = _dsa_sample_attn(pos3[:, :, 0], page_table, z_s[:, :aw].reshape(db, N_HEADS, HEAD_DIM),
                                 k_new, v_new, new_sel, rb_t, cache_k, cache_v, layer=l)
        hs1, vn_s = _mix(att_s.reshape(db, aw), z_s, hs, g_sgu[l], ws_first, bs_first, w_out_b,
                         first_rows_only=True, tm=db, tn=512, name="mix_sample")
        qx_s = _proj(hs1, g_x[l], w_xq_b, norm=True, tm=db, tn=xw, name="xq_proj_sample")
        hs2 = _cross(qx_s.reshape(db, 1, xw), cache_mem_k[l].reshape(db, mlen, xw),
                     cache_mem_v[l].reshape(db, mlen, xw), hs1.reshape(db, 1, d), w_xo_b,
                     tq=1, name="cross_sample").reshape(db, d)
        outs["ks"].append(k_new.reshape(db, dseq, N_KV_HEADS, HEAD_DIM))
        outs["vs"].append(v_new.reshape(db, dseq, N_KV_HEADS, HEAD_DIM))
        outs["is"].append(ki_new.reshape(db, dseq, IDX_DIM))
        outs["sv"].append(vn_s.reshape(db, dseq, gw))

        assert depth == 1
        tb = 256
        n_tok = batch * seq + db
        n_pad = -(-n_tok // tb) * tb
        h_all = jnp.concatenate([hp2, hs2, jnp.zeros((n_pad - n_tok, d), F32)], axis=0)
        ids_t, gates_t = _peer_route(h_all, g_ffn[l], w_pq_t, sk_b, tb=tb)
        packed = _pack_tables(peer_u[l], peer_v[l], eb=256)
        y_all = _peer_gather(ids_t.T, gates_t.T, h_all, g_ffn[l], g_final, packed, g_tok=8)

    y_prompt = y_all[:batch * seq].reshape(batch, seq, d)
    y_sample = y_all[batch * seq:n_tok].reshape(db, dseq, d)
    st = lambda n: jnp.stack(outs[n])
    return (y_prompt, y_sample, st("kp"), st("vp"), st("ip"), st("mkp"), st("mvp"),
            st("ks"), st("vs"), st("is"), st("sv"))
```

```python
import functools
import math

import jax
import jax.numpy as jnp
from jax import lax
from jax.experimental import pallas as pl
from jax.experimental.pallas import tpu as pltpu

F32 = jnp.float32
BF16 = jnp.bfloat16
I32 = jnp.int32

N_HEADS = 16
HEAD_DIM = 128
N_KV_HEADS = 4
GQA = N_HEADS // N_KV_HEADS
IDX_HEADS = 16
IDX_DIM = 64
TOPK_MAX = 256
NUM_BUCKETS = 32
MAX_DISTANCE = 128
CHUNK = 128
GMLP_GROUPS = 16
X_HEADS = 4
N_KEYS = 128
PEER_HEADS = 8
PEER_TOPK = 16
EPS = 1e-6

LANES = 128
SUBLANES = 8
VMEM_LIMIT_BYTES = 56 * 1024 * 1024

NEG = -1e30
INT_MIN = -(2 ** 31)


def _dot_nt(a, b):
    return lax.dot_general(a, b, (((1,), (1,)), ((), ())), preferred_element_type=F32)


def _dot(a, b):
    return jnp.dot(a, b, preferred_element_type=F32)


def _params(*sem):
    return pltpu.CompilerParams(dimension_semantics=sem, vmem_limit_bytes=VMEM_LIMIT_BYTES)


def _sortable(x):
    bits = pltpu.bitcast(x, I32)
    return bits ^ (jnp.right_shift(bits, 31) & 0x7FFFFFFF)


def _t5_bucket(dist):
    n = jnp.maximum(dist, 0)
    max_exact = NUM_BUCKETS // 2
    nf = jnp.maximum(n, 1).astype(F32)
    large = max_exact + (jnp.log(nf / max_exact) / math.log(MAX_DISTANCE / max_exact)
                         * (NUM_BUCKETS - max_exact)).astype(I32)
    large = jnp.minimum(large, NUM_BUCKETS - 1)
    return jnp.where(n < max_exact, n, large)


def _kth_largest_key(count_ge, shape, k):
    def body(it, t):
        cand = t | jnp.left_shift(jnp.int32(1), 31 - it)
        cnt = count_ge(cand ^ INT_MIN)
        return jnp.where(cnt >= k, cand, t)

    t = lax.fori_loop(0, 32, body, jnp.zeros(shape, I32))
    return t ^ INT_MIN


def _proj_kernel(x_ref, g_ref, w_ref, o_ref, xn_ref, *, norm):
    @pl.when(pl.program_id(1) == 0)
    def _():
        x = x_ref[...]
        if norm:
            x = x * lax.rsqrt(jnp.mean(x * x, axis=-1, keepdims=True) + EPS) * g_ref[...]
        xn_ref[...] = x.astype(BF16)

    o_ref[...] = _dot(xn_ref[...], w_ref[...])


def _proj(x, g, w, *, norm, tm, tn, name):
    m, k = x.shape
    n = w.shape[1]
    tm, tn = min(tm, m), min(tn, n)
    return pl.pallas_call(
        functools.partial(_proj_kernel, norm=norm),
        grid=(m // tm, n // tn),
        in_specs=[pl.BlockSpec((tm, k), lambda i, j: (i, 0)),
                  pl.BlockSpec((1, k), lambda i, j: (0, 0)),
                  pl.BlockSpec((k, tn), lambda i, j: (0, j))],
        out_specs=pl.BlockSpec((tm, tn), lambda i, j: (i, j)),
        out_shape=jax.ShapeDtypeStruct((m, n), F32),
        scratch_shapes=[pltpu.VMEM((tm, k), BF16)],
        compiler_params=_params("arbitrary", "arbitrary"),
        name=name,
    )(x, g.reshape(1, k), w)


def _bias_tiles_kernel(rb_ref, o_ref):
    tq = lax.broadcasted_iota(I32, (LANES, LANES), 0)
    c = lax.broadcasted_iota(I32, (LANES, LANES), 1)
    for part, off in enumerate((2 * LANES, LANES, 0)):
        dist = tq - c + off
        bucket = _t5_bucket(dist)
        for h in range(N_HEADS):
            tile = lax.fori_loop(
                0, NUM_BUCKETS, lambda b, acc: jnp.where(bucket == b, rb_ref[b, h], acc),
                jnp.zeros((LANES, LANES), F32))
            if off == 0:
                tile = jnp.where(dist < 0, NEG, tile)
            o_ref[part, h] = tile


def _bias_tiles(rel_bias):
    return pl.pallas_call(
        _bias_tiles_kernel,
        in_specs=[pl.BlockSpec(memory_space=pltpu.SMEM)],
        out_specs=pl.BlockSpec(memory_space=pltpu.VMEM),
        out_shape=jax.ShapeDtypeStruct((3, N_HEADS, LANES, LANES), F32),
        name="t5_bias_tiles",
    )(rel_bias)


def _dsa_prompt_kernel(q_ref, k_ref, v_ref, qi_ref, zsk_ref, zsq_ref, bt_ref, o_ref,
                       kbf, vbf, kilo, kihi, key3, qs, m_s, l_s, acc_s, *, topk):
    i = pl.program_id(1)
    tq = q_ref.shape[1]
    s = k_ref.shape[1]
    nkb = s // LANES
    cw = 2 * LANES

    @pl.when(i == 0)
    def _():
        kbf[...] = k_ref[0].astype(BF16)
        vbf[...] = v_ref[0].astype(BF16)
        zs = zsk_ref[0]
        lane = lax.broadcasted_iota(I32, zs.shape, 1)
        kilo[...] = jnp.where(lane < IDX_DIM, zs, 0.0).astype(BF16)
        kihi[...] = jnp.where(lane >= IDX_DIM, pltpu.roll(zs, IDX_DIM, axis=1), 0.0).astype(BF16)

    qi = qi_ref[0].astype(BF16)
    coef = zsq_ref[0][:, IDX_DIM:IDX_DIM + IDX_HEADS] * (IDX_HEADS ** -0.5 * IDX_DIM ** -0.5)
    t_pos = i * tq + lax.broadcasted_iota(I32, (tq, cw), 0)
    t_last = i * tq + tq - 1
    for c in range(s // cw):
        @pl.when(c * cw <= t_last)
        def _():
            klo = kilo[c * cw:(c + 1) * cw, :]
            khi = kihi[c * cw:(c + 1) * cw, :]
            acc = jnp.zeros((tq, cw), F32)
            for p in range(IDX_HEADS // 2):
                a = qi[:, p * LANES:(p + 1) * LANES]
                acc = acc + coef[:, 2 * p:2 * p + 1] * jnp.maximum(_dot_nt(a, klo), 0.0)
                acc = acc + coef[:, 2 * p + 1:2 * p + 2] * jnp.maximum(_dot_nt(a, khi), 0.0)
            s_pos = c * cw + lax.broadcasted_iota(I32, (tq, cw), 1)
            key = _sortable(jnp.where(s_pos <= t_pos, acc, -jnp.inf))
            for u in range(cw // LANES):
                key3[c * (cw // LANES) + u] = key[:, u * LANES:(u + 1) * LANES]

        @pl.when(c * cw > t_last)
        def _():
            for u in range(cw // LANES):
                key3[c * (cw // LANES) + u] = jnp.full((tq, LANES), INT_MIN, I32)

    grp = 4
    n_grp = jnp.right_shift(i + grp, grp.bit_length() - 1)

    def count_ge(cs):
        def body(g, acc):
            blk = key3[pl.ds(pl.multiple_of(g * grp, grp), grp)]
            return acc + jnp.sum(jnp.where(blk >= cs[None], 1.0, 0.0), axis=0)
        acc = lax.fori_loop(0, n_grp, body, jnp.zeros((tq, LANES), F32))
        return jnp.sum(acc, axis=1, keepdims=True)

    thr = _kth_largest_key(count_ge, (tq, 1), float(topk))
    thr4 = jnp.concatenate([thr] * GQA, axis=0)

    qb = q_ref[0].astype(BF16)
    for kh in range(N_KV_HEADS):
        qs[kh] = jnp.concatenate(
            [qb[:, (GQA * kh + g) * HEAD_DIM:(GQA * kh + g + 1) * HEAD_DIM] for g in range(GQA)], axis=0)
    m_s[...] = jnp.full(m_s.shape, NEG, F32)
    l_s[...] = jnp.zeros(l_s.shape, F32)
    acc_s[...] = jnp.zeros(acc_s.shape, F32)
    scale = HEAD_DIM ** -0.5

    def fbody(j, carry):
        part = jnp.clip(j - i + 2, 0, 2)
        key4 = jnp.concatenate([key3[j]] * GQA, axis=0)
        mask = key4 >= thr4
        r0 = pl.multiple_of(j * LANES, LANES)
        for kh in range(N_KV_HEADS):
            kj = kbf[pl.ds(r0, LANES), kh * HEAD_DIM:(kh + 1) * HEAD_DIM]
            vj = vbf[pl.ds(r0, LANES), kh * HEAD_DIM:(kh + 1) * HEAD_DIM]
            bias = bt_ref[part, GQA * kh:GQA * (kh + 1)].reshape(GQA * tq, LANES)
            lg = jnp.where(mask, _dot_nt(qs[kh], kj) * scale + bias, NEG)
            m_old = m_s[kh]
            m_new = jnp.maximum(m_old, jnp.max(lg, axis=1, keepdims=True))
            alpha = jnp.exp(m_old - m_new)
            p = jnp.where(mask, jnp.exp(lg - m_new), 0.0)
            l_s[kh] = alpha * l_s[kh] + jnp.sum(p, axis=1, keepdims=True)
            acc_s[kh] = alpha * acc_s[kh] + _dot(p.astype(BF16), vj)
            m_s[kh] = m_new
        return carry

    lax.fori_loop(0, i + 1, fbody, 0)
    for kh in range(N_KV_HEADS):
        out = acc_s[kh] / l_s[kh]
        for g in range(GQA):
            h = GQA * kh + g
            o_ref[0, :, h * HEAD_DIM:(h + 1) * HEAD_DIM] = out[g * tq:(g + 1) * tq]


def _dsa_prompt(z, zs, btiles, *, batch, seq, topk):
    tq = LANES
    aw = N_HEADS * HEAD_DIM
    kvw = N_KV_HEADS * HEAD_DIM
    qiw = IDX_HEADS * IDX_DIM
    return pl.pallas_call(
        functools.partial(_dsa_prompt_kernel, topk=topk),
        grid=(batch, seq // tq),
        in_specs=[
            pl.BlockSpec((1, tq, aw), lambda b, i: (b, i, 0)),
            pl.BlockSpec((1, seq, kvw), lambda b, i: (b, 0, aw // kvw)),
            pl.BlockSpec((1, seq, kvw), lambda b, i: (b, 0, aw // kvw + 1)),
            pl.BlockSpec((1, tq, qiw), lambda b, i: (b, i, (aw + 2 * kvw) // qiw)),
            pl.BlockSpec((1, seq, LANES), lambda b, i: (b, 0, 0)),
            pl.BlockSpec((1, tq, LANES), lambda b, i: (b, i, 0)),
            pl.BlockSpec((3, N_HEADS, LANES, LANES), lambda b, i: (0, 0, 0, 0)),
        ],
        out_specs=pl.BlockSpec((1, tq, aw), lambda b, i: (b, i, 0)),
        out_shape=jax.ShapeDtypeStruct((batch, seq, aw), F32),
        scratch_shapes=[
            pltpu.VMEM((seq, kvw), BF16), pltpu.VMEM((seq, kvw), BF16),
            pltpu.VMEM((seq, LANES), BF16), pltpu.VMEM((seq, LANES), BF16),
            pltpu.VMEM((seq // LANES, tq, LANES), I32),
            pltpu.VMEM((N_KV_HEADS, GQA * tq, HEAD_DIM), BF16),
            pltpu.VMEM((N_KV_HEADS, GQA * tq, LANES), F32),
            pltpu.VMEM((N_KV_HEADS, GQA * tq, LANES), F32),
            pltpu.VMEM((N_KV_HEADS, GQA * tq, HEAD_DIM), F32),
        ],
        compiler_params=_params("arbitrary", "arbitrary"),
        name="dsa_prompt",
    )(z, z, z, z, zs, zs, btiles)


SAMPLE_PAGES_PER_STEP = 8


def _dsa_sample_score_kernel(pt_ref, qi_ref, wi_ref, kinew_ref, *rest):
    kidx_refs, (o_ref, onew_ref) = rest[:-2], rest[-2:]
    qi = qi_ref[0].astype(BF16)
    coef = wi_ref[0] * (IDX_HEADS ** -0.5 * IDX_DIM ** -0.5)

    def score(ki):
        return jnp.sum(jnp.maximum(_dot_nt(qi, ki.astype(BF16)), 0.0) * coef, axis=0, keepdims=True)

    o_ref[0] = jnp.concatenate([score(r[0]) for r in kidx_refs], axis=1)

    @pl.when(pl.program_id(1) == pl.num_programs(1) - 1)
    def _():
        onew_ref[0] = score(jnp.broadcast_to(kinew_ref[0], (LANES, kinew_ref.shape[2])))


def _dsa_sample_scores(page_table, qi3, wi3, ki_new, cache_kidx):
    db, n_pages = page_table.shape
    page = cache_kidx.shape[1]
    pg = math.gcd(SAMPLE_PAGES_PER_STEP, n_pages)
    row = lambda b, j, pt: (b, 0, 0)
    grid_spec = pltpu.PrefetchScalarGridSpec(
        num_scalar_prefetch=1,
        grid=(db, n_pages // pg),
        in_specs=[pl.BlockSpec((1, IDX_HEADS, IDX_DIM), row),
                  pl.BlockSpec((1, IDX_HEADS, 1), row),
                  pl.BlockSpec((1, 1, IDX_DIM), row)]
        + [pl.BlockSpec((1, page, IDX_DIM), functools.partial(lambda b, j, pt, u: (pt[b, j * pg + u], 0, 0), u=u))
           for u in range(pg)],
        out_specs=[pl.BlockSpec((1, 1, pg * page), lambda b, j, pt: (b, 0, j)),
                   pl.BlockSpec((1, 1, LANES), row)],
    )
    return pl.pallas_call(
        _dsa_sample_score_kernel,
        grid_spec=grid_spec,
        out_shape=[jax.ShapeDtypeStruct((db, 1, n_pages * page), F32),
                   jax.ShapeDtypeStruct((db, 1, LANES), F32)],
        compiler_params=_params("arbitrary", "arbitrary"),
        name="dsa_sample_scores",
    )(page_table, qi3, wi3, ki_new, *([cache_kidx] * pg))


def _dsa_sample_select_kernel(s_ref, snew_ref, tri_ref, pos_ref, new_ref, rank_s, nsel_s, *, topk):
    db, past = s_ref.shape
    k = float(topk)
    key = _sortable(s_ref[...])
    key_new = _sortable(snew_ref[:, 0:1])

    def count_ge(cs):
        return (jnp.sum(jnp.where(key >= cs, 1.0, 0.0), axis=1, keepdims=True)
                + jnp.where(key_new >= cs, 1.0, 0.0))

    thr = _kth_largest_key(count_ge, (db, 1), k)
    gt = key > thr
    eq = key == thr
    need = k - (jnp.sum(jnp.where(gt, 1.0, 0.0), axis=1, keepdims=True) + jnp.where(key_new > thr, 1.0, 0.0))

    tri = tri_ref[...]

    def cumsum(flags):
        out, off = [], jnp.zeros((db, 1), F32)
        for blk in range(past // LANES):
            pc = _dot(flags[:, blk * LANES:(blk + 1) * LANES].astype(BF16), tri) + off
            out.append(pc)
            off = pc[:, LANES - 1:LANES]
        return jnp.concatenate(out, axis=1), off

    eq_rank, eq_total = cumsum(jnp.where(eq, 1.0, 0.0))
    sel = jnp.where(jnp.logical_or(gt, jnp.logical_and(eq, eq_rank <= need)), 1.0, 0.0)
    new_sel = jnp.logical_or(key_new > thr, jnp.logical_and(key_new == thr, eq_total < need))
    new_ref[...] = jnp.broadcast_to(jnp.where(new_sel, 1.0, 0.0), new_ref.shape)
    rank, n_sel = cumsum(sel)
    rank = jnp.where(sel > 0.0, rank, 0.0)
    for b in range(db):
        rank_s[b] = rank[b:b + 1, :]
        nsel_s[b] = jnp.broadcast_to(n_sel[b:b + 1, :], (1, LANES))
    ch = min(8 * LANES, past)
    slot = lax.broadcasted_iota(I32, (topk, ch), 0).astype(F32) + 1.0
    lane = lax.broadcasted_iota(I32, (topk, ch), 1).astype(F32)
    slot_col = lax.broadcasted_iota(I32, (topk, 1), 0).astype(F32)

    def extract(b, carry):
        r = rank_s[b]
        pos = jnp.zeros((topk, 1), F32)
        for c in range(past // ch):
            hit = r[:, c * ch:(c + 1) * ch] == slot
            pos = pos + jnp.sum(jnp.where(hit, lane + float(c * ch), 0.0), axis=1, keepdims=True)
        pos = jnp.where(slot_col < nsel_s[b][:, 0:1], pos, -1.0)
        pos_ref[b] = jnp.broadcast_to(pos, (topk, LANES)).astype(I32)
        return carry

    lax.fori_loop(0, db, extract, 0)


def _dsa_sample_select(scores, snew, *, topk):
    db = scores.shape[0]
    tri = (lax.broadcasted_iota(I32, (LANES, LANES), 0) <= lax.broadcasted_iota(I32, (LANES, LANES), 1)).astype(BF16)
    return pl.pallas_call(
        functools.partial(_dsa_sample_select_kernel, topk=topk),
        out_shape=[jax.ShapeDtypeStruct((db, topk, LANES), I32), jax.ShapeDtypeStruct((db, LANES), F32)],
        scratch_shapes=[pltpu.VMEM((db, 1, scores.shape[1]), F32), pltpu.VMEM((db, 1, LANES), F32)],
        compiler_params=pltpu.CompilerParams(vmem_limit_bytes=VMEM_LIMIT_BYTES),
        name="dsa_sample_select",
    )(scores, snew, tri)


def _dsa_sample_attn_kernel(pos_s, pt_s, q_ref, knew_ref, vnew_ref, posrow_ref, newsel_ref, rbt_ref,
                            ck_hbm, cv_hbm, o_ref, kbuf, vbuf, sem_k, sem_v, *, layer, past, page):
    b = pl.program_id(0)
    topk = kbuf.shape[1]
    slot = lax.rem(b, 2)
    page_shift = page.bit_length() - 1
    unroll = 8

    def row_copies(bb, j, sl):
        p = jnp.maximum(pos_s[bb * topk + j], 0)
        pg = pt_s[bb, jnp.right_shift(p, page_shift)]
        off = p & (page - 1)
        return (pltpu.make_async_copy(ck_hbm.at[layer, pg, off], kbuf.at[sl, j], sem_k.at[sl]),
                pltpu.make_async_copy(cv_hbm.at[layer, pg, off], vbuf.at[sl, j], sem_v.at[sl]))

    def issue(bb, sl):
        def body(i, carry):
            for u in range(unroll):
                ck, cv = row_copies(bb, i * unroll + u, sl)
                ck.start()
                cv.start()
            return carry
        lax.fori_loop(0, topk // unroll, body, 0)

    def wait(sl):
        def body(i, carry):
            for u in range(unroll):
                j = i * unroll + u
                pltpu.make_async_copy(ck_hbm.at[layer, 0, 0], kbuf.at[sl, j], sem_k.at[sl]).wait()
                pltpu.make_async_copy(cv_hbm.at[layer, 0, 0], vbuf.at[sl, j], sem_v.at[sl]).wait()
            return carry
        lax.fori_loop(0, topk // unroll, body, 0)

    @pl.when(b == 0)
    def _():
        issue(0, 0)

    @pl.when(b + 1 < pl.num_programs(0))
    def _():
        issue(b + 1, 1 - slot)

    wait(slot)

    q = q_ref[0].astype(BF16)
    posr = posrow_ref[0]
    valid = posr >= 0
    bucket = _t5_bucket(past - posr)
    bias = jnp.zeros((N_HEADS, topk), F32)
    for bk in range(NUM_BUCKETS):
        bias = jnp.where(bucket == bk, rbt_ref[:, bk:bk + 1], bias)
    shift = GQA.bit_length() - 1
    kv_of_row = jnp.right_shift(lax.broadcasted_iota(I32, (N_HEADS, topk), 0), shift)
    kv_of_row_d = jnp.right_shift(lax.broadcasted_iota(I32, (N_HEADS, HEAD_DIM), 0), shift)
    scale = HEAD_DIM ** -0.5
    qf = q.astype(F32)
    knew = knew_ref[0].astype(BF16).astype(F32)
    vnew = vnew_ref[0].astype(BF16).astype(F32)
    lg = jnp.zeros((N_HEADS, topk), F32)
    lg_new = jnp.zeros((N_HEADS, 1), F32)
    v_new_rows = jnp.zeros((N_HEADS, HEAD_DIM), F32)
    for kh in range(N_KV_HEADS):
        hs = slice(kh * HEAD_DIM, (kh + 1) * HEAD_DIM)
        lg = jnp.where(kv_of_row == kh, _dot_nt(q, kbuf[slot, :, kh, :].astype(BF16)), lg)
        lg_new = jnp.where(kv_of_row[:, 0:1] == kh, jnp.sum(qf * knew[:, hs], axis=1, keepdims=True), lg_new)
        v_new_rows = jnp.where(kv_of_row_d == kh, jnp.broadcast_to(vnew[:, hs], (N_HEADS, HEAD_DIM)), v_new_rows)
    lg = jnp.where(valid, lg * scale + bias, NEG)
    new_sel = newsel_ref[0][:, 0:1] > 0.5
    lg_new = jnp.where(new_sel, lg_new * scale + rbt_ref[:, 0:1], NEG)
    m = jnp.maximum(jnp.max(lg, axis=1, keepdims=True), lg_new)
    p = jnp.where(valid, jnp.exp(lg - m), 0.0)
    p_new = jnp.where(new_sel, jnp.exp(lg_new - m), 0.0)
    denom = jnp.sum(p, axis=1, keepdims=True) + p_new
    pb = (p / denom).astype(BF16)
    out = (p_new / denom).astype(BF16).astype(F32) * v_new_rows
    for kh in range(N_KV_HEADS):
        out = out + jnp.where(kv_of_row_d == kh, _dot(pb, vbuf[slot, :, kh, :].astype(BF16)), 0.0)
    o_ref[0] = out


def _dsa_sample_attn(pos, page_table, q3, k_new, v_new, new_sel, rb_t, cache_k, cache_v, *, layer):
    db, topk = pos.shape
    n_pages = page_table.shape[1]
    page, n_kv, hd = cache_k.shape[2:]
    assert page & (page - 1) == 0
    kvw = n_kv * hd
    row = lambda b, ps, pt: (b, 0, 0)
    grid_spec = pltpu.PrefetchScalarGridSpec(
        num_scalar_prefetch=2,
        grid=(db,),
        in_specs=[
            pl.BlockSpec((1, N_HEADS, HEAD_DIM), row),
            pl.BlockSpec((1, 1, kvw), row),
            pl.BlockSpec((1, 1, kvw), row),
            pl.BlockSpec((1, 1, topk), row),
            pl.BlockSpec((1, 1, LANES), row),
            pl.BlockSpec((N_HEADS, NUM_BUCKETS), lambda b, ps, pt: (0, 0)),
            pl.BlockSpec(memory_space=pl.ANY),
            pl.BlockSpec(memory_space=pl.ANY),
        ],
        out_specs=pl.BlockSpec((1, N_HEADS, HEAD_DIM), row),
        scratch_shapes=[pltpu.VMEM((2, topk, n_kv, hd), F32), pltpu.VMEM((2, topk, n_kv, hd), F32),
                        pltpu.SemaphoreType.DMA((2,)), pltpu.SemaphoreType.DMA((2,))],
    )
    return pl.pallas_call(
        functools.partial(_dsa_sample_attn_kernel, layer=layer, past=n_pages * page, page=page),
        grid_spec=grid_spec,
        out_shape=jax.ShapeDtypeStruct((db, N_HEADS, HEAD_DIM), F32),
        compiler_params=_params("arbitrary"),
        name="dsa_sample_attn",
    )(pos.reshape(-1), page_table, q3, k_new, v_new, pos.reshape(db, 1, topk),
      new_sel.reshape(db, 1, LANES), rb_t, cache_k, cache_v)


def _mix_kernel(att_ref, zu_ref, zv_ref, ga_ref, gb_ref, gs_ref, ws_ref, bs_ref, h_ref, w_ref,
                o_ref, vn_ref, m_ref, *, first_rows_only):
    aw = att_ref.shape[1]

    @pl.when(pl.program_id(1) == 0)
    def _():
        tm = zu_ref.shape[0]
        v = jax.nn.gelu(zv_ref[...])
        vn = v * lax.rsqrt(jnp.mean(v * v, axis=-1, keepdims=True) + EPS) * gs_ref[...]
        vn_ref[...] = vn
        m_ref[:, :aw] = (jax.nn.sigmoid(ga_ref[...]) * att_ref[...]).astype(BF16)
        if first_rows_only:
            sgu = jax.nn.gelu(zu_ref[...]) * (vn * ws_ref[...] + bs_ref[...])
            m_ref[:, aw:] = (jax.nn.sigmoid(gb_ref[...]) * sgu).astype(BF16)
        else:
            row = lax.broadcasted_iota(I32, (CHUNK, CHUNK), 0)
            col = lax.broadcasted_iota(I32, (CHUNK, CHUNK), 1)
            for g in range(GMLP_GROUPS):
                cs = slice(g * LANES, (g + 1) * LANES)
                wt = jnp.where(col <= row, ws_ref[g], 0.0).astype(BF16)
                bcol = bs_ref[:, g:g + 1]
                for c in range(tm // CHUNK):
                    rs = slice(c * CHUNK, (c + 1) * CHUNK)
                    mixed = _dot(wt, vn_ref[rs, cs].astype(BF16)) + bcol
                    sgu = jax.nn.gelu(zu_ref[rs, cs]) * mixed
                    m_ref[rs, aw + g * LANES:aw + (g + 1) * LANES] = (
                        jax.nn.sigmoid(gb_ref[rs, cs]) * sgu).astype(BF16)

    o_ref[...] = h_ref[...] + _dot(m_ref[...], w_ref[...])


def _mix(att, z, h, g_sgu, ws, bs, w_out, *, first_rows_only, tm, tn, name):
    m, d = h.shape
    aw = att.shape[1]
    gw = g_sgu.shape[0]
    tm, tn = min(tm, m), min(tn, d)
    zb = lambda c: pl.BlockSpec((tm, gw), lambda i, j: (i, c))
    full = lambda a: pl.BlockSpec(a.shape, lambda i, j: (0,) * a.ndim)
    first = z.shape[1] // gw - 4
    return pl.pallas_call(
        functools.partial(_mix_kernel, first_rows_only=first_rows_only),
        grid=(m // tm, d // tn),
        in_specs=[pl.BlockSpec((tm, aw), lambda i, j: (i, 0)),
                  zb(first), zb(first + 1), zb(first + 2), zb(first + 3),
                  pl.BlockSpec((1, gw), lambda i, j: (0, 0)),
                  full(ws), full(bs),
                  pl.BlockSpec((tm, tn), lambda i, j: (i, j)),
                  pl.BlockSpec((aw + gw, tn), lambda i, j: (0, j))],
        out_specs=[pl.BlockSpec((tm, tn), lambda i, j: (i, j)),
                   pl.BlockSpec((tm, gw), lambda i, j: (i, 0))],
        out_shape=[jax.ShapeDtypeStruct((m, d), F32), jax.ShapeDtypeStruct((m, gw), F32)],
        scratch_shapes=[pltpu.VMEM((tm, aw + gw), BF16)],
        compiler_params=_params("arbitrary", "arbitrary"),
        name=name,
    )(att, z, z, z, z, g_sgu.reshape(1, gw), ws, bs, h, w_out)


def _cross_kernel(q_ref, mk_ref, mv_ref, h_ref, wo_ref, o_ref):
    rows = q_ref.shape[1]
    q = q_ref[0]
    if rows < SUBLANES:
        q = jnp.broadcast_to(q, (SUBLANES, q.shape[1]))
    qb = q.astype(BF16)
    mk = mk_ref[0].astype(BF16)
    mv = mv_ref[0].astype(BF16)
    hd = mk.shape[1] // X_HEADS
    outs = []
    for hh in range(X_HEADS):
        sl = slice(hh * hd, (hh + 1) * hd)
        lg = _dot_nt(qb[:, sl], mk[:, sl]) * hd ** -0.5
        e = jnp.exp(lg - jnp.max(lg, axis=1, keepdims=True))
        p = e / jnp.sum(e, axis=1, keepdims=True)
        outs.append(_dot(p.astype(BF16), mv[:, sl]))
    y = _dot(jnp.concatenate(outs, axis=1).astype(BF16), wo_ref[...])
    o_ref[0] = h_ref[0] + y[:rows]


def _cross(q, mk, mv, h, w_xo, *, tq, name):
    b, t, xw = q.shape
    d = h.shape[2]
    mlen = mk.shape[1]
    return pl.pallas_call(
        _cross_kernel,
        grid=(b, t // tq),
        in_specs=[pl.BlockSpec((1, tq, xw), lambda bb, i: (bb, i, 0)),
                  pl.BlockSpec((1, mlen, xw), lambda bb, i: (bb, 0, 0)),
                  pl.BlockSpec((1, mlen, xw), lambda bb, i: (bb, 0, 0)),
                  pl.BlockSpec((1, tq, d), lambda bb, i: (bb, i, 0)),
                  pl.BlockSpec((xw, d), lambda bb, i: (0, 0))],
        out_specs=pl.BlockSpec((1, tq, d), lambda bb, i: (bb, i, 0)),
        out_shape=jax.ShapeDtypeStruct((b, t, d), F32),
        compiler_params=_params("arbitrary", "arbitrary"),
        name=name,
    )(q, mk, mv, h, w_xo)


def _top_rows(s, k, payload=None):
    r = s.shape[0]
    rid = lax.broadcasted_iota(I32, s.shape, 0).astype(F32)
    vals, picks = [], []
    for _ in range(k):
        m = jnp.max(s, axis=0, keepdims=True)
        am = jnp.min(jnp.where(s == m, rid, float(r)), axis=0, keepdims=True)
        hit = rid == am
        vals.append(m)
        picks.append(am if payload is None else jnp.sum(jnp.where(hit, payload, 0.0), axis=0, keepdims=True))
        s = jnp.where(hit, -jnp.inf, s)
    return jnp.concatenate(vals, axis=0), jnp.concatenate(picks, axis=0)


def _peer_route_kernel(h_ref, g_ref, w_ref, sk_ref, ids_ref, gate_ref, xn_ref):
    @pl.when(pl.program_id(1) == 0)
    def _():
        x = h_ref[...]
        xn_ref[...] = (x * lax.rsqrt(jnp.mean(x * x, axis=-1, keepdims=True) + EPS) * g_ref[...]).astype(BF16)

    half = sk_ref.shape[2]
    qt = _dot_nt(w_ref[...], xn_ref[...]).astype(BF16)
    v0, i0 = _top_rows(_dot(sk_ref[0], qt[:half]), PEER_TOPK)
    v1, i1 = _top_rows(_dot(sk_ref[1], qt[half:]), PEER_TOPK)
    cand = jnp.concatenate([v0[a:a + 1] + v1 for a in range(PEER_TOPK)], axis=0)
    eid = jnp.concatenate([i0[a:a + 1] * float(N_KEYS) + i1 for a in range(PEER_TOPK)], axis=0)
    sc, e = _top_rows(cand, PEER_TOPK, payload=eid)
    ex = jnp.exp(sc - sc[0:1])
    gate_ref[...] = ex / jnp.sum(ex, axis=0, keepdims=True)
    ids_ref[...] = e.astype(I32)


def _peer_route(h, g_ffn, w_pq_t, sub_keys, *, tb):
    m, d = h.shape
    qd = w_pq_t.shape[0] // PEER_HEADS
    return pl.pallas_call(
        _peer_route_kernel,
        grid=(m // tb, PEER_HEADS),
        in_specs=[pl.BlockSpec((tb, d), lambda t, hh: (t, 0)),
                  pl.BlockSpec((1, d), lambda t, hh: (0, 0)),
                  pl.BlockSpec((qd, d), lambda t, hh: (hh, 0)),
                  pl.BlockSpec(sub_keys.shape, lambda t, hh: (0, 0, 0))],
        out_specs=[pl.BlockSpec((PEER_TOPK, tb), lambda t, hh: (hh, t)),
                   pl.BlockSpec((PEER_TOPK, tb), lambda t, hh: (hh, t))],
        out_shape=[jax.ShapeDtypeStruct((PEER_HEADS * PEER_TOPK, m), I32),
                   jax.ShapeDtypeStruct((PEER_HEADS * PEER_TOPK, m), F32)],
        scratch_shapes=[pltpu.VMEM((tb, d), BF16)],
        compiler_params=_params("arbitrary", "arbitrary"),
        name="peer_route",
    )(h, g_ffn.reshape(1, d), w_pq_t, sub_keys)


def _pack_tables_kernel(u_ref, v_ref, o_ref):
    nr = o_ref.shape[1] // 2

    def words(x):
        bits = pltpu.bitcast(x.astype(BF16).astype(F32), I32)
        half = bits.shape[1] // 2
        return (bits[:, half:] & jnp.int32(-65536)) | lax.shift_right_logical(bits[:, :half], 16)

    wu = words(u_ref[...])
    wv = words(v_ref[...])
    for r in range(nr):
        o_ref[:, r, :] = wu[:, r * LANES:(r + 1) * LANES]
        o_ref[:, nr + r, :] = wv[:, r * LANES:(r + 1) * LANES]


def _pack_tables(u_tab, v_tab, *, eb):
    n, d = u_tab.shape
    nr = d // (2 * LANES)
    return pl.pallas_call(
        _pack_tables_kernel,
        grid=(n // eb,),
        in_specs=[pl.BlockSpec((eb, d), lambda i: (i, 0)), pl.BlockSpec((eb, d), lambda i: (i, 0))],
        out_specs=pl.BlockSpec((eb, 2 * nr, LANES), lambda i: (i, 0, 0)),
        out_shape=jax.ShapeDtypeStruct((n, 2 * nr, LANES), I32),
        compiler_params=_params("arbitrary"),
        name="peer_pack_tables",
    )(u_tab, v_tab)


def _unpack_words(w):
    return pltpu.bitcast(jnp.left_shift(w, 16), F32), pltpu.bitcast(w & jnp.int32(-65536), F32)


def _peer_gather_kernel(ids_hbm, gate_ref, h_ref, gffn_ref, gfin_ref, tab_hbm, o_ref,
                        ids_s, buf, part_s, sem_ids, sem_rows, *, n_steps):
    s = pl.program_id(0)
    g_tok = h_ref.shape[0]
    nr = buf.shape[1] // 2
    n_e = buf.shape[2]
    n_buf = buf.shape[0]
    ahead = n_buf - 1
    assert g_tok % n_buf == 0 and ahead <= g_tok
    cur = lax.rem(s, 2)
    nxt = 1 - cur

    def ids_copy(step, slot):
        return pltpu.make_async_copy(ids_hbm.at[step], ids_s.at[slot], sem_ids.at[slot])

    def row_copy(e, k, slot):
        return pltpu.make_async_copy(tab_hbm.at[e], buf.at[slot, :, k, :], sem_rows.at[slot])

    def issue(ids_slot, t, slot):
        for k in range(n_e):
            row_copy(ids_s[ids_slot, t, k], k, slot).start(priority=k % 2)

    def wait(slot):
        for k in range(n_e):
            row_copy(0, k, slot).wait()

    @pl.when(s == 0)
    def _():
        ids_copy(0, 0).start()
        ids_copy(0, 0).wait()
        for t in range(ahead):
            issue(0, t, t % n_buf)

    @pl.when(s + 1 < n_steps)
    def _():
        ids_copy(s + 1, nxt).start()

    h = h_ref[...]
    xn = h * lax.rsqrt(jnp.mean(h * h, axis=-1, keepdims=True) + EPS) * gffn_ref[...]
    eye = lax.broadcasted_iota(I32, (n_e, n_e), 0) == lax.broadcasted_iota(I32, (n_e, n_e), 1)

    for t in range(g_tok):
        slot = t % n_buf
        ta = t + ahead
        if ta < g_tok:
            issue(cur, ta, ta % n_buf)
        else:
            @pl.when(s + 1 < n_steps)
            def _():
                if ta == g_tok:
                    ids_copy(s + 1, nxt).wait()
                issue(nxt, ta - g_tok, ta % n_buf)
        wait(slot)
        x = xn[t:t + 1, :]
        hacc = jnp.zeros((n_e, LANES), F32)
        for r in range(nr):
            lo, hi = _unpack_words(buf[slot, r])
            hacc = hacc + lo * x[:, r * LANES:(r + 1) * LANES] + hi * x[:, (nr + r) * LANES:(nr + r + 1) * LANES]
        act = jax.nn.gelu(jnp.sum(hacc, axis=-1, keepdims=True))
        gcol = jnp.sum(jnp.where(eye, jnp.broadcast_to(gate_ref[t:t + 1, :], (n_e, n_e)), 0.0),
                       axis=-1, keepdims=True)
        w = jnp.broadcast_to(gcol * act, (n_e, LANES))
        for r in range(nr):
            lo, hi = _unpack_words(buf[slot, nr + r])
            part_s[t, :, r * LANES:(r + 1) * LANES] = jnp.sum(
                (lo * w).reshape(n_e // SUBLANES, SUBLANES, LANES), axis=0)
            part_s[t, :, (nr + r) * LANES:(nr + r + 1) * LANES] = jnp.sum(
                (hi * w).reshape(n_e // SUBLANES, SUBLANES, LANES), axis=0)

    y = h + jnp.sum(part_s[...], axis=1)
    o_ref[...] = y * lax.rsqrt(jnp.mean(y * y, axis=-1, keepdims=True) + EPS) * gfin_ref[...]


PEER_GATHER_BUFFERS = 4


def _peer_gather(ids, gates, h, g_ffn, g_final, packed_tab, *, g_tok):
    m, d = h.shape
    n_e = ids.shape[1]
    n_steps = m // g_tok
    n_rows = packed_tab.shape[1]
    ids3 = ids.reshape(n_steps, g_tok, n_e)
    return pl.pallas_call(
        functools.partial(_peer_gather_kernel, n_steps=n_steps),
        grid=(n_steps,),
        in_specs=[pl.BlockSpec(memory_space=pl.ANY),
                  pl.BlockSpec((g_tok, n_e), lambda s: (s, 0)),
                  pl.BlockSpec((g_tok, d), lambda s: (s, 0)),
                  pl.BlockSpec((1, d), lambda s: (0, 0)),
                  pl.BlockSpec((1, d), lambda s: (0, 0)),
                  pl.BlockSpec(memory_space=pl.ANY)],
        out_specs=pl.BlockSpec((g_tok, d), lambda s: (s, 0)),
        out_shape=jax.ShapeDtypeStruct((m, d), F32),
        scratch_shapes=[pltpu.SMEM((2, g_tok, n_e), I32),
                        pltpu.VMEM((PEER_GATHER_BUFFERS, n_rows, n_e, LANES), I32),
                        pltpu.VMEM((g_tok, SUBLANES, d), F32),
                        pltpu.SemaphoreType.DMA((2,)), pltpu.SemaphoreType.DMA((PEER_GATHER_BUFFERS,))],
        compiler_params=_params("arbitrary"),
        name="peer_gather",
    )(ids3, gates, h, g_ffn.reshape(1, d), g_final.reshape(1, d), packed_tab)


def kernel(x_prompt, x_sample, cache_k, cache_v, cache_kidx, cache_mem_k, cache_mem_v, page_table,
           mem_prompt, rel_bias, g_in, w_in, g_sgu, w_s, b_s, w_out, g_x, w_xq, w_xk, w_xv, w_xo,
           g_ffn, w_pq, sub_keys, peer_u, peer_v, g_final):
    batch, seq, d = x_prompt.shape
    db, dseq, _ = x_sample.shape
    depth = w_in.shape[0]
    assert dseq == 1, "sample group is one new token per sequence"
    n_pages = page_table.shape[1]
    page = cache_k.shape[2]
    past = n_pages * page
    aw, kvw, qiw, gw = N_HEADS * HEAD_DIM, N_KV_HEADS * HEAD_DIM, IDX_HEADS * IDX_DIM, GMLP_GROUPS * LANES
    xw = w_xq.shape[2]
    mlen = mem_prompt.shape[1]
    small_lo = aw + 2 * kvw + qiw
    small_hi = small_lo + IDX_DIM + IDX_HEADS
    assert NUM_BUCKETS // 2 + int(math.log((LANES + 1) / (NUM_BUCKETS // 2)) / math.log(
        MAX_DISTANCE / (NUM_BUCKETS // 2)) * (NUM_BUCKETS // 2)) >= NUM_BUCKETS - 1

    hp = x_prompt.reshape(batch * seq, d)
    hs = x_sample.reshape(db, d)
    btiles = _bias_tiles(rel_bias)
    rb_t = rel_bias.T
    outs = {n: [] for n in ("kp", "vp", "ip", "mkp", "mvp", "ks", "vs", "is", "sv")}

    for l in range(depth):
        w_main = jnp.concatenate([w_in[l][:, :small_lo].astype(BF16), w_in[l][:, small_hi:].astype(BF16)], axis=1)
        w_small = jnp.pad(w_in[l][:, small_lo:small_hi], ((0, 0), (0, LANES - (small_hi - small_lo)))).astype(BF16)
        w_out_b = w_out[l].astype(BF16)
        w_xq_b, w_xk_b, w_xv_b, w_xo_b = (w.astype(BF16) for w in (w_xq[l], w_xk[l], w_xv[l], w_xo[l]))
        w_pq_t = w_pq[l].T.astype(BF16)
        sk_b = sub_keys[l].astype(BF16)
        ws_first = jnp.repeat(w_s[l][:, 0, 0], LANES).reshape(1, gw)
        bs_first = jnp.repeat(b_s[l][:, 0], LANES).reshape(1, gw)
        ones_d = jnp.ones((d,), F32)

        z = _proj(hp, g_in[l], w_main, norm=True, tm=512, tn=1024, name="in_proj_prompt")
        zs = _proj(hp, g_in[l], w_small, norm=True, tm=512, tn=LANES, name="in_proj_idx_prompt")
        att = _dsa_prompt(z.reshape(batch, seq, -1), zs.reshape(batch, seq, LANES), btiles,
                          batch=batch, seq=seq, topk=min(TOPK_MAX, seq // 4))
        hp1, _ = _mix(att.reshape(batch * seq, aw), z, hp, g_sgu[l], w_s[l], b_s[l].T, w_out_b,
                      first_rows_only=False, tm=256, tn=512, name="mix_prompt")
        mem = mem_prompt.reshape(batch * mlen, d)
        mk = _proj(mem, ones_d, w_xk_b, norm=False, tm=256, tn=xw, name="mem_k_proj")
        mv = _proj(mem, ones_d, w_xv_b, norm=False, tm=256, tn=xw, name="mem_v_proj")
        qx = _proj(hp1, g_x[l], w_xq_b, norm=True, tm=512, tn=xw, name="xq_proj_prompt")
        hp2 = _cross(qx.reshape(batch, seq, xw), mk.reshape(batch, mlen, xw), mv.reshape(batch, mlen, xw),
                     hp1.reshape(batch, seq, d), w_xo_b, tq=256, name="cross_prompt").reshape(batch * seq, d)
        outs["kp"].append(z[:, aw:aw + kvw].reshape(batch, seq, N_KV_HEADS, HEAD_DIM))
        outs["vp"].append(z[:, aw + kvw:aw + 2 * kvw].reshape(batch, seq, N_KV_HEADS, HEAD_DIM))
        outs["ip"].append(zs[:, :IDX_DIM].reshape(batch, seq, IDX_DIM))
        outs["mkp"].append(mk.reshape(batch, mlen, X_HEADS, xw // X_HEADS))
        outs["mvp"].append(mv.reshape(batch, mlen, X_HEADS, xw // X_HEADS))

        z_s = _proj(hs, g_in[l], w_main, norm=True, tm=db, tn=512, name="in_proj_sample")
        zs_s = _proj(hs, g_in[l], w_small, norm=True, tm=db, tn=LANES, name="in_proj_idx_sample")
        qi3 = z_s[:, aw + 2 * kvw:aw + 2 * kvw + qiw].reshape(db, IDX_HEADS, IDX_DIM)
        wi3 = zs_s[:, IDX_DIM:IDX_DIM + IDX_HEADS].reshape(db, IDX_HEADS, 1)
        ki_new = zs_s[:, :IDX_DIM].reshape(db, 1, IDX_DIM)
        k_new = z_s[:, aw:aw + kvw].reshape(db, 1, kvw)
        v_new = z_s[:, aw + kvw:aw + 2 * kvw].reshape(db, 1, kvw)
        scores3, snew3 = _dsa_sample_scores(page_table, qi3, wi3, ki_new, cache_kidx[l])
        pos3, new_sel = _dsa_sample_select(scores3.reshape(db, past), snew3.reshape(db, LANES),
                                           topk=min(TOPK_MAX, (past + dseq) // 4))
        att_s = _dsa_sample_attn(pos3[:, :, 0], page_table, z_s[:, :aw].reshape(db, N_HEADS, HEAD_DIM),
                                 k_new, v_new, new_sel, rb_t, cache_k, cache_v, layer=l)
        hs1, vn_s = _mix(att_s.reshape(db, aw), z_s, hs, g_sgu[l], ws_first, bs_first, w_out_b,
                         first_rows_only=True, tm=db, tn=512, name="mix_sample")
        qx_s = _proj(hs1, g_x[l], w_xq_b, norm=True, tm=db, tn=xw, name="xq_proj_sample")
        hs2 = _cross(qx_s.reshape(db, 1, xw), cache_mem_k[l].reshape(db, mlen, xw),
                     cache_mem_v[l].reshape(db, mlen, xw), hs1.reshape(db, 1, d), w_xo_b,
                     tq=1, name="cross_sample").reshape(db, d)
        outs["ks"].append(k_new.reshape(db, dseq, N_KV_HEADS, HEAD_DIM))
        outs["vs"].append(v_new.reshape(db, dseq, N_KV_HEADS, HEAD_DIM))
        outs["is"].append(ki_new.reshape(db, dseq, IDX_DIM))
        outs["sv"].append(vn_s.reshape(db, dseq, gw))

        assert depth == 1
        packed = _pack_tables(peer_u[l], peer_v[l], eb=256)
        ids_p, gates_p = _peer_route(hp2, g_ffn[l], w_pq_t, sk_b, tb=256)
        y_p = _peer_gather(ids_p.T, gates_p.T, hp2, g_ffn[l], g_final, packed, g_tok=8)
        hs2_pad = jnp.pad(hs2, ((0, -db % LANES), (0, 0)))
        ids_s, gates_s = _peer_route(hs2_pad, g_ffn[l], w_pq_t, sk_b, tb=LANES)
        y_s = _peer_gather(ids_s.T[:db], gates_s.T[:db], hs2, g_ffn[l], g_final, packed, g_tok=8)

    y_prompt = y_p.reshape(batch, seq, d)
    y_sample = y_s.reshape(db, dseq, d)
    st = lambda n: jnp.stack(outs[n])
    return (y_prompt, y_sample, st("kp"), st("vp"), st("ip"), st("mkp"), st("mvp"),
            st("ks"), st("vs"), st("is"), st("sv"))
```

```python
import functools
import math

import jax
import jax.numpy as jnp
from jax import lax
from jax.experimental import pallas as pl
from jax.experimental.pallas import tpu as pltpu

F32 = jnp.float32
BF16 = jnp.bfloat16
I32 = jnp.int32

N_HEADS = 16
HEAD_DIM = 128
N_KV_HEADS = 4
GQA = N_HEADS // N_KV_HEADS
IDX_HEADS = 16
IDX_DIM = 64
TOPK_MAX = 256
NUM_BUCKETS = 32
MAX_DISTANCE = 128
CHUNK = 128
GMLP_GROUPS = 16
X_HEADS = 4
N_KEYS = 128
PEER_HEADS = 8
PEER_TOPK = 16
EPS = 1e-6

LANES = 128
SUBLANES = 8
VMEM_LIMIT_BYTES = 56 * 1024 * 1024

NEG = -1e30
INT_MIN = -(2 ** 31)


def _dot_nt(a, b):
    return lax.dot_general(a, b, (((1,), (1,)), ((), ())), preferred_element_type=F32)


def _dot(a, b):
    return jnp.dot(a, b, preferred_element_type=F32)


def _params(*sem):
    return pltpu.CompilerParams(dimension_semantics=sem, vmem_limit_bytes=VMEM_LIMIT_BYTES)


def _sortable(x):
    bits = pltpu.bitcast(x, I32)
    return bits ^ (jnp.right_shift(bits, 31) & 0x7FFFFFFF)


def _t5_bucket(dist):
    n = jnp.maximum(dist, 0)
    max_exact = NUM_BUCKETS // 2
    nf = jnp.maximum(n, 1).astype(F32)
    large = max_exact + (jnp.log(nf / max_exact) / math.log(MAX_DISTANCE / max_exact)
                         * (NUM_BUCKETS - max_exact)).astype(I32)
    large = jnp.minimum(large, NUM_BUCKETS - 1)
    return jnp.where(n < max_exact, n, large)


def _kth_largest_key(count_ge, shape, k):
    def body(it, t):
        cand = t | jnp.left_shift(jnp.int32(1), 31 - it)
        cnt = count_ge(cand ^ INT_MIN)
        return jnp.where(cnt >= k, cand, t)

    t = lax.fori_loop(0, 32, body, jnp.zeros(shape, I32))
    return t ^ INT_MIN


def _prefix_matrix():
    return (lax.broadcasted_iota(I32, (LANES, LANES), 0) <= lax.broadcasted_iota(I32, (LANES, LANES), 1)).astype(BF16)


def _proj_kernel(x_ref, g_ref, w_ref, o_ref, xn_ref, *, norm):
    @pl.when(pl.program_id(1) == 0)
    def _():
        x = x_ref[...]
        if norm:
            x = x * lax.rsqrt(jnp.mean(x * x, axis=-1, keepdims=True) + EPS) * g_ref[...]
        xn_ref[...] = x.astype(BF16)

    o_ref[...] = _dot(xn_ref[...], w_ref[...])


def _proj(x, g, w, *, norm, tm, tn, name):
    m, k = x.shape
    n = w.shape[1]
    tm, tn = min(tm, m), min(tn, n)
    return pl.pallas_call(
        functools.partial(_proj_kernel, norm=norm),
        grid=(m // tm, n // tn),
        in_specs=[pl.BlockSpec((tm, k), lambda i, j: (i, 0)),
                  pl.BlockSpec((1, k), lambda i, j: (0, 0)),
                  pl.BlockSpec((k, tn), lambda i, j: (0, j))],
        out_specs=pl.BlockSpec((tm, tn), lambda i, j: (i, j)),
        out_shape=jax.ShapeDtypeStruct((m, n), F32),
        scratch_shapes=[pltpu.VMEM((tm, k), BF16)],
        compiler_params=_params("arbitrary", "arbitrary"),
        name=name,
    )(x, g.reshape(1, k), w)


def _bias_tiles_kernel(rb_ref, o_ref):
    tq = lax.broadcasted_iota(I32, (LANES, LANES), 0)
    c = lax.broadcasted_iota(I32, (LANES, LANES), 1)
    for part, off in enumerate((2 * LANES, LANES, 0)):
        dist = tq - c + off
        bucket = _t5_bucket(dist)
        for h in range(N_HEADS):
            tile = lax.fori_loop(
                0, NUM_BUCKETS, lambda b, acc: jnp.where(bucket == b, rb_ref[b, h], acc),
                jnp.zeros((LANES, LANES), F32))
            if off == 0:
                tile = jnp.where(dist < 0, NEG, tile)
            o_ref[part, h] = tile


def _bias_tiles(rel_bias):
    return pl.pallas_call(
        _bias_tiles_kernel,
        in_specs=[pl.BlockSpec(memory_space=pltpu.SMEM)],
        out_specs=pl.BlockSpec(memory_space=pltpu.VMEM),
        out_shape=jax.ShapeDtypeStruct((3, N_HEADS, LANES, LANES), F32),
        name="t5_bias_tiles",
    )(rel_bias)


def _dsa_prompt_kernel(q_ref, k_ref, v_ref, qi_ref, zsk_ref, zsq_ref, bt_ref, tri_ref, o_ref,
                       kbf, vbf, kilo, kihi, key3, qs, m_s, l_s, acc_s, *, topk):
    i = pl.program_id(1)
    tq = q_ref.shape[1]
    s = k_ref.shape[1]
    nkb = s // LANES
    cw = 2 * LANES

    @pl.when(i == 0)
    def _():
        kbf[...] = k_ref[0].astype(BF16)
        vbf[...] = v_ref[0].astype(BF16)
        zs = zsk_ref[0]
        lane = lax.broadcasted_iota(I32, zs.shape, 1)
        kilo[...] = jnp.where(lane < IDX_DIM, zs, 0.0).astype(BF16)
        kihi[...] = jnp.where(lane >= IDX_DIM, pltpu.roll(zs, IDX_DIM, axis=1), 0.0).astype(BF16)

    qi = qi_ref[0].astype(BF16)
    coef = zsq_ref[0][:, IDX_DIM:IDX_DIM + IDX_HEADS] * (IDX_HEADS ** -0.5 * IDX_DIM ** -0.5)
    t_pos = i * tq + lax.broadcasted_iota(I32, (tq, cw), 0)
    t_last = i * tq + tq - 1
    for c in range(s // cw):
        @pl.when(c * cw <= t_last)
        def _():
            klo = kilo[c * cw:(c + 1) * cw, :]
            khi = kihi[c * cw:(c + 1) * cw, :]
            acc = jnp.zeros((tq, cw), F32)
            for p in range(IDX_HEADS // 2):
                a = qi[:, p * LANES:(p + 1) * LANES]
                acc = acc + coef[:, 2 * p:2 * p + 1] * jnp.maximum(_dot_nt(a, klo), 0.0)
                acc = acc + coef[:, 2 * p + 1:2 * p + 2] * jnp.maximum(_dot_nt(a, khi), 0.0)
            s_pos = c * cw + lax.broadcasted_iota(I32, (tq, cw), 1)
            key = _sortable(jnp.where(s_pos <= t_pos, acc, -jnp.inf))
            for u in range(cw // LANES):
                key3[c * (cw // LANES) + u] = key[:, u * LANES:(u + 1) * LANES]

        @pl.when(c * cw > t_last)
        def _():
            for u in range(cw // LANES):
                key3[c * (cw // LANES) + u] = jnp.full((tq, LANES), INT_MIN, I32)

    def count(pred):
        return jnp.sum(jnp.sum(jnp.where(pred, 1.0, 0.0), axis=0), axis=1, keepdims=True)

    thr = _kth_largest_key(lambda cs: count(key3[...] >= cs[None]), (tq, 1), float(topk))

    n_ge = count(key3[...] >= thr[None])

    @pl.when(jnp.max(n_ge) > float(topk))
    def _():
        need = float(topk) - count(key3[...] > thr[None])
        before = jnp.zeros((tq, 1), F32)
        for j in range(nkb):
            kj = key3[j]
            eq = kj == thr
            rank = _dot(jnp.where(eq, 1.0, 0.0).astype(BF16), tri_ref[...]) + before
            key3[j] = jnp.where(jnp.logical_and(eq, rank > need), INT_MIN, kj)
            before = rank[:, LANES - 1:LANES]

    thr4 = jnp.concatenate([thr] * GQA, axis=0)

    qb = q_ref[0].astype(BF16)
    for kh in range(N_KV_HEADS):
        qs[kh] = jnp.concatenate(
            [qb[:, (GQA * kh + g) * HEAD_DIM:(GQA * kh + g + 1) * HEAD_DIM] for g in range(GQA)], axis=0)
    m_s[...] = jnp.full(m_s.shape, NEG, F32)
    l_s[...] = jnp.zeros(l_s.shape, F32)
    acc_s[...] = jnp.zeros(acc_s.shape, F32)
    scale = HEAD_DIM ** -0.5

    def fbody(j, carry):
        part = jnp.clip(j - i + 2, 0, 2)
        key4 = jnp.concatenate([key3[j]] * GQA, axis=0)
        mask = key4 >= thr4
        r0 = pl.multiple_of(j * LANES, LANES)
        for kh in range(N_KV_HEADS):
            kj = kbf[pl.ds(r0, LANES), kh * HEAD_DIM:(kh + 1) * HEAD_DIM]
            vj = vbf[pl.ds(r0, LANES), kh * HEAD_DIM:(kh + 1) * HEAD_DIM]
            bias = bt_ref[part, GQA * kh:GQA * (kh + 1)].reshape(GQA * tq, LANES)
            lg = jnp.where(mask, _dot_nt(qs[kh], kj) * scale + bias, NEG)
            m_old = m_s[kh]
            m_new = jnp.maximum(m_old, jnp.max(lg, axis=1, keepdims=True))
            alpha = jnp.exp(m_old - m_new)
            p = jnp.where(mask, jnp.exp(lg - m_new), 0.0)
            l_s[kh] = alpha * l_s[kh] + jnp.sum(p, axis=1, keepdims=True)
            acc_s[kh] = alpha * acc_s[kh] + _dot(p.astype(BF16), vj)
            m_s[kh] = m_new
        return carry

    lax.fori_loop(0, i + 1, fbody, 0)
    for kh in range(N_KV_HEADS):
        out = acc_s[kh] / l_s[kh]
        for g in range(GQA):
            h = GQA * kh + g
            o_ref[0, :, h * HEAD_DIM:(h + 1) * HEAD_DIM] = out[g * tq:(g + 1) * tq]


def _dsa_prompt(z, zs, btiles, *, batch, seq, topk):
    tq = LANES
    aw = N_HEADS * HEAD_DIM
    kvw = N_KV_HEADS * HEAD_DIM
    qiw = IDX_HEADS * IDX_DIM
    return pl.pallas_call(
        functools.partial(_dsa_prompt_kernel, topk=topk),
        grid=(batch, seq // tq),
        in_specs=[
            pl.BlockSpec((1, tq, aw), lambda b, i: (b, i, 0)),
            pl.BlockSpec((1, seq, kvw), lambda b, i: (b, 0, aw // kvw)),
            pl.BlockSpec((1, seq, kvw), lambda b, i: (b, 0, aw // kvw + 1)),
            pl.BlockSpec((1, tq, qiw), lambda b, i: (b, i, (aw + 2 * kvw) // qiw)),
            pl.BlockSpec((1, seq, LANES), lambda b, i: (b, 0, 0)),
            pl.BlockSpec((1, tq, LANES), lambda b, i: (b, i, 0)),
            pl.BlockSpec((3, N_HEADS, LANES, LANES), lambda b, i: (0, 0, 0, 0)),
            pl.BlockSpec((LANES, LANES), lambda b, i: (0, 0)),
        ],
        out_specs=pl.BlockSpec((1, tq, aw), lambda b, i: (b, i, 0)),
        out_shape=jax.ShapeDtypeStruct((batch, seq, aw), F32),
        scratch_shapes=[
            pltpu.VMEM((seq, kvw), BF16), pltpu.VMEM((seq, kvw), BF16),
            pltpu.VMEM((seq, LANES), BF16), pltpu.VMEM((seq, LANES), BF16),
            pltpu.VMEM((seq // LANES, tq, LANES), I32),
            pltpu.VMEM((N_KV_HEADS, GQA * tq, HEAD_DIM), BF16),
            pltpu.VMEM((N_KV_HEADS, GQA * tq, LANES), F32),
            pltpu.VMEM((N_KV_HEADS, GQA * tq, LANES), F32),
            pltpu.VMEM((N_KV_HEADS, GQA * tq, HEAD_DIM), F32),
        ],
        compiler_params=_params("arbitrary", "arbitrary"),
        name="dsa_prompt",
    )(z, z, z, z, zs, zs, btiles, _prefix_matrix())


SAMPLE_PAGES_PER_STEP = 8


def _dsa_sample_score_kernel(pt_ref, qi_ref, wi_ref, kinew_ref, *rest):
    kidx_refs, (o_ref, onew_ref) = rest[:-2], rest[-2:]
    qi = qi_ref[0].astype(BF16)
    coef = wi_ref[0] * (IDX_HEADS ** -0.5 * IDX_DIM ** -0.5)

    def score(ki):
        return jnp.sum(jnp.maximum(_dot_nt(qi, ki.astype(BF16)), 0.0) * coef, axis=0, keepdims=True)

    o_ref[0] = jnp.concatenate([score(r[0]) for r in kidx_refs], axis=1)

    @pl.when(pl.program_id(1) == pl.num_programs(1) - 1)
    def _():
        onew_ref[0] = score(jnp.broadcast_to(kinew_ref[0], (LANES, kinew_ref.shape[2])))


def _dsa_sample_scores(page_table, qi3, wi3, ki_new, cache_kidx):
    db, n_pages = page_table.shape
    page = cache_kidx.shape[1]
    pg = math.gcd(SAMPLE_PAGES_PER_STEP, n_pages)
    row = lambda b, j, pt: (b, 0, 0)
    grid_spec = pltpu.PrefetchScalarGridSpec(
        num_scalar_prefetch=1,
        grid=(db, n_pages // pg),
        in_specs=[pl.BlockSpec((1, IDX_HEADS, IDX_DIM), row),
                  pl.BlockSpec((1, IDX_HEADS, 1), row),
                  pl.BlockSpec((1, 1, IDX_DIM), row)]
        + [pl.BlockSpec((1, page, IDX_DIM), functools.partial(lambda b, j, pt, u: (pt[b, j * pg + u], 0, 0), u=u))
           for u in range(pg)],
        out_specs=[pl.BlockSpec((1, 1, pg * page), lambda b, j, pt: (b, 0, j)),
                   pl.BlockSpec((1, 1, LANES), row)],
    )
    return pl.pallas_call(
        _dsa_sample_score_kernel,
        grid_spec=grid_spec,
        out_shape=[jax.ShapeDtypeStruct((db, 1, n_pages * page), F32),
                   jax.ShapeDtypeStruct((db, 1, LANES), F32)],
        compiler_params=_params("arbitrary", "arbitrary"),
        name="dsa_sample_scores",
    )(page_table, qi3, wi3, ki_new, *([cache_kidx] * pg))


def _dsa_sample_select_kernel(s_ref, snew_ref, tri_ref, pos_ref, new_ref, rank_s, nsel_s, *, topk):
    db, past = s_ref.shape
    k = float(topk)
    key = _sortable(s_ref[...])
    key_new = _sortable(snew_ref[:, 0:1])

    def count_ge(cs):
        return (jnp.sum(jnp.where(key >= cs, 1.0, 0.0), axis=1, keepdims=True)
                + jnp.where(key_new >= cs, 1.0, 0.0))

    thr = _kth_largest_key(count_ge, (db, 1), k)
    gt = key > thr
    eq = key == thr
    need = k - (jnp.sum(jnp.where(gt, 1.0, 0.0), axis=1, keepdims=True) + jnp.where(key_new > thr, 1.0, 0.0))

    tri = tri_ref[...]

    def cumsum(flags):
        out, off = [], jnp.zeros((db, 1), F32)
        for blk in range(past // LANES):
            pc = _dot(flags[:, blk * LANES:(blk + 1) * LANES].astype(BF16), tri) + off
            out.append(pc)
            off = pc[:, LANES - 1:LANES]
        return jnp.concatenate(out, axis=1), off

    eq_rank, eq_total = cumsum(jnp.where(eq, 1.0, 0.0))
    sel = jnp.where(jnp.logical_or(gt, jnp.logical_and(eq, eq_rank <= need)), 1.0, 0.0)
    new_sel = jnp.logical_or(key_new > thr, jnp.logical_and(key_new == thr, eq_total < need))
    new_ref[...] = jnp.broadcast_to(jnp.where(new_sel, 1.0, 0.0), new_ref.shape)
    rank, n_sel = cumsum(sel)
    rank = jnp.where(sel > 0.0, rank, 0.0)
    for b in range(db):
        rank_s[b] = rank[b:b + 1, :]
        nsel_s[b] = jnp.broadcast_to(n_sel[b:b + 1, :], (1, LANES))
    ch = min(8 * LANES, past)
    slot = lax.broadcasted_iota(I32, (topk, ch), 0).astype(F32) + 1.0
    lane = lax.broadcasted_iota(I32, (topk, ch), 1).astype(F32)
    slot_col = lax.broadcasted_iota(I32, (topk, 1), 0).astype(F32)

    def extract(b, carry):
        r = rank_s[b]
        pos = jnp.zeros((topk, 1), F32)
        for c in range(past // ch):
            hit = r[:, c * ch:(c + 1) * ch] == slot
            pos = pos + jnp.sum(jnp.where(hit, lane + float(c * ch), 0.0), axis=1, keepdims=True)
        pos = jnp.where(slot_col < nsel_s[b][:, 0:1], pos, -1.0)
        pos_ref[b] = jnp.broadcast_to(pos, (topk, LANES)).astype(I32)
        return carry

    lax.fori_loop(0, db, extract, 0)


def _dsa_sample_select(scores, snew, *, topk):
    db = scores.shape[0]
    tri = _prefix_matrix()
    return pl.pallas_call(
        functools.partial(_dsa_sample_select_kernel, topk=topk),
        out_shape=[jax.ShapeDtypeStruct((db, topk, LANES), I32), jax.ShapeDtypeStruct((db, LANES), F32)],
        scratch_shapes=[pltpu.VMEM((db, 1, scores.shape[1]), F32), pltpu.VMEM((db, 1, LANES), F32)],
        compiler_params=pltpu.CompilerParams(vmem_limit_bytes=VMEM_LIMIT_BYTES),
        name="dsa_sample_select",
    )(scores, snew, tri)


def _dsa_sample_attn_kernel(pos_s, pt_s, q_ref, knew_ref, vnew_ref, posrow_ref, newsel_ref, rbt_ref,
                            ck_hbm, cv_hbm, o_ref, kbuf, vbuf, sem_k, sem_v, *, layer, past, page):
    b = pl.program_id(0)
    topk = kbuf.shape[1]
    slot = lax.rem(b, 2)
    page_shift = page.bit_length() - 1
    unroll = 8

    def row_copies(bb, j, sl):
        p = jnp.maximum(pos_s[bb * topk + j], 0)
        pg = pt_s[bb, jnp.right_shift(p, page_shift)]
        off = p & (page - 1)
        return (pltpu.make_async_copy(ck_hbm.at[layer, pg, off], kbuf.at[sl, j], sem_k.at[sl]),
                pltpu.make_async_copy(cv_hbm.at[layer, pg, off], vbuf.at[sl, j], sem_v.at[sl]))

    def issue(bb, sl):
        def body(i, carry):
            for u in range(unroll):
                ck, cv = row_copies(bb, i * unroll + u, sl)
                ck.start()
                cv.start()
            return carry
        lax.fori_loop(0, topk // unroll, body, 0)

    def wait(sl):
        def body(i, carry):
            for u in range(unroll):
                j = i * unroll + u
                pltpu.make_async_copy(ck_hbm.at[layer, 0, 0], kbuf.at[sl, j], sem_k.at[sl]).wait()
                pltpu.make_async_copy(cv_hbm.at[layer, 0, 0], vbuf.at[sl, j], sem_v.at[sl]).wait()
            return carry
        lax.fori_loop(0, topk // unroll, body, 0)

    @pl.when(b == 0)
    def _():
        issue(0, 0)

    @pl.when(b + 1 < pl.num_programs(0))
    def _():
        issue(b + 1, 1 - slot)

    wait(slot)

    q = q_ref[0].astype(BF16)
    posr = posrow_ref[0]
    valid = posr >= 0
    bucket = _t5_bucket(past - posr)
    bias = jnp.zeros((N_HEADS, topk), F32)
    for bk in range(NUM_BUCKETS):
        bias = jnp.where(bucket == bk, rbt_ref[:, bk:bk + 1], bias)
    shift = GQA.bit_length() - 1
    kv_of_row = jnp.right_shift(lax.broadcasted_iota(I32, (N_HEADS, topk), 0), shift)
    kv_of_row_d = jnp.right_shift(lax.broadcasted_iota(I32, (N_HEADS, HEAD_DIM), 0), shift)
    scale = HEAD_DIM ** -0.5
    qf = q.astype(F32)
    knew = knew_ref[0].astype(BF16).astype(F32)
    vnew = vnew_ref[0].astype(BF16).astype(F32)
    lg = jnp.zeros((N_HEADS, topk), F32)
    lg_new = jnp.zeros((N_HEADS, 1), F32)
    v_new_rows = jnp.zeros((N_HEADS, HEAD_DIM), F32)
    for kh in range(N_KV_HEADS):
        hs = slice(kh * HEAD_DIM, (kh + 1) * HEAD_DIM)
        lg = jnp.where(kv_of_row == kh, _dot_nt(q, kbuf[slot, :, kh, :].astype(BF16)), lg)
        lg_new = jnp.where(kv_of_row[:, 0:1] == kh, jnp.sum(qf * knew[:, hs], axis=1, keepdims=True), lg_new)
        v_new_rows = jnp.where(kv_of_row_d == kh, jnp.broadcast_to(vnew[:, hs], (N_HEADS, HEAD_DIM)), v_new_rows)
    lg = jnp.where(valid, lg * scale + bias, NEG)
    new_sel = newsel_ref[0][:, 0:1] > 0.5
    lg_new = jnp.where(new_sel, lg_new * scale + rbt_ref[:, 0:1], NEG)
    m = jnp.maximum(jnp.max(lg, axis=1, keepdims=True), lg_new)
    p = jnp.where(valid, jnp.exp(lg - m), 0.0)
    p_new = jnp.where(new_sel, jnp.exp(lg_new - m), 0.0)
    denom = jnp.sum(p, axis=1, keepdims=True) + p_new
    pb = (p / denom).astype(BF16)
    out = (p_new / denom).astype(BF16).astype(F32) * v_new_rows
    for kh in range(N_KV_HEADS):
        out = out + jnp.where(kv_of_row_d == kh, _dot(pb, vbuf[slot, :, kh, :].astype(BF16)), 0.0)
    o_ref[0] = out


def _dsa_sample_attn(pos, page_table, q3, k_new, v_new, new_sel, rb_t, cache_k, cache_v, *, layer):
    db, topk = pos.shape
    n_pages = page_table.shape[1]
    page, n_kv, hd = cache_k.shape[2:]
    assert page & (page - 1) == 0
    kvw = n_kv * hd
    row = lambda b, ps, pt: (b, 0, 0)
    grid_spec = pltpu.PrefetchScalarGridSpec(
        num_scalar_prefetch=2,
        grid=(db,),
        in_specs=[
            pl.BlockSpec((1, N_HEADS, HEAD_DIM), row),
            pl.BlockSpec((1, 1, kvw), row),
            pl.BlockSpec((1, 1, kvw), row),
            pl.BlockSpec((1, 1, topk), row),
            pl.BlockSpec((1, 1, LANES), row),
            pl.BlockSpec((N_HEADS, NUM_BUCKETS), lambda b, ps, pt: (0, 0)),
            pl.BlockSpec(memory_space=pl.ANY),
            pl.BlockSpec(memory_space=pl.ANY),
        ],
        out_specs=pl.BlockSpec((1, N_HEADS, HEAD_DIM), row),
        scratch_shapes=[pltpu.VMEM((2, topk, n_kv, hd), F32), pltpu.VMEM((2, topk, n_kv, hd), F32),
                        pltpu.SemaphoreType.DMA((2,)), pltpu.SemaphoreType.DMA((2,))],
    )
    return pl.pallas_call(
        functools.partial(_dsa_sample_attn_kernel, layer=layer, past=n_pages * page, page=page),
        grid_spec=grid_spec,
        out_shape=jax.ShapeDtypeStruct((db, N_HEADS, HEAD_DIM), F32),
        compiler_params=_params("arbitrary"),
        name="dsa_sample_attn",
    )(pos.reshape(-1), page_table, q3, k_new, v_new, pos.reshape(db, 1, topk),
      new_sel.reshape(db, 1, LANES), rb_t, cache_k, cache_v)


def _gate_kernel(att_ref, zu_ref, zv_ref, ga_ref, gb_ref, gs_ref, ws_ref, bs_ref, m_ref, vn_ref, *, first_rows_only):
    aw = att_ref.shape[1]
    tm = zu_ref.shape[0]
    v = jax.nn.gelu(zv_ref[...])
    vn = v * lax.rsqrt(jnp.mean(v * v, axis=-1, keepdims=True) + EPS) * gs_ref[...]
    vn_ref[...] = vn
    m_ref[:, :aw] = (jax.nn.sigmoid(ga_ref[...]) * att_ref[...]).astype(BF16)
    if first_rows_only:
        sgu = jax.nn.gelu(zu_ref[...]) * (vn * ws_ref[...] + bs_ref[...])
        m_ref[:, aw:] = (jax.nn.sigmoid(gb_ref[...]) * sgu).astype(BF16)
    else:
        row = lax.broadcasted_iota(I32, (CHUNK, CHUNK), 0)
        col = lax.broadcasted_iota(I32, (CHUNK, CHUNK), 1)
        for g in range(GMLP_GROUPS):
            cs = slice(g * LANES, (g + 1) * LANES)
            wt = jnp.where(col <= row, ws_ref[g], 0.0).astype(BF16)
            bcol = bs_ref[:, g:g + 1]
            for c in range(tm // CHUNK):
                rs = slice(c * CHUNK, (c + 1) * CHUNK)
                mixed = _dot(wt, vn_ref[rs, cs].astype(BF16)) + bcol
                sgu = jax.nn.gelu(zu_ref[rs, cs]) * mixed
                m_ref[rs, aw + g * LANES:aw + (g + 1) * LANES] = (
                    jax.nn.sigmoid(gb_ref[rs, cs]) * sgu).astype(BF16)


def _gate(att, z, g_sgu, ws, bs, *, first_rows_only, tm, name):
    m, aw = att.shape
    gw = g_sgu.shape[0]
    tm = min(tm, m)
    zb = lambda c: pl.BlockSpec((tm, gw), lambda i: (i, c))
    full = lambda a: pl.BlockSpec(a.shape, lambda i: (0,) * a.ndim)
    first = z.shape[1] // gw - 4
    return pl.pallas_call(
        functools.partial(_gate_kernel, first_rows_only=first_rows_only),
        grid=(m // tm,),
        in_specs=[pl.BlockSpec((tm, aw), lambda i: (i, 0)),
                  zb(first), zb(first + 1), zb(first + 2), zb(first + 3),
                  pl.BlockSpec((1, gw), lambda i: (0, 0)), full(ws), full(bs)],
        out_specs=[pl.BlockSpec((tm, aw + gw), lambda i: (i, 0)), pl.BlockSpec((tm, gw), lambda i: (i, 0))],
        out_shape=[jax.ShapeDtypeStruct((m, aw + gw), BF16), jax.ShapeDtypeStruct((m, gw), F32)],
        compiler_params=_params("arbitrary"),
        name=name,
    )(att, z, z, z, z, g_sgu.reshape(1, gw), ws, bs)


def _matmul_residual_kernel(x_ref, w_ref, h_ref, o_ref):
    o_ref[...] = h_ref[...] + _dot(x_ref[...], w_ref[...])


def _matmul_residual(x, w, h, *, tm, tn, name):
    m, k = x.shape
    n = w.shape[1]
    tm, tn = min(tm, m), min(tn, n)
    return pl.pallas_call(
        _matmul_residual_kernel,
        grid=(m // tm, n // tn),
        in_specs=[pl.BlockSpec((tm, k), lambda i, j: (i, 0)),
                  pl.BlockSpec((k, tn), lambda i, j: (0, j)),
                  pl.BlockSpec((tm, tn), lambda i, j: (i, j))],
        out_specs=pl.BlockSpec((tm, tn), lambda i, j: (i, j)),
        out_shape=jax.ShapeDtypeStruct((m, n), F32),
        compiler_params=_params("arbitrary", "arbitrary"),
        name=name,
    )(x, w, h)


def _cross_kernel(q_ref, mk_ref, mv_ref, h_ref, wo_ref, o_ref):
    rows = q_ref.shape[1]
    q = q_ref[0]
    if rows < SUBLANES:
        q = jnp.broadcast_to(q, (SUBLANES, q.shape[1]))
    qb = q.astype(BF16)
    mk = mk_ref[0].astype(BF16)
    mv = mv_ref[0].astype(BF16)
    hd = mk.shape[1] // X_HEADS
    outs = []
    for hh in range(X_HEADS):
        sl = slice(hh * hd, (hh + 1) * hd)
        lg = _dot_nt(qb[:, sl], mk[:, sl]) * hd ** -0.5
        e = jnp.exp(lg - jnp.max(lg, axis=1, keepdims=True))
        p = e / jnp.sum(e, axis=1, keepdims=True)
        outs.append(_dot(p.astype(BF16), mv[:, sl]))
    y = _dot(jnp.concatenate(outs, axis=1).astype(BF16), wo_ref[...])
    o_ref[0] = h_ref[0] + y[:rows]


def _cross(q, mk, mv, h, w_xo, *, tq, name):
    b, t, xw = q.shape
    d = h.shape[2]
    mlen = mk.shape[1]
    return pl.pallas_call(
        _cross_kernel,
        grid=(b, t // tq),
        in_specs=[pl.BlockSpec((1, tq, xw), lambda bb, i: (bb, i, 0)),
                  pl.BlockSpec((1, mlen, xw), lambda bb, i: (bb, 0, 0)),
                  pl.BlockSpec((1, mlen, xw), lambda bb, i: (bb, 0, 0)),
                  pl.BlockSpec((1, tq, d), lambda bb, i: (bb, i, 0)),
                  pl.BlockSpec((xw, d), lambda bb, i: (0, 0))],
        out_specs=pl.BlockSpec((1, tq, d), lambda bb, i: (bb, i, 0)),
        out_shape=jax.ShapeDtypeStruct((b, t, d), F32),
        compiler_params=_params("arbitrary", "arbitrary"),
        name=name,
    )(q, mk, mv, h, w_xo)


def _top_rows(s, k, payload=None):
    r = s.shape[0]
    rid = lax.broadcasted_iota(I32, s.shape, 0).astype(F32)
    vals, picks = [], []
    for _ in range(k):
        m = jnp.max(s, axis=0, keepdims=True)
        am = jnp.min(jnp.where(s == m, rid, float(r)), axis=0, keepdims=True)
        hit = rid == am
        vals.append(m)
        picks.append(am if payload is None else jnp.sum(jnp.where(hit, payload, 0.0), axis=0, keepdims=True))
        s = jnp.where(hit, -jnp.inf, s)
    return jnp.concatenate(vals, axis=0), jnp.concatenate(picks, axis=0)


def _peer_route_kernel(h_ref, g_ref, w_ref, sk_ref, ids_ref, gate_ref, xn_ref):
    @pl.when(pl.program_id(1) == 0)
    def _():
        x = h_ref[...]
        xn_ref[...] = (x * lax.rsqrt(jnp.mean(x * x, axis=-1, keepdims=True) + EPS) * g_ref[...]).astype(BF16)

    half = sk_ref.shape[2]
    qt = _dot_nt(w_ref[...], xn_ref[...]).astype(BF16)
    v0, i0 = _top_rows(_dot(sk_ref[0], qt[:half]), PEER_TOPK)
    v1, i1 = _top_rows(_dot(sk_ref[1], qt[half:]), PEER_TOPK)
    cand = jnp.concatenate([v0[a:a + 1] + v1 for a in range(PEER_TOPK)], axis=0)
    eid = jnp.concatenate([i0[a:a + 1] * float(N_KEYS) + i1 for a in range(PEER_TOPK)], axis=0)
    sc, e = _top_rows(cand, PEER_TOPK, payload=eid)
    ex = jnp.exp(sc - sc[0:1])
    gate_ref[...] = ex / jnp.sum(ex, axis=0, keepdims=True)
    ids_ref[...] = e.astype(I32)


def _peer_route(h, g_ffn, w_pq_t, sub_keys, *, tb):
    m, d = h.shape
    qd = w_pq_t.shape[0] // PEER_HEADS
    return pl.pallas_call(
        _peer_route_kernel,
        grid=(m // tb, PEER_HEADS),
        in_specs=[pl.BlockSpec((tb, d), lambda t, hh: (t, 0)),
                  pl.BlockSpec((1, d), lambda t, hh: (0, 0)),
                  pl.BlockSpec((qd, d), lambda t, hh: (hh, 0)),
                  pl.BlockSpec(sub_keys.shape, lambda t, hh: (0, 0, 0))],
        out_specs=[pl.BlockSpec((PEER_TOPK, tb), lambda t, hh: (hh, t)),
                   pl.BlockSpec((PEER_TOPK, tb), lambda t, hh: (hh, t))],
        out_shape=[jax.ShapeDtypeStruct((PEER_HEADS * PEER_TOPK, m), I32),
                   jax.ShapeDtypeStruct((PEER_HEADS * PEER_TOPK, m), F32)],
        scratch_shapes=[pltpu.VMEM((tb, d), BF16)],
        compiler_params=_params("arbitrary", "arbitrary"),
        name="peer_route",
    )(h, g_ffn.reshape(1, d), w_pq_t, sub_keys)


def _pack_tables_kernel(u_ref, v_ref, o_ref):
    nr = o_ref.shape[1] // 2

    def words(x):
        bits = pltpu.bitcast(x.astype(BF16).astype(F32), I32)
        half = bits.shape[1] // 2
        return (bits[:, half:] & jnp.int32(-65536)) | lax.shift_right_logical(bits[:, :half], 16)

    wu = words(u_ref[...])
    wv = words(v_ref[...])
    for r in range(nr):
        o_ref[:, r, :] = wu[:, r * LANES:(r + 1) * LANES]
        o_ref[:, nr + r, :] = wv[:, r * LANES:(r + 1) * LANES]


def _pack_tables(u_tab, v_tab, *, eb):
    n, d = u_tab.shape
    nr = d // (2 * LANES)
    return pl.pallas_call(
        _pack_tables_kernel,
        grid=(n // eb,),
        in_specs=[pl.BlockSpec((eb, d), lambda i: (i, 0)), pl.BlockSpec((eb, d), lambda i: (i, 0))],
        out_specs=pl.BlockSpec((eb, 2 * nr, LANES), lambda i: (i, 0, 0)),
        out_shape=jax.ShapeDtypeStruct((n, 2 * nr, LANES), I32),
        compiler_params=_params("arbitrary"),
        name="peer_pack_tables",
    )(u_tab, v_tab)


def _unpack_words(w):
    return pltpu.bitcast(jnp.left_shift(w, 16), F32), pltpu.bitcast(w & jnp.int32(-65536), F32)


def _peer_gather_kernel(ids_hbm, gate_ref, h_ref, gffn_ref, gfin_ref, tab_hbm, o_ref,
                        ids_s, buf, part_s, sem_ids, sem_rows, *, n_steps):
    s = pl.program_id(0)
    g_tok = h_ref.shape[0]
    nr = buf.shape[1] // 2
    n_e = buf.shape[2]
    n_buf = buf.shape[0]
    ahead = n_buf - 1
    assert g_tok % n_buf == 0 and ahead <= g_tok
    cur = lax.rem(s, 2)
    nxt = 1 - cur

    def ids_copy(step, slot):
        return pltpu.make_async_copy(ids_hbm.at[step], ids_s.at[slot], sem_ids.at[slot])

    def row_copy(e, k, slot):
        return pltpu.make_async_copy(tab_hbm.at[e], buf.at[slot, :, k, :], sem_rows.at[slot])

    def issue(ids_slot, t, slot):
        for k in range(n_e):
            row_copy(ids_s[ids_slot, t, k], k, slot).start(priority=k % 2)

    def wait(slot):
        for k in range(n_e):
            row_copy(0, k, slot).wait()

    @pl.when(s == 0)
    def _():
        ids_copy(0, 0).start()
        ids_copy(0, 0).wait()
        for t in range(ahead):
            issue(0, t, t % n_buf)

    @pl.when(s + 1 < n_steps)
    def _():
        ids_copy(s + 1, nxt).start()

    h = h_ref[...]
    xn = h * lax.rsqrt(jnp.mean(h * h, axis=-1, keepdims=True) + EPS) * gffn_ref[...]
    eye = lax.broadcasted_iota(I32, (n_e, n_e), 0) == lax.broadcasted_iota(I32, (n_e, n_e), 1)

    for t in range(g_tok):
        slot = t % n_buf
        ta = t + ahead
        if ta < g_tok:
            issue(cur, ta, ta % n_buf)
        else:
            @pl.when(s + 1 < n_steps)
            def _():
                if ta == g_tok:
                    ids_copy(s + 1, nxt).wait()
                issue(nxt, ta - g_tok, ta % n_buf)
        wait(slot)
        x = xn[t:t + 1, :]
        hacc = jnp.zeros((n_e, LANES), F32)
        for r in range(nr):
            lo, hi = _unpack_words(buf[slot, r])
            hacc = hacc + lo * x[:, r * LANES:(r + 1) * LANES] + hi * x[:, (nr + r) * LANES:(nr + r + 1) * LANES]
        act = jax.nn.gelu(jnp.sum(hacc, axis=-1, keepdims=True))
        gcol = jnp.sum(jnp.where(eye, jnp.broadcast_to(gate_ref[t:t + 1, :], (n_e, n_e)), 0.0),
                       axis=-1, keepdims=True)
        w = jnp.broadcast_to(gcol * act, (n_e, LANES))
        for r in range(nr):
            lo, hi = _unpack_words(buf[slot, nr + r])
            part_s[t, :, r * LANES:(r + 1) * LANES] = jnp.sum(
                (lo * w).reshape(n_e // SUBLANES, SUBLANES, LANES), axis=0)
            part_s[t, :, (nr + r) * LANES:(nr + r + 1) * LANES] = jnp.sum(
                (hi * w).reshape(n_e // SUBLANES, SUBLANES, LANES), axis=0)

    y = h + jnp.sum(part_s[...], axis=1)
    o_ref[...] = y * lax.rsqrt(jnp.mean(y * y, axis=-1, keepdims=True) + EPS) * gfin_ref[...]


PEER_GATHER_BUFFERS = 4


def _peer_gather(ids, gates, h, g_ffn, g_final, packed_tab, *, g_tok):
    m, d = h.shape
    n_e = ids.shape[1]
    n_steps = m // g_tok
    n_rows = packed_tab.shape[1]
    ids3 = ids.reshape(n_steps, g_tok, n_e)
    return pl.pallas_call(
        functools.partial(_peer_gather_kernel, n_steps=n_steps),
        grid=(n_steps,),
        in_specs=[pl.BlockSpec(memory_space=pl.ANY),
                  pl.BlockSpec((g_tok, n_e), lambda s: (s, 0)),
                  pl.BlockSpec((g_tok, d), lambda s: (s, 0)),
                  pl.BlockSpec((1, d), lambda s: (0, 0)),
                  pl.BlockSpec((1, d), lambda s: (0, 0)),
                  pl.BlockSpec(memory_space=pl.ANY)],
        out_specs=pl.BlockSpec((g_tok, d), lambda s: (s, 0)),
        out_shape=jax.ShapeDtypeStruct((m, d), F32),
        scratch_shapes=[pltpu.SMEM((2, g_tok, n_e), I32),
                        pltpu.VMEM((PEER_GATHER_BUFFERS, n_rows, n_e, LANES), I32),
                        pltpu.VMEM((g_tok, SUBLANES, d), F32),
                        pltpu.SemaphoreType.DMA((2,)), pltpu.SemaphoreType.DMA((PEER_GATHER_BUFFERS,))],
        compiler_params=_params("arbitrary"),
        name="peer_gather",
    )(ids3, gates, h, g_ffn.reshape(1, d), g_final.reshape(1, d), packed_tab)


def kernel(x_prompt, x_sample, cache_k, cache_v, cache_kidx, cache_mem_k, cache_mem_v, page_table,
           mem_prompt, rel_bias, g_in, w_in, g_sgu, w_s, b_s, w_out, g_x, w_xq, w_xk, w_xv, w_xo,
           g_ffn, w_pq, sub_keys, peer_u, peer_v, g_final):
    batch, seq, d = x_prompt.shape
    db, dseq, _ = x_sample.shape
    depth = w_in.shape[0]
    assert dseq == 1, "sample group is one new token per sequence"
    n_pages = page_table.shape[1]
    page = cache_k.shape[2]
    past = n_pages * page
    aw, kvw, qiw, gw = N_HEADS * HEAD_DIM, N_KV_HEADS * HEAD_DIM, IDX_HEADS * IDX_DIM, GMLP_GROUPS * LANES
    xw = w_xq.shape[2]
    mlen = mem_prompt.shape[1]
    small_lo = aw + 2 * kvw + qiw
    small_hi = small_lo + IDX_DIM + IDX_HEADS
    assert NUM_BUCKETS // 2 + int(math.log((LANES + 1) / (NUM_BUCKETS // 2)) / math.log(
        MAX_DISTANCE / (NUM_BUCKETS // 2)) * (NUM_BUCKETS // 2)) >= NUM_BUCKETS - 1

    hp = x_prompt.reshape(batch * seq, d)
    hs = x_sample.reshape(db, d)
    btiles = _bias_tiles(rel_bias)
    rb_t = rel_bias.T
    outs = {n: [] for n in ("kp", "vp", "ip", "mkp", "mvp", "ks", "vs", "is", "sv")}

    for l in range(depth):
        w_main = jnp.concatenate([w_in[l][:, :small_lo].astype(BF16), w_in[l][:, small_hi:].astype(BF16)], axis=1)
        w_small = jnp.pad(w_in[l][:, small_lo:small_hi], ((0, 0), (0, LANES - (small_hi - small_lo)))).astype(BF16)
        w_out_b = w_out[l].astype(BF16)
        w_xq_b, w_xk_b, w_xv_b, w_xo_b = (w.astype(BF16) for w in (w_xq[l], w_xk[l], w_xv[l], w_xo[l]))
        w_pq_t = w_pq[l].T.astype(BF16)
        sk_b = sub_keys[l].astype(BF16)
        ws_first = jnp.repeat(w_s[l][:, 0, 0], LANES).reshape(1, gw)
        bs_first = jnp.repeat(b_s[l][:, 0], LANES).reshape(1, gw)
        ones_d = jnp.ones((d,), F32)

        z = _proj(hp, g_in[l], w_main, norm=True, tm=512, tn=1024, name="in_proj_prompt")
        zs = _proj(hp, g_in[l], w_small, norm=True, tm=512, tn=LANES, name="in_proj_idx_prompt")
        att = _dsa_prompt(z.reshape(batch, seq, -1), zs.reshape(batch, seq, LANES), btiles,
                          batch=batch, seq=seq, topk=min(TOPK_MAX, seq // 4))
        m_p, _ = _gate(att.reshape(batch * seq, aw), z, g_sgu[l], w_s[l], b_s[l].T,
                       first_rows_only=False, tm=256, name="gate_prompt")
        hp1 = _matmul_residual(m_p, w_out_b, hp, tm=1024, tn=1024, name="out_proj_prompt")
        mem = mem_prompt.reshape(batch * mlen, d)
        mk = _proj(mem, ones_d, w_xk_b, norm=False, tm=256, tn=xw, name="mem_k_proj")
        mv = _proj(mem, ones_d, w_xv_b, norm=False, tm=256, tn=xw, name="mem_v_proj")
        qx = _proj(hp1, g_x[l], w_xq_b, norm=True, tm=512, tn=xw, name="xq_proj_prompt")
        hp2 = _cross(qx.reshape(batch, seq, xw), mk.reshape(batch, mlen, xw), mv.reshape(batch, mlen, xw),
                     hp1.reshape(batch, seq, d), w_xo_b, tq=256, name="cross_prompt").reshape(batch * seq, d)
        outs["kp"].append(z[:, aw:aw + kvw].reshape(batch, seq, N_KV_HEADS, HEAD_DIM))
        outs["vp"].append(z[:, aw + kvw:aw + 2 * kvw].reshape(batch, seq, N_KV_HEADS, HEAD_DIM))
        outs["ip"].append(zs[:, :IDX_DIM].reshape(batch, seq, IDX_DIM))
        outs["mkp"].append(mk.reshape(batch, mlen, X_HEADS, xw // X_HEADS))
        outs["mvp"].append(mv.reshape(batch, mlen, X_HEADS, xw // X_HEADS))

        z_s = _proj(hs, g_in[l], w_main, norm=True, tm=db, tn=512, name="in_proj_sample")
        zs_s = _proj(hs, g_in[l], w_small, norm=True, tm=db, tn=LANES, name="in_proj_idx_sample")
        qi3 = z_s[:, aw + 2 * kvw:aw + 2 * kvw + qiw].reshape(db, IDX_HEADS, IDX_DIM)
        wi3 = zs_s[:, IDX_DIM:IDX_DIM + IDX_HEADS].reshape(db, IDX_HEADS, 1)
        ki_new = zs_s[:, :IDX_DIM].reshape(db, 1, IDX_DIM)
        k_new = z_s[:, aw:aw + kvw].reshape(db, 1, kvw)
        v_new = z_s[:, aw + kvw:aw + 2 * kvw].reshape(db, 1, kvw)
        scores3, snew3 = _dsa_sample_scores(page_table, qi3, wi3, ki_new, cache_kidx[l])
        pos3, new_sel = _dsa_sample_select(scores3.reshape(db, past), snew3.reshape(db, LANES),
                                           topk=min(TOPK_MAX, (past + dseq) // 4))
        att_s = _dsa_sample_attn(pos3[:, :, 0], page_table, z_s[:, :aw].reshape(db, N_HEADS, HEAD_DIM),
                                 k_new, v_new, new_sel, rb_t, cache_k, cache_v, layer=l)
        m_s, vn_s = _gate(att_s.reshape(db, aw), z_s, g_sgu[l], ws_first, bs_first,
                          first_rows_only=True, tm=db, name="gate_sample")
        hs1 = _matmul_residual(m_s, w_out_b, hs, tm=db, tn=1024, name="out_proj_sample")
        qx_s = _proj(hs1, g_x[l], w_xq_b, norm=True, tm=db, tn=xw, name="xq_proj_sample")
        hs2 = _cross(qx_s.reshape(db, 1, xw), cache_mem_k[l].reshape(db, mlen, xw),
                     cache_mem_v[l].reshape(db, mlen, xw), hs1.reshape(db, 1, d), w_xo_b,
                     tq=1, name="cross_sample").reshape(db, d)
        outs["ks"].append(k_new.reshape(db, dseq, N_KV_HEADS, HEAD_DIM))
        outs["vs"].append(v_new.reshape(db, dseq, N_KV_HEADS, HEAD_DIM))
        outs["is"].append(ki_new.reshape(db, dseq, IDX_DIM))
        outs["sv"].append(vn_s.reshape(db, dseq, gw))

        assert depth == 1
        packed = _pack_tables(peer_u[l], peer_v[l], eb=256)
        ids_p, gates_p = _peer_route(hp2, g_ffn[l], w_pq_t, sk_b, tb=256)
        y_p = _peer_gather(ids_p.T, gates_p.T, hp2, g_ffn[l], g_final, packed, g_tok=8)
        hs2_pad = jnp.pad(hs2, ((0, -db % LANES), (0, 0)))
        ids_s, gates_s = _peer_route(hs2_pad, g_ffn[l], w_pq_t, sk_b, tb=LANES)
        y_s = _peer_gather(ids_s.T[:db], gates_s.T[:db], hs2, g_ffn[l], g_final, packed, g_tok=8)

    y_prompt = y_p.reshape(batch, seq, d)
    y_sample = y_s.reshape(db, dseq, d)
    st = lambda n: jnp.stack(outs[n])
    return (y_prompt, y_sample, st("kp"), st("vp"), st("ip"), st("mkp"), st("mvp"),
            st("ks"), st("vs"), st("is"), st("sv"))
```

```python
import functools
import math

import jax
import jax.numpy as jnp
from jax import lax
from jax.experimental import pallas as pl
from jax.experimental.pallas import tpu as pltpu

F32 = jnp.float32
BF16 = jnp.bfloat16
I32 = jnp.int32

N_HEADS = 16
HEAD_DIM = 128
N_KV_HEADS = 4
GQA = N_HEADS // N_KV_HEADS
IDX_HEADS = 16
IDX_DIM = 64
TOPK_MAX = 256
NUM_BUCKETS = 32
MAX_DISTANCE = 128
CHUNK = 128
GMLP_GROUPS = 16
X_HEADS = 4
N_KEYS = 128
PEER_HEADS = 8
PEER_TOPK = 16
EPS = 1e-6

LANES = 128
SUBLANES = 8
VMEM_LIMIT_BYTES = 56 * 1024 * 1024

NEG = -1e30
INT_MIN = -(2 ** 31)


def _dot_nt(a, b):
    return lax.dot_general(a, b, (((1,), (1,)), ((), ())), preferred_element_type=F32)


def _dot(a, b):
    return jnp.dot(a, b, preferred_element_type=F32)


def _params(*sem):
    return pltpu.CompilerParams(dimension_semantics=sem, vmem_limit_bytes=VMEM_LIMIT_BYTES)


def _sortable(x):
    bits = pltpu.bitcast(x, I32)
    return bits ^ (jnp.right_shift(bits, 31) & 0x7FFFFFFF)


def _t5_bucket(dist):
    n = jnp.maximum(dist, 0)
    max_exact = NUM_BUCKETS // 2
    nf = jnp.maximum(n, 1).astype(F32)
    large = max_exact + (jnp.log(nf / max_exact) / math.log(MAX_DISTANCE / max_exact)
                         * (NUM_BUCKETS - max_exact)).astype(I32)
    large = jnp.minimum(large, NUM_BUCKETS - 1)
    return jnp.where(n < max_exact, n, large)


def _kth_largest_key(count_ge, shape, k):
    def body(it, t):
        cand = t | jnp.left_shift(jnp.int32(1), 31 - it)
        cnt = count_ge(cand ^ INT_MIN)
        return jnp.where(cnt >= k, cand, t)

    t = lax.fori_loop(0, 32, body, jnp.zeros(shape, I32))
    return t ^ INT_MIN


def _prefix_matrix():
    return (lax.broadcasted_iota(I32, (LANES, LANES), 0) <= lax.broadcasted_iota(I32, (LANES, LANES), 1)).astype(BF16)


def _proj_kernel(x_ref, g_ref, w_ref, o_ref, xn_ref, *, norm):
    @pl.when(pl.program_id(1) == 0)
    def _():
        x = x_ref[...]
        if norm:
            x = x * lax.rsqrt(jnp.mean(x * x, axis=-1, keepdims=True) + EPS) * g_ref[...]
        xn_ref[...] = x.astype(BF16)

    o_ref[...] = _dot(xn_ref[...], w_ref[...])


def _proj(x, g, w, *, norm, tm, tn, name):
    m, k = x.shape
    n = w.shape[1]
    tm, tn = min(tm, m), min(tn, n)
    return pl.pallas_call(
        functools.partial(_proj_kernel, norm=norm),
        grid=(m // tm, n // tn),
        in_specs=[pl.BlockSpec((tm, k), lambda i, j: (i, 0)),
                  pl.BlockSpec((1, k), lambda i, j: (0, 0)),
                  pl.BlockSpec((k, tn), lambda i, j: (0, j))],
        out_specs=pl.BlockSpec((tm, tn), lambda i, j: (i, j)),
        out_shape=jax.ShapeDtypeStruct((m, n), F32),
        scratch_shapes=[pltpu.VMEM((tm, k), BF16)],
        compiler_params=_params("arbitrary", "arbitrary"),
        name=name,
    )(x, g.reshape(1, k), w)


def _bias_tiles_kernel(rb_ref, o_ref):
    tq = lax.broadcasted_iota(I32, (LANES, LANES), 0)
    c = lax.broadcasted_iota(I32, (LANES, LANES), 1)
    for part, off in enumerate((2 * LANES, LANES, 0)):
        dist = tq - c + off
        bucket = _t5_bucket(dist)
        for h in range(N_HEADS):
            tile = lax.fori_loop(
                0, NUM_BUCKETS, lambda b, acc: jnp.where(bucket == b, rb_ref[b, h], acc),
                jnp.zeros((LANES, LANES), F32))
            if off == 0:
                tile = jnp.where(dist < 0, NEG, tile)
            o_ref[part, h] = tile


def _bias_tiles(rel_bias):
    return pl.pallas_call(
        _bias_tiles_kernel,
        in_specs=[pl.BlockSpec(memory_space=pltpu.SMEM)],
        out_specs=pl.BlockSpec(memory_space=pltpu.VMEM),
        out_shape=jax.ShapeDtypeStruct((3, N_HEADS, LANES, LANES), F32),
        name="t5_bias_tiles",
    )(rel_bias)


def _dsa_prompt_kernel(q_ref, k_ref, v_ref, qi_ref, zsk_ref, zsq_ref, bt_ref, tri_ref, o_ref,
                       kbf, vbf, kilo, kihi, key3, qs, m_s, l_s, acc_s, *, topk):
    i = pl.program_id(1)
    tq = q_ref.shape[1]
    s = k_ref.shape[1]
    nkb = s // LANES
    cw = 2 * LANES

    @pl.when(i == 0)
    def _():
        kbf[...] = k_ref[0].astype(BF16)
        vbf[...] = v_ref[0].astype(BF16)
        zs = zsk_ref[0]
        lane = lax.broadcasted_iota(I32, zs.shape, 1)
        kilo[...] = jnp.where(lane < IDX_DIM, zs, 0.0).astype(BF16)
        kihi[...] = jnp.where(lane >= IDX_DIM, pltpu.roll(zs, IDX_DIM, axis=1), 0.0).astype(BF16)

    qi = qi_ref[0].astype(BF16)
    coef = zsq_ref[0][:, IDX_DIM:IDX_DIM + IDX_HEADS] * (IDX_HEADS ** -0.5 * IDX_DIM ** -0.5)
    t_pos = i * tq + lax.broadcasted_iota(I32, (tq, cw), 0)
    t_last = i * tq + tq - 1
    for c in range(s // cw):
        @pl.when(c * cw <= t_last)
        def _():
            klo = kilo[c * cw:(c + 1) * cw, :]
            khi = kihi[c * cw:(c + 1) * cw, :]
            acc = jnp.zeros((tq, cw), F32)
            for p in range(IDX_HEADS // 2):
                a = qi[:, p * LANES:(p + 1) * LANES]
                acc = acc + coef[:, 2 * p:2 * p + 1] * jnp.maximum(_dot_nt(a, klo), 0.0)
                acc = acc + coef[:, 2 * p + 1:2 * p + 2] * jnp.maximum(_dot_nt(a, khi), 0.0)
            s_pos = c * cw + lax.broadcasted_iota(I32, (tq, cw), 1)
            key = _sortable(jnp.where(s_pos <= t_pos, acc, -jnp.inf))
            for u in range(cw // LANES):
                key3[c * (cw // LANES) + u] = key[:, u * LANES:(u + 1) * LANES]

        @pl.when(c * cw > t_last)
        def _():
            for u in range(cw // LANES):
                key3[c * (cw // LANES) + u] = jnp.full((tq, LANES), INT_MIN, I32)

    def count(pred):
        return jnp.sum(jnp.sum(jnp.where(pred, 1.0, 0.0), axis=0), axis=1, keepdims=True)

    thr = _kth_largest_key(lambda cs: count(key3[...] >= cs[None]), (tq, 1), float(topk))

    n_ge = count(key3[...] >= thr[None])

    @pl.when(jnp.max(n_ge) > float(topk))
    def _():
        need = float(topk) - count(key3[...] > thr[None])
        before = jnp.zeros((tq, 1), F32)
        for j in range(nkb):
            kj = key3[j]
            eq = kj == thr
            rank = _dot(jnp.where(eq, 1.0, 0.0).astype(BF16), tri_ref[...]) + before
            key3[j] = jnp.where(jnp.logical_and(eq, rank > need), INT_MIN, kj)
            before = rank[:, LANES - 1:LANES]

    thr4 = jnp.concatenate([thr] * GQA, axis=0)

    qb = q_ref[0].astype(BF16)
    for kh in range(N_KV_HEADS):
        qs[kh] = jnp.concatenate(
            [qb[:, (GQA * kh + g) * HEAD_DIM:(GQA * kh + g + 1) * HEAD_DIM] for g in range(GQA)], axis=0)
    m_s[...] = jnp.full(m_s.shape, NEG, F32)
    l_s[...] = jnp.zeros(l_s.shape, F32)
    acc_s[...] = jnp.zeros(acc_s.shape, F32)
    scale = HEAD_DIM ** -0.5

    def fbody(jj, carry):
        ja = 2 * jj
        jb = ja + 1
        part_a = jnp.clip(ja - i + 2, 0, 2)
        part_b = jnp.clip(jb - i + 2, 0, 2)
        key_b = jnp.where(jb <= i, key3[jb], INT_MIN)
        key8 = jnp.concatenate([key3[ja], key_b], axis=1)
        mask = jnp.concatenate([key8] * GQA, axis=0) >= thr4
        r0 = pl.multiple_of(ja * LANES, 2 * LANES)
        for kh in range(N_KV_HEADS):
            kj = kbf[pl.ds(r0, 2 * LANES), kh * HEAD_DIM:(kh + 1) * HEAD_DIM]
            vj = vbf[pl.ds(r0, 2 * LANES), kh * HEAD_DIM:(kh + 1) * HEAD_DIM]
            bias = jnp.concatenate(
                [bt_ref[part_a, GQA * kh:GQA * (kh + 1)].reshape(GQA * tq, LANES),
                 bt_ref[part_b, GQA * kh:GQA * (kh + 1)].reshape(GQA * tq, LANES)], axis=1)
            lg = jnp.where(mask, _dot_nt(qs[kh], kj) * scale + bias, NEG)
            m_old = m_s[kh]
            m_new = jnp.maximum(m_old, jnp.max(lg, axis=1, keepdims=True))
            alpha = jnp.exp(m_old - m_new)
            p = jnp.where(mask, jnp.exp(lg - jnp.concatenate([m_new, m_new], axis=1)), 0.0)
            l_s[kh] = alpha * l_s[kh] + jnp.sum(p, axis=1, keepdims=True)
            acc_s[kh] = alpha * acc_s[kh] + _dot(p.astype(BF16), vj)
            m_s[kh] = m_new
        return carry

    assert nkb % 2 == 0
    lax.fori_loop(0, jnp.right_shift(i + 2, 1), fbody, 0)
    for kh in range(N_KV_HEADS):
        out = acc_s[kh] / l_s[kh]
        for g in range(GQA):
            h = GQA * kh + g
            o_ref[0, :, h * HEAD_DIM:(h + 1) * HEAD_DIM] = out[g * tq:(g + 1) * tq]


def _dsa_prompt(z, zs, btiles, *, batch, seq, topk):
    tq = LANES
    aw = N_HEADS * HEAD_DIM
    kvw = N_KV_HEADS * HEAD_DIM
    qiw = IDX_HEADS * IDX_DIM
    return pl.pallas_call(
        functools.partial(_dsa_prompt_kernel, topk=topk),
        grid=(batch, seq // tq),
        in_specs=[
            pl.BlockSpec((1, tq, aw), lambda b, i: (b, i, 0)),
            pl.BlockSpec((1, seq, kvw), lambda b, i: (b, 0, aw // kvw)),
            pl.BlockSpec((1, seq, kvw), lambda b, i: (b, 0, aw // kvw + 1)),
            pl.BlockSpec((1, tq, qiw), lambda b, i: (b, i, (aw + 2 * kvw) // qiw)),
            pl.BlockSpec((1, seq, LANES), lambda b, i: (b, 0, 0)),
            pl.BlockSpec((1, tq, LANES), lambda b, i: (b, i, 0)),
            pl.BlockSpec((3, N_HEADS, LANES, LANES), lambda b, i: (0, 0, 0, 0)),
            pl.BlockSpec((LANES, LANES), lambda b, i: (0, 0)),
        ],
        out_specs=pl.BlockSpec((1, tq, aw), lambda b, i: (b, i, 0)),
        out_shape=jax.ShapeDtypeStruct((batch, seq, aw), F32),
        scratch_shapes=[
            pltpu.VMEM((seq, kvw), BF16), pltpu.VMEM((seq, kvw), BF16),
            pltpu.VMEM((seq, LANES), BF16), pltpu.VMEM((seq, LANES), BF16),
            pltpu.VMEM((seq // LANES, tq, LANES), I32),
            pltpu.VMEM((N_KV_HEADS, GQA * tq, HEAD_DIM), BF16),
            pltpu.VMEM((N_KV_HEADS, GQA * tq, LANES), F32),
            pltpu.VMEM((N_KV_HEADS, GQA * tq, LANES), F32),
            pltpu.VMEM((N_KV_HEADS, GQA * tq, HEAD_DIM), F32),
        ],
        compiler_params=_params("arbitrary", "arbitrary"),
        name="dsa_prompt",
    )(z, z, z, z, zs, zs, btiles, _prefix_matrix())


SAMPLE_PAGES_PER_STEP = 8


def _dsa_sample_score_kernel(pt_ref, qi_ref, wi_ref, kinew_ref, *rest):
    kidx_refs, (o_ref, onew_ref) = rest[:-2], rest[-2:]
    qi = qi_ref[0].astype(BF16)
    coef = wi_ref[0] * (IDX_HEADS ** -0.5 * IDX_DIM ** -0.5)

    def score(ki):
        return jnp.sum(jnp.maximum(_dot_nt(qi, ki.astype(BF16)), 0.0) * coef, axis=0, keepdims=True)

    o_ref[0] = jnp.concatenate([score(r[0]) for r in kidx_refs], axis=1)

    @pl.when(pl.program_id(1) == pl.num_programs(1) - 1)
    def _():
        onew_ref[0] = score(jnp.broadcast_to(kinew_ref[0], (LANES, kinew_ref.shape[2])))


def _dsa_sample_scores(page_table, qi3, wi3, ki_new, cache_kidx):
    db, n_pages = page_table.shape
    page = cache_kidx.shape[1]
    pg = math.gcd(SAMPLE_PAGES_PER_STEP, n_pages)
    row = lambda b, j, pt: (b, 0, 0)
    grid_spec = pltpu.PrefetchScalarGridSpec(
        num_scalar_prefetch=1,
        grid=(db, n_pages // pg),
        in_specs=[pl.BlockSpec((1, IDX_HEADS, IDX_DIM), row),
                  pl.BlockSpec((1, IDX_HEADS, 1), row),
                  pl.BlockSpec((1, 1, IDX_DIM), row)]
        + [pl.BlockSpec((1, page, IDX_DIM), functools.partial(lambda b, j, pt, u: (pt[b, j * pg + u], 0, 0), u=u))
           for u in range(pg)],
        out_specs=[pl.BlockSpec((1, 1, pg * page), lambda b, j, pt: (b, 0, j)),
                   pl.BlockSpec((1, 1, LANES), row)],
    )
    return pl.pallas_call(
        _dsa_sample_score_kernel,
        grid_spec=grid_spec,
        out_shape=[jax.ShapeDtypeStruct((db, 1, n_pages * page), F32),
                   jax.ShapeDtypeStruct((db, 1, LANES), F32)],
        compiler_params=_params("arbitrary", "arbitrary"),
        name="dsa_sample_scores",
    )(page_table, qi3, wi3, ki_new, *([cache_kidx] * pg))


def _dsa_sample_select_kernel(s_ref, snew_ref, tri_ref, pos_ref, new_ref, rank_s, nsel_s, *, topk):
    db, past = s_ref.shape
    k = float(topk)
    key = _sortable(s_ref[...])
    key_new = _sortable(snew_ref[:, 0:1])

    def count_ge(cs):
        return (jnp.sum(jnp.where(key >= cs, 1.0, 0.0), axis=1, keepdims=True)
                + jnp.where(key_new >= cs, 1.0, 0.0))

    thr = _kth_largest_key(count_ge, (db, 1), k)
    gt = key > thr
    eq = key == thr
    need = k - (jnp.sum(jnp.where(gt, 1.0, 0.0), axis=1, keepdims=True) + jnp.where(key_new > thr, 1.0, 0.0))

    tri = tri_ref[...]

    def cumsum(flags):
        out, off = [], jnp.zeros((db, 1), F32)
        for blk in range(past // LANES):
            pc = _dot(flags[:, blk * LANES:(blk + 1) * LANES].astype(BF16), tri) + off
            out.append(pc)
            off = pc[:, LANES - 1:LANES]
        return jnp.concatenate(out, axis=1), off

    eq_rank, eq_total = cumsum(jnp.where(eq, 1.0, 0.0))
    sel = jnp.where(jnp.logical_or(gt, jnp.logical_and(eq, eq_rank <= need)), 1.0, 0.0)
    new_sel = jnp.logical_or(key_new > thr, jnp.logical_and(key_new == thr, eq_total < need))
    new_ref[...] = jnp.broadcast_to(jnp.where(new_sel, 1.0, 0.0), new_ref.shape)
    rank, n_sel = cumsum(sel)
    rank = jnp.where(sel > 0.0, rank, 0.0)
    for b in range(db):
        rank_s[b] = rank[b:b + 1, :]
        nsel_s[b] = jnp.broadcast_to(n_sel[b:b + 1, :], (1, LANES))
    ch = min(8 * LANES, past)
    slot = lax.broadcasted_iota(I32, (topk, ch), 0).astype(F32) + 1.0
    lane = lax.broadcasted_iota(I32, (topk, ch), 1).astype(F32)
    slot_col = lax.broadcasted_iota(I32, (topk, 1), 0).astype(F32)

    def extract(b, carry):
        r = rank_s[b]
        pos = jnp.zeros((topk, 1), F32)
        for c in range(past // ch):
            hit = r[:, c * ch:(c + 1) * ch] == slot
            pos = pos + jnp.sum(jnp.where(hit, lane + float(c * ch), 0.0), axis=1, keepdims=True)
        pos = jnp.where(slot_col < nsel_s[b][:, 0:1], pos, -1.0)
        pos_ref[b] = jnp.broadcast_to(pos, (topk, LANES)).astype(I32)
        return carry

    lax.fori_loop(0, db, extract, 0)


def _dsa_sample_select(scores, snew, *, topk):
    db = scores.shape[0]
    tri = _prefix_matrix()
    return pl.pallas_call(
        functools.partial(_dsa_sample_select_kernel, topk=topk),
        out_shape=[jax.ShapeDtypeStruct((db, topk, LANES), I32), jax.ShapeDtypeStruct((db, LANES), F32)],
        scratch_shapes=[pltpu.VMEM((db, 1, scores.shape[1]), F32), pltpu.VMEM((db, 1, LANES), F32)],
        compiler_params=pltpu.CompilerParams(vmem_limit_bytes=VMEM_LIMIT_BYTES),
        name="dsa_sample_select",
    )(scores, snew, tri)


def _dsa_sample_attn_kernel(pos_s, pt_s, q_ref, knew_ref, vnew_ref, posrow_ref, newsel_ref, rbt_ref,
                            ck_hbm, cv_hbm, o_ref, kbuf, vbuf, sem_k, sem_v, *, layer, past, page):
    b = pl.program_id(0)
    topk = kbuf.shape[1]
    slot = lax.rem(b, 2)
    page_shift = page.bit_length() - 1
    unroll = 8

    def row_copies(bb, j, sl):
        p = jnp.maximum(pos_s[bb * topk + j], 0)
        pg = pt_s[bb, jnp.right_shift(p, page_shift)]
        off = p & (page - 1)
        return (pltpu.make_async_copy(ck_hbm.at[layer, pg, off], kbuf.at[sl, j], sem_k.at[sl]),
                pltpu.make_async_copy(cv_hbm.at[layer, pg, off], vbuf.at[sl, j], sem_v.at[sl]))

    def issue(bb, sl):
        def body(i, carry):
            for u in range(unroll):
                ck, cv = row_copies(bb, i * unroll + u, sl)
                ck.start()
                cv.start()
            return carry
        lax.fori_loop(0, topk // unroll, body, 0)

    def wait(sl):
        def body(i, carry):
            for u in range(unroll):
                j = i * unroll + u
                pltpu.make_async_copy(ck_hbm.at[layer, 0, 0], kbuf.at[sl, j], sem_k.at[sl]).wait()
                pltpu.make_async_copy(cv_hbm.at[layer, 0, 0], vbuf.at[sl, j], sem_v.at[sl]).wait()
            return carry
        lax.fori_loop(0, topk // unroll, body, 0)

    @pl.when(b == 0)
    def _():
        issue(0, 0)

    @pl.when(b + 1 < pl.num_programs(0))
    def _():
        issue(b + 1, 1 - slot)

    wait(slot)

    q = q_ref[0].astype(BF16)
    posr = posrow_ref[0]
    valid = posr >= 0
    bucket = _t5_bucket(past - posr)
    bias = jnp.zeros((N_HEADS, topk), F32)
    for bk in range(NUM_BUCKETS):
        bias = jnp.where(bucket == bk, rbt_ref[:, bk:bk + 1], bias)
    shift = GQA.bit_length() - 1
    kv_of_row = jnp.right_shift(lax.broadcasted_iota(I32, (N_HEADS, topk), 0), shift)
    kv_of_row_d = jnp.right_shift(lax.broadcasted_iota(I32, (N_HEADS, HEAD_DIM), 0), shift)
    scale = HEAD_DIM ** -0.5
    qf = q.astype(F32)
    knew = knew_ref[0].astype(BF16).astype(F32)
    vnew = vnew_ref[0].astype(BF16).astype(F32)
    lg = jnp.zeros((N_HEADS, topk), F32)
    lg_new = jnp.zeros((N_HEADS, 1), F32)
    v_new_rows = jnp.zeros((N_HEADS, HEAD_DIM), F32)
    for kh in range(N_KV_HEADS):
        hs = slice(kh * HEAD_DIM, (kh + 1) * HEAD_DIM)
        lg = jnp.where(kv_of_row == kh, _dot_nt(q, kbuf[slot, :, kh, :].astype(BF16)), lg)
        lg_new = jnp.where(kv_of_row[:, 0:1] == kh, jnp.sum(qf * knew[:, hs], axis=1, keepdims=True), lg_new)
        v_new_rows = jnp.where(kv_of_row_d == kh, jnp.broadcast_to(vnew[:, hs], (N_HEADS, HEAD_DIM)), v_new_rows)
    lg = jnp.where(valid, lg * scale + bias, NEG)
    new_sel = newsel_ref[0][:, 0:1] > 0.5
    lg_new = jnp.where(new_sel, lg_new * scale + rbt_ref[:, 0:1], NEG)
    m = jnp.maximum(jnp.max(lg, axis=1, keepdims=True), lg_new)
    p = jnp.where(valid, jnp.exp(lg - m), 0.0)
    p_new = jnp.where(new_sel, jnp.exp(lg_new - m), 0.0)
    denom = jnp.sum(p, axis=1, keepdims=True) + p_new
    pb = (p / denom).astype(BF16)
    out = (p_new / denom).astype(BF16).astype(F32) * v_new_rows
    for kh in range(N_KV_HEADS):
        out = out + jnp.where(kv_of_row_d == kh, _dot(pb, vbuf[slot, :, kh, :].astype(BF16)), 0.0)
    o_ref[0] = out


def _dsa_sample_attn(pos, page_table, q3, k_new, v_new, new_sel, rb_t, cache_k, cache_v, *, layer):
    db, topk = pos.shape
    n_pages = page_table.shape[1]
    page, n_kv, hd = cache_k.shape[2:]
    assert page & (page - 1) == 0
    kvw = n_kv * hd
    row = lambda b, ps, pt: (b, 0, 0)
    grid_spec = pltpu.PrefetchScalarGridSpec(
        num_scalar_prefetch=2,
        grid=(db,),
        in_specs=[
            pl.BlockSpec((1, N_HEADS, HEAD_DIM), row),
            pl.BlockSpec((1, 1, kvw), row),
            pl.BlockSpec((1, 1, kvw), row),
            pl.BlockSpec((1, 1, topk), row),
            pl.BlockSpec((1, 1, LANES), row),
            pl.BlockSpec((N_HEADS, NUM_BUCKETS), lambda b, ps, pt: (0, 0)),
            pl.BlockSpec(memory_space=pl.ANY),
            pl.BlockSpec(memory_space=pl.ANY),
        ],
        out_specs=pl.BlockSpec((1, N_HEADS, HEAD_DIM), row),
        scratch_shapes=[pltpu.VMEM((2, topk, n_kv, hd), F32), pltpu.VMEM((2, topk, n_kv, hd), F32),
                        pltpu.SemaphoreType.DMA((2,)), pltpu.SemaphoreType.DMA((2,))],
    )
    return pl.pallas_call(
        functools.partial(_dsa_sample_attn_kernel, layer=layer, past=n_pages * page, page=page),
        grid_spec=grid_spec,
        out_shape=jax.ShapeDtypeStruct((db, N_HEADS, HEAD_DIM), F32),
        compiler_params=_params("arbitrary"),
        name="dsa_sample_attn",
    )(pos.reshape(-1), page_table, q3, k_new, v_new, pos.reshape(db, 1, topk),
      new_sel.reshape(db, 1, LANES), rb_t, cache_k, cache_v)


def _gate_kernel(att_ref, zu_ref, zv_ref, ga_ref, gb_ref, gs_ref, ws_ref, bs_ref, m_ref, vn_ref, *, first_rows_only):
    aw = att_ref.shape[1]
    tm = zu_ref.shape[0]
    v = jax.nn.gelu(zv_ref[...])
    vn = v * lax.rsqrt(jnp.mean(v * v, axis=-1, keepdims=True) + EPS) * gs_ref[...]
    vn_ref[...] = vn
    m_ref[:, :aw] = (jax.nn.sigmoid(ga_ref[...]) * att_ref[...]).astype(BF16)
    if first_rows_only:
        sgu = jax.nn.gelu(zu_ref[...]) * (vn * ws_ref[...] + bs_ref[...])
        m_ref[:, aw:] = (jax.nn.sigmoid(gb_ref[...]) * sgu).astype(BF16)
    else:
        row = lax.broadcasted_iota(I32, (CHUNK, CHUNK), 0)
        col = lax.broadcasted_iota(I32, (CHUNK, CHUNK), 1)
        for g in range(GMLP_GROUPS):
            cs = slice(g * LANES, (g + 1) * LANES)
            wt = jnp.where(col <= row, ws_ref[g], 0.0).astype(BF16)
            bcol = bs_ref[:, g:g + 1]
            for c in range(tm // CHUNK):
                rs = slice(c * CHUNK, (c + 1) * CHUNK)
                mixed = _dot(wt, vn_ref[rs, cs].astype(BF16)) + bcol
                sgu = jax.nn.gelu(zu_ref[rs, cs]) * mixed
                m_ref[rs, aw + g * LANES:aw + (g + 1) * LANES] = (
                    jax.nn.sigmoid(gb_ref[rs, cs]) * sgu).astype(BF16)


def _gate(att, z, g_sgu, ws, bs, *, first_rows_only, tm, name):
    m, aw = att.shape
    gw = g_sgu.shape[0]
    tm = min(tm, m)
    zb = lambda c: pl.BlockSpec((tm, gw), lambda i: (i, c))
    full = lambda a: pl.BlockSpec(a.shape, lambda i: (0,) * a.ndim)
    first = z.shape[1] // gw - 4
    return pl.pallas_call(
        functools.partial(_gate_kernel, first_rows_only=first_rows_only),
        grid=(m // tm,),
        in_specs=[pl.BlockSpec((tm, aw), lambda i: (i, 0)),
                  zb(first), zb(first + 1), zb(first + 2), zb(first + 3),
                  pl.BlockSpec((1, gw), lambda i: (0, 0)), full(ws), full(bs)],
        out_specs=[pl.BlockSpec((tm, aw + gw), lambda i: (i, 0)), pl.BlockSpec((tm, gw), lambda i: (i, 0))],
        out_shape=[jax.ShapeDtypeStruct((m, aw + gw), BF16), jax.ShapeDtypeStruct((m, gw), F32)],
        compiler_params=_params("arbitrary"),
        name=name,
    )(att, z, z, z, z, g_sgu.reshape(1, gw), ws, bs)


def _matmul_residual_kernel(x_ref, w_ref, h_ref, o_ref):
    o_ref[...] = h_ref[...] + _dot(x_ref[...], w_ref[...])


def _matmul_residual(x, w, h, *, tm, tn, name):
    m, k = x.shape
    n = w.shape[1]
    tm, tn = min(tm, m), min(tn, n)
    return pl.pallas_call(
        _matmul_residual_kernel,
        grid=(m // tm, n // tn),
        in_specs=[pl.BlockSpec((tm, k), lambda i, j: (i, 0)),
                  pl.BlockSpec((k, tn), lambda i, j: (0, j)),
                  pl.BlockSpec((tm, tn), lambda i, j: (i, j))],
        out_specs=pl.BlockSpec((tm, tn), lambda i, j: (i, j)),
        out_shape=jax.ShapeDtypeStruct((m, n), F32),
        compiler_params=_params("arbitrary", "arbitrary"),
        name=name,
    )(x, w, h)


def _cross_kernel(q_ref, mk_ref, mv_ref, h_ref, wo_ref, o_ref):
    rows = q_ref.shape[1]
    q = q_ref[0]
    if rows < SUBLANES:
        q = jnp.broadcast_to(q, (SUBLANES, q.shape[1]))
    qb = q.astype(BF16)
    mk = mk_ref[0].astype(BF16)
    mv = mv_ref[0].astype(BF16)
    hd = mk.shape[1] // X_HEADS
    outs = []
    for hh in range(X_HEADS):
        sl = slice(hh * hd, (hh + 1) * hd)
        lg = _dot_nt(qb[:, sl], mk[:, sl]) * hd ** -0.5
        e = jnp.exp(lg - jnp.max(lg, axis=1, keepdims=True))
        p = e / jnp.sum(e, axis=1, keepdims=True)
        outs.append(_dot(p.astype(BF16), mv[:, sl]))
    y = _dot(jnp.concatenate(outs, axis=1).astype(BF16), wo_ref[...])
    o_ref[0] = h_ref[0] + y[:rows]


def _cross(q, mk, mv, h, w_xo, *, tq, name):
    b, t, xw = q.shape
    d = h.shape[2]
    mlen = mk.shape[1]
    return pl.pallas_call(
        _cross_kernel,
        grid=(b, t // tq),
        in_specs=[pl.BlockSpec((1, tq, xw), lambda bb, i: (bb, i, 0)),
                  pl.BlockSpec((1, mlen, xw), lambda bb, i: (bb, 0, 0)),
                  pl.BlockSpec((1, mlen, xw), lambda bb, i: (bb, 0, 0)),
                  pl.BlockSpec((1, tq, d), lambda bb, i: (bb, i, 0)),
                  pl.BlockSpec((xw, d), lambda bb, i: (0, 0))],
        out_specs=pl.BlockSpec((1, tq, d), lambda bb, i: (bb, i, 0)),
        out_shape=jax.ShapeDtypeStruct((b, t, d), F32),
        compiler_params=_params("arbitrary", "arbitrary"),
        name=name,
    )(q, mk, mv, h, w_xo)


def _top_rows(s, k, payload=None):
    r = s.shape[0]
    rid = lax.broadcasted_iota(I32, s.shape, 0).astype(F32)
    vals, picks = [], []
    for _ in range(k):
        m = jnp.max(s, axis=0, keepdims=True)
        am = jnp.min(jnp.where(s == m, rid, float(r)), axis=0, keepdims=True)
        hit = rid == am
        vals.append(m)
        picks.append(am if payload is None else jnp.sum(jnp.where(hit, payload, 0.0), axis=0, keepdims=True))
        s = jnp.where(hit, -jnp.inf, s)
    return jnp.concatenate(vals, axis=0), jnp.concatenate(picks, axis=0)


def _peer_route_kernel(h_ref, g_ref, w_ref, sk_ref, ids_ref, gate_ref, xn_ref):
    @pl.when(pl.program_id(1) == 0)
    def _():
        x = h_ref[...]
        xn_ref[...] = (x * lax.rsqrt(jnp.mean(x * x, axis=-1, keepdims=True) + EPS) * g_ref[...]).astype(BF16)

    half = sk_ref.shape[2]
    qt = _dot_nt(w_ref[...], xn_ref[...]).astype(BF16)
    v0, i0 = _top_rows(_dot(sk_ref[0], qt[:half]), PEER_TOPK)
    v1, i1 = _top_rows(_dot(sk_ref[1], qt[half:]), PEER_TOPK)
    widths = [PEER_TOPK // (a + 1) for a in range(PEER_TOPK)]
    pad = -sum(widths) % SUBLANES
    cand = jnp.concatenate([v0[a:a + 1] + v1[:n] for a, n in enumerate(widths)]
                           + [jnp.full((pad, v0.shape[1]), -jnp.inf, F32)], axis=0)
    eid = jnp.concatenate([i0[a:a + 1] * float(N_KEYS) + i1[:n] for a, n in enumerate(widths)]
                          + [jnp.zeros((pad, v0.shape[1]), F32)], axis=0)
    sc, e = _top_rows(cand, PEER_TOPK, payload=eid)
    ex = jnp.exp(sc - sc[0:1])
    gate_ref[...] = ex / jnp.sum(ex, axis=0, keepdims=True)
    ids_ref[...] = e.astype(I32)


def _peer_route(h, g_ffn, w_pq_t, sub_keys, *, tb):
    m, d = h.shape
    qd = w_pq_t.shape[0] // PEER_HEADS
    return pl.pallas_call(
        _peer_route_kernel,
        grid=(m // tb, PEER_HEADS),
        in_specs=[pl.BlockSpec((tb, d), lambda t, hh: (t, 0)),
                  pl.BlockSpec((1, d), lambda t, hh: (0, 0)),
                  pl.BlockSpec((qd, d), lambda t, hh: (hh, 0)),
                  pl.BlockSpec(sub_keys.shape, lambda t, hh: (0, 0, 0))],
        out_specs=[pl.BlockSpec((PEER_TOPK, tb), lambda t, hh: (hh, t)),
                   pl.BlockSpec((PEER_TOPK, tb), lambda t, hh: (hh, t))],
        out_shape=[jax.ShapeDtypeStruct((PEER_HEADS * PEER_TOPK, m), I32),
                   jax.ShapeDtypeStruct((PEER_HEADS * PEER_TOPK, m), F32)],
        scratch_shapes=[pltpu.VMEM((tb, d), BF16)],
        compiler_params=_params("arbitrary", "arbitrary"),
        name="peer_route",
    )(h, g_ffn.reshape(1, d), w_pq_t, sub_keys)


def _pack_tables_kernel(u_ref, v_ref, o_ref):
    nr = o_ref.shape[1] // 2

    def words(x):
        bits = pltpu.bitcast(x.astype(BF16).astype(F32), I32)
        half = bits.shape[1] // 2
        return (bits[:, half:] & jnp.int32(-65536)) | lax.shift_right_logical(bits[:, :half], 16)

    wu = words(u_ref[...])
    wv = words(v_ref[...])
    for r in range(nr):
        o_ref[:, r, :] = wu[:, r * LANES:(r + 1) * LANES]
        o_ref[:, nr + r, :] = wv[:, r * LANES:(r + 1) * LANES]


def _pack_tables(u_tab, v_tab, *, eb):
    n, d = u_tab.shape
    nr = d // (2 * LANES)
    return pl.pallas_call(
        _pack_tables_kernel,
        grid=(n // eb,),
        in_specs=[pl.BlockSpec((eb, d), lambda i: (i, 0)), pl.BlockSpec((eb, d), lambda i: (i, 0))],
        out_specs=pl.BlockSpec((eb, 2 * nr, LANES), lambda i: (i, 0, 0)),
        out_shape=jax.ShapeDtypeStruct((n, 2 * nr, LANES), I32),
        compiler_params=_params("arbitrary"),
        name="peer_pack_tables",
    )(u_tab, v_tab)


def _unpack_words(w):
    return pltpu.bitcast(jnp.left_shift(w, 16), F32), pltpu.bitcast(w & jnp.int32(-65536), F32)


def _peer_gather_kernel(ids_hbm, gate_ref, h_ref, gffn_ref, gfin_ref, tab_hbm, o_ref,
                        ids_s, buf, part_s, sem_ids, sem_rows, *, n_steps):
    s = pl.program_id(0)
    g_tok = h_ref.shape[0]
    nr = buf.shape[1] // 2
    n_e = buf.shape[2]
    n_buf = buf.shape[0]
    ahead = n_buf - 1
    assert g_tok % n_buf == 0 and ahead <= g_tok
    cur = lax.rem(s, 2)
    nxt = 1 - cur

    def ids_copy(step, slot):
        return pltpu.make_async_copy(ids_hbm.at[step], ids_s.at[slot], sem_ids.at[slot])

    def row_copy(e, k, slot):
        return pltpu.make_async_copy(tab_hbm.at[e], buf.at[slot, :, k, :], sem_rows.at[slot])

    def issue(ids_slot, t, slot):
        for k in range(n_e):
            row_copy(ids_s[ids_slot, t, k], k, slot).start(priority=k % 2)

    def wait(slot):
        for k in range(n_e):
            row_copy(0, k, slot).wait()

    @pl.when(s == 0)
    def _():
        ids_copy(0, 0).start()
        ids_copy(0, 0).wait()
        for t in range(ahead):
            issue(0, t, t % n_buf)

    @pl.when(s + 1 < n_steps)
    def _():
        ids_copy(s + 1, nxt).start()

    h = h_ref[...]
    xn = h * lax.rsqrt(jnp.mean(h * h, axis=-1, keepdims=True) + EPS) * gffn_ref[...]
    eye = lax.broadcasted_iota(I32, (n_e, n_e), 0) == lax.broadcasted_iota(I32, (n_e, n_e), 1)

    for t in range(g_tok):
        slot = t % n_buf
        ta = t + ahead
        if ta < g_tok:
            issue(cur, ta, ta % n_buf)
        else:
            @pl.when(s + 1 < n_steps)
            def _():
                if ta == g_tok:
                    ids_copy(s + 1, nxt).wait()
                issue(nxt, ta - g_tok, ta % n_buf)
        wait(slot)
        x = xn[t:t + 1, :]
        hacc = jnp.zeros((n_e, LANES), F32)
        for r in range(nr):
            lo, hi = _unpack_words(buf[slot, r])
            hacc = hacc + lo * x[:, r * LANES:(r + 1) * LANES] + hi * x[:, (nr + r) * LANES:(nr + r + 1) * LANES]
        act = jax.nn.gelu(jnp.sum(hacc, axis=-1, keepdims=True))
        gcol = jnp.sum(jnp.where(eye, jnp.broadcast_to(gate_ref[t:t + 1, :], (n_e, n_e)), 0.0),
                       axis=-1, keepdims=True)
        w = jnp.broadcast_to(gcol * act, (n_e, LANES))
        for r in range(nr):
            lo, hi = _unpack_words(buf[slot, nr + r])
            part_s[t, :, r * LANES:(r + 1) * LANES] = jnp.sum(
                (lo * w).reshape(n_e // SUBLANES, SUBLANES, LANES), axis=0)
            part_s[t, :, (nr + r) * LANES:(nr + r + 1) * LANES] = jnp.sum(
                (hi * w).reshape(n_e // SUBLANES, SUBLANES, LANES), axis=0)

    y = h + jnp.sum(part_s[...], axis=1)
    o_ref[...] = y * lax.rsqrt(jnp.mean(y * y, axis=-1, keepdims=True) + EPS) * gfin_ref[...]


PEER_GATHER_BUFFERS = 4


def _peer_gather(ids, gates, h, g_ffn, g_final, packed_tab, *, g_tok):
    m, d = h.shape
    n_e = ids.shape[1]
    n_steps = m // g_tok
    n_rows = packed_tab.shape[1]
    ids3 = ids.reshape(n_steps, g_tok, n_e)
    return pl.pallas_call(
        functools.partial(_peer_gather_kernel, n_steps=n_steps),
        grid=(n_steps,),
        in_specs=[pl.BlockSpec(memory_space=pl.ANY),
                  pl.BlockSpec((g_tok, n_e), lambda s: (s, 0)),
                  pl.BlockSpec((g_tok, d), lambda s: (s, 0)),
                  pl.BlockSpec((1, d), lambda s: (0, 0)),
                  pl.BlockSpec((1, d), lambda s: (0, 0)),
                  pl.BlockSpec(memory_space=pl.ANY)],
        out_specs=pl.BlockSpec((g_tok, d), lambda s: (s, 0)),
        out_shape=jax.ShapeDtypeStruct((m, d), F32),
        scratch_shapes=[pltpu.SMEM((2, g_tok, n_e), I32),
                        pltpu.VMEM((PEER_GATHER_BUFFERS, n_rows, n_e, LANES), I32),
                        pltpu.VMEM((g_tok, SUBLANES, d), F32),
                        pltpu.SemaphoreType.DMA((2,)), pltpu.SemaphoreType.DMA((PEER_GATHER_BUFFERS,))],
        compiler_params=_params("arbitrary"),
        name="peer_gather",
    )(ids3, gates, h, g_ffn.reshape(1, d), g_final.reshape(1, d), packed_tab)


def kernel(x_prompt, x_sample, cache_k, cache_v, cache_kidx, cache_mem_k, cache_mem_v, page_table,
           mem_prompt, rel_bias, g_in, w_in, g_sgu, w_s, b_s, w_out, g_x, w_xq, w_xk, w_xv, w_xo,
           g_ffn, w_pq, sub_keys, peer_u, peer_v, g_final):
    batch, seq, d = x_prompt.shape
    db, dseq, _ = x_sample.shape
    depth = w_in.shape[0]
    assert dseq == 1, "sample group is one new token per sequence"
    n_pages = page_table.shape[1]
    page = cache_k.shape[2]
    past = n_pages * page
    aw, kvw, qiw, gw = N_HEADS * HEAD_DIM, N_KV_HEADS * HEAD_DIM, IDX_HEADS * IDX_DIM, GMLP_GROUPS * LANES
    xw = w_xq.shape[2]
    mlen = mem_prompt.shape[1]
    small_lo = aw + 2 * kvw + qiw
    small_hi = small_lo + IDX_DIM + IDX_HEADS
    assert NUM_BUCKETS // 2 + int(math.log((LANES + 1) / (NUM_BUCKETS // 2)) / math.log(
        MAX_DISTANCE / (NUM_BUCKETS // 2)) * (NUM_BUCKETS // 2)) >= NUM_BUCKETS - 1

    hp = x_prompt.reshape(batch * seq, d)
    hs = x_sample.reshape(db, d)
    btiles = _bias_tiles(rel_bias)
    rb_t = rel_bias.T
    outs = {n: [] for n in ("kp", "vp", "ip", "mkp", "mvp", "ks", "vs", "is", "sv")}

    for l in range(depth):
        w_main = jnp.concatenate([w_in[l][:, :small_lo].astype(BF16), w_in[l][:, small_hi:].astype(BF16)], axis=1)
        w_small = jnp.pad(w_in[l][:, small_lo:small_hi], ((0, 0), (0, LANES - (small_hi - small_lo)))).astype(BF16)
        w_out_b = w_out[l].astype(BF16)
        w_xq_b, w_xk_b, w_xv_b, w_xo_b = (w.astype(BF16) for w in (w_xq[l], w_xk[l], w_xv[l], w_xo[l]))
        w_pq_t = w_pq[l].T.astype(BF16)
        sk_b = sub_keys[l].astype(BF16)
        ws_first = jnp.repeat(w_s[l][:, 0, 0], LANES).reshape(1, gw)
        bs_first = jnp.repeat(b_s[l][:, 0], LANES).reshape(1, gw)
        ones_d = jnp.ones((d,), F32)

        z = _proj(hp, g_in[l], w_main, norm=True, tm=512, tn=1024, name="in_proj_prompt")
        zs = _proj(hp, g_in[l], w_small, norm=True, tm=512, tn=LANES, name="in_proj_idx_prompt")
        att = _dsa_prompt(z.reshape(batch, seq, -1), zs.reshape(batch, seq, LANES), btiles,
                          batch=batch, seq=seq, topk=min(TOPK_MAX, seq // 4))
        m_p, _ = _gate(att.reshape(batch * seq, aw), z, g_sgu[l], w_s[l], b_s[l].T,
                       first_rows_only=False, tm=256, name="gate_prompt")
        hp1 = _matmul_residual(m_p, w_out_b, hp, tm=1024, tn=1024, name="out_proj_prompt")
        mem = mem_prompt.reshape(batch * mlen, d)
        mk = _proj(mem, ones_d, w_xk_b, norm=False, tm=256, tn=xw, name="mem_k_proj")
        mv = _proj(mem, ones_d, w_xv_b, norm=False, tm=256, tn=xw, name="mem_v_proj")
        qx = _proj(hp1, g_x[l], w_xq_b, norm=True, tm=512, tn=xw, name="xq_proj_prompt")
        hp2 = _cross(qx.reshape(batch, seq, xw), mk.reshape(batch, mlen, xw), mv.reshape(batch, mlen, xw),
                     hp1.reshape(batch, seq, d), w_xo_b, tq=256, name="cross_prompt").reshape(batch * seq, d)
        outs["kp"].append(z[:, aw:aw + kvw].reshape(batch, seq, N_KV_HEADS, HEAD_DIM))
        outs["vp"].append(z[:, aw + kvw:aw + 2 * kvw].reshape(batch, seq, N_KV_HEADS, HEAD_DIM))
        outs["ip"].append(zs[:, :IDX_DIM].reshape(batch, seq, IDX_DIM))
        outs["mkp"].append(mk.reshape(batch, mlen, X_HEADS, xw // X_HEADS))
        outs["mvp"].append(mv.reshape(batch, mlen, X_HEADS, xw // X_HEADS))

        z_s = _proj(hs, g_in[l], w_main, norm=True, tm=db, tn=512, name="in_proj_sample")
        zs_s = _proj(hs, g_in[l], w_small, norm=True, tm=db, tn=LANES, name="in_proj_idx_sample")
        qi3 = z_s[:, aw + 2 * kvw:aw + 2 * kvw + qiw].reshape(db, IDX_HEADS, IDX_DIM)
        wi3 = zs_s[:, IDX_DIM:IDX_DIM + IDX_HEADS].reshape(db, IDX_HEADS, 1)
        ki_new = zs_s[:, :IDX_DIM].reshape(db, 1, IDX_DIM)
        k_new = z_s[:, aw:aw + kvw].reshape(db, 1, kvw)
        v_new = z_s[:, aw + kvw:aw + 2 * kvw].reshape(db, 1, kvw)
        scores3, snew3 = _dsa_sample_scores(page_table, qi3, wi3, ki_new, cache_kidx[l])
        pos3, new_sel = _dsa_sample_select(scores3.reshape(db, past), snew3.reshape(db, LANES),
                                           topk=min(TOPK_MAX, (past + dseq) // 4))
        att_s = _dsa_sample_attn(pos3[:, :, 0], page_table, z_s[:, :aw].reshape(db, N_HEADS, HEAD_DIM),
                                 k_new, v_new, new_sel, rb_t, cache_k, cache_v, layer=l)
        m_s, vn_s = _gate(att_s.reshape(db, aw), z_s, g_sgu[l], ws_first, bs_first,
                          first_rows_only=True, tm=db, name="gate_sample")
        hs1 = _matmul_residual(m_s, w_out_b, hs, tm=db, tn=1024, name="out_proj_sample")
        qx_s = _proj(hs1, g_x[l], w_xq_b, norm=True, tm=db, tn=xw, name="xq_proj_sample")
        hs2 = _cross(qx_s.reshape(db, 1, xw), cache_mem_k[l].reshape(db, mlen, xw),
                     cache_mem_v[l].reshape(db, mlen, xw), hs1.reshape(db, 1, d), w_xo_b,
                     tq=1, name="cross_sample").reshape(db, d)
        outs["ks"].append(k_new.reshape(db, dseq, N_KV_HEADS, HEAD_DIM))
        outs["vs"].append(v_new.reshape(db, dseq, N_KV_HEADS, HEAD_DIM))
        outs["is"].append(ki_new.reshape(db, dseq, IDX_DIM))
        outs["sv"].append(vn_s.reshape(db, dseq, gw))

        assert depth == 1
        packed = _pack_tables(peer_u[l], peer_v[l], eb=256)
        ids_p, gates_p = _peer_route(hp2, g_ffn[l], w_pq_t, sk_b, tb=256)
        y_p = _peer_gather(ids_p.T, gates_p.T, hp2, g_ffn[l], g_final, packed, g_tok=8)
        hs2_pad = jnp.pad(hs2, ((0, -db % LANES), (0, 0)))
        ids_s, gates_s = _peer_route(hs2_pad, g_ffn[l], w_pq_t, sk_b, tb=LANES)
        y_s = _peer_gather(ids_s.T[:db], gates_s.T[:db], hs2, g_ffn[l], g_final, packed, g_tok=8)

    y_prompt = y_p.reshape(batch, seq, d)
    y_sample = y_s.reshape(db, dseq, d)
    st = lambda n: jnp.stack(outs[n])
    return (y_prompt, y_sample, st("kp"), st("vp"), st("ip"), st("mkp"), st("mvp"),
            st("ks"), st("vs"), st("is"), st("sv"))
```

```python
import functools
import math

import jax
import jax.numpy as jnp
from jax import lax
from jax.experimental import pallas as pl
from jax.experimental.pallas import tpu as pltpu

F32 = jnp.float32
BF16 = jnp.bfloat16
I32 = jnp.int32

N_HEADS = 16
HEAD_DIM = 128
N_KV_HEADS = 4
GQA = N_HEADS // N_KV_HEADS
IDX_HEADS = 16
IDX_DIM = 64
TOPK_MAX = 256
NUM_BUCKETS = 32
MAX_DISTANCE = 128
CHUNK = 128
GMLP_GROUPS = 16
X_HEADS = 4
N_KEYS = 128
PEER_HEADS = 8
PEER_TOPK = 16
EPS = 1e-6

LANES = 128
SUBLANES = 8
VMEM_LIMIT_BYTES = 56 * 1024 * 1024

NEG = -1e30
INT_MIN = -(2 ** 31)


def _dot_nt(a, b):
    return lax.dot_general(a, b, (((1,), (1,)), ((), ())), preferred_element_type=F32)


def _dot(a, b):
    return jnp.dot(a, b, preferred_element_type=F32)


def _params(*sem):
    return pltpu.CompilerParams(dimension_semantics=sem, vmem_limit_bytes=VMEM_LIMIT_BYTES)


def _sortable(x):
    bits = pltpu.bitcast(x, I32)
    return bits ^ (jnp.right_shift(bits, 31) & 0x7FFFFFFF)


def _t5_bucket(dist):
    n = jnp.maximum(dist, 0)
    max_exact = NUM_BUCKETS // 2
    nf = jnp.maximum(n, 1).astype(F32)
    large = max_exact + (jnp.log(nf / max_exact) / math.log(MAX_DISTANCE / max_exact)
                         * (NUM_BUCKETS - max_exact)).astype(I32)
    large = jnp.minimum(large, NUM_BUCKETS - 1)
    return jnp.where(n < max_exact, n, large)


def _kth_largest_key(count_ge, shape, k):
    def body(it, t):
        cand = t | jnp.left_shift(jnp.int32(1), 31 - it)
        cnt = count_ge(cand ^ INT_MIN)
        return jnp.where(cnt >= k, cand, t)

    t = lax.fori_loop(0, 32, body, jnp.zeros(shape, I32))
    return t ^ INT_MIN


def _prefix_matrix():
    return (lax.broadcasted_iota(I32, (LANES, LANES), 0) <= lax.broadcasted_iota(I32, (LANES, LANES), 1)).astype(BF16)


def _proj_kernel(x_ref, g_ref, w_ref, o_ref, xn_ref, *, norm):
    @pl.when(pl.program_id(1) == 0)
    def _():
        x = x_ref[...]
        if norm:
            x = x * lax.rsqrt(jnp.mean(x * x, axis=-1, keepdims=True) + EPS) * g_ref[...]
        xn_ref[...] = x.astype(BF16)

    o_ref[...] = _dot(xn_ref[...], w_ref[...])


def _proj(x, g, w, *, norm, tm, tn, name):
    m, k = x.shape
    n = w.shape[1]
    tm, tn = min(tm, m), min(tn, n)
    return pl.pallas_call(
        functools.partial(_proj_kernel, norm=norm),
        grid=(m // tm, n // tn),
        in_specs=[pl.BlockSpec((tm, k), lambda i, j: (i, 0)),
                  pl.BlockSpec((1, k), lambda i, j: (0, 0)),
                  pl.BlockSpec((k, tn), lambda i, j: (0, j))],
        out_specs=pl.BlockSpec((tm, tn), lambda i, j: (i, j)),
        out_shape=jax.ShapeDtypeStruct((m, n), F32),
        scratch_shapes=[pltpu.VMEM((tm, k), BF16)],
        compiler_params=_params("arbitrary", "arbitrary"),
        name=name,
    )(x, g.reshape(1, k), w)


def _bias_tiles_kernel(rb_ref, o_ref):
    tq = lax.broadcasted_iota(I32, (LANES, LANES), 0)
    c = lax.broadcasted_iota(I32, (LANES, LANES), 1)
    for part, off in enumerate((2 * LANES, LANES, 0)):
        dist = tq - c + off
        bucket = _t5_bucket(dist)
        for h in range(N_HEADS):
            tile = lax.fori_loop(
                0, NUM_BUCKETS, lambda b, acc: jnp.where(bucket == b, rb_ref[b, h], acc),
                jnp.zeros((LANES, LANES), F32))
            if off == 0:
                tile = jnp.where(dist < 0, NEG, tile)
            o_ref[part, h] = tile


def _bias_tiles(rel_bias):
    return pl.pallas_call(
        _bias_tiles_kernel,
        in_specs=[pl.BlockSpec(memory_space=pltpu.SMEM)],
        out_specs=pl.BlockSpec(memory_space=pltpu.VMEM),
        out_shape=jax.ShapeDtypeStruct((3, N_HEADS, LANES, LANES), F32),
        name="t5_bias_tiles",
    )(rel_bias)


def _dsa_prompt_kernel(q_ref, k_ref, v_ref, qi_ref, zsk_ref, zsq_ref, bt_ref, tri_ref, o_ref,
                       kbf, vbf, kilo, kihi, key3, qs, m_s, l_s, acc_s, *, topk):
    i = pl.program_id(1)
    tq = q_ref.shape[1]
    s = k_ref.shape[1]
    nkb = s // LANES
    cw = 2 * LANES

    @pl.when(i == 0)
    def _():
        kbf[...] = k_ref[0].astype(BF16)
        vbf[...] = v_ref[0].astype(BF16)
        zs = zsk_ref[0]
        lane = lax.broadcasted_iota(I32, zs.shape, 1)
        kilo[...] = jnp.where(lane < IDX_DIM, zs, 0.0).astype(BF16)
        kihi[...] = jnp.where(lane >= IDX_DIM, pltpu.roll(zs, IDX_DIM, axis=1), 0.0).astype(BF16)

    qi = qi_ref[0].astype(BF16)
    coef = zsq_ref[0][:, IDX_DIM:IDX_DIM + IDX_HEADS] * (IDX_HEADS ** -0.5 * IDX_DIM ** -0.5)
    t_pos = i * tq + lax.broadcasted_iota(I32, (tq, cw), 0)
    t_last = i * tq + tq - 1
    for c in range(s // cw):
        @pl.when(c * cw <= t_last)
        def _():
            klo = kilo[c * cw:(c + 1) * cw, :]
            khi = kihi[c * cw:(c + 1) * cw, :]
            acc = jnp.zeros((tq, cw), F32)
            for p in range(IDX_HEADS // 2):
                a = qi[:, p * LANES:(p + 1) * LANES]
                acc = acc + coef[:, 2 * p:2 * p + 1] * jnp.maximum(_dot_nt(a, klo), 0.0)
                acc = acc + coef[:, 2 * p + 1:2 * p + 2] * jnp.maximum(_dot_nt(a, khi), 0.0)
            s_pos = c * cw + lax.broadcasted_iota(I32, (tq, cw), 1)
            key = _sortable(jnp.where(s_pos <= t_pos, acc, -jnp.inf))
            for u in range(cw // LANES):
                key3[c * (cw // LANES) + u] = key[:, u * LANES:(u + 1) * LANES]

        @pl.when(c * cw > t_last)
        def _():
            for u in range(cw // LANES):
                key3[c * (cw // LANES) + u] = jnp.full((tq, LANES), INT_MIN, I32)

    def count(pred):
        return jnp.sum(jnp.sum(jnp.where(pred, 1.0, 0.0), axis=0), axis=1, keepdims=True)

    thr = _kth_largest_key(lambda cs: count(key3[...] >= cs[None]), (tq, 1), float(topk))

    n_ge = count(key3[...] >= thr[None])

    @pl.when(jnp.max(n_ge) > float(topk))
    def _():
        need = float(topk) - count(key3[...] > thr[None])
        before = jnp.zeros((tq, 1), F32)
        for j in range(nkb):
            kj = key3[j]
            eq = kj == thr
            rank = _dot(jnp.where(eq, 1.0, 0.0).astype(BF16), tri_ref[...]) + before
            key3[j] = jnp.where(jnp.logical_and(eq, rank > need), INT_MIN, kj)
            before = rank[:, LANES - 1:LANES]

    thr4 = jnp.concatenate([thr] * GQA, axis=0)

    qb = q_ref[0].astype(BF16)
    for kh in range(N_KV_HEADS):
        qs[kh] = jnp.concatenate(
            [qb[:, (GQA * kh + g) * HEAD_DIM:(GQA * kh + g + 1) * HEAD_DIM] for g in range(GQA)], axis=0)
    m_s[...] = jnp.full(m_s.shape, NEG, F32)
    l_s[...] = jnp.zeros(l_s.shape, F32)
    acc_s[...] = jnp.zeros(acc_s.shape, F32)
    scale = HEAD_DIM ** -0.5

    def fbody(jj, carry):
        ja = 2 * jj
        jb = ja + 1
        part_a = jnp.clip(ja - i + 2, 0, 2)
        part_b = jnp.clip(jb - i + 2, 0, 2)
        key_b = jnp.where(jb <= i, key3[jb], INT_MIN)
        key8 = jnp.concatenate([key3[ja], key_b], axis=1)
        mask = jnp.concatenate([key8] * GQA, axis=0) >= thr4
        r0 = pl.multiple_of(ja * LANES, 2 * LANES)
        for kh in range(N_KV_HEADS):
            kj = kbf[pl.ds(r0, 2 * LANES), kh * HEAD_DIM:(kh + 1) * HEAD_DIM]
            vj = vbf[pl.ds(r0, 2 * LANES), kh * HEAD_DIM:(kh + 1) * HEAD_DIM]
            bias = jnp.concatenate(
                [bt_ref[part_a, GQA * kh:GQA * (kh + 1)].reshape(GQA * tq, LANES),
                 bt_ref[part_b, GQA * kh:GQA * (kh + 1)].reshape(GQA * tq, LANES)], axis=1)
            lg = jnp.where(mask, _dot_nt(qs[kh], kj) * scale + bias, NEG)
            m_old = m_s[kh]
            m_new = jnp.maximum(m_old, jnp.max(lg, axis=1, keepdims=True))
            alpha = jnp.exp(m_old - m_new)
            p = jnp.where(mask, jnp.exp(lg - jnp.concatenate([m_new, m_new], axis=1)), 0.0)
            l_s[kh] = alpha * l_s[kh] + jnp.sum(p, axis=1, keepdims=True)
            acc_s[kh] = alpha * acc_s[kh] + _dot(p.astype(BF16), vj)
            m_s[kh] = m_new
        return carry

    assert nkb % 2 == 0
    lax.fori_loop(0, jnp.right_shift(i + 2, 1), fbody, 0)
    for kh in range(N_KV_HEADS):
        out = acc_s[kh] / l_s[kh]
        for g in range(GQA):
            h = GQA * kh + g
            o_ref[0, :, h * HEAD_DIM:(h + 1) * HEAD_DIM] = out[g * tq:(g + 1) * tq]


def _dsa_prompt(z, zs, btiles, *, batch, seq, topk):
    tq = LANES
    aw = N_HEADS * HEAD_DIM
    kvw = N_KV_HEADS * HEAD_DIM
    qiw = IDX_HEADS * IDX_DIM
    return pl.pallas_call(
        functools.partial(_dsa_prompt_kernel, topk=topk),
        grid=(batch, seq // tq),
        in_specs=[
            pl.BlockSpec((1, tq, aw), lambda b, i: (b, i, 0)),
            pl.BlockSpec((1, seq, kvw), lambda b, i: (b, 0, aw // kvw)),
            pl.BlockSpec((1, seq, kvw), lambda b, i: (b, 0, aw // kvw + 1)),
            pl.BlockSpec((1, tq, qiw), lambda b, i: (b, i, (aw + 2 * kvw) // qiw)),
            pl.BlockSpec((1, seq, LANES), lambda b, i: (b, 0, 0)),
            pl.BlockSpec((1, tq, LANES), lambda b, i: (b, i, 0)),
            pl.BlockSpec((3, N_HEADS, LANES, LANES), lambda b, i: (0, 0, 0, 0)),
            pl.BlockSpec((LANES, LANES), lambda b, i: (0, 0)),
        ],
        out_specs=pl.BlockSpec((1, tq, aw), lambda b, i: (b, i, 0)),
        out_shape=jax.ShapeDtypeStruct((batch, seq, aw), F32),
        scratch_shapes=[
            pltpu.VMEM((seq, kvw), BF16), pltpu.VMEM((seq, kvw), BF16),
            pltpu.VMEM((seq, LANES), BF16), pltpu.VMEM((seq, LANES), BF16),
            pltpu.VMEM((seq // LANES, tq, LANES), I32),
            pltpu.VMEM((N_KV_HEADS, GQA * tq, HEAD_DIM), BF16),
            pltpu.VMEM((N_KV_HEADS, GQA * tq, LANES), F32),
            pltpu.VMEM((N_KV_HEADS, GQA * tq, LANES), F32),
            pltpu.VMEM((N_KV_HEADS, GQA * tq, HEAD_DIM), F32),
        ],
        compiler_params=_params("arbitrary", "arbitrary"),
        name="dsa_prompt",
    )(z, z, z, z, zs, zs, btiles, _prefix_matrix())


SAMPLE_PAGES_PER_STEP = 8


def _dsa_sample_score_kernel(pt_ref, qi_ref, wi_ref, kinew_ref, *rest):
    kidx_refs, (o_ref, onew_ref) = rest[:-2], rest[-2:]
    qi = qi_ref[0].astype(BF16)
    coef = wi_ref[0] * (IDX_HEADS ** -0.5 * IDX_DIM ** -0.5)

    def score(ki):
        return jnp.sum(jnp.maximum(_dot_nt(qi, ki.astype(BF16)), 0.0) * coef, axis=0, keepdims=True)

    o_ref[0] = jnp.concatenate([score(r[0]) for r in kidx_refs], axis=1)

    @pl.when(pl.program_id(1) == pl.num_programs(1) - 1)
    def _():
        onew_ref[0] = score(jnp.broadcast_to(kinew_ref[0], (LANES, kinew_ref.shape[2])))


def _dsa_sample_scores(page_table, qi3, wi3, ki_new, cache_kidx):
    db, n_pages = page_table.shape
    page = cache_kidx.shape[1]
    pg = math.gcd(SAMPLE_PAGES_PER_STEP, n_pages)
    row = lambda b, j, pt: (b, 0, 0)
    grid_spec = pltpu.PrefetchScalarGridSpec(
        num_scalar_prefetch=1,
        grid=(db, n_pages // pg),
        in_specs=[pl.BlockSpec((1, IDX_HEADS, IDX_DIM), row),
                  pl.BlockSpec((1, IDX_HEADS, 1), row),
                  pl.BlockSpec((1, 1, IDX_DIM), row)]
        + [pl.BlockSpec((1, page, IDX_DIM), functools.partial(lambda b, j, pt, u: (pt[b, j * pg + u], 0, 0), u=u))
           for u in range(pg)],
        out_specs=[pl.BlockSpec((1, 1, pg * page), lambda b, j, pt: (b, 0, j)),
                   pl.BlockSpec((1, 1, LANES), row)],
    )
    return pl.pallas_call(
        _dsa_sample_score_kernel,
        grid_spec=grid_spec,
        out_shape=[jax.ShapeDtypeStruct((db, 1, n_pages * page), F32),
                   jax.ShapeDtypeStruct((db, 1, LANES), F32)],
        compiler_params=_params("arbitrary", "arbitrary"),
        name="dsa_sample_scores",
    )(page_table, qi3, wi3, ki_new, *([cache_kidx] * pg))


def _dsa_sample_select_kernel(s_ref, snew_ref, tri_ref, pos_ref, new_ref, rank_s, nsel_s, *, topk):
    db, past = s_ref.shape
    k = float(topk)
    key = _sortable(s_ref[...])
    key_new = _sortable(snew_ref[:, 0:1])

    def count_ge(cs):
        return (jnp.sum(jnp.where(key >= cs, 1.0, 0.0), axis=1, keepdims=True)
                + jnp.where(key_new >= cs, 1.0, 0.0))

    thr = _kth_largest_key(count_ge, (db, 1), k)
    gt = key > thr
    eq = key == thr
    need = k - (jnp.sum(jnp.where(gt, 1.0, 0.0), axis=1, keepdims=True) + jnp.where(key_new > thr, 1.0, 0.0))

    tri = tri_ref[...]

    def cumsum(flags):
        out, off = [], jnp.zeros((db, 1), F32)
        for blk in range(past // LANES):
            pc = _dot(flags[:, blk * LANES:(blk + 1) * LANES].astype(BF16), tri) + off
            out.append(pc)
            off = pc[:, LANES - 1:LANES]
        return jnp.concatenate(out, axis=1), off

    eq_rank, eq_total = cumsum(jnp.where(eq, 1.0, 0.0))
    sel = jnp.where(jnp.logical_or(gt, jnp.logical_and(eq, eq_rank <= need)), 1.0, 0.0)
    new_sel = jnp.logical_or(key_new > thr, jnp.logical_and(key_new == thr, eq_total < need))
    new_ref[...] = jnp.broadcast_to(jnp.where(new_sel, 1.0, 0.0), new_ref.shape)
    rank, n_sel = cumsum(sel)
    rank = jnp.where(sel > 0.0, rank, 0.0)
    for b in range(db):
        rank_s[b] = rank[b:b + 1, :]
        nsel_s[b] = jnp.broadcast_to(n_sel[b:b + 1, :], (1, LANES))
    ch = min(8 * LANES, past)
    slot = lax.broadcasted_iota(I32, (topk, ch), 0).astype(F32) + 1.0
    lane = lax.broadcasted_iota(I32, (topk, ch), 1).astype(F32)
    slot_col = lax.broadcasted_iota(I32, (topk, 1), 0).astype(F32)

    def extract(b, carry):
        r = rank_s[b]
        pos = jnp.zeros((topk, 1), F32)
        for c in range(past // ch):
            hit = r[:, c * ch:(c + 1) * ch] == slot
            pos = pos + jnp.sum(jnp.where(hit, lane + float(c * ch), 0.0), axis=1, keepdims=True)
        pos = jnp.where(slot_col < nsel_s[b][:, 0:1], pos, -1.0)
        pos_ref[b] = jnp.broadcast_to(pos, (topk, LANES)).astype(I32)
        return carry

    lax.fori_loop(0, db, extract, 0)


def _dsa_sample_select(scores, snew, *, topk):
    db = scores.shape[0]
    tri = _prefix_matrix()
    return pl.pallas_call(
        functools.partial(_dsa_sample_select_kernel, topk=topk),
        out_shape=[jax.ShapeDtypeStruct((db, topk, LANES), I32), jax.ShapeDtypeStruct((db, LANES), F32)],
        scratch_shapes=[pltpu.VMEM((db, 1, scores.shape[1]), F32), pltpu.VMEM((db, 1, LANES), F32)],
        compiler_params=pltpu.CompilerParams(vmem_limit_bytes=VMEM_LIMIT_BYTES),
        name="dsa_sample_select",
    )(scores, snew, tri)


def _dsa_sample_attn_kernel(pos_s, pt_s, q_ref, knew_ref, vnew_ref, posrow_ref, newsel_ref, rbt_ref,
                            ck_hbm, cv_hbm, o_ref, kbuf, vbuf, sem_k, sem_v, *, layer, past, page):
    b = pl.program_id(0)
    topk = kbuf.shape[1]
    slot = lax.rem(b, 2)
    page_shift = page.bit_length() - 1
    unroll = 8

    def row_copies(bb, j, sl):
        p = jnp.maximum(pos_s[bb * topk + j], 0)
        pg = pt_s[bb, jnp.right_shift(p, page_shift)]
        off = p & (page - 1)
        return (pltpu.make_async_copy(ck_hbm.at[layer, pg, off], kbuf.at[sl, j], sem_k.at[sl]),
                pltpu.make_async_copy(cv_hbm.at[layer, pg, off], vbuf.at[sl, j], sem_v.at[sl]))

    def issue(bb, sl):
        def body(i, carry):
            for u in range(unroll):
                ck, cv = row_copies(bb, i * unroll + u, sl)
                ck.start()
                cv.start()
            return carry
        lax.fori_loop(0, topk // unroll, body, 0)

    def wait(sl):
        def body(i, carry):
            for u in range(unroll):
                j = i * unroll + u
                pltpu.make_async_copy(ck_hbm.at[layer, 0, 0], kbuf.at[sl, j], sem_k.at[sl]).wait()
                pltpu.make_async_copy(cv_hbm.at[layer, 0, 0], vbuf.at[sl, j], sem_v.at[sl]).wait()
            return carry
        lax.fori_loop(0, topk // unroll, body, 0)

    @pl.when(b == 0)
    def _():
        issue(0, 0)

    @pl.when(b + 1 < pl.num_programs(0))
    def _():
        issue(b + 1, 1 - slot)

    wait(slot)

    q = q_ref[0].astype(BF16)
    posr = posrow_ref[0]
    valid = posr >= 0
    bucket = _t5_bucket(past - posr)
    bias = jnp.zeros((N_HEADS, topk), F32)
    for bk in range(NUM_BUCKETS):
        bias = jnp.where(bucket == bk, rbt_ref[:, bk:bk + 1], bias)
    shift = GQA.bit_length() - 1
    kv_of_row = jnp.right_shift(lax.broadcasted_iota(I32, (N_HEADS, topk), 0), shift)
    kv_of_row_d = jnp.right_shift(lax.broadcasted_iota(I32, (N_HEADS, HEAD_DIM), 0), shift)
    scale = HEAD_DIM ** -0.5
    qf = q.astype(F32)
    knew = knew_ref[0].astype(BF16).astype(F32)
    vnew = vnew_ref[0].astype(BF16).astype(F32)
    lg = jnp.zeros((N_HEADS, topk), F32)
    lg_new = jnp.zeros((N_HEADS, 1), F32)
    v_new_rows = jnp.zeros((N_HEADS, HEAD_DIM), F32)
    for kh in range(N_KV_HEADS):
        hs = slice(kh * HEAD_DIM, (kh + 1) * HEAD_DIM)
        lg = jnp.where(kv_of_row == kh, _dot_nt(q, kbuf[slot, :, kh, :].astype(BF16)), lg)
        lg_new = jnp.where(kv_of_row[:, 0:1] == kh, jnp.sum(qf * knew[:, hs], axis=1, keepdims=True), lg_new)
        v_new_rows = jnp.where(kv_of_row_d == kh, jnp.broadcast_to(vnew[:, hs], (N_HEADS, HEAD_DIM)), v_new_rows)
    lg = jnp.where(valid, lg * scale + bias, NEG)
    new_sel = newsel_ref[0][:, 0:1] > 0.5
    lg_new = jnp.where(new_sel, lg_new * scale + rbt_ref[:, 0:1], NEG)
    m = jnp.maximum(jnp.max(lg, axis=1, keepdims=True), lg_new)
    p = jnp.where(valid, jnp.exp(lg - m), 0.0)
    p_new = jnp.where(new_sel, jnp.exp(lg_new - m), 0.0)
    denom = jnp.sum(p, axis=1, keepdims=True) + p_new
    pb = (p / denom).astype(BF16)
    out = (p_new / denom).astype(BF16).astype(F32) * v_new_rows
    for kh in range(N_KV_HEADS):
        out = out + jnp.where(kv_of_row_d == kh, _dot(pb, vbuf[slot, :, kh, :].astype(BF16)), 0.0)
    o_ref[0] = out


def _dsa_sample_attn(pos, page_table, q3, k_new, v_new, new_sel, rb_t, cache_k, cache_v, *, layer):
    db, topk = pos.shape
    n_pages = page_table.shape[1]
    page, n_kv, hd = cache_k.shape[2:]
    assert page & (page - 1) == 0
    kvw = n_kv * hd
    row = lambda b, ps, pt: (b, 0, 0)
    grid_spec = pltpu.PrefetchScalarGridSpec(
        num_scalar_prefetch=2,
        grid=(db,),
        in_specs=[
            pl.BlockSpec((1, N_HEADS, HEAD_DIM), row),
            pl.BlockSpec((1, 1, kvw), row),
            pl.BlockSpec((1, 1, kvw), row),
            pl.BlockSpec((1, 1, topk), row),
            pl.BlockSpec((1, 1, LANES), row),
            pl.BlockSpec((N_HEADS, NUM_BUCKETS), lambda b, ps, pt: (0, 0)),
            pl.BlockSpec(memory_space=pl.ANY),
            pl.BlockSpec(memory_space=pl.ANY),
        ],
        out_specs=pl.BlockSpec((1, N_HEADS, HEAD_DIM), row),
        scratch_shapes=[pltpu.VMEM((2, topk, n_kv, hd), F32), pltpu.VMEM((2, topk, n_kv, hd), F32),
                        pltpu.SemaphoreType.DMA((2,)), pltpu.SemaphoreType.DMA((2,))],
    )
    return pl.pallas_call(
        functools.partial(_dsa_sample_attn_kernel, layer=layer, past=n_pages * page, page=page),
        grid_spec=grid_spec,
        out_shape=jax.ShapeDtypeStruct((db, N_HEADS, HEAD_DIM), F32),
        compiler_params=_params("arbitrary"),
        name="dsa_sample_attn",
    )(pos.reshape(-1), page_table, q3, k_new, v_new, pos.reshape(db, 1, topk),
      new_sel.reshape(db, 1, LANES), rb_t, cache_k, cache_v)


def _gate_kernel(att_ref, zu_ref, zv_ref, ga_ref, gb_ref, gs_ref, ws_ref, bs_ref, m_ref, vn_ref, *, first_rows_only):
    aw = att_ref.shape[1]
    tm = zu_ref.shape[0]
    v = jax.nn.gelu(zv_ref[...])
    vn = v * lax.rsqrt(jnp.mean(v * v, axis=-1, keepdims=True) + EPS) * gs_ref[...]
    vn_ref[...] = vn
    m_ref[:, :aw] = (jax.nn.sigmoid(ga_ref[...]) * att_ref[...]).astype(BF16)
    if first_rows_only:
        sgu = jax.nn.gelu(zu_ref[...]) * (vn * ws_ref[...] + bs_ref[...])
        m_ref[:, aw:] = (jax.nn.sigmoid(gb_ref[...]) * sgu).astype(BF16)
    else:
        row = lax.broadcasted_iota(I32, (CHUNK, CHUNK), 0)
        col = lax.broadcasted_iota(I32, (CHUNK, CHUNK), 1)
        for g in range(GMLP_GROUPS):
            cs = slice(g * LANES, (g + 1) * LANES)
            wt = jnp.where(col <= row, ws_ref[g], 0.0).astype(BF16)
            bcol = bs_ref[:, g:g + 1]
            for c in range(tm // CHUNK):
                rs = slice(c * CHUNK, (c + 1) * CHUNK)
                mixed = _dot(wt, vn_ref[rs, cs].astype(BF16)) + bcol
                sgu = jax.nn.gelu(zu_ref[rs, cs]) * mixed
                m_ref[rs, aw + g * LANES:aw + (g + 1) * LANES] = (
                    jax.nn.sigmoid(gb_ref[rs, cs]) * sgu).astype(BF16)


def _gate(att, z, g_sgu, ws, bs, *, first_rows_only, tm, name):
    m, aw = att.shape
    gw = g_sgu.shape[0]
    tm = min(tm, m)
    zb = lambda c: pl.BlockSpec((tm, gw), lambda i: (i, c))
    full = lambda a: pl.BlockSpec(a.shape, lambda i: (0,) * a.ndim)
    first = z.shape[1] // gw - 4
    return pl.pallas_call(
        functools.partial(_gate_kernel, first_rows_only=first_rows_only),
        grid=(m // tm,),
        in_specs=[pl.BlockSpec((tm, aw), lambda i: (i, 0)),
                  zb(first), zb(first + 1), zb(first + 2), zb(first + 3),
                  pl.BlockSpec((1, gw), lambda i: (0, 0)), full(ws), full(bs)],
        out_specs=[pl.BlockSpec((tm, aw + gw), lambda i: (i, 0)), pl.BlockSpec((tm, gw), lambda i: (i, 0))],
        out_shape=[jax.ShapeDtypeStruct((m, aw + gw), BF16), jax.ShapeDtypeStruct((m, gw), F32)],
        compiler_params=_params("arbitrary"),
        name=name,
    )(att, z, z, z, z, g_sgu.reshape(1, gw), ws, bs)


def _matmul_residual_kernel(x_ref, w_ref, h_ref, o_ref):
    o_ref[...] = h_ref[...] + _dot(x_ref[...], w_ref[...])


def _matmul_residual(x, w, h, *, tm, tn, name):
    m, k = x.shape
    n = w.shape[1]
    tm, tn = min(tm, m), min(tn, n)
    return pl.pallas_call(
        _matmul_residual_kernel,
        grid=(m // tm, n // tn),
        in_specs=[pl.BlockSpec((tm, k), lambda i, j: (i, 0)),
                  pl.BlockSpec((k, tn), lambda i, j: (0, j)),
                  pl.BlockSpec((tm, tn), lambda i, j: (i, j))],
        out_specs=pl.BlockSpec((tm, tn), lambda i, j: (i, j)),
        out_shape=jax.ShapeDtypeStruct((m, n), F32),
        compiler_params=_params("arbitrary", "arbitrary"),
        name=name,
    )(x, w, h)


def _cross_kernel(q_ref, mk_ref, mv_ref, h_ref, wo_ref, o_ref):
    rows = q_ref.shape[1]
    q = q_ref[0]
    if rows < SUBLANES:
        q = jnp.broadcast_to(q, (SUBLANES, q.shape[1]))
    qb = q.astype(BF16)
    mk = mk_ref[0].astype(BF16)
    mv = mv_ref[0].astype(BF16)
    hd = mk.shape[1] // X_HEADS
    outs = []
    for hh in range(X_HEADS):
        sl = slice(hh * hd, (hh + 1) * hd)
        lg = _dot_nt(qb[:, sl], mk[:, sl]) * hd ** -0.5
        e = jnp.exp(lg - jnp.max(lg, axis=1, keepdims=True))
        p = e / jnp.sum(e, axis=1, keepdims=True)
        outs.append(_dot(p.astype(BF16), mv[:, sl]))
    y = _dot(jnp.concatenate(outs, axis=1).astype(BF16), wo_ref[...])
    o_ref[0] = h_ref[0] + y[:rows]


def _cross(q, mk, mv, h, w_xo, *, tq, name):
    b, t, xw = q.shape
    d = h.shape[2]
    mlen = mk.shape[1]
    return pl.pallas_call(
        _cross_kernel,
        grid=(b, t // tq),
        in_specs=[pl.BlockSpec((1, tq, xw), lambda bb, i: (bb, i, 0)),
                  pl.BlockSpec((1, mlen, xw), lambda bb, i: (bb, 0, 0)),
                  pl.BlockSpec((1, mlen, xw), lambda bb, i: (bb, 0, 0)),
                  pl.BlockSpec((1, tq, d), lambda bb, i: (bb, i, 0)),
                  pl.BlockSpec((xw, d), lambda bb, i: (0, 0))],
        out_specs=pl.BlockSpec((1, tq, d), lambda bb, i: (bb, i, 0)),
        out_shape=jax.ShapeDtypeStruct((b, t, d), F32),
        compiler_params=_params("arbitrary", "arbitrary"),
        name=name,
    )(q, mk, mv, h, w_xo)


def _top_rows(s, k, payload=None):
    r = s.shape[0]
    rid = lax.broadcasted_iota(I32, s.shape, 0).astype(F32)
    vals, picks = [], []
    for _ in range(k):
        m = jnp.max(s, axis=0, keepdims=True)
        am = jnp.min(jnp.where(s == m, rid, float(r)), axis=0, keepdims=True)
        hit = rid == am
        vals.append(m)
        picks.append(am if payload is None else jnp.sum(jnp.where(hit, payload, 0.0), axis=0, keepdims=True))
        s = jnp.where(hit, -jnp.inf, s)
    return jnp.concatenate(vals, axis=0), jnp.concatenate(picks, axis=0)


def _peer_route_kernel(h_ref, g_ref, w_ref, sk_ref, ids_ref, gate_ref, xn_ref):
    @pl.when(pl.program_id(1) == 0)
    def _():
        x = h_ref[...]
        xn_ref[...] = (x * lax.rsqrt(jnp.mean(x * x, axis=-1, keepdims=True) + EPS) * g_ref[...]).astype(BF16)

    half = sk_ref.shape[2]
    qt = _dot_nt(w_ref[...], xn_ref[...]).astype(BF16)
    v0, i0 = _top_rows(_dot(sk_ref[0], qt[:half]), PEER_TOPK)
    v1, i1 = _top_rows(_dot(sk_ref[1], qt[half:]), PEER_TOPK)
    widths = [PEER_TOPK // (a + 1) for a in range(PEER_TOPK)]
    pad = -sum(widths) % SUBLANES
    cand = jnp.concatenate([v0[a:a + 1] + v1[:n] for a, n in enumerate(widths)]
                           + [jnp.full((pad, v0.shape[1]), -jnp.inf, F32)], axis=0)
    eid = jnp.concatenate([i0[a:a + 1] * float(N_KEYS) + i1[:n] for a, n in enumerate(widths)]
                          + [jnp.zeros((pad, v0.shape[1]), F32)], axis=0)
    sc, e = _top_rows(cand, PEER_TOPK, payload=eid)
    ex = jnp.exp(sc - sc[0:1])
    gate_ref[...] = ex / jnp.sum(ex, axis=0, keepdims=True)
    ids_ref[...] = e.astype(I32)


def _peer_route(h, g_ffn, w_pq_t, sub_keys, *, tb):
    m, d = h.shape
    qd = w_pq_t.shape[0] // PEER_HEADS
    return pl.pallas_call(
        _peer_route_kernel,
        grid=(m // tb, PEER_HEADS),
        in_specs=[pl.BlockSpec((tb, d), lambda t, hh: (t, 0)),
                  pl.BlockSpec((1, d), lambda t, hh: (0, 0)),
                  pl.BlockSpec((qd, d), lambda t, hh: (hh, 0)),
                  pl.BlockSpec(sub_keys.shape, lambda t, hh: (0, 0, 0))],
        out_specs=[pl.BlockSpec((PEER_TOPK, tb), lambda t, hh: (hh, t)),
                   pl.BlockSpec((PEER_TOPK, tb), lambda t, hh: (hh, t))],
        out_shape=[jax.ShapeDtypeStruct((PEER_HEADS * PEER_TOPK, m), I32),
                   jax.ShapeDtypeStruct((PEER_HEADS * PEER_TOPK, m), F32)],
        scratch_shapes=[pltpu.VMEM((tb, d), BF16)],
        compiler_params=_params("arbitrary", "arbitrary"),
        name="peer_route",
    )(h, g_ffn.reshape(1, d), w_pq_t, sub_keys)


def _pack_tables_kernel(u_ref, v_ref, o_ref):
    nr = o_ref.shape[1] // 2

    def words(x):
        bits = pltpu.bitcast(x.astype(BF16).astype(F32), I32)
        half = bits.shape[1] // 2
        return (bits[:, half:] & jnp.int32(-65536)) | lax.shift_right_logical(bits[:, :half], 16)

    wu = words(u_ref[...])
    wv = words(v_ref[...])
    for r in range(nr):
        o_ref[:, r, :] = wu[:, r * LANES:(r + 1) * LANES]
        o_ref[:, nr + r, :] = wv[:, r * LANES:(r + 1) * LANES]


def _pack_tables(u_tab, v_tab, *, eb):
    n, d = u_tab.shape
    nr = d // (2 * LANES)
    return pl.pallas_call(
        _pack_tables_kernel,
        grid=(n // eb,),
        in_specs=[pl.BlockSpec((eb, d), lambda i: (i, 0)), pl.BlockSpec((eb, d), lambda i: (i, 0))],
        out_specs=pl.BlockSpec((eb, 2 * nr, LANES), lambda i: (i, 0, 0)),
        out_shape=jax.ShapeDtypeStruct((n, 2 * nr, LANES), I32),
        compiler_params=_params("arbitrary"),
        name="peer_pack_tables",
    )(u_tab, v_tab)


def _unpack_words(w):
    return pltpu.bitcast(jnp.left_shift(w, 16), F32), pltpu.bitcast(w & jnp.int32(-65536), F32)


def _peer_gather_kernel(ids_hbm, gate_ref, h_ref, gffn_ref, gfin_ref, tab_hbm, o_ref,
                        ids_s, buf, part_s, sem_ids, sem_rows, *, n_steps):
    s = pl.program_id(0)
    g_tok = h_ref.shape[0]
    nr = buf.shape[1] // 2
    n_e = buf.shape[2]
    n_buf = buf.shape[0]
    ahead = n_buf - 1
    assert g_tok % n_buf == 0 and ahead <= g_tok
    cur = lax.rem(s, 2)
    nxt = 1 - cur

    def ids_copy(step, slot):
        return pltpu.make_async_copy(ids_hbm.at[step], ids_s.at[slot], sem_ids.at[slot])

    def row_copy(e, k, slot):
        return pltpu.make_async_copy(tab_hbm.at[e], buf.at[slot, :, k, :], sem_rows.at[slot])

    def issue(ids_slot, t, slot):
        for k in range(n_e):
            row_copy(ids_s[ids_slot, t, k], k, slot).start(priority=k % 2)

    def wait(slot):
        for k in range(n_e):
            row_copy(0, k, slot).wait()

    @pl.when(s == 0)
    def _():
        ids_copy(0, 0).start()
        ids_copy(0, 0).wait()
        for t in range(ahead):
            issue(0, t, t % n_buf)

    ids_copy(s + 1, nxt).start()

    h = h_ref[...]
    xn = h * lax.rsqrt(jnp.mean(h * h, axis=-1, keepdims=True) + EPS) * gffn_ref[...]
    eye = lax.broadcasted_iota(I32, (n_e, n_e), 0) == lax.broadcasted_iota(I32, (n_e, n_e), 1)

    for t in range(g_tok):
        slot = t % n_buf
        ta = t + ahead
        if ta < g_tok:
            issue(cur, ta, ta % n_buf)
        else:
            if ta == g_tok:
                ids_copy(s + 1, nxt).wait()
            issue(nxt, ta - g_tok, ta % n_buf)
        wait(slot)
        x = xn[t:t + 1, :]
        hacc = jnp.zeros((n_e, LANES), F32)
        for r in range(nr):
            lo, hi = _unpack_words(buf[slot, r])
            hacc = hacc + lo * x[:, r * LANES:(r + 1) * LANES] + hi * x[:, (nr + r) * LANES:(nr + r + 1) * LANES]
        act = jax.nn.gelu(jnp.sum(hacc, axis=-1, keepdims=True))
        gcol = jnp.sum(jnp.where(eye, jnp.broadcast_to(gate_ref[t:t + 1, :], (n_e, n_e)), 0.0),
                       axis=-1, keepdims=True)
        w = jnp.broadcast_to(gcol * act, (n_e, LANES))
        for r in range(nr):
            lo, hi = _unpack_words(buf[slot, nr + r])
            part_s[t, :, r * LANES:(r + 1) * LANES] = jnp.sum(
                (lo * w).reshape(n_e // SUBLANES, SUBLANES, LANES), axis=0)
            part_s[t, :, (nr + r) * LANES:(nr + r + 1) * LANES] = jnp.sum(
                (hi * w).reshape(n_e // SUBLANES, SUBLANES, LANES), axis=0)

    @pl.when(s == n_steps - 1)
    def _():
        for t in range(ahead):
            wait((g_tok + t) % n_buf)

    y = h + jnp.sum(part_s[...], axis=1)
    o_ref[...] = y * lax.rsqrt(jnp.mean(y * y, axis=-1, keepdims=True) + EPS) * gfin_ref[...]


PEER_GATHER_BUFFERS = 8


def _peer_gather(ids, gates, h, g_ffn, g_final, packed_tab, *, g_tok):
    m, d = h.shape
    n_e = ids.shape[1]
    n_steps = m // g_tok
    n_rows = packed_tab.shape[1]
    ids3 = jnp.pad(ids.reshape(n_steps, g_tok, n_e), ((0, 1), (0, 0), (0, 0)))
    return pl.pallas_call(
        functools.partial(_peer_gather_kernel, n_steps=n_steps),
        grid=(n_steps,),
        in_specs=[pl.BlockSpec(memory_space=pl.ANY),
                  pl.BlockSpec((g_tok, n_e), lambda s: (s, 0)),
                  pl.BlockSpec((g_tok, d), lambda s: (s, 0)),
                  pl.BlockSpec((1, d), lambda s: (0, 0)),
                  pl.BlockSpec((1, d), lambda s: (0, 0)),
                  pl.BlockSpec(memory_space=pl.ANY)],
        out_specs=pl.BlockSpec((g_tok, d), lambda s: (s, 0)),
        out_shape=jax.ShapeDtypeStruct((m, d), F32),
        scratch_shapes=[pltpu.SMEM((2, g_tok, n_e), I32),
                        pltpu.VMEM((PEER_GATHER_BUFFERS, n_rows, n_e, LANES), I32),
                        pltpu.VMEM((g_tok, SUBLANES, d), F32),
                        pltpu.SemaphoreType.DMA((2,)), pltpu.SemaphoreType.DMA((PEER_GATHER_BUFFERS,))],
        compiler_params=_params("arbitrary"),
        name="peer_gather",
    )(ids3, gates, h, g_ffn.reshape(1, d), g_final.reshape(1, d), packed_tab)


def kernel(x_prompt, x_sample, cache_k, cache_v, cache_kidx, cache_mem_k, cache_mem_v, page_table,
           mem_prompt, rel_bias, g_in, w_in, g_sgu, w_s, b_s, w_out, g_x, w_xq, w_xk, w_xv, w_xo,
           g_ffn, w_pq, sub_keys, peer_u, peer_v, g_final):
    batch, seq, d = x_prompt.shape
    db, dseq, _ = x_sample.shape
    depth = w_in.shape[0]
    assert dseq == 1, "sample group is one new token per sequence"
    n_pages = page_table.shape[1]
    page = cache_k.shape[2]
    past = n_pages * page
    aw, kvw, qiw, gw = N_HEADS * HEAD_DIM, N_KV_HEADS * HEAD_DIM, IDX_HEADS * IDX_DIM, GMLP_GROUPS * LANES
    xw = w_xq.shape[2]
    mlen = mem_prompt.shape[1]
    small_lo = aw + 2 * kvw + qiw
    small_hi = small_lo + IDX_DIM + IDX_HEADS
    assert NUM_BUCKETS // 2 + int(math.log((LANES + 1) / (NUM_BUCKETS // 2)) / math.log(
        MAX_DISTANCE / (NUM_BUCKETS // 2)) * (NUM_BUCKETS // 2)) >= NUM_BUCKETS - 1

    hp = x_prompt.reshape(batch * seq, d)
    hs = x_sample.reshape(db, d)
    btiles = _bias_tiles(rel_bias)
    rb_t = rel_bias.T
    outs = {n: [] for n in ("kp", "vp", "ip", "mkp", "mvp", "ks", "vs", "is", "sv")}

    for l in range(depth):
        w_main = jnp.concatenate([w_in[l][:, :small_lo].astype(BF16), w_in[l][:, small_hi:].astype(BF16)], axis=1)
        w_small = jnp.pad(w_in[l][:, small_lo:small_hi], ((0, 0), (0, LANES - (small_hi - small_lo)))).astype(BF16)
        w_out_b = w_out[l].astype(BF16)
        w_xq_b, w_xk_b, w_xv_b, w_xo_b = (w.astype(BF16) for w in (w_xq[l], w_xk[l], w_xv[l], w_xo[l]))
        w_pq_t = w_pq[l].T.astype(BF16)
        sk_b = sub_keys[l].astype(BF16)
        ws_first = jnp.repeat(w_s[l][:, 0, 0], LANES).reshape(1, gw)
        bs_first = jnp.repeat(b_s[l][:, 0], LANES).reshape(1, gw)
        ones_d = jnp.ones((d,), F32)

        z = _proj(hp, g_in[l], w_main, norm=True, tm=512, tn=1024, name="in_proj_prompt")
        zs = _proj(hp, g_in[l], w_small, norm=True, tm=512, tn=LANES, name="in_proj_idx_prompt")
        att = _dsa_prompt(z.reshape(batch, seq, -1), zs.reshape(batch, seq, LANES), btiles,
                          batch=batch, seq=seq, topk=min(TOPK_MAX, seq // 4))
        m_p, _ = _gate(att.reshape(batch * seq, aw), z, g_sgu[l], w_s[l], b_s[l].T,
                       first_rows_only=False, tm=256, name="gate_prompt")
        hp1 = _matmul_residual(m_p, w_out_b, hp, tm=1024, tn=1024, name="out_proj_prompt")
        mem = mem_prompt.reshape(batch * mlen, d)
        mk = _proj(mem, ones_d, w_xk_b, norm=False, tm=256, tn=xw, name="mem_k_proj")
        mv = _proj(mem, ones_d, w_xv_b, norm=False, tm=256, tn=xw, name="mem_v_proj")
        qx = _proj(hp1, g_x[l], w_xq_b, norm=True, tm=512, tn=xw, name="xq_proj_prompt")
        hp2 = _cross(qx.reshape(batch, seq, xw), mk.reshape(batch, mlen, xw), mv.reshape(batch, mlen, xw),
                     hp1.reshape(batch, seq, d), w_xo_b, tq=256, name="cross_prompt").reshape(batch * seq, d)
        outs["kp"].append(z[:, aw:aw + kvw].reshape(batch, seq, N_KV_HEADS, HEAD_DIM))
        outs["vp"].append(z[:, aw + kvw:aw + 2 * kvw].reshape(batch, seq, N_KV_HEADS, HEAD_DIM))
        outs["ip"].append(zs[:, :IDX_DIM].reshape(batch, seq, IDX_DIM))
        outs["mkp"].append(mk.reshape(batch, mlen, X_HEADS, xw // X_HEADS))
        outs["mvp"].append(mv.reshape(batch, mlen, X_HEADS, xw // X_HEADS))

        z_s = _proj(hs, g_in[l], w_main, norm=True, tm=db, tn=512, name="in_proj_sample")
        zs_s = _proj(hs, g_in[l], w_small, norm=True, tm=db, tn=LANES, name="in_proj_idx_sample")
        qi3 = z_s[:, aw + 2 * kvw:aw + 2 * kvw + qiw].reshape(db, IDX_HEADS, IDX_DIM)
        wi3 = zs_s[:, IDX_DIM:IDX_DIM + IDX_HEADS].reshape(db, IDX_HEADS, 1)
        ki_new = zs_s[:, :IDX_DIM].reshape(db, 1, IDX_DIM)
        k_new = z_s[:, aw:aw + kvw].reshape(db, 1, kvw)
        v_new = z_s[:, aw + kvw:aw + 2 * kvw].reshape(db, 1, kvw)
        scores3, snew3 = _dsa_sample_scores(page_table, qi3, wi3, ki_new, cache_kidx[l])
        pos3, new_sel = _dsa_sample_select(scores3.reshape(db, past), snew3.reshape(db, LANES),
                                           topk=min(TOPK_MAX, (past + dseq) // 4))
        att_s = _dsa_sample_attn(pos3[:, :, 0], page_table, z_s[:, :aw].reshape(db, N_HEADS, HEAD_DIM),
                                 k_new, v_new, new_sel, rb_t, cache_k, cache_v, layer=l)
        m_s, vn_s = _gate(att_s.reshape(db, aw), z_s, g_sgu[l], ws_first, bs_first,
                          first_rows_only=True, tm=db, name="gate_sample")
        hs1 = _matmul_residual(m_s, w_out_b, hs, tm=db, tn=1024, name="out_proj_sample")
        qx_s = _proj(hs1, g_x[l], w_xq_b, norm=True, tm=db, tn=xw, name="xq_proj_sample")
        hs2 = _cross(qx_s.reshape(db, 1, xw), cache_mem_k[l].reshape(db, mlen, xw),
                     cache_mem_v[l].reshape(db, mlen, xw), hs1.reshape(db, 1, d), w_xo_b,
                     tq=1, name="cross_sample").reshape(db, d)
        outs["ks"].append(k_new.reshape(db, dseq, N_KV_HEADS, HEAD_DIM))
        outs["vs"].append(v_new.reshape(db, dseq, N_KV_HEADS, HEAD_DIM))
        outs["is"].append(ki_new.reshape(db, dseq, IDX_DIM))
        outs["sv"].append(vn_s.reshape(db, dseq, gw))

        assert depth == 1
        packed = _pack_tables(peer_u[l], peer_v[l], eb=256)
        ids_p, gates_p = _peer_route(hp2, g_ffn[l], w_pq_t, sk_b, tb=256)
        y_p = _peer_gather(ids_p.T, gates_p.T, hp2, g_ffn[l], g_final, packed, g_tok=8)
        hs2_pad = jnp.pad(hs2, ((0, -db % LANES), (0, 0)))
        ids_s, gates_s = _peer_route(hs2_pad, g_ffn[l], w_pq_t, sk_b, tb=LANES)
        y_s = _peer_gather(ids_s.T[:db], gates_s.T[:db], hs2, g_ffn[l], g_final, packed, g_tok=8)

    y_prompt = y_p.reshape(batch, seq, d)
    y_sample = y_s.reshape(db, dseq, d)
    st = lambda n: jnp.stack(outs[n])
    return (y_prompt, y_sample, st("kp"), st("vp"), st("ip"), st("mkp"), st("mvp"),
            st("ks"), st("vs"), st("is"), st("sv"))
```

```python
import functools
import math

import jax
import jax.numpy as jnp
from jax import lax
from jax.experimental import pallas as pl
from jax.experimental.pallas import tpu as pltpu

F32 = jnp.float32
BF16 = jnp.bfloat16
I32 = jnp.int32

N_HEADS = 16
HEAD_DIM = 128
N_KV_HEADS = 4
GQA = N_HEADS // N_KV_HEADS
IDX_HEADS = 16
IDX_DIM = 64
TOPK_MAX = 256
NUM_BUCKETS = 32
MAX_DISTANCE = 128
CHUNK = 128
GMLP_GROUPS = 16
X_HEADS = 4
N_KEYS = 128
PEER_HEADS = 8
PEER_TOPK = 16
EPS = 1e-6

LANES = 128
SUBLANES = 8
VMEM_LIMIT_BYTES = 56 * 1024 * 1024

NEG = -1e30
INT_MIN = -(2 ** 31)


def _dot_nt(a, b):
    return lax.dot_general(a, b, (((1,), (1,)), ((), ())), preferred_element_type=F32)


def _dot(a, b):
    return jnp.dot(a, b, preferred_element_type=F32)


def _params(*sem):
    return pltpu.CompilerParams(dimension_semantics=sem, vmem_limit_bytes=VMEM_LIMIT_BYTES)


def _sortable(x):
    bits = pltpu.bitcast(x, I32)
    return bits ^ (jnp.right_shift(bits, 31) & 0x7FFFFFFF)


def _t5_bucket(dist):
    n = jnp.maximum(dist, 0)
    max_exact = NUM_BUCKETS // 2
    nf = jnp.maximum(n, 1).astype(F32)
    large = max_exact + (jnp.log(nf / max_exact) / math.log(MAX_DISTANCE / max_exact)
                         * (NUM_BUCKETS - max_exact)).astype(I32)
    large = jnp.minimum(large, NUM_BUCKETS - 1)
    return jnp.where(n < max_exact, n, large)


def _kth_largest_key(count_ge, shape, k):
    def body(it, t):
        cand = t | jnp.left_shift(jnp.int32(1), 31 - it)
        cnt = count_ge(cand ^ INT_MIN)
        return jnp.where(cnt >= k, cand, t)

    t = lax.fori_loop(0, 32, body, jnp.zeros(shape, I32))
    return t ^ INT_MIN


def _prefix_matrix():
    return (lax.broadcasted_iota(I32, (LANES, LANES), 0) <= lax.broadcasted_iota(I32, (LANES, LANES), 1)).astype(BF16)


def _proj_kernel(x_ref, g_ref, w_ref, *rest, norm):
    (w2_ref, o_ref, o2_ref, xn_ref) = rest if len(rest) == 4 else (None, rest[0], None, rest[1])

    @pl.when(pl.program_id(1) == 0)
    def _():
        x = x_ref[...]
        if norm:
            x = x * lax.rsqrt(jnp.mean(x * x, axis=-1, keepdims=True) + EPS) * g_ref[...]
        xn_ref[...] = x.astype(BF16)
        if w2_ref is not None:
            o2_ref[...] = _dot(xn_ref[...], w2_ref[...])

    o_ref[...] = _dot(xn_ref[...], w_ref[...])


def _proj(x, g, w, *, norm, tm, tn, name, w_side=None):
    m, k = x.shape
    n = w.shape[1]
    tm, tn = min(tm, m), min(tn, n)
    in_specs = [pl.BlockSpec((tm, k), lambda i, j: (i, 0)),
                pl.BlockSpec((1, k), lambda i, j: (0, 0)),
                pl.BlockSpec((k, tn), lambda i, j: (0, j))]
    out_specs = [pl.BlockSpec((tm, tn), lambda i, j: (i, j))]
    out_shape = [jax.ShapeDtypeStruct((m, n), F32)]
    args = [x, g.reshape(1, k), w]
    if w_side is not None:
        n2 = w_side.shape[1]
        in_specs.append(pl.BlockSpec((k, n2), lambda i, j: (0, 0)))
        out_specs.append(pl.BlockSpec((tm, n2), lambda i, j: (i, 0)))
        out_shape.append(jax.ShapeDtypeStruct((m, n2), F32))
        args.append(w_side)
    outs = pl.pallas_call(
        functools.partial(_proj_kernel, norm=norm),
        grid=(m // tm, n // tn),
        in_specs=in_specs,
        out_specs=out_specs,
        out_shape=out_shape,
        scratch_shapes=[pltpu.VMEM((tm, k), BF16)],
        compiler_params=_params("arbitrary", "arbitrary"),
        name=name,
    )(*args)
    return outs[0] if w_side is None else outs


def _bias_tiles_kernel(rb_ref, o_ref):
    tq = lax.broadcasted_iota(I32, (LANES, LANES), 0)
    c = lax.broadcasted_iota(I32, (LANES, LANES), 1)
    for part, off in enumerate((2 * LANES, LANES, 0)):
        dist = tq - c + off
        bucket = _t5_bucket(dist)
        for h in range(N_HEADS):
            tile = lax.fori_loop(
                0, NUM_BUCKETS, lambda b, acc: jnp.where(bucket == b, rb_ref[b, h], acc),
                jnp.zeros((LANES, LANES), F32))
            if off == 0:
                tile = jnp.where(dist < 0, NEG, tile)
            o_ref[part, h] = tile


def _bias_tiles(rel_bias):
    return pl.pallas_call(
        _bias_tiles_kernel,
        in_specs=[pl.BlockSpec(memory_space=pltpu.SMEM)],
        out_specs=pl.BlockSpec(memory_space=pltpu.VMEM),
        out_shape=jax.ShapeDtypeStruct((3, N_HEADS, LANES, LANES), F32),
        name="t5_bias_tiles",
    )(rel_bias)


def _dsa_prompt_kernel(q_ref, k_ref, v_ref, qi_ref, zsk_ref, zsq_ref, bt_ref, tri_ref, o_ref,
                       kbf, vbf, kilo, kihi, key3, qs, m_s, l_s, acc_s, *, topk):
    i = pl.program_id(1)
    tq = q_ref.shape[1]
    s = k_ref.shape[1]
    nkb = s // LANES
    cw = 2 * LANES

    @pl.when(i == 0)
    def _():
        kbf[...] = k_ref[0].astype(BF16)
        vbf[...] = v_ref[0].astype(BF16)
        zs = zsk_ref[0]
        lane = lax.broadcasted_iota(I32, zs.shape, 1)
        kilo[...] = jnp.where(lane < IDX_DIM, zs, 0.0).astype(BF16)
        kihi[...] = jnp.where(lane >= IDX_DIM, pltpu.roll(zs, IDX_DIM, axis=1), 0.0).astype(BF16)

    qi = qi_ref[0].astype(BF16)
    coef = zsq_ref[0][:, IDX_DIM:IDX_DIM + IDX_HEADS] * (IDX_HEADS ** -0.5 * IDX_DIM ** -0.5)
    t_pos = i * tq + lax.broadcasted_iota(I32, (tq, cw), 0)
    t_last = i * tq + tq - 1
    for c in range(s // cw):
        @pl.when(c * cw <= t_last)
        def _():
            klo = kilo[c * cw:(c + 1) * cw, :]
            khi = kihi[c * cw:(c + 1) * cw, :]
            acc = jnp.zeros((tq, cw), F32)
            for p in range(IDX_HEADS // 2):
                a = qi[:, p * LANES:(p + 1) * LANES]
                acc = acc + coef[:, 2 * p:2 * p + 1] * jnp.maximum(_dot_nt(a, klo), 0.0)
                acc = acc + coef[:, 2 * p + 1:2 * p + 2] * jnp.maximum(_dot_nt(a, khi), 0.0)
            s_pos = c * cw + lax.broadcasted_iota(I32, (tq, cw), 1)
            key = _sortable(jnp.where(s_pos <= t_pos, acc, -jnp.inf))
            for u in range(cw // LANES):
                key3[c * (cw // LANES) + u] = key[:, u * LANES:(u + 1) * LANES]

        @pl.when(c * cw > t_last)
        def _():
            for u in range(cw // LANES):
                key3[c * (cw // LANES) + u] = jnp.full((tq, LANES), INT_MIN, I32)

    def count(pred):
        return jnp.sum(jnp.sum(jnp.where(pred, 1.0, 0.0), axis=0), axis=1, keepdims=True)

    thr = _kth_largest_key(lambda cs: count(key3[...] >= cs[None]), (tq, 1), float(topk))

    n_ge = count(key3[...] >= thr[None])

    @pl.when(jnp.max(n_ge) > float(topk))
    def _():
        need = float(topk) - count(key3[...] > thr[None])
        before = jnp.zeros((tq, 1), F32)
        for j in range(nkb):
            kj = key3[j]
            eq = kj == thr
            rank = _dot(jnp.where(eq, 1.0, 0.0).astype(BF16), tri_ref[...]) + before
            key3[j] = jnp.where(jnp.logical_and(eq, rank > need), INT_MIN, kj)
            before = rank[:, LANES - 1:LANES]

    thr4 = jnp.concatenate([thr] * GQA, axis=0)

    qb = q_ref[0].astype(BF16)
    for kh in range(N_KV_HEADS):
        qs[kh] = jnp.concatenate(
            [qb[:, (GQA * kh + g) * HEAD_DIM:(GQA * kh + g + 1) * HEAD_DIM] for g in range(GQA)], axis=0)
    m_s[...] = jnp.full(m_s.shape, NEG, F32)
    l_s[...] = jnp.zeros(l_s.shape, F32)
    acc_s[...] = jnp.zeros(acc_s.shape, F32)
    scale = HEAD_DIM ** -0.5

    def fbody(jj, carry):
        ja = 2 * jj
        jb = ja + 1
        part_a = jnp.clip(ja - i + 2, 0, 2)
        part_b = jnp.clip(jb - i + 2, 0, 2)
        key_b = jnp.where(jb <= i, key3[jb], INT_MIN)
        key8 = jnp.concatenate([key3[ja], key_b], axis=1)
        mask = jnp.concatenate([key8] * GQA, axis=0) >= thr4
        r0 = pl.multiple_of(ja * LANES, 2 * LANES)
        for kh in range(N_KV_HEADS):
            kj = kbf[pl.ds(r0, 2 * LANES), kh * HEAD_DIM:(kh + 1) * HEAD_DIM]
            vj = vbf[pl.ds(r0, 2 * LANES), kh * HEAD_DIM:(kh + 1) * HEAD_DIM]
            bias = jnp.concatenate(
                [bt_ref[part_a, GQA * kh:GQA * (kh + 1)].reshape(GQA * tq, LANES),
                 bt_ref[part_b, GQA * kh:GQA * (kh + 1)].reshape(GQA * tq, LANES)], axis=1)
            lg = jnp.where(mask, _dot_nt(qs[kh], kj) * scale + bias, NEG)
            m_old = m_s[kh]
            m_new = jnp.maximum(m_old, jnp.max(lg, axis=1, keepdims=True))
            alpha = jnp.exp(m_old - m_new)
            p = jnp.where(mask, jnp.exp(lg - jnp.concatenate([m_new, m_new], axis=1)), 0.0)
            l_s[kh] = alpha * l_s[kh] + jnp.sum(p, axis=1, keepdims=True)
            acc_s[kh] = alpha * acc_s[kh] + _dot(p.astype(BF16), vj)
            m_s[kh] = m_new
        return carry

    assert nkb % 2 == 0
    lax.fori_loop(0, jnp.right_shift(i + 2, 1), fbody, 0)
    for kh in range(N_KV_HEADS):
        out = acc_s[kh] / l_s[kh]
        for g in range(GQA):
            h = GQA * kh + g
            o_ref[0, :, h * HEAD_DIM:(h + 1) * HEAD_DIM] = out[g * tq:(g + 1) * tq]


def _dsa_prompt(z, zs, btiles, *, batch, seq, topk):
    tq = LANES
    aw = N_HEADS * HEAD_DIM
    kvw = N_KV_HEADS * HEAD_DIM
    qiw = IDX_HEADS * IDX_DIM
    return pl.pallas_call(
        functools.partial(_dsa_prompt_kernel, topk=topk),
        grid=(batch, seq // tq),
        in_specs=[
            pl.BlockSpec((1, tq, aw), lambda b, i: (b, i, 0)),
            pl.BlockSpec((1, seq, kvw), lambda b, i: (b, 0, aw // kvw)),
            pl.BlockSpec((1, seq, kvw), lambda b, i: (b, 0, aw // kvw + 1)),
            pl.BlockSpec((1, tq, qiw), lambda b, i: (b, i, (aw + 2 * kvw) // qiw)),
            pl.BlockSpec((1, seq, LANES), lambda b, i: (b, 0, 0)),
            pl.BlockSpec((1, tq, LANES), lambda b, i: (b, i, 0)),
            pl.BlockSpec((3, N_HEADS, LANES, LANES), lambda b, i: (0, 0, 0, 0)),
            pl.BlockSpec((LANES, LANES), lambda b, i: (0, 0)),
        ],
        out_specs=pl.BlockSpec((1, tq, aw), lambda b, i: (b, i, 0)),
        out_shape=jax.ShapeDtypeStruct((batch, seq, aw), F32),
        scratch_shapes=[
            pltpu.VMEM((seq, kvw), BF16), pltpu.VMEM((seq, kvw), BF16),
            pltpu.VMEM((seq, LANES), BF16), pltpu.VMEM((seq, LANES), BF16),
            pltpu.VMEM((seq // LANES, tq, LANES), I32),
            pltpu.VMEM((N_KV_HEADS, GQA * tq, HEAD_DIM), BF16),
            pltpu.VMEM((N_KV_HEADS, GQA * tq, LANES), F32),
            pltpu.VMEM((N_KV_HEADS, GQA * tq, LANES), F32),
            pltpu.VMEM((N_KV_HEADS, GQA * tq, HEAD_DIM), F32),
        ],
        compiler_params=_params("arbitrary", "arbitrary"),
        name="dsa_prompt",
    )(z, z, z, z, zs, zs, btiles, _prefix_matrix())


SAMPLE_PAGES_PER_STEP = 16


def _dsa_sample_score_kernel(pt_ref, qi_ref, wi_ref, kinew_ref, *rest):
    kidx_refs, (o_ref, onew_ref) = rest[:-2], rest[-2:]
    qi = qi_ref[0].astype(BF16)
    coef = wi_ref[0] * (IDX_HEADS ** -0.5 * IDX_DIM ** -0.5)

    def score(ki):
        return jnp.sum(jnp.maximum(_dot_nt(qi, ki.astype(BF16)), 0.0) * coef, axis=0, keepdims=True)

    o_ref[0] = jnp.concatenate([score(r[0]) for r in kidx_refs], axis=1)

    @pl.when(pl.program_id(1) == pl.num_programs(1) - 1)
    def _():
        onew_ref[0] = score(jnp.broadcast_to(kinew_ref[0], (LANES, kinew_ref.shape[2])))


def _dsa_sample_scores(page_table, qi3, wi3, ki_new, cache_kidx):
    db, n_pages = page_table.shape
    page = cache_kidx.shape[1]
    pg = math.gcd(SAMPLE_PAGES_PER_STEP, n_pages)
    row = lambda b, j, pt: (b, 0, 0)
    grid_spec = pltpu.PrefetchScalarGridSpec(
        num_scalar_prefetch=1,
        grid=(db, n_pages // pg),
        in_specs=[pl.BlockSpec((1, IDX_HEADS, IDX_DIM), row),
                  pl.BlockSpec((1, IDX_HEADS, 1), row),
                  pl.BlockSpec((1, 1, IDX_DIM), row)]
        + [pl.BlockSpec((1, page, IDX_DIM), functools.partial(lambda b, j, pt, u: (pt[b, j * pg + u], 0, 0), u=u))
           for u in range(pg)],
        out_specs=[pl.BlockSpec((1, 1, pg * page), lambda b, j, pt: (b, 0, j)),
                   pl.BlockSpec((1, 1, LANES), row)],
    )
    return pl.pallas_call(
        _dsa_sample_score_kernel,
        grid_spec=grid_spec,
        out_shape=[jax.ShapeDtypeStruct((db, 1, n_pages * page), F32),
                   jax.ShapeDtypeStruct((db, 1, LANES), F32)],
        compiler_params=_params("arbitrary", "arbitrary"),
        name="dsa_sample_scores",
    )(page_table, qi3, wi3, ki_new, *([cache_kidx] * pg))


def _dsa_sample_select_kernel(s_ref, snew_ref, tri_ref, pos_ref, new_ref, rank_s, nsel_s, *, topk):
    db, past = s_ref.shape
    k = float(topk)
    key = _sortable(s_ref[...])
    key_new = _sortable(snew_ref[:, 0:1])

    def count_ge(cs):
        return (jnp.sum(jnp.where(key >= cs, 1.0, 0.0), axis=1, keepdims=True)
                + jnp.where(key_new >= cs, 1.0, 0.0))

    thr = _kth_largest_key(count_ge, (db, 1), k)
    gt = key > thr
    eq = key == thr
    need = k - (jnp.sum(jnp.where(gt, 1.0, 0.0), axis=1, keepdims=True) + jnp.where(key_new > thr, 1.0, 0.0))

    tri = tri_ref[...]

    def cumsum(flags):
        out, off = [], jnp.zeros((db, 1), F32)
        for blk in range(past // LANES):
            pc = _dot(flags[:, blk * LANES:(blk + 1) * LANES].astype(BF16), tri) + off
            out.append(pc)
            off = pc[:, LANES - 1:LANES]
        return jnp.concatenate(out, axis=1), off

    eq_rank, eq_total = cumsum(jnp.where(eq, 1.0, 0.0))
    sel = jnp.where(jnp.logical_or(gt, jnp.logical_and(eq, eq_rank <= need)), 1.0, 0.0)
    new_sel = jnp.logical_or(key_new > thr, jnp.logical_and(key_new == thr, eq_total < need))
    new_ref[...] = jnp.broadcast_to(jnp.where(new_sel, 1.0, 0.0), new_ref.shape)
    rank, n_sel = cumsum(sel)
    rank = jnp.where(sel > 0.0, rank, 0.0)
    for b in range(db):
        rank_s[b] = rank[b:b + 1, :]
        nsel_s[b] = jnp.broadcast_to(n_sel[b:b + 1, :], (1, LANES))
    ch = min(8 * LANES, past)
    slot = lax.broadcasted_iota(I32, (topk, ch), 0).astype(F32) + 1.0
    lane = lax.broadcasted_iota(I32, (topk, ch), 1).astype(F32)
    slot_col = lax.broadcasted_iota(I32, (topk, 1), 0).astype(F32)

    def extract(b, carry):
        r = rank_s[b]
        pos = jnp.zeros((topk, 1), F32)
        for c in range(past // ch):
            hit = r[:, c * ch:(c + 1) * ch] == slot
            pos = pos + jnp.sum(jnp.where(hit, lane + float(c * ch), 0.0), axis=1, keepdims=True)
        pos = jnp.where(slot_col < nsel_s[b][:, 0:1], pos, -1.0)
        pos_ref[b] = jnp.broadcast_to(pos, (topk, LANES)).astype(I32)
        return carry

    lax.fori_loop(0, db, extract, 0)


def _dsa_sample_select(scores, snew, *, topk):
    db = scores.shape[0]
    tri = _prefix_matrix()
    return pl.pallas_call(
        functools.partial(_dsa_sample_select_kernel, topk=topk),
        out_shape=[jax.ShapeDtypeStruct((db, topk, LANES), I32), jax.ShapeDtypeStruct((db, LANES), F32)],
        scratch_shapes=[pltpu.VMEM((db, 1, scores.shape[1]), F32), pltpu.VMEM((db, 1, LANES), F32)],
        compiler_params=pltpu.CompilerParams(vmem_limit_bytes=VMEM_LIMIT_BYTES),
        name="dsa_sample_select",
    )(scores, snew, tri)


def _dsa_sample_attn_kernel(pos_s, pt_s, q_ref, knew_ref, vnew_ref, posrow_ref, newsel_ref, rbt_ref,
                            ck_hbm, cv_hbm, o_ref, kbuf, vbuf, sem_k, sem_v, *, layer, past, page):
    b = pl.program_id(0)
    topk = kbuf.shape[1]
    slot = lax.rem(b, 2)
    page_shift = page.bit_length() - 1
    unroll = 8

    def row_copies(bb, j, sl):
        p = jnp.maximum(pos_s[bb * topk + j], 0)
        pg = pt_s[bb, jnp.right_shift(p, page_shift)]
        off = p & (page - 1)
        return (pltpu.make_async_copy(ck_hbm.at[layer, pg, off], kbuf.at[sl, j], sem_k.at[sl]),
                pltpu.make_async_copy(cv_hbm.at[layer, pg, off], vbuf.at[sl, j], sem_v.at[sl]))

    def issue(bb, sl):
        def body(i, carry):
            for u in range(unroll):
                ck, cv = row_copies(bb, i * unroll + u, sl)
                ck.start()
                cv.start()
            return carry
        lax.fori_loop(0, topk // unroll, body, 0)

    def wait(sl):
        def body(i, carry):
            for u in range(unroll):
                j = i * unroll + u
                pltpu.make_async_copy(ck_hbm.at[layer, 0, 0], kbuf.at[sl, j], sem_k.at[sl]).wait()
                pltpu.make_async_copy(cv_hbm.at[layer, 0, 0], vbuf.at[sl, j], sem_v.at[sl]).wait()
            return carry
        lax.fori_loop(0, topk // unroll, body, 0)

    @pl.when(b == 0)
    def _():
        issue(0, 0)

    @pl.when(b + 1 < pl.num_programs(0))
    def _():
        issue(b + 1, 1 - slot)

    wait(slot)

    q = q_ref[0].astype(BF16)
    posr = posrow_ref[0]
    valid = posr >= 0
    bucket = _t5_bucket(past - posr)
    bias = jnp.zeros((N_HEADS, topk), F32)
    for bk in range(NUM_BUCKETS):
        bias = jnp.where(bucket == bk, rbt_ref[:, bk:bk + 1], bias)
    shift = GQA.bit_length() - 1
    kv_of_row = jnp.right_shift(lax.broadcasted_iota(I32, (N_HEADS, topk), 0), shift)
    kv_of_row_d = jnp.right_shift(lax.broadcasted_iota(I32, (N_HEADS, HEAD_DIM), 0), shift)
    scale = HEAD_DIM ** -0.5
    qf = q.astype(F32)
    knew = knew_ref[0].astype(BF16).astype(F32)
    vnew = vnew_ref[0].astype(BF16).astype(F32)
    lg = jnp.zeros((N_HEADS, topk), F32)
    lg_new = jnp.zeros((N_HEADS, 1), F32)
    v_new_rows = jnp.zeros((N_HEADS, HEAD_DIM), F32)
    for kh in range(N_KV_HEADS):
        hs = slice(kh * HEAD_DIM, (kh + 1) * HEAD_DIM)
        lg = jnp.where(kv_of_row == kh, _dot_nt(q, kbuf[slot, :, kh, :].astype(BF16)), lg)
        lg_new = jnp.where(kv_of_row[:, 0:1] == kh, jnp.sum(qf * knew[:, hs], axis=1, keepdims=True), lg_new)
        v_new_rows = jnp.where(kv_of_row_d == kh, jnp.broadcast_to(vnew[:, hs], (N_HEADS, HEAD_DIM)), v_new_rows)
    lg = jnp.where(valid, lg * scale + bias, NEG)
    new_sel = newsel_ref[0][:, 0:1] > 0.5
    lg_new = jnp.where(new_sel, lg_new * scale + rbt_ref[:, 0:1], NEG)
    m = jnp.maximum(jnp.max(lg, axis=1, keepdims=True), lg_new)
    p = jnp.where(valid, jnp.exp(lg - m), 0.0)
    p_new = jnp.where(new_sel, jnp.exp(lg_new - m), 0.0)
    denom = jnp.sum(p, axis=1, keepdims=True) + p_new
    pb = (p / denom).astype(BF16)
    out = (p_new / denom).astype(BF16).astype(F32) * v_new_rows
    for kh in range(N_KV_HEADS):
        out = out + jnp.where(kv_of_row_d == kh, _dot(pb, vbuf[slot, :, kh, :].astype(BF16)), 0.0)
    o_ref[0] = out


def _dsa_sample_attn(pos, page_table, q3, k_new, v_new, new_sel, rb_t, cache_k, cache_v, *, layer):
    db, topk = pos.shape
    n_pages = page_table.shape[1]
    page, n_kv, hd = cache_k.shape[2:]
    assert page & (page - 1) == 0
    kvw = n_kv * hd
    row = lambda b, ps, pt: (b, 0, 0)
    grid_spec = pltpu.PrefetchScalarGridSpec(
        num_scalar_prefetch=2,
        grid=(db,),
        in_specs=[
            pl.BlockSpec((1, N_HEADS, HEAD_DIM), row),
            pl.BlockSpec((1, 1, kvw), row),
            pl.BlockSpec((1, 1, kvw), row),
            pl.BlockSpec((1, 1, topk), row),
            pl.BlockSpec((1, 1, LANES), row),
            pl.BlockSpec((N_HEADS, NUM_BUCKETS), lambda b, ps, pt: (0, 0)),
            pl.BlockSpec(memory_space=pl.ANY),
            pl.BlockSpec(memory_space=pl.ANY),
        ],
        out_specs=pl.BlockSpec((1, N_HEADS, HEAD_DIM), row),
        scratch_shapes=[pltpu.VMEM((2, topk, n_kv, hd), F32), pltpu.VMEM((2, topk, n_kv, hd), F32),
                        pltpu.SemaphoreType.DMA((2,)), pltpu.SemaphoreType.DMA((2,))],
    )
    return pl.pallas_call(
        functools.partial(_dsa_sample_attn_kernel, layer=layer, past=n_pages * page, page=page),
        grid_spec=grid_spec,
        out_shape=jax.ShapeDtypeStruct((db, N_HEADS, HEAD_DIM), F32),
        compiler_params=_params("arbitrary"),
        name="dsa_sample_attn",
    )(pos.reshape(-1), page_table, q3, k_new, v_new, pos.reshape(db, 1, topk),
      new_sel.reshape(db, 1, LANES), rb_t, cache_k, cache_v)


def _gate_kernel(att_ref, zu_ref, zv_ref, ga_ref, gb_ref, gs_ref, ws_ref, bs_ref, m_ref, vn_ref, *, first_rows_only):
    aw = att_ref.shape[1]
    tm = zu_ref.shape[0]
    v = jax.nn.gelu(zv_ref[...])
    vn = v * lax.rsqrt(jnp.mean(v * v, axis=-1, keepdims=True) + EPS) * gs_ref[...]
    vn_ref[...] = vn
    m_ref[:, :aw] = (jax.nn.sigmoid(ga_ref[...]) * att_ref[...]).astype(BF16)
    if first_rows_only:
        sgu = jax.nn.gelu(zu_ref[...]) * (vn * ws_ref[...] + bs_ref[...])
        m_ref[:, aw:] = (jax.nn.sigmoid(gb_ref[...]) * sgu).astype(BF16)
    else:
        row = lax.broadcasted_iota(I32, (CHUNK, CHUNK), 0)
        col = lax.broadcasted_iota(I32, (CHUNK, CHUNK), 1)
        for g in range(GMLP_GROUPS):
            cs = slice(g * LANES, (g + 1) * LANES)
            wt = jnp.where(col <= row, ws_ref[g], 0.0).astype(BF16)
            bcol = bs_ref[:, g:g + 1]
            for c in range(tm // CHUNK):
                rs = slice(c * CHUNK, (c + 1) * CHUNK)
                mixed = _dot(wt, vn_ref[rs, cs].astype(BF16)) + bcol
                sgu = jax.nn.gelu(zu_ref[rs, cs]) * mixed
                m_ref[rs, aw + g * LANES:aw + (g + 1) * LANES] = (
                    jax.nn.sigmoid(gb_ref[rs, cs]) * sgu).astype(BF16)


def _gate(att, z, g_sgu, ws, bs, *, first_rows_only, tm, name):
    m, aw = att.shape
    gw = g_sgu.shape[0]
    tm = min(tm, m)
    zb = lambda c: pl.BlockSpec((tm, gw), lambda i: (i, c))
    full = lambda a: pl.BlockSpec(a.shape, lambda i: (0,) * a.ndim)
    first = z.shape[1] // gw - 4
    return pl.pallas_call(
        functools.partial(_gate_kernel, first_rows_only=first_rows_only),
        grid=(m // tm,),
        in_specs=[pl.BlockSpec((tm, aw), lambda i: (i, 0)),
                  zb(first), zb(first + 1), zb(first + 2), zb(first + 3),
                  pl.BlockSpec((1, gw), lambda i: (0, 0)), full(ws), full(bs)],
        out_specs=[pl.BlockSpec((tm, aw + gw), lambda i: (i, 0)), pl.BlockSpec((tm, gw), lambda i: (i, 0))],
        out_shape=[jax.ShapeDtypeStruct((m, aw + gw), BF16), jax.ShapeDtypeStruct((m, gw), F32)],
        compiler_params=_params("arbitrary"),
        name=name,
    )(att, z, z, z, z, g_sgu.reshape(1, gw), ws, bs)


def _matmul_residual_kernel(x_ref, w_ref, h_ref, o_ref):
    o_ref[...] = h_ref[...] + _dot(x_ref[...], w_ref[...])


def _matmul_residual(x, w, h, *, tm, tn, name):
    m, k = x.shape
    n = w.shape[1]
    tm, tn = min(tm, m), min(tn, n)
    return pl.pallas_call(
        _matmul_residual_kernel,
        grid=(m // tm, n // tn),
        in_specs=[pl.BlockSpec((tm, k), lambda i, j: (i, 0)),
                  pl.BlockSpec((k, tn), lambda i, j: (0, j)),
                  pl.BlockSpec((tm, tn), lambda i, j: (i, j))],
        out_specs=pl.BlockSpec((tm, tn), lambda i, j: (i, j)),
        out_shape=jax.ShapeDtypeStruct((m, n), F32),
        compiler_params=_params("arbitrary", "arbitrary"),
        name=name,
    )(x, w, h)


def _cross_kernel(q_ref, mk_ref, mv_ref, h_ref, wo_ref, o_ref):
    rows = q_ref.shape[1]
    q = q_ref[0]
    if rows < SUBLANES:
        q = jnp.broadcast_to(q, (SUBLANES, q.shape[1]))
    qb = q.astype(BF16)
    mk = mk_ref[0].astype(BF16)
    mv = mv_ref[0].astype(BF16)
    hd = mk.shape[1] // X_HEADS
    outs = []
    for hh in range(X_HEADS):
        sl = slice(hh * hd, (hh + 1) * hd)
        lg = _dot_nt(qb[:, sl], mk[:, sl]) * hd ** -0.5
        e = jnp.exp(lg - jnp.max(lg, axis=1, keepdims=True))
        p = e / jnp.sum(e, axis=1, keepdims=True)
        outs.append(_dot(p.astype(BF16), mv[:, sl]))
    y = _dot(jnp.concatenate(outs, axis=1).astype(BF16), wo_ref[...])
    o_ref[0] = h_ref[0] + y[:rows]


def _cross(q, mk, mv, h, w_xo, *, tq, name):
    b, t, xw = q.shape
    d = h.shape[2]
    mlen = mk.shape[1]
    return pl.pallas_call(
        _cross_kernel,
        grid=(b, t // tq),
        in_specs=[pl.BlockSpec((1, tq, xw), lambda bb, i: (bb, i, 0)),
                  pl.BlockSpec((1, mlen, xw), lambda bb, i: (bb, 0, 0)),
                  pl.BlockSpec((1, mlen, xw), lambda bb, i: (bb, 0, 0)),
                  pl.BlockSpec((1, tq, d), lambda bb, i: (bb, i, 0)),
                  pl.BlockSpec((xw, d), lambda bb, i: (0, 0))],
        out_specs=pl.BlockSpec((1, tq, d), lambda bb, i: (bb, i, 0)),
        out_shape=jax.ShapeDtypeStruct((b, t, d), F32),
        compiler_params=_params("arbitrary", "arbitrary"),
        name=name,
    )(q, mk, mv, h, w_xo)


def _top_rows(s, k, payload=None):
    r = s.shape[0]
    rid = lax.broadcasted_iota(I32, s.shape, 0).astype(F32)
    vals, picks = [], []
    for _ in range(k):
        m = jnp.max(s, axis=0, keepdims=True)
        am = jnp.min(jnp.where(s == m, rid, float(r)), axis=0, keepdims=True)
        hit = rid == am
        vals.append(m)
        picks.append(am if payload is None else jnp.sum(jnp.where(hit, payload, 0.0), axis=0, keepdims=True))
        s = jnp.where(hit, -jnp.inf, s)
    return jnp.concatenate(vals, axis=0), jnp.concatenate(picks, axis=0)


def _peer_route_kernel(h_ref, g_ref, w_ref, sk_ref, ids_ref, gate_ref, xn_ref):
    @pl.when(pl.program_id(1) == 0)
    def _():
        x = h_ref[...]
        xn_ref[...] = (x * lax.rsqrt(jnp.mean(x * x, axis=-1, keepdims=True) + EPS) * g_ref[...]).astype(BF16)

    half = sk_ref.shape[2]
    qt = _dot_nt(w_ref[...], xn_ref[...]).astype(BF16)
    v0, i0 = _top_rows(_dot(sk_ref[0], qt[:half]), PEER_TOPK)
    v1, i1 = _top_rows(_dot(sk_ref[1], qt[half:]), PEER_TOPK)
    widths = [PEER_TOPK // (a + 1) for a in range(PEER_TOPK)]
    pad = -sum(widths) % SUBLANES
    cand = jnp.concatenate([v0[a:a + 1] + v1[:n] for a, n in enumerate(widths)]
                           + [jnp.full((pad, v0.shape[1]), -jnp.inf, F32)], axis=0)
    eid = jnp.concatenate([i0[a:a + 1] * float(N_KEYS) + i1[:n] for a, n in enumerate(widths)]
                          + [jnp.zeros((pad, v0.shape[1]), F32)], axis=0)
    sc, e = _top_rows(cand, PEER_TOPK, payload=eid)
    ex = jnp.exp(sc - sc[0:1])
    gate_ref[...] = ex / jnp.sum(ex, axis=0, keepdims=True)
    ids_ref[...] = e.astype(I32)


def _peer_route(h, g_ffn, w_pq_t, sub_keys, *, tb):
    m, d = h.shape
    qd = w_pq_t.shape[0] // PEER_HEADS
    tb = min(tb, m)
    return pl.pallas_call(
        _peer_route_kernel,
        grid=(m // tb, PEER_HEADS),
        in_specs=[pl.BlockSpec((tb, d), lambda t, hh: (t, 0)),
                  pl.BlockSpec((1, d), lambda t, hh: (0, 0)),
                  pl.BlockSpec((qd, d), lambda t, hh: (hh, 0)),
                  pl.BlockSpec(sub_keys.shape, lambda t, hh: (0, 0, 0))],
        out_specs=[pl.BlockSpec((PEER_TOPK, tb), lambda t, hh: (hh, t)),
                   pl.BlockSpec((PEER_TOPK, tb), lambda t, hh: (hh, t))],
        out_shape=[jax.ShapeDtypeStruct((PEER_HEADS * PEER_TOPK, m), I32),
                   jax.ShapeDtypeStruct((PEER_HEADS * PEER_TOPK, m), F32)],
        scratch_shapes=[pltpu.VMEM((tb, d), BF16)],
        compiler_params=_params("arbitrary", "arbitrary"),
        name="peer_route",
    )(h, g_ffn.reshape(1, d), w_pq_t, sub_keys)


def _pack_tables_kernel(u_ref, v_ref, o_ref):
    nr = o_ref.shape[1] // 2

    def words(x):
        bits = pltpu.bitcast(x.astype(BF16).astype(F32), I32)
        half = bits.shape[1] // 2
        return (bits[:, half:] & jnp.int32(-65536)) | lax.shift_right_logical(bits[:, :half], 16)

    wu = words(u_ref[...])
    wv = words(v_ref[...])
    for r in range(nr):
        o_ref[:, r, :] = wu[:, r * LANES:(r + 1) * LANES]
        o_ref[:, nr + r, :] = wv[:, r * LANES:(r + 1) * LANES]


def _pack_tables(u_tab, v_tab, *, eb):
    n, d = u_tab.shape
    nr = d // (2 * LANES)
    return pl.pallas_call(
        _pack_tables_kernel,
        grid=(n // eb,),
        in_specs=[pl.BlockSpec((eb, d), lambda i: (i, 0)), pl.BlockSpec((eb, d), lambda i: (i, 0))],
        out_specs=pl.BlockSpec((eb, 2 * nr, LANES), lambda i: (i, 0, 0)),
        out_shape=jax.ShapeDtypeStruct((n, 2 * nr, LANES), I32),
        compiler_params=_params("arbitrary"),
        name="peer_pack_tables",
    )(u_tab, v_tab)


def _unpack_words(w):
    return pltpu.bitcast(jnp.left_shift(w, 16), F32), pltpu.bitcast(w & jnp.int32(-65536), F32)


def _peer_gather_kernel(ids_hbm, gate_ref, h_ref, gffn_ref, gfin_ref, tab_hbm, o_ref,
                        ids_s, buf, part_s, sem_ids, sem_rows, *, n_steps):
    s = pl.program_id(0)
    g_tok = h_ref.shape[0]
    nr = buf.shape[1] // 2
    n_e = buf.shape[2]
    n_buf = buf.shape[0]
    ahead = n_buf - 1
    assert g_tok % n_buf == 0 and ahead <= g_tok
    cur = lax.rem(s, 2)
    nxt = 1 - cur

    def ids_copy(step, slot):
        return pltpu.make_async_copy(ids_hbm.at[step], ids_s.at[slot], sem_ids.at[slot])

    def row_copy(e, k, slot):
        return pltpu.make_async_copy(tab_hbm.at[e], buf.at[slot, :, k, :], sem_rows.at[slot])

    def issue(ids_slot, t, slot):
        for k in range(n_e):
            row_copy(ids_s[ids_slot, t, k], k, slot).start(priority=k % 2)

    def wait(slot):
        for k in range(n_e):
            row_copy(0, k, slot).wait()

    @pl.when(s == 0)
    def _():
        ids_copy(0, 0).start()
        ids_copy(0, 0).wait()
        for t in range(ahead):
            issue(0, t, t % n_buf)

    @pl.when(s + 1 < n_steps)
    def _():
        ids_copy(s + 1, nxt).start()

    h = h_ref[...]
    xn = h * lax.rsqrt(jnp.mean(h * h, axis=-1, keepdims=True) + EPS) * gffn_ref[...]
    eye = lax.broadcasted_iota(I32, (n_e, n_e), 0) == lax.broadcasted_iota(I32, (n_e, n_e), 1)

    for t in range(g_tok):
        slot = t % n_buf
        ta = t + ahead
        if ta < g_tok:
            issue(cur, ta, ta % n_buf)
        else:
            @pl.when(s + 1 < n_steps)
            def _():
                if ta == g_tok:
                    ids_copy(s + 1, nxt).wait()
                issue(nxt, ta - g_tok, ta % n_buf)
        wait(slot)
        x = xn[t:t + 1, :]
        hacc = jnp.zeros((n_e, LANES), F32)
        for r in range(nr):
            lo, hi = _unpack_words(buf[slot, r])
            hacc = hacc + lo * x[:, r * LANES:(r + 1) * LANES] + hi * x[:, (nr + r) * LANES:(nr + r + 1) * LANES]
        act = jax.nn.gelu(jnp.sum(hacc, axis=-1, keepdims=True))
        gcol = jnp.sum(jnp.where(eye, jnp.broadcast_to(gate_ref[t:t + 1, :], (n_e, n_e)), 0.0),
                       axis=-1, keepdims=True)
        w = jnp.broadcast_to(gcol * act, (n_e, LANES))
        for r in range(nr):
            lo, hi = _unpack_words(buf[slot, nr + r])
            part_s[t, :, r * LANES:(r + 1) * LANES] = jnp.sum(
                (lo * w).reshape(n_e // SUBLANES, SUBLANES, LANES), axis=0)
            part_s[t, :, (nr + r) * LANES:(nr + r + 1) * LANES] = jnp.sum(
                (hi * w).reshape(n_e // SUBLANES, SUBLANES, LANES), axis=0)

    y = h + jnp.sum(part_s[...], axis=1)
    o_ref[...] = y * lax.rsqrt(jnp.mean(y * y, axis=-1, keepdims=True) + EPS) * gfin_ref[...]


PEER_GATHER_BUFFERS = 4


def _peer_gather(ids, gates, h, g_ffn, g_final, packed_tab, *, g_tok):
    m, d = h.shape
    n_e = ids.shape[1]
    n_steps = m // g_tok
    n_rows = packed_tab.shape[1]
    ids3 = ids.reshape(n_steps, g_tok, n_e)
    return pl.pallas_call(
        functools.partial(_peer_gather_kernel, n_steps=n_steps),
        grid=(n_steps,),
        in_specs=[pl.BlockSpec(memory_space=pl.ANY),
                  pl.BlockSpec((g_tok, n_e), lambda s: (s, 0)),
                  pl.BlockSpec((g_tok, d), lambda s: (s, 0)),
                  pl.BlockSpec((1, d), lambda s: (0, 0)),
                  pl.BlockSpec((1, d), lambda s: (0, 0)),
                  pl.BlockSpec(memory_space=pl.ANY)],
        out_specs=pl.BlockSpec((g_tok, d), lambda s: (s, 0)),
        out_shape=jax.ShapeDtypeStruct((m, d), F32),
        scratch_shapes=[pltpu.SMEM((2, g_tok, n_e), I32),
                        pltpu.VMEM((PEER_GATHER_BUFFERS, n_rows, n_e, LANES), I32),
                        pltpu.VMEM((g_tok, SUBLANES, d), F32),
                        pltpu.SemaphoreType.DMA((2,)), pltpu.SemaphoreType.DMA((PEER_GATHER_BUFFERS,))],
        compiler_params=_params("arbitrary"),
        name="peer_gather",
    )(ids3, gates, h, g_ffn.reshape(1, d), g_final.reshape(1, d), packed_tab)


def kernel(x_prompt, x_sample, cache_k, cache_v, cache_kidx, cache_mem_k, cache_mem_v, page_table,
           mem_prompt, rel_bias, g_in, w_in, g_sgu, w_s, b_s, w_out, g_x, w_xq, w_xk, w_xv, w_xo,
           g_ffn, w_pq, sub_keys, peer_u, peer_v, g_final):
    batch, seq, d = x_prompt.shape
    db, dseq, _ = x_sample.shape
    depth = w_in.shape[0]
    assert dseq == 1, "sample group is one new token per sequence"
    n_pages = page_table.shape[1]
    page = cache_k.shape[2]
    past = n_pages * page
    aw, kvw, qiw, gw = N_HEADS * HEAD_DIM, N_KV_HEADS * HEAD_DIM, IDX_HEADS * IDX_DIM, GMLP_GROUPS * LANES
    xw = w_xq.shape[2]
    mlen = mem_prompt.shape[1]
    small_lo = aw + 2 * kvw + qiw
    small_hi = small_lo + IDX_DIM + IDX_HEADS
    assert NUM_BUCKETS // 2 + int(math.log((LANES + 1) / (NUM_BUCKETS // 2)) / math.log(
        MAX_DISTANCE / (NUM_BUCKETS // 2)) * (NUM_BUCKETS // 2)) >= NUM_BUCKETS - 1

    hp = x_prompt.reshape(batch * seq, d)
    hs = x_sample.reshape(db, d)
    btiles = _bias_tiles(rel_bias)
    rb_t = rel_bias.T
    outs = {n: [] for n in ("kp", "vp", "ip", "mkp", "mvp", "ks", "vs", "is", "sv")}

    for l in range(depth):
        w_main = jnp.concatenate([w_in[l][:, :small_lo].astype(BF16), w_in[l][:, small_hi:].astype(BF16)], axis=1)
        w_small = jnp.pad(w_in[l][:, small_lo:small_hi], ((0, 0), (0, LANES - (small_hi - small_lo)))).astype(BF16)
        w_out_b = w_out[l].astype(BF16)
        w_xq_b, w_xk_b, w_xv_b, w_xo_b = (w.astype(BF16) for w in (w_xq[l], w_xk[l], w_xv[l], w_xo[l]))
        w_pq_t = w_pq[l].T.astype(BF16)
        sk_b = sub_keys[l].astype(BF16)
        ws_first = jnp.repeat(w_s[l][:, 0, 0], LANES).reshape(1, gw)
        bs_first = jnp.repeat(b_s[l][:, 0], LANES).reshape(1, gw)
        ones_d = jnp.ones((d,), F32)

        z, zs = _proj(hp, g_in[l], w_main, norm=True, tm=512, tn=1024, name="in_proj_prompt", w_side=w_small)
        att = _dsa_prompt(z.reshape(batch, seq, -1), zs.reshape(batch, seq, LANES), btiles,
                          batch=batch, seq=seq, topk=min(TOPK_MAX, seq // 4))
        m_p, _ = _gate(att.reshape(batch * seq, aw), z, g_sgu[l], w_s[l], b_s[l].T,
                       first_rows_only=False, tm=256, name="gate_prompt")
        hp1 = _matmul_residual(m_p, w_out_b, hp, tm=1024, tn=1024, name="out_proj_prompt")
        mem = mem_prompt.reshape(batch * mlen, d)
        mk = _proj(mem, ones_d, w_xk_b, norm=False, tm=256, tn=xw, name="mem_k_proj")
        mv = _proj(mem, ones_d, w_xv_b, norm=False, tm=256, tn=xw, name="mem_v_proj")
        qx = _proj(hp1, g_x[l], w_xq_b, norm=True, tm=512, tn=xw, name="xq_proj_prompt")
        hp2 = _cross(qx.reshape(batch, seq, xw), mk.reshape(batch, mlen, xw), mv.reshape(batch, mlen, xw),
                     hp1.reshape(batch, seq, d), w_xo_b, tq=256, name="cross_prompt").reshape(batch * seq, d)
        outs["kp"].append(z[:, aw:aw + kvw].reshape(batch, seq, N_KV_HEADS, HEAD_DIM))
        outs["vp"].append(z[:, aw + kvw:aw + 2 * kvw].reshape(batch, seq, N_KV_HEADS, HEAD_DIM))
        outs["ip"].append(zs[:, :IDX_DIM].reshape(batch, seq, IDX_DIM))
        outs["mkp"].append(mk.reshape(batch, mlen, X_HEADS, xw // X_HEADS))
        outs["mvp"].append(mv.reshape(batch, mlen, X_HEADS, xw // X_HEADS))

        z_s, zs_s = _proj(hs, g_in[l], w_main, norm=True, tm=db, tn=512, name="in_proj_sample", w_side=w_small)
        qi3 = z_s[:, aw + 2 * kvw:aw + 2 * kvw + qiw].reshape(db, IDX_HEADS, IDX_DIM)
        wi3 = zs_s[:, IDX_DIM:IDX_DIM + IDX_HEADS].reshape(db, IDX_HEADS, 1)
        ki_new = zs_s[:, :IDX_DIM].reshape(db, 1, IDX_DIM)
        k_new = z_s[:, aw:aw + kvw].reshape(db, 1, kvw)
        v_new = z_s[:, aw + kvw:aw + 2 * kvw].reshape(db, 1, kvw)
        scores3, snew3 = _dsa_sample_scores(page_table, qi3, wi3, ki_new, cache_kidx[l])
        pos3, new_sel = _dsa_sample_select(scores3.reshape(db, past), snew3.reshape(db, LANES),
                                           topk=min(TOPK_MAX, (past + dseq) // 4))
        att_s = _dsa_sample_attn(pos3[:, :, 0], page_table, z_s[:, :aw].reshape(db, N_HEADS, HEAD_DIM),
                                 k_new, v_new, new_sel, rb_t, cache_k, cache_v, layer=l)
        m_s, vn_s = _gate(att_s.reshape(db, aw), z_s, g_sgu[l], ws_first, bs_first,
                          first_rows_only=True, tm=db, name="gate_sample")
        hs1 = _matmul_residual(m_s, w_out_b, hs, tm=db, tn=1024, name="out_proj_sample")
        qx_s = _proj(hs1, g_x[l], w_xq_b, norm=True, tm=db, tn=xw, name="xq_proj_sample")
        hs2 = _cross(qx_s.reshape(db, 1, xw), cache_mem_k[l].reshape(db, mlen, xw),
                     cache_mem_v[l].reshape(db, mlen, xw), hs1.reshape(db, 1, d), w_xo_b,
                     tq=1, name="cross_sample").reshape(db, d)
        outs["ks"].append(k_new.reshape(db, dseq, N_KV_HEADS, HEAD_DIM))
        outs["vs"].append(v_new.reshape(db, dseq, N_KV_HEADS, HEAD_DIM))
        outs["is"].append(ki_new.reshape(db, dseq, IDX_DIM))
        outs["sv"].append(vn_s.reshape(db, dseq, gw))

        assert depth == 1
        packed = _pack_tables(peer_u[l], peer_v[l], eb=256)
        ids_p, gates_p = _peer_route(hp2, g_ffn[l], w_pq_t, sk_b, tb=512)
        y_p = _peer_gather(ids_p.T, gates_p.T, hp2, g_ffn[l], g_final, packed, g_tok=8)
        hs2_pad = jnp.pad(hs2, ((0, -db % LANES), (0, 0)))
        ids_s, gates_s = _peer_route(hs2_pad, g_ffn[l], w_pq_t, sk_b, tb=LANES)
        y_s = _peer_gather(ids_s.T[:db], gates_s.T[:db], hs2, g_ffn[l], g_final, packed, g_tok=8)

    y_prompt = y_p.reshape(batch, seq, d)
    y_sample = y_s.reshape(db, dseq, d)
    st = lambda n: jnp.stack(outs[n])
    return (y_prompt, y_sample, st("kp"), st("vp"), st("ip"), st("mkp"), st("mvp"),
            st("ks"), st("vs"), st("is"), st("sv"))
```

```python
import functools
import math

import jax
import jax.numpy as jnp
from jax import lax
from jax.experimental import pallas as pl
from jax.experimental.pallas import tpu as pltpu

F32 = jnp.float32
BF16 = jnp.bfloat16
I32 = jnp.int32

N_HEADS = 16
HEAD_DIM = 128
N_KV_HEADS = 4
GQA = N_HEADS // N_KV_HEADS
IDX_HEADS = 16
IDX_DIM = 64
TOPK_MAX = 256
NUM_BUCKETS = 32
MAX_DISTANCE = 128
CHUNK = 128
GMLP_GROUPS = 16
X_HEADS = 4
N_KEYS = 128
PEER_HEADS = 8
PEER_TOPK = 16
EPS = 1e-6

LANES = 128
SUBLANES = 8
VMEM_LIMIT_BYTES = 56 * 1024 * 1024

NEG = -1e30
INT_MIN = -(2 ** 31)


def _dot_nt(a, b):
    return lax.dot_general(a, b, (((1,), (1,)), ((), ())), preferred_element_type=F32)


def _dot(a, b):
    return jnp.dot(a, b, preferred_element_type=F32)


def _params(*sem):
    return pltpu.CompilerParams(dimension_semantics=sem, vmem_limit_bytes=VMEM_LIMIT_BYTES)


def _sortable(x):
    bits = pltpu.bitcast(x, I32)
    return bits ^ (jnp.right_shift(bits, 31) & 0x7FFFFFFF)


def _t5_bucket(dist):
    n = jnp.maximum(dist, 0)
    max_exact = NUM_BUCKETS // 2
    nf = jnp.maximum(n, 1).astype(F32)
    large = max_exact + (jnp.log(nf / max_exact) / math.log(MAX_DISTANCE / max_exact)
                         * (NUM_BUCKETS - max_exact)).astype(I32)
    large = jnp.minimum(large, NUM_BUCKETS - 1)
    return jnp.where(n < max_exact, n, large)


def _kth_largest_key(count_ge, shape, k):
    def body(it, t):
        cand = t | jnp.left_shift(jnp.int32(1), 31 - it)
        cnt = count_ge(cand ^ INT_MIN)
        return jnp.where(cnt >= k, cand, t)

    t = lax.fori_loop(0, 32, body, jnp.zeros(shape, I32))
    return t ^ INT_MIN


def _prefix_matrix():
    return (lax.broadcasted_iota(I32, (LANES, LANES), 0) <= lax.broadcasted_iota(I32, (LANES, LANES), 1)).astype(BF16)


def _proj_kernel(x_ref, g_ref, w_ref, *rest, norm):
    (w2_ref, o_ref, o2_ref, xn_ref) = rest if len(rest) == 4 else (None, rest[0], None, rest[1])

    @pl.when(pl.program_id(1) == 0)
    def _():
        x = x_ref[...]
        if norm:
            x = x * lax.rsqrt(jnp.mean(x * x, axis=-1, keepdims=True) + EPS) * g_ref[...]
        xn_ref[...] = x.astype(BF16)
        if w2_ref is not None:
            o2_ref[...] = _dot(xn_ref[...], w2_ref[...])

    o_ref[...] = _dot(xn_ref[...], w_ref[...])


def _proj(x, g, w, *, norm, tm, tn, name, w_side=None):
    m, k = x.shape
    n = w.shape[1]
    tm, tn = min(tm, m), min(tn, n)
    in_specs = [pl.BlockSpec((tm, k), lambda i, j: (i, 0)),
                pl.BlockSpec((1, k), lambda i, j: (0, 0)),
                pl.BlockSpec((k, tn), lambda i, j: (0, j))]
    out_specs = [pl.BlockSpec((tm, tn), lambda i, j: (i, j))]
    out_shape = [jax.ShapeDtypeStruct((m, n), F32)]
    args = [x, g.reshape(1, k), w]
    if w_side is not None:
        n2 = w_side.shape[1]
        in_specs.append(pl.BlockSpec((k, n2), lambda i, j: (0, 0)))
        out_specs.append(pl.BlockSpec((tm, n2), lambda i, j: (i, 0)))
        out_shape.append(jax.ShapeDtypeStruct((m, n2), F32))
        args.append(w_side)
    outs = pl.pallas_call(
        functools.partial(_proj_kernel, norm=norm),
        grid=(m // tm, n // tn),
        in_specs=in_specs,
        out_specs=out_specs,
        out_shape=out_shape,
        scratch_shapes=[pltpu.VMEM((tm, k), BF16)],
        compiler_params=_params("arbitrary", "arbitrary"),
        name=name,
    )(*args)
    return outs[0] if w_side is None else outs


def _bias_tiles_kernel(rb_ref, o_ref):
    tq = lax.broadcasted_iota(I32, (LANES, LANES), 0)
    c = lax.broadcasted_iota(I32, (LANES, LANES), 1)
    for part, off in enumerate((2 * LANES, LANES, 0)):
        dist = tq - c + off
        bucket = _t5_bucket(dist)
        for h in range(N_HEADS):
            tile = lax.fori_loop(
                0, NUM_BUCKETS, lambda b, acc: jnp.where(bucket == b, rb_ref[b, h], acc),
                jnp.zeros((LANES, LANES), F32))
            if off == 0:
                tile = jnp.where(dist < 0, NEG, tile)
            o_ref[part, h] = tile


def _bias_tiles(rel_bias):
    return pl.pallas_call(
        _bias_tiles_kernel,
        in_specs=[pl.BlockSpec(memory_space=pltpu.SMEM)],
        out_specs=pl.BlockSpec(memory_space=pltpu.VMEM),
        out_shape=jax.ShapeDtypeStruct((3, N_HEADS, LANES, LANES), F32),
        name="t5_bias_tiles",
    )(rel_bias)


def _dsa_prompt_kernel(q_ref, k_ref, v_ref, qi_ref, zsk_ref, zsq_ref, bt_ref, tri_ref, o_ref,
                       kbf, vbf, kilo, kihi, key3, qs, m_s, l_s, acc_s, *, topk):
    i = pl.program_id(1)
    tq = q_ref.shape[1]
    s = k_ref.shape[1]
    nkb = s // LANES
    cw = 2 * LANES

    @pl.when(i == 0)
    def _():
        kbf[...] = k_ref[0].astype(BF16)
        vbf[...] = v_ref[0].astype(BF16)
        zs = zsk_ref[0]
        lane = lax.broadcasted_iota(I32, zs.shape, 1)
        kilo[...] = jnp.where(lane < IDX_DIM, zs, 0.0).astype(BF16)
        kihi[...] = jnp.where(lane >= IDX_DIM, pltpu.roll(zs, IDX_DIM, axis=1), 0.0).astype(BF16)

    qi = qi_ref[0].astype(BF16)
    coef = zsq_ref[0][:, IDX_DIM:IDX_DIM + IDX_HEADS] * (IDX_HEADS ** -0.5 * IDX_DIM ** -0.5)
    t_pos = i * tq + lax.broadcasted_iota(I32, (tq, cw), 0)
    t_last = i * tq + tq - 1
    for c in range(s // cw):
        @pl.when(c * cw <= t_last)
        def _():
            klo = kilo[c * cw:(c + 1) * cw, :]
            khi = kihi[c * cw:(c + 1) * cw, :]
            acc = jnp.zeros((tq, cw), F32)
            for p in range(IDX_HEADS // 2):
                a = qi[:, p * LANES:(p + 1) * LANES]
                acc = acc + coef[:, 2 * p:2 * p + 1] * jnp.maximum(_dot_nt(a, klo), 0.0)
                acc = acc + coef[:, 2 * p + 1:2 * p + 2] * jnp.maximum(_dot_nt(a, khi), 0.0)
            s_pos = c * cw + lax.broadcasted_iota(I32, (tq, cw), 1)
            key = _sortable(jnp.where(s_pos <= t_pos, acc, -jnp.inf))
            for u in range(cw // LANES):
                key3[c * (cw // LANES) + u] = key[:, u * LANES:(u + 1) * LANES]

        @pl.when(c * cw > t_last)
        def _():
            for u in range(cw // LANES):
                key3[c * (cw // LANES) + u] = jnp.full((tq, LANES), INT_MIN, I32)

    def count(pred):
        return jnp.sum(jnp.sum(jnp.where(pred, 1.0, 0.0), axis=0), axis=1, keepdims=True)

    thr = _kth_largest_key(lambda cs: count(key3[...] >= cs[None]), (tq, 1), float(topk))

    n_ge = count(key3[...] >= thr[None])

    @pl.when(jnp.max(n_ge) > float(topk))
    def _():
        need = float(topk) - count(key3[...] > thr[None])
        before = jnp.zeros((tq, 1), F32)
        for j in range(nkb):
            kj = key3[j]
            eq = kj == thr
            rank = _dot(jnp.where(eq, 1.0, 0.0).astype(BF16), tri_ref[...]) + before
            key3[j] = jnp.where(jnp.logical_and(eq, rank > need), INT_MIN, kj)
            before = rank[:, LANES - 1:LANES]

    thr4 = jnp.concatenate([thr] * GQA, axis=0)

    qb = q_ref[0].astype(BF16)
    for kh in range(N_KV_HEADS):
        qs[kh] = jnp.concatenate(
            [qb[:, (GQA * kh + g) * HEAD_DIM:(GQA * kh + g + 1) * HEAD_DIM] for g in range(GQA)], axis=0)
    m_s[...] = jnp.full(m_s.shape, NEG, F32)
    l_s[...] = jnp.zeros(l_s.shape, F32)
    acc_s[...] = jnp.zeros(acc_s.shape, F32)
    scale = HEAD_DIM ** -0.5

    def fbody(jj, carry):
        ja = 2 * jj
        jb = ja + 1
        part_a = jnp.clip(ja - i + 2, 0, 2)
        part_b = jnp.clip(jb - i + 2, 0, 2)
        key_b = jnp.where(jb <= i, key3[jb], INT_MIN)
        key8 = jnp.concatenate([key3[ja], key_b], axis=1)
        mask = jnp.concatenate([key8] * GQA, axis=0) >= thr4
        r0 = pl.multiple_of(ja * LANES, 2 * LANES)
        for kh in range(N_KV_HEADS):
            kj = kbf[pl.ds(r0, 2 * LANES), kh * HEAD_DIM:(kh + 1) * HEAD_DIM]
            vj = vbf[pl.ds(r0, 2 * LANES), kh * HEAD_DIM:(kh + 1) * HEAD_DIM]
            bias = jnp.concatenate(
                [bt_ref[part_a, GQA * kh:GQA * (kh + 1)].reshape(GQA * tq, LANES),
                 bt_ref[part_b, GQA * kh:GQA * (kh + 1)].reshape(GQA * tq, LANES)], axis=1)
            lg = jnp.where(mask, _dot_nt(qs[kh], kj) * scale + bias, NEG)
            m_old = m_s[kh]
            m_new = jnp.maximum(m_old, jnp.max(lg, axis=1, keepdims=True))
            alpha = jnp.exp(m_old - m_new)
            p = jnp.where(mask, jnp.exp(lg - jnp.concatenate([m_new, m_new], axis=1)), 0.0)
            l_s[kh] = alpha * l_s[kh] + jnp.sum(p, axis=1, keepdims=True)
            acc_s[kh] = alpha * acc_s[kh] + _dot(p.astype(BF16), vj)
            m_s[kh] = m_new
        return carry

    assert nkb % 2 == 0
    lax.fori_loop(0, jnp.right_shift(i + 2, 1), fbody, 0)
    for kh in range(N_KV_HEADS):
        out = acc_s[kh] / l_s[kh]
        for g in range(GQA):
            h = GQA * kh + g
            o_ref[0, :, h * HEAD_DIM:(h + 1) * HEAD_DIM] = out[g * tq:(g + 1) * tq]


def _dsa_prompt(z, zs, btiles, *, batch, seq, topk):
    tq = LANES
    aw = N_HEADS * HEAD_DIM
    kvw = N_KV_HEADS * HEAD_DIM
    qiw = IDX_HEADS * IDX_DIM
    return pl.pallas_call(
        functools.partial(_dsa_prompt_kernel, topk=topk),
        grid=(batch, seq // tq),
        in_specs=[
            pl.BlockSpec((1, tq, aw), lambda b, i: (b, i, 0)),
            pl.BlockSpec((1, seq, kvw), lambda b, i: (b, 0, aw // kvw)),
            pl.BlockSpec((1, seq, kvw), lambda b, i: (b, 0, aw // kvw + 1)),
            pl.BlockSpec((1, tq, qiw), lambda b, i: (b, i, (aw + 2 * kvw) // qiw)),
            pl.BlockSpec((1, seq, LANES), lambda b, i: (b, 0, 0)),
            pl.BlockSpec((1, tq, LANES), lambda b, i: (b, i, 0)),
            pl.BlockSpec((3, N_HEADS, LANES, LANES), lambda b, i: (0, 0, 0, 0)),
            pl.BlockSpec((LANES, LANES), lambda b, i: (0, 0)),
        ],
        out_specs=pl.BlockSpec((1, tq, aw), lambda b, i: (b, i, 0)),
        out_shape=jax.ShapeDtypeStruct((batch, seq, aw), F32),
        scratch_shapes=[
            pltpu.VMEM((seq, kvw), BF16), pltpu.VMEM((seq, kvw), BF16),
            pltpu.VMEM((seq, LANES), BF16), pltpu.VMEM((seq, LANES), BF16),
            pltpu.VMEM((seq // LANES, tq, LANES), I32),
            pltpu.VMEM((N_KV_HEADS, GQA * tq, HEAD_DIM), BF16),
            pltpu.VMEM((N_KV_HEADS, GQA * tq, LANES), F32),
            pltpu.VMEM((N_KV_HEADS, GQA * tq, LANES), F32),
            pltpu.VMEM((N_KV_HEADS, GQA * tq, HEAD_DIM), F32),
        ],
        compiler_params=_params("arbitrary", "arbitrary"),
        name="dsa_prompt",
    )(z, z, z, z, zs, zs, btiles, _prefix_matrix())


SAMPLE_PAGES_PER_STEP = 32


def _dsa_sample_score_kernel(pt_ref, qi_ref, wi_ref, kinew_ref, *rest):
    kidx_refs, (o_ref, onew_ref) = rest[:-2], rest[-2:]
    qi = qi_ref[0].astype(BF16)
    coef = wi_ref[0] * (IDX_HEADS ** -0.5 * IDX_DIM ** -0.5)

    def score(ki_t):
        return jnp.sum(jnp.maximum(_dot(qi, ki_t.astype(BF16)), 0.0) * coef, axis=0, keepdims=True)

    o_ref[0] = jnp.concatenate([score(r[0]) for r in kidx_refs], axis=1)

    @pl.when(pl.program_id(1) == pl.num_programs(1) - 1)
    def _():
        onew_ref[0] = score(jnp.broadcast_to(kinew_ref[0], (kinew_ref.shape[1], LANES)))


def _dsa_sample_scores(page_table, qi3, wi3, ki_new_t, cache_kidx_t):
    db, n_pages = page_table.shape
    page = cache_kidx_t.shape[2]
    pg = math.gcd(SAMPLE_PAGES_PER_STEP, n_pages)
    row = lambda b, j, pt: (b, 0, 0)
    grid_spec = pltpu.PrefetchScalarGridSpec(
        num_scalar_prefetch=1,
        grid=(db, n_pages // pg),
        in_specs=[pl.BlockSpec((1, IDX_HEADS, IDX_DIM), row),
                  pl.BlockSpec((1, IDX_HEADS, 1), row),
                  pl.BlockSpec((1, IDX_DIM, 1), row)]
        + [pl.BlockSpec((1, IDX_DIM, page), functools.partial(lambda b, j, pt, u: (pt[b, j * pg + u], 0, 0), u=u))
           for u in range(pg)],
        out_specs=[pl.BlockSpec((1, 1, pg * page), lambda b, j, pt: (b, 0, j)),
                   pl.BlockSpec((1, 1, LANES), row)],
    )
    return pl.pallas_call(
        _dsa_sample_score_kernel,
        grid_spec=grid_spec,
        out_shape=[jax.ShapeDtypeStruct((db, 1, n_pages * page), F32),
                   jax.ShapeDtypeStruct((db, 1, LANES), F32)],
        compiler_params=_params("arbitrary", "arbitrary"),
        name="dsa_sample_scores",
    )(page_table, qi3, wi3, ki_new_t, *([cache_kidx_t] * pg))


def _dsa_sample_select_kernel(s_ref, snew_ref, tri_ref, pos_ref, new_ref, rank_s, nsel_s, *, topk):
    db, past = s_ref.shape
    k = float(topk)
    key = _sortable(s_ref[...])
    key_new = _sortable(snew_ref[:, 0:1])

    def count_ge(cs):
        return (jnp.sum(jnp.where(key >= cs, 1.0, 0.0), axis=1, keepdims=True)
                + jnp.where(key_new >= cs, 1.0, 0.0))

    thr = _kth_largest_key(count_ge, (db, 1), k)
    gt = key > thr
    eq = key == thr
    need = k - (jnp.sum(jnp.where(gt, 1.0, 0.0), axis=1, keepdims=True) + jnp.where(key_new > thr, 1.0, 0.0))

    tri = tri_ref[...]

    def cumsum(flags):
        out, off = [], jnp.zeros((db, 1), F32)
        for blk in range(past // LANES):
            pc = _dot(flags[:, blk * LANES:(blk + 1) * LANES].astype(BF16), tri) + off
            out.append(pc)
            off = pc[:, LANES - 1:LANES]
        return jnp.concatenate(out, axis=1), off

    eq_rank, eq_total = cumsum(jnp.where(eq, 1.0, 0.0))
    sel = jnp.where(jnp.logical_or(gt, jnp.logical_and(eq, eq_rank <= need)), 1.0, 0.0)
    new_sel = jnp.logical_or(key_new > thr, jnp.logical_and(key_new == thr, eq_total < need))
    new_ref[...] = jnp.broadcast_to(jnp.where(new_sel, 1.0, 0.0), new_ref.shape)
    rank, n_sel = cumsum(sel)
    rank = jnp.where(sel > 0.0, rank, 0.0)
    for b in range(db):
        rank_s[b] = rank[b:b + 1, :]
        nsel_s[b] = jnp.broadcast_to(n_sel[b:b + 1, :], (1, LANES))
    ch = min(8 * LANES, past)
    slot = lax.broadcasted_iota(I32, (topk, ch), 0).astype(F32) + 1.0
    lane = lax.broadcasted_iota(I32, (topk, ch), 1).astype(F32)
    slot_col = lax.broadcasted_iota(I32, (topk, 1), 0).astype(F32)

    def extract(b, carry):
        r = rank_s[b]
        pos = jnp.zeros((topk, 1), F32)
        for c in range(past // ch):
            hit = r[:, c * ch:(c + 1) * ch] == slot
            pos = pos + jnp.sum(jnp.where(hit, lane + float(c * ch), 0.0), axis=1, keepdims=True)
        pos = jnp.where(slot_col < nsel_s[b][:, 0:1], pos, -1.0)
        pos_ref[b] = jnp.broadcast_to(pos, (topk, LANES)).astype(I32)
        return carry

    lax.fori_loop(0, db, extract, 0)


def _dsa_sample_select(scores, snew, *, topk):
    db = scores.shape[0]
    tri = _prefix_matrix()
    return pl.pallas_call(
        functools.partial(_dsa_sample_select_kernel, topk=topk),
        out_shape=[jax.ShapeDtypeStruct((db, topk, LANES), I32), jax.ShapeDtypeStruct((db, LANES), F32)],
        scratch_shapes=[pltpu.VMEM((db, 1, scores.shape[1]), F32), pltpu.VMEM((db, 1, LANES), F32)],
        compiler_params=pltpu.CompilerParams(vmem_limit_bytes=VMEM_LIMIT_BYTES),
        name="dsa_sample_select",
    )(scores, snew, tri)


def _dsa_sample_attn_kernel(pos_s, pt_s, q_ref, knew_ref, vnew_ref, posrow_ref, newsel_ref, rbt_ref,
                            ck_hbm, cv_hbm, o_ref, kbuf, vbuf, sem_k, sem_v, *, layer, past, page):
    b = pl.program_id(0)
    topk = kbuf.shape[1]
    slot = lax.rem(b, 2)
    page_shift = page.bit_length() - 1
    unroll = 8

    def row_copies(bb, j, sl):
        p = jnp.maximum(pos_s[bb * topk + j], 0)
        pg = pt_s[bb, jnp.right_shift(p, page_shift)]
        off = p & (page - 1)
        return (pltpu.make_async_copy(ck_hbm.at[layer, pg, off], kbuf.at[sl, j], sem_k.at[sl]),
                pltpu.make_async_copy(cv_hbm.at[layer, pg, off], vbuf.at[sl, j], sem_v.at[sl]))

    def issue(bb, sl):
        def body(i, carry):
            for u in range(unroll):
                ck, cv = row_copies(bb, i * unroll + u, sl)
                ck.start()
                cv.start()
            return carry
        lax.fori_loop(0, topk // unroll, body, 0)

    def wait(sl):
        def body(i, carry):
            for u in range(unroll):
                j = i * unroll + u
                pltpu.make_async_copy(ck_hbm.at[layer, 0, 0], kbuf.at[sl, j], sem_k.at[sl]).wait()
                pltpu.make_async_copy(cv_hbm.at[layer, 0, 0], vbuf.at[sl, j], sem_v.at[sl]).wait()
            return carry
        lax.fori_loop(0, topk // unroll, body, 0)

    @pl.when(b == 0)
    def _():
        issue(0, 0)

    @pl.when(b + 1 < pl.num_programs(0))
    def _():
        issue(b + 1, 1 - slot)

    wait(slot)

    q = q_ref[0].astype(BF16)
    posr = posrow_ref[0]
    valid = posr >= 0
    bucket = _t5_bucket(past - posr)
    bias = jnp.zeros((N_HEADS, topk), F32)
    for bk in range(NUM_BUCKETS):
        bias = jnp.where(bucket == bk, rbt_ref[:, bk:bk + 1], bias)
    shift = GQA.bit_length() - 1
    kv_of_row = jnp.right_shift(lax.broadcasted_iota(I32, (N_HEADS, topk), 0), shift)
    kv_of_row_d = jnp.right_shift(lax.broadcasted_iota(I32, (N_HEADS, HEAD_DIM), 0), shift)
    scale = HEAD_DIM ** -0.5
    qf = q.astype(F32)
    knew = knew_ref[0].astype(BF16).astype(F32)
    vnew = vnew_ref[0].astype(BF16).astype(F32)
    lg = jnp.zeros((N_HEADS, topk), F32)
    lg_new = jnp.zeros((N_HEADS, 1), F32)
    v_new_rows = jnp.zeros((N_HEADS, HEAD_DIM), F32)
    for kh in range(N_KV_HEADS):
        hs = slice(kh * HEAD_DIM, (kh + 1) * HEAD_DIM)
        lg = jnp.where(kv_of_row == kh, _dot_nt(q, kbuf[slot, :, kh, :].astype(BF16)), lg)
        lg_new = jnp.where(kv_of_row[:, 0:1] == kh, jnp.sum(qf * knew[:, hs], axis=1, keepdims=True), lg_new)
        v_new_rows = jnp.where(kv_of_row_d == kh, jnp.broadcast_to(vnew[:, hs], (N_HEADS, HEAD_DIM)), v_new_rows)
    lg = jnp.where(valid, lg * scale + bias, NEG)
    new_sel = newsel_ref[0][:, 0:1] > 0.5
    lg_new = jnp.where(new_sel, lg_new * scale + rbt_ref[:, 0:1], NEG)
    m = jnp.maximum(jnp.max(lg, axis=1, keepdims=True), lg_new)
    p = jnp.where(valid, jnp.exp(lg - m), 0.0)
    p_new = jnp.where(new_sel, jnp.exp(lg_new - m), 0.0)
    denom = jnp.sum(p, axis=1, keepdims=True) + p_new
    pb = (p / denom).astype(BF16)
    out = (p_new / denom).astype(BF16).astype(F32) * v_new_rows
    for kh in range(N_KV_HEADS):
        out = out + jnp.where(kv_of_row_d == kh, _dot(pb, vbuf[slot, :, kh, :].astype(BF16)), 0.0)
    o_ref[0] = out


def _dsa_sample_attn(pos, page_table, q3, k_new, v_new, new_sel, rb_t, cache_k, cache_v, *, layer):
    db, topk = pos.shape
    n_pages = page_table.shape[1]
    page, n_kv, hd = cache_k.shape[2:]
    assert page & (page - 1) == 0
    kvw = n_kv * hd
    row = lambda b, ps, pt: (b, 0, 0)
    grid_spec = pltpu.PrefetchScalarGridSpec(
        num_scalar_prefetch=2,
        grid=(db,),
        in_specs=[
            pl.BlockSpec((1, N_HEADS, HEAD_DIM), row),
            pl.BlockSpec((1, 1, kvw), row),
            pl.BlockSpec((1, 1, kvw), row),
            pl.BlockSpec((1, 1, topk), row),
            pl.BlockSpec((1, 1, LANES), row),
            pl.BlockSpec((N_HEADS, NUM_BUCKETS), lambda b, ps, pt: (0, 0)),
            pl.BlockSpec(memory_space=pl.ANY),
            pl.BlockSpec(memory_space=pl.ANY),
        ],
        out_specs=pl.BlockSpec((1, N_HEADS, HEAD_DIM), row),
        scratch_shapes=[pltpu.VMEM((2, topk, n_kv, hd), F32), pltpu.VMEM((2, topk, n_kv, hd), F32),
                        pltpu.SemaphoreType.DMA((2,)), pltpu.SemaphoreType.DMA((2,))],
    )
    return pl.pallas_call(
        functools.partial(_dsa_sample_attn_kernel, layer=layer, past=n_pages * page, page=page),
        grid_spec=grid_spec,
        out_shape=jax.ShapeDtypeStruct((db, N_HEADS, HEAD_DIM), F32),
        compiler_params=_params("arbitrary"),
        name="dsa_sample_attn",
    )(pos.reshape(-1), page_table, q3, k_new, v_new, pos.reshape(db, 1, topk),
      new_sel.reshape(db, 1, LANES), rb_t, cache_k, cache_v)


def _gate_kernel(att_ref, zu_ref, zv_ref, ga_ref, gb_ref, gs_ref, ws_ref, bs_ref, m_ref, vn_ref, *, first_rows_only):
    aw = att_ref.shape[1]
    tm = zu_ref.shape[0]
    v = jax.nn.gelu(zv_ref[...])
    vn = v * lax.rsqrt(jnp.mean(v * v, axis=-1, keepdims=True) + EPS) * gs_ref[...]
    vn_ref[...] = vn
    m_ref[:, :aw] = (jax.nn.sigmoid(ga_ref[...]) * att_ref[...]).astype(BF16)
    if first_rows_only:
        sgu = jax.nn.gelu(zu_ref[...]) * (vn * ws_ref[...] + bs_ref[...])
        m_ref[:, aw:] = (jax.nn.sigmoid(gb_ref[...]) * sgu).astype(BF16)
    else:
        row = lax.broadcasted_iota(I32, (CHUNK, CHUNK), 0)
        col = lax.broadcasted_iota(I32, (CHUNK, CHUNK), 1)
        for g in range(GMLP_GROUPS):
            cs = slice(g * LANES, (g + 1) * LANES)
            wt = jnp.where(col <= row, ws_ref[g], 0.0).astype(BF16)
            bcol = bs_ref[:, g:g + 1]
            for c in range(tm // CHUNK):
                rs = slice(c * CHUNK, (c + 1) * CHUNK)
                mixed = _dot(wt, vn_ref[rs, cs].astype(BF16)) + bcol
                sgu = jax.nn.gelu(zu_ref[rs, cs]) * mixed
                m_ref[rs, aw + g * LANES:aw + (g + 1) * LANES] = (
                    jax.nn.sigmoid(gb_ref[rs, cs]) * sgu).astype(BF16)


def _gate(att, z, g_sgu, ws, bs, *, first_rows_only, tm, name):
    m, aw = att.shape
    gw = g_sgu.shape[0]
    tm = min(tm, m)
    zb = lambda c: pl.BlockSpec((tm, gw), lambda i: (i, c))
    full = lambda a: pl.BlockSpec(a.shape, lambda i: (0,) * a.ndim)
    first = z.shape[1] // gw - 4
    return pl.pallas_call(
        functools.partial(_gate_kernel, first_rows_only=first_rows_only),
        grid=(m // tm,),
        in_specs=[pl.BlockSpec((tm, aw), lambda i: (i, 0)),
                  zb(first), zb(first + 1), zb(first + 2), zb(first + 3),
                  pl.BlockSpec((1, gw), lambda i: (0, 0)), full(ws), full(bs)],
        out_specs=[pl.BlockSpec((tm, aw + gw), lambda i: (i, 0)), pl.BlockSpec((tm, gw), lambda i: (i, 0))],
        out_shape=[jax.ShapeDtypeStruct((m, aw + gw), BF16), jax.ShapeDtypeStruct((m, gw), F32)],
        compiler_params=_params("arbitrary"),
        name=name,
    )(att, z, z, z, z, g_sgu.reshape(1, gw), ws, bs)


def _matmul_residual_kernel(x_ref, w_ref, h_ref, o_ref):
    o_ref[...] = h_ref[...] + _dot(x_ref[...], w_ref[...])


def _matmul_residual(x, w, h, *, tm, tn, name):
    m, k = x.shape
    n = w.shape[1]
    tm, tn = min(tm, m), min(tn, n)
    return pl.pallas_call(
        _matmul_residual_kernel,
        grid=(m // tm, n // tn),
        in_specs=[pl.BlockSpec((tm, k), lambda i, j: (i, 0)),
                  pl.BlockSpec((k, tn), lambda i, j: (0, j)),
                  pl.BlockSpec((tm, tn), lambda i, j: (i, j))],
        out_specs=pl.BlockSpec((tm, tn), lambda i, j: (i, j)),
        out_shape=jax.ShapeDtypeStruct((m, n), F32),
        compiler_params=_params("arbitrary", "arbitrary"),
        name=name,
    )(x, w, h)


def _cross_kernel(q_ref, mk_ref, mv_ref, h_ref, wo_ref, o_ref):
    rows = q_ref.shape[1]
    q = q_ref[0]
    if rows < SUBLANES:
        q = jnp.broadcast_to(q, (SUBLANES, q.shape[1]))
    qb = q.astype(BF16)
    mk = mk_ref[0].astype(BF16)
    mv = mv_ref[0].astype(BF16)
    hd = mk.shape[1] // X_HEADS
    outs = []
    for hh in range(X_HEADS):
        sl = slice(hh * hd, (hh + 1) * hd)
        lg = _dot_nt(qb[:, sl], mk[:, sl]) * hd ** -0.5
        e = jnp.exp(lg - jnp.max(lg, axis=1, keepdims=True))
        p = e / jnp.sum(e, axis=1, keepdims=True)
        outs.append(_dot(p.astype(BF16), mv[:, sl]))
    y = _dot(jnp.concatenate(outs, axis=1).astype(BF16), wo_ref[...])
    o_ref[0] = h_ref[0] + y[:rows]


def _cross(q, mk, mv, h, w_xo, *, tq, name):
    b, t, xw = q.shape
    d = h.shape[2]
    mlen = mk.shape[1]
    return pl.pallas_call(
        _cross_kernel,
        grid=(b, t // tq),
        in_specs=[pl.BlockSpec((1, tq, xw), lambda bb, i: (bb, i, 0)),
                  pl.BlockSpec((1, mlen, xw), lambda bb, i: (bb, 0, 0)),
                  pl.BlockSpec((1, mlen, xw), lambda bb, i: (bb, 0, 0)),
                  pl.BlockSpec((1, tq, d), lambda bb, i: (bb, i, 0)),
                  pl.BlockSpec((xw, d), lambda bb, i: (0, 0))],
        out_specs=pl.BlockSpec((1, tq, d), lambda bb, i: (bb, i, 0)),
        out_shape=jax.ShapeDtypeStruct((b, t, d), F32),
        compiler_params=_params("arbitrary", "arbitrary"),
        name=name,
    )(q, mk, mv, h, w_xo)


def _top_rows(s, k, payload=None):
    r = s.shape[0]
    rid = lax.broadcasted_iota(I32, s.shape, 0).astype(F32)
    vals, picks = [], []
    for _ in range(k):
        m = jnp.max(s, axis=0, keepdims=True)
        am = jnp.min(jnp.where(s == m, rid, float(r)), axis=0, keepdims=True)
        hit = rid == am
        vals.append(m)
        picks.append(am if payload is None else jnp.sum(jnp.where(hit, payload, 0.0), axis=0, keepdims=True))
        s = jnp.where(hit, -jnp.inf, s)
    return jnp.concatenate(vals, axis=0), jnp.concatenate(picks, axis=0)


def _peer_route_kernel(h_ref, g_ref, w_ref, sk_ref, ids_ref, gate_ref, xn_ref):
    @pl.when(pl.program_id(1) == 0)
    def _():
        x = h_ref[...]
        xn_ref[...] = (x * lax.rsqrt(jnp.mean(x * x, axis=-1, keepdims=True) + EPS) * g_ref[...]).astype(BF16)

    half = sk_ref.shape[2]
    qt = _dot_nt(w_ref[...], xn_ref[...]).astype(BF16)
    v0, i0 = _top_rows(_dot(sk_ref[0], qt[:half]), PEER_TOPK)
    v1, i1 = _top_rows(_dot(sk_ref[1], qt[half:]), PEER_TOPK)
    widths = [PEER_TOPK // (a + 1) for a in range(PEER_TOPK)]
    pad = -sum(widths) % SUBLANES
    cand = jnp.concatenate([v0[a:a + 1] + v1[:n] for a, n in enumerate(widths)]
                           + [jnp.full((pad, v0.shape[1]), -jnp.inf, F32)], axis=0)
    eid = jnp.concatenate([i0[a:a + 1] * float(N_KEYS) + i1[:n] for a, n in enumerate(widths)]
                          + [jnp.zeros((pad, v0.shape[1]), F32)], axis=0)
    sc, e = _top_rows(cand, PEER_TOPK, payload=eid)
    ex = jnp.exp(sc - sc[0:1])
    gate_ref[...] = ex / jnp.sum(ex, axis=0, keepdims=True)
    ids_ref[...] = e.astype(I32)


def _peer_route(h, g_ffn, w_pq_t, sub_keys, *, tb):
    m, d = h.shape
    qd = w_pq_t.shape[0] // PEER_HEADS
    tb = min(tb, m)
    return pl.pallas_call(
        _peer_route_kernel,
        grid=(m // tb, PEER_HEADS),
        in_specs=[pl.BlockSpec((tb, d), lambda t, hh: (t, 0)),
                  pl.BlockSpec((1, d), lambda t, hh: (0, 0)),
                  pl.BlockSpec((qd, d), lambda t, hh: (hh, 0)),
                  pl.BlockSpec(sub_keys.shape, lambda t, hh: (0, 0, 0))],
        out_specs=[pl.BlockSpec((PEER_TOPK, tb), lambda t, hh: (hh, t)),
                   pl.BlockSpec((PEER_TOPK, tb), lambda t, hh: (hh, t))],
        out_shape=[jax.ShapeDtypeStruct((PEER_HEADS * PEER_TOPK, m), I32),
                   jax.ShapeDtypeStruct((PEER_HEADS * PEER_TOPK, m), F32)],
        scratch_shapes=[pltpu.VMEM((tb, d), BF16)],
        compiler_params=_params("arbitrary", "arbitrary"),
        name="peer_route",
    )(h, g_ffn.reshape(1, d), w_pq_t, sub_keys)


def _pack_tables_kernel(u_ref, v_ref, o_ref):
    nr = o_ref.shape[1] // 2

    def words(x):
        bits = pltpu.bitcast(x.astype(BF16).astype(F32), I32)
        half = bits.shape[1] // 2
        return (bits[:, half:] & jnp.int32(-65536)) | lax.shift_right_logical(bits[:, :half], 16)

    wu = words(u_ref[...])
    wv = words(v_ref[...])
    for r in range(nr):
        o_ref[:, r, :] = wu[:, r * LANES:(r + 1) * LANES]
        o_ref[:, nr + r, :] = wv[:, r * LANES:(r + 1) * LANES]


def _pack_tables(u_tab, v_tab, *, eb):
    n, d = u_tab.shape
    nr = d // (2 * LANES)
    return pl.pallas_call(
        _pack_tables_kernel,
        grid=(n // eb,),
        in_specs=[pl.BlockSpec((eb, d), lambda i: (i, 0)), pl.BlockSpec((eb, d), lambda i: (i, 0))],
        out_specs=pl.BlockSpec((eb, 2 * nr, LANES), lambda i: (i, 0, 0)),
        out_shape=jax.ShapeDtypeStruct((n, 2 * nr, LANES), I32),
        compiler_params=_params("arbitrary"),
        name="peer_pack_tables",
    )(u_tab, v_tab)


def _unpack_words(w):
    return pltpu.bitcast(jnp.left_shift(w, 16), F32), pltpu.bitcast(w & jnp.int32(-65536), F32)


def _peer_gather_kernel(ids_hbm, gate_ref, h_ref, gffn_ref, gfin_ref, tab_hbm, o_ref,
                        ids_s, buf, part_s, sem_ids, sem_rows, *, n_steps):
    s = pl.program_id(0)
    g_tok = h_ref.shape[0]
    nr = buf.shape[1] // 2
    n_e = buf.shape[2]
    n_buf = buf.shape[0]
    ahead = n_buf - 1
    assert g_tok % n_buf == 0 and ahead <= g_tok
    cur = lax.rem(s, 2)
    nxt = 1 - cur

    def ids_copy(step, slot):
        return pltpu.make_async_copy(ids_hbm.at[step], ids_s.at[slot], sem_ids.at[slot])

    def row_copy(e, k, slot):
        return pltpu.make_async_copy(tab_hbm.at[e], buf.at[slot, :, k, :], sem_rows.at[slot])

    def issue(ids_slot, t, slot):
        for k in range(n_e):
            row_copy(ids_s[ids_slot, t, k], k, slot).start(priority=k % 2)

    def wait(slot):
        for k in range(n_e):
            row_copy(0, k, slot).wait()

    @pl.when(s == 0)
    def _():
        ids_copy(0, 0).start()
        ids_copy(0, 0).wait()
        for t in range(ahead):
            issue(0, t, t % n_buf)

    @pl.when(s + 1 < n_steps)
    def _():
        ids_copy(s + 1, nxt).start()

    h = h_ref[...]
    xn = h * lax.rsqrt(jnp.mean(h * h, axis=-1, keepdims=True) + EPS) * gffn_ref[...]
    eye = lax.broadcasted_iota(I32, (n_e, n_e), 0) == lax.broadcasted_iota(I32, (n_e, n_e), 1)

    for t in range(g_tok):
        slot = t % n_buf
        ta = t + ahead
        if ta < g_tok:
            issue(cur, ta, ta % n_buf)
        else:
            @pl.when(s + 1 < n_steps)
            def _():
                if ta == g_tok:
                    ids_copy(s + 1, nxt).wait()
                issue(nxt, ta - g_tok, ta % n_buf)
        wait(slot)
        x = xn[t:t + 1, :]
        hacc = jnp.zeros((n_e, LANES), F32)
        for r in range(nr):
            lo, hi = _unpack_words(buf[slot, r])
            hacc = hacc + lo * x[:, r * LANES:(r + 1) * LANES] + hi * x[:, (nr + r) * LANES:(nr + r + 1) * LANES]
        act = jax.nn.gelu(jnp.sum(hacc, axis=-1, keepdims=True))
        gcol = jnp.sum(jnp.where(eye, jnp.broadcast_to(gate_ref[t:t + 1, :], (n_e, n_e)), 0.0),
                       axis=-1, keepdims=True)
        w = jnp.broadcast_to(gcol * act, (n_e, LANES))
        for r in range(nr):
            lo, hi = _unpack_words(buf[slot, nr + r])
            part_s[t, :, r * LANES:(r + 1) * LANES] = jnp.sum(
                (lo * w).reshape(n_e // SUBLANES, SUBLANES, LANES), axis=0)
            part_s[t, :, (nr + r) * LANES:(nr + r + 1) * LANES] = jnp.sum(
                (hi * w).reshape(n_e // SUBLANES, SUBLANES, LANES), axis=0)

    y = h + jnp.sum(part_s[...], axis=1)
    o_ref[...] = y * lax.rsqrt(jnp.mean(y * y, axis=-1, keepdims=True) + EPS) * gfin_ref[...]


PEER_GATHER_BUFFERS = 4


def _peer_gather(ids, gates, h, g_ffn, g_final, packed_tab, *, g_tok):
    m, d = h.shape
    n_e = ids.shape[1]
    n_steps = m // g_tok
    n_rows = packed_tab.shape[1]
    ids3 = ids.reshape(n_steps, g_tok, n_e)
    return pl.pallas_call(
        functools.partial(_peer_gather_kernel, n_steps=n_steps),
        grid=(n_steps,),
        in_specs=[pl.BlockSpec(memory_space=pl.ANY),
                  pl.BlockSpec((g_tok, n_e), lambda s: (s, 0)),
                  pl.BlockSpec((g_tok, d), lambda s: (s, 0)),
                  pl.BlockSpec((1, d), lambda s: (0, 0)),
                  pl.BlockSpec((1, d), lambda s: (0, 0)),
                  pl.BlockSpec(memory_space=pl.ANY)],
        out_specs=pl.BlockSpec((g_tok, d), lambda s: (s, 0)),
        out_shape=jax.ShapeDtypeStruct((m, d), F32),
        scratch_shapes=[pltpu.SMEM((2, g_tok, n_e), I32),
                        pltpu.VMEM((PEER_GATHER_BUFFERS, n_rows, n_e, LANES), I32),
                        pltpu.VMEM((g_tok, SUBLANES, d), F32),
                        pltpu.SemaphoreType.DMA((2,)), pltpu.SemaphoreType.DMA((PEER_GATHER_BUFFERS,))],
        compiler_params=_params("arbitrary"),
        name="peer_gather",
    )(ids3, gates, h, g_ffn.reshape(1, d), g_final.reshape(1, d), packed_tab)


def kernel(x_prompt, x_sample, cache_k, cache_v, cache_kidx, cache_mem_k, cache_mem_v, page_table,
           mem_prompt, rel_bias, g_in, w_in, g_sgu, w_s, b_s, w_out, g_x, w_xq, w_xk, w_xv, w_xo,
           g_ffn, w_pq, sub_keys, peer_u, peer_v, g_final):
    batch, seq, d = x_prompt.shape
    db, dseq, _ = x_sample.shape
    depth = w_in.shape[0]
    assert dseq == 1, "sample group is one new token per sequence"
    n_pages = page_table.shape[1]
    page = cache_k.shape[2]
    past = n_pages * page
    aw, kvw, qiw, gw = N_HEADS * HEAD_DIM, N_KV_HEADS * HEAD_DIM, IDX_HEADS * IDX_DIM, GMLP_GROUPS * LANES
    xw = w_xq.shape[2]
    mlen = mem_prompt.shape[1]
    small_lo = aw + 2 * kvw + qiw
    small_hi = small_lo + IDX_DIM + IDX_HEADS
    assert NUM_BUCKETS // 2 + int(math.log((LANES + 1) / (NUM_BUCKETS // 2)) / math.log(
        MAX_DISTANCE / (NUM_BUCKETS // 2)) * (NUM_BUCKETS // 2)) >= NUM_BUCKETS - 1

    hp = x_prompt.reshape(batch * seq, d)
    hs = x_sample.reshape(db, d)
    btiles = _bias_tiles(rel_bias)
    rb_t = rel_bias.T
    outs = {n: [] for n in ("kp", "vp", "ip", "mkp", "mvp", "ks", "vs", "is", "sv")}

    for l in range(depth):
        w_main = jnp.concatenate([w_in[l][:, :small_lo].astype(BF16), w_in[l][:, small_hi:].astype(BF16)], axis=1)
        w_small = jnp.pad(w_in[l][:, small_lo:small_hi], ((0, 0), (0, LANES - (small_hi - small_lo)))).astype(BF16)
        w_out_b = w_out[l].astype(BF16)
        w_xq_b, w_xk_b, w_xv_b, w_xo_b = (w.astype(BF16) for w in (w_xq[l], w_xk[l], w_xv[l], w_xo[l]))
        w_pq_t = w_pq[l].T.astype(BF16)
        sk_b = sub_keys[l].astype(BF16)
        ws_first = jnp.repeat(w_s[l][:, 0, 0], LANES).reshape(1, gw)
        bs_first = jnp.repeat(b_s[l][:, 0], LANES).reshape(1, gw)
        ones_d = jnp.ones((d,), F32)

        z, zs = _proj(hp, g_in[l], w_main, norm=True, tm=512, tn=1024, name="in_proj_prompt", w_side=w_small)
        att = _dsa_prompt(z.reshape(batch, seq, -1), zs.reshape(batch, seq, LANES), btiles,
                          batch=batch, seq=seq, topk=min(TOPK_MAX, seq // 4))
        m_p, _ = _gate(att.reshape(batch * seq, aw), z, g_sgu[l], w_s[l], b_s[l].T,
                       first_rows_only=False, tm=256, name="gate_prompt")
        hp1 = _matmul_residual(m_p, w_out_b, hp, tm=1024, tn=1024, name="out_proj_prompt")
        mem = mem_prompt.reshape(batch * mlen, d)
        mk = _proj(mem, ones_d, w_xk_b, norm=False, tm=256, tn=xw, name="mem_k_proj")
        mv = _proj(mem, ones_d, w_xv_b, norm=False, tm=256, tn=xw, name="mem_v_proj")
        qx = _proj(hp1, g_x[l], w_xq_b, norm=True, tm=512, tn=xw, name="xq_proj_prompt")
        hp2 = _cross(qx.reshape(batch, seq, xw), mk.reshape(batch, mlen, xw), mv.reshape(batch, mlen, xw),
                     hp1.reshape(batch, seq, d), w_xo_b, tq=256, name="cross_prompt").reshape(batch * seq, d)
        outs["kp"].append(z[:, aw:aw + kvw].reshape(batch, seq, N_KV_HEADS, HEAD_DIM))
        outs["vp"].append(z[:, aw + kvw:aw + 2 * kvw].reshape(batch, seq, N_KV_HEADS, HEAD_DIM))
        outs["ip"].append(zs[:, :IDX_DIM].reshape(batch, seq, IDX_DIM))
        outs["mkp"].append(mk.reshape(batch, mlen, X_HEADS, xw // X_HEADS))
        outs["mvp"].append(mv.reshape(batch, mlen, X_HEADS, xw // X_HEADS))

        z_s, zs_s = _proj(hs, g_in[l], w_main, norm=True, tm=db, tn=512, name="in_proj_sample", w_side=w_small)
        qi3 = z_s[:, aw + 2 * kvw:aw + 2 * kvw + qiw].reshape(db, IDX_HEADS, IDX_DIM)
        wi3 = zs_s[:, IDX_DIM:IDX_DIM + IDX_HEADS].reshape(db, IDX_HEADS, 1)
        ki_new = zs_s[:, :IDX_DIM].reshape(db, 1, IDX_DIM)
        k_new = z_s[:, aw:aw + kvw].reshape(db, 1, kvw)
        v_new = z_s[:, aw + kvw:aw + 2 * kvw].reshape(db, 1, kvw)
        scores3, snew3 = _dsa_sample_scores(page_table, qi3, wi3, ki_new.reshape(db, IDX_DIM, 1),
                                            jnp.swapaxes(cache_kidx[l], 1, 2))
        pos3, new_sel = _dsa_sample_select(scores3.reshape(db, past), snew3.reshape(db, LANES),
                                           topk=min(TOPK_MAX, (past + dseq) // 4))
        att_s = _dsa_sample_attn(pos3[:, :, 0], page_table, z_s[:, :aw].reshape(db, N_HEADS, HEAD_DIM),
                                 k_new, v_new, new_sel, rb_t, cache_k, cache_v, layer=l)
        m_s, vn_s = _gate(att_s.reshape(db, aw), z_s, g_sgu[l], ws_first, bs_first,
                          first_rows_only=True, tm=db, name="gate_sample")
        hs1 = _matmul_residual(m_s, w_out_b, hs, tm=db, tn=1024, name="out_proj_sample")
        qx_s = _proj(hs1, g_x[l], w_xq_b, norm=True, tm=db, tn=xw, name="xq_proj_sample")
        hs2 = _cross(qx_s.reshape(db, 1, xw), cache_mem_k[l].reshape(db, mlen, xw),
                     cache_mem_v[l].reshape(db, mlen, xw), hs1.reshape(db, 1, d), w_xo_b,
                     tq=1, name="cross_sample").reshape(db, d)
        outs["ks"].append(k_new.reshape(db, dseq, N_KV_HEADS, HEAD_DIM))
        outs["vs"].append(v_new.reshape(db, dseq, N_KV_HEADS, HEAD_DIM))
        outs["is"].append(ki_new.reshape(db, dseq, IDX_DIM))
        outs["sv"].append(vn_s.reshape(db, dseq, gw))

        assert depth == 1
        packed = _pack_tables(peer_u[l], peer_v[l], eb=256)
        ids_p, gates_p = _peer_route(hp2, g_ffn[l], w_pq_t, sk_b, tb=512)
        y_p = _peer_gather(ids_p.T, gates_p.T, hp2, g_ffn[l], g_final, packed, g_tok=8)
        hs2_pad = jnp.pad(hs2, ((0, -db % LANES), (0, 0)))
        ids_s, gates_s = _peer_route(hs2_pad, g_ffn[l], w_pq_t, sk_b, tb=LANES)
        y_s = _peer_gather(ids_s.T[:db], gates_s.T[:db], hs2, g_ffn[l], g_final, packed, g_tok=8)

    y_prompt = y_p.reshape(batch, seq, d)
    y_sample = y_s.reshape(db, dseq, d)
    st = lambda n: jnp.stack(outs[n])
    return (y_prompt, y_sample, st("kp"), st("vp"), st("ip"), st("mkp"), st("mvp"),
            st("ks"), st("vs"), st("is"), st("sv"))
```

```python
import functools
import math

import jax
import jax.numpy as jnp
from jax import lax
from jax.experimental import pallas as pl
from jax.experimental.pallas import tpu as pltpu

F32 = jnp.float32
BF16 = jnp.bfloat16
I32 = jnp.int32

N_HEADS = 16
HEAD_DIM = 128
N_KV_HEADS = 4
GQA = N_HEADS // N_KV_HEADS
IDX_HEADS = 16
IDX_DIM = 64
TOPK_MAX = 256
NUM_BUCKETS = 32
MAX_DISTANCE = 128
CHUNK = 128
GMLP_GROUPS = 16
X_HEADS = 4
N_KEYS = 128
PEER_HEADS = 8
PEER_TOPK = 16
EPS = 1e-6

LANES = 128
SUBLANES = 8
VMEM_LIMIT_BYTES = 56 * 1024 * 1024

NEG = -1e30
INT_MIN = -(2 ** 31)


def _dot_nt(a, b):
    return lax.dot_general(a, b, (((1,), (1,)), ((), ())), preferred_element_type=F32)


def _dot(a, b):
    return jnp.dot(a, b, preferred_element_type=F32)


def _params(*sem):
    return pltpu.CompilerParams(dimension_semantics=sem, vmem_limit_bytes=VMEM_LIMIT_BYTES)


def _sortable(x):
    bits = pltpu.bitcast(x, I32)
    return bits ^ (jnp.right_shift(bits, 31) & 0x7FFFFFFF)


def _t5_bucket(dist):
    n = jnp.maximum(dist, 0)
    max_exact = NUM_BUCKETS // 2
    nf = jnp.maximum(n, 1).astype(F32)
    large = max_exact + (jnp.log(nf / max_exact) / math.log(MAX_DISTANCE / max_exact)
                         * (NUM_BUCKETS - max_exact)).astype(I32)
    large = jnp.minimum(large, NUM_BUCKETS - 1)
    return jnp.where(n < max_exact, n, large)


def _kth_largest_key(count_ge, shape, k):
    def body(it, t):
        cand = t | jnp.left_shift(jnp.int32(1), 31 - it)
        cnt = count_ge(cand ^ INT_MIN)
        return jnp.where(cnt >= k, cand, t)

    t = lax.fori_loop(0, 32, body, jnp.zeros(shape, I32))
    return t ^ INT_MIN


def _prefix_matrix():
    return (lax.broadcasted_iota(I32, (LANES, LANES), 0) <= lax.broadcasted_iota(I32, (LANES, LANES), 1)).astype(BF16)


def _proj_kernel(x_ref, g_ref, w_ref, *rest, norm):
    (w2_ref, o_ref, o2_ref, xn_ref) = rest if len(rest) == 4 else (None, rest[0], None, rest[1])

    @pl.when(pl.program_id(1) == 0)
    def _():
        x = x_ref[...]
        if norm:
            x = x * lax.rsqrt(jnp.mean(x * x, axis=-1, keepdims=True) + EPS) * g_ref[...]
        xn_ref[...] = x.astype(BF16)
        if w2_ref is not None:
            o2_ref[...] = _dot(xn_ref[...], w2_ref[...])

    o_ref[...] = _dot(xn_ref[...], w_ref[...])


def _proj(x, g, w, *, norm, tm, tn, name, w_side=None):
    m, k = x.shape
    n = w.shape[1]
    tm, tn = min(tm, m), min(tn, n)
    in_specs = [pl.BlockSpec((tm, k), lambda i, j: (i, 0)),
                pl.BlockSpec((1, k), lambda i, j: (0, 0)),
                pl.BlockSpec((k, tn), lambda i, j: (0, j))]
    out_specs = [pl.BlockSpec((tm, tn), lambda i, j: (i, j))]
    out_shape = [jax.ShapeDtypeStruct((m, n), F32)]
    args = [x, g.reshape(1, k), w]
    if w_side is not None:
        n2 = w_side.shape[1]
        in_specs.append(pl.BlockSpec((k, n2), lambda i, j: (0, 0)))
        out_specs.append(pl.BlockSpec((tm, n2), lambda i, j: (i, 0)))
        out_shape.append(jax.ShapeDtypeStruct((m, n2), F32))
        args.append(w_side)
    outs = pl.pallas_call(
        functools.partial(_proj_kernel, norm=norm),
        grid=(m // tm, n // tn),
        in_specs=in_specs,
        out_specs=out_specs,
        out_shape=out_shape,
        scratch_shapes=[pltpu.VMEM((tm, k), BF16)],
        compiler_params=_params("arbitrary", "arbitrary"),
        name=name,
    )(*args)
    return outs[0] if w_side is None else outs


def _proj_wt_kernel(x_ref, g_ref, wt_ref, w2t_ref, o_ref, o2_ref, xn_ref):
    @pl.when(pl.program_id(1) == 0)
    def _():
        x = x_ref[...]
        xn_ref[...] = (x * lax.rsqrt(jnp.mean(x * x, axis=-1, keepdims=True) + EPS) * g_ref[...]).astype(BF16)
        o2_ref[...] = _dot_nt(xn_ref[...], w2t_ref[...])

    o_ref[...] = _dot_nt(xn_ref[...], wt_ref[...])


def _proj_wt(x, g, w_t, w_side_t, *, skip_lo, skip_hi, tm, tn, name):
    m, k = x.shape
    gap = skip_hi - skip_lo
    n = w_t.shape[0] - gap
    n2 = w_side_t.shape[0]
    tm, tn = min(tm, m), min(tn, n)
    assert skip_lo % tn == 0 and n % tn == 0

    def w_rows(i, j):
        start = j * tn
        return (pl.multiple_of(jnp.where(start < skip_lo, start, start + gap), math.gcd(tn, gap)), 0)

    return pl.pallas_call(
        _proj_wt_kernel,
        grid=(m // tm, n // tn),
        in_specs=[pl.BlockSpec((tm, k), lambda i, j: (i, 0)),
                  pl.BlockSpec((1, k), lambda i, j: (0, 0)),
                  pl.BlockSpec((pl.Element(tn), pl.Element(k)), w_rows),
                  pl.BlockSpec((n2, k), lambda i, j: (0, 0))],
        out_specs=[pl.BlockSpec((tm, tn), lambda i, j: (i, j)), pl.BlockSpec((tm, n2), lambda i, j: (i, 0))],
        out_shape=[jax.ShapeDtypeStruct((m, n), F32), jax.ShapeDtypeStruct((m, n2), F32)],
        scratch_shapes=[pltpu.VMEM((tm, k), BF16)],
        compiler_params=_params("arbitrary", "arbitrary"),
        name=name,
    )(x, g.reshape(1, k), w_t, w_side_t)


def _bias_tiles_kernel(rb_ref, o_ref):
    tq = lax.broadcasted_iota(I32, (LANES, LANES), 0)
    c = lax.broadcasted_iota(I32, (LANES, LANES), 1)
    for part, off in enumerate((2 * LANES, LANES, 0)):
        dist = tq - c + off
        bucket = _t5_bucket(dist)
        for h in range(N_HEADS):
            tile = lax.fori_loop(
                0, NUM_BUCKETS, lambda b, acc: jnp.where(bucket == b, rb_ref[b, h], acc),
                jnp.zeros((LANES, LANES), F32))
            if off == 0:
                tile = jnp.where(dist < 0, NEG, tile)
            o_ref[part, h] = tile


def _bias_tiles(rel_bias):
    return pl.pallas_call(
        _bias_tiles_kernel,
        in_specs=[pl.BlockSpec(memory_space=pltpu.SMEM)],
        out_specs=pl.BlockSpec(memory_space=pltpu.VMEM),
        out_shape=jax.ShapeDtypeStruct((3, N_HEADS, LANES, LANES), F32),
        name="t5_bias_tiles",
    )(rel_bias)


def _dsa_prompt_kernel(q_ref, k_ref, v_ref, qi_ref, zsk_ref, zsq_ref, bt_ref, tri_ref, o_ref,
                       kbf, vbf, kilo, kihi, key3, qs, m_s, l_s, acc_s, *, topk):
    i = pl.program_id(1)
    tq = q_ref.shape[1]
    s = k_ref.shape[1]
    nkb = s // LANES
    cw = 2 * LANES

    @pl.when(i == 0)
    def _():
        kbf[...] = k_ref[0].astype(BF16)
        vbf[...] = v_ref[0].astype(BF16)
        zs = zsk_ref[0]
        lane = lax.broadcasted_iota(I32, zs.shape, 1)
        kilo[...] = jnp.where(lane < IDX_DIM, zs, 0.0).astype(BF16)
        kihi[...] = jnp.where(lane >= IDX_DIM, pltpu.roll(zs, IDX_DIM, axis=1), 0.0).astype(BF16)

    qi = qi_ref[0].astype(BF16)
    coef = zsq_ref[0][:, IDX_DIM:IDX_DIM + IDX_HEADS] * (IDX_HEADS ** -0.5 * IDX_DIM ** -0.5)
    t_pos = i * tq + lax.broadcasted_iota(I32, (tq, cw), 0)
    t_last = i * tq + tq - 1
    for c in range(s // cw):
        @pl.when(c * cw <= t_last)
        def _():
            klo = kilo[c * cw:(c + 1) * cw, :]
            khi = kihi[c * cw:(c + 1) * cw, :]
            acc = jnp.zeros((tq, cw), F32)
            for p in range(IDX_HEADS // 2):
                a = qi[:, p * LANES:(p + 1) * LANES]
                acc = acc + coef[:, 2 * p:2 * p + 1] * jnp.maximum(_dot_nt(a, klo), 0.0)
                acc = acc + coef[:, 2 * p + 1:2 * p + 2] * jnp.maximum(_dot_nt(a, khi), 0.0)
            s_pos = c * cw + lax.broadcasted_iota(I32, (tq, cw), 1)
            key = _sortable(jnp.where(s_pos <= t_pos, acc, -jnp.inf))
            for u in range(cw // LANES):
                key3[c * (cw // LANES) + u] = key[:, u * LANES:(u + 1) * LANES]

        @pl.when(c * cw > t_last)
        def _():
            for u in range(cw // LANES):
                key3[c * (cw // LANES) + u] = jnp.full((tq, LANES), INT_MIN, I32)

    def count(pred):
        return jnp.sum(jnp.sum(jnp.where(pred, 1.0, 0.0), axis=0), axis=1, keepdims=True)

    thr = _kth_largest_key(lambda cs: count(key3[...] >= cs[None]), (tq, 1), float(topk))

    n_ge = count(key3[...] >= thr[None])

    @pl.when(jnp.max(n_ge) > float(topk))
    def _():
        need = float(topk) - count(key3[...] > thr[None])
        before = jnp.zeros((tq, 1), F32)
        for j in range(nkb):
            kj = key3[j]
            eq = kj == thr
            rank = _dot(jnp.where(eq, 1.0, 0.0).astype(BF16), tri_ref[...]) + before
            key3[j] = jnp.where(jnp.logical_and(eq, rank > need), INT_MIN, kj)
            before = rank[:, LANES - 1:LANES]

    thr4 = jnp.concatenate([thr] * GQA, axis=0)

    qb = q_ref[0].astype(BF16)
    for kh in range(N_KV_HEADS):
        qs[kh] = jnp.concatenate(
            [qb[:, (GQA * kh + g) * HEAD_DIM:(GQA * kh + g + 1) * HEAD_DIM] for g in range(GQA)], axis=0)
    m_s[...] = jnp.full(m_s.shape, NEG, F32)
    l_s[...] = jnp.zeros(l_s.shape, F32)
    acc_s[...] = jnp.zeros(acc_s.shape, F32)
    scale = HEAD_DIM ** -0.5

    def fbody(jj, carry):
        ja = 2 * jj
        jb = ja + 1
        part_a = jnp.clip(ja - i + 2, 0, 2)
        part_b = jnp.clip(jb - i + 2, 0, 2)
        key_b = jnp.where(jb <= i, key3[jb], INT_MIN)
        key8 = jnp.concatenate([key3[ja], key_b], axis=1)
        mask = jnp.concatenate([key8] * GQA, axis=0) >= thr4
        r0 = pl.multiple_of(ja * LANES, 2 * LANES)
        for kh in range(N_KV_HEADS):
            kj = kbf[pl.ds(r0, 2 * LANES), kh * HEAD_DIM:(kh + 1) * HEAD_DIM]
            vj = vbf[pl.ds(r0, 2 * LANES), kh * HEAD_DIM:(kh + 1) * HEAD_DIM]
            bias = jnp.concatenate(
                [bt_ref[part_a, GQA * kh:GQA * (kh + 1)].reshape(GQA * tq, LANES),
                 bt_ref[part_b, GQA * kh:GQA * (kh + 1)].reshape(GQA * tq, LANES)], axis=1)
            lg = jnp.where(mask, _dot_nt(qs[kh], kj) * scale + bias, NEG)
            m_old = m_s[kh]
            m_new = jnp.maximum(m_old, jnp.max(lg, axis=1, keepdims=True))
            alpha = jnp.exp(m_old - m_new)
            p = jnp.where(mask, jnp.exp(lg - jnp.concatenate([m_new, m_new], axis=1)), 0.0)
            l_s[kh] = alpha * l_s[kh] + jnp.sum(p, axis=1, keepdims=True)
            acc_s[kh] = alpha * acc_s[kh] + _dot(p.astype(BF16), vj)
            m_s[kh] = m_new
        return carry

    assert nkb % 2 == 0
    lax.fori_loop(0, jnp.right_shift(i + 2, 1), fbody, 0)
    for kh in range(N_KV_HEADS):
        out = acc_s[kh] / l_s[kh]
        for g in range(GQA):
            h = GQA * kh + g
            o_ref[0, :, h * HEAD_DIM:(h + 1) * HEAD_DIM] = out[g * tq:(g + 1) * tq]


def _dsa_prompt(z, zs, btiles, *, batch, seq, topk):
    tq = LANES
    aw = N_HEADS * HEAD_DIM
    kvw = N_KV_HEADS * HEAD_DIM
    qiw = IDX_HEADS * IDX_DIM
    return pl.pallas_call(
        functools.partial(_dsa_prompt_kernel, topk=topk),
        grid=(batch, seq // tq),
        in_specs=[
            pl.BlockSpec((1, tq, aw), lambda b, i: (b, i, 0)),
            pl.BlockSpec((1, seq, kvw), lambda b, i: (b, 0, aw // kvw)),
            pl.BlockSpec((1, seq, kvw), lambda b, i: (b, 0, aw // kvw + 1)),
            pl.BlockSpec((1, tq, qiw), lambda b, i: (b, i, (aw + 2 * kvw) // qiw)),
            pl.BlockSpec((1, seq, LANES), lambda b, i: (b, 0, 0)),
            pl.BlockSpec((1, tq, LANES), lambda b, i: (b, i, 0)),
            pl.BlockSpec((3, N_HEADS, LANES, LANES), lambda b, i: (0, 0, 0, 0)),
            pl.BlockSpec((LANES, LANES), lambda b, i: (0, 0)),
        ],
        out_specs=pl.BlockSpec((1, tq, aw), lambda b, i: (b, i, 0)),
        out_shape=jax.ShapeDtypeStruct((batch, seq, aw), F32),
        scratch_shapes=[
            pltpu.VMEM((seq, kvw), BF16), pltpu.VMEM((seq, kvw), BF16),
            pltpu.VMEM((seq, LANES), BF16), pltpu.VMEM((seq, LANES), BF16),
            pltpu.VMEM((seq // LANES, tq, LANES), I32),
            pltpu.VMEM((N_KV_HEADS, GQA * tq, HEAD_DIM), BF16),
            pltpu.VMEM((N_KV_HEADS, GQA * tq, LANES), F32),
            pltpu.VMEM((N_KV_HEADS, GQA * tq, LANES), F32),
            pltpu.VMEM((N_KV_HEADS, GQA * tq, HEAD_DIM), F32),
        ],
        compiler_params=_params("arbitrary", "arbitrary"),
        name="dsa_prompt",
    )(z, z, z, z, zs, zs, btiles, _prefix_matrix())


SAMPLE_PAGES_PER_STEP = 32


def _dsa_sample_score_kernel(pt_ref, qi_ref, wi_ref, kinew_ref, *rest):
    kidx_refs, (o_ref, onew_ref) = rest[:-2], rest[-2:]
    qi = qi_ref[0].astype(BF16)
    coef = wi_ref[0] * (IDX_HEADS ** -0.5 * IDX_DIM ** -0.5)

    def score(ki_t):
        return jnp.sum(jnp.maximum(_dot(qi, ki_t.astype(BF16)), 0.0) * coef, axis=0, keepdims=True)

    o_ref[0] = jnp.concatenate([score(r[0]) for r in kidx_refs], axis=1)

    @pl.when(pl.program_id(1) == pl.num_programs(1) - 1)
    def _():
        onew_ref[0] = score(jnp.broadcast_to(kinew_ref[0], (kinew_ref.shape[1], LANES)))


def _dsa_sample_scores(page_table, qi3, wi3, ki_new_t, cache_kidx_t):
    db, n_pages = page_table.shape
    page = cache_kidx_t.shape[2]
    pg = math.gcd(SAMPLE_PAGES_PER_STEP, n_pages)
    row = lambda b, j, pt: (b, 0, 0)
    grid_spec = pltpu.PrefetchScalarGridSpec(
        num_scalar_prefetch=1,
        grid=(db, n_pages // pg),
        in_specs=[pl.BlockSpec((1, IDX_HEADS, IDX_DIM), row),
                  pl.BlockSpec((1, IDX_HEADS, 1), row),
                  pl.BlockSpec((1, IDX_DIM, 1), row)]
        + [pl.BlockSpec((1, IDX_DIM, page), functools.partial(lambda b, j, pt, u: (pt[b, j * pg + u], 0, 0), u=u))
           for u in range(pg)],
        out_specs=[pl.BlockSpec((1, 1, pg * page), lambda b, j, pt: (b, 0, j)),
                   pl.BlockSpec((1, 1, LANES), row)],
    )
    return pl.pallas_call(
        _dsa_sample_score_kernel,
        grid_spec=grid_spec,
        out_shape=[jax.ShapeDtypeStruct((db, 1, n_pages * page), F32),
                   jax.ShapeDtypeStruct((db, 1, LANES), F32)],
        compiler_params=_params("arbitrary", "arbitrary"),
        name="dsa_sample_scores",
    )(page_table, qi3, wi3, ki_new_t, *([cache_kidx_t] * pg))


def _dsa_sample_select_kernel(s_ref, snew_ref, tri_ref, pos_ref, new_ref, rank_s, nsel_s, *, topk):
    db, past = s_ref.shape
    k = float(topk)
    key = _sortable(s_ref[...])
    key_new = _sortable(snew_ref[:, 0:1])

    def count_ge(cs):
        return (jnp.sum(jnp.where(key >= cs, 1.0, 0.0), axis=1, keepdims=True)
                + jnp.where(key_new >= cs, 1.0, 0.0))

    thr = _kth_largest_key(count_ge, (db, 1), k)
    gt = key > thr
    eq = key == thr
    need = k - (jnp.sum(jnp.where(gt, 1.0, 0.0), axis=1, keepdims=True) + jnp.where(key_new > thr, 1.0, 0.0))

    tri = tri_ref[...]

    def cumsum(flags):
        out, off = [], jnp.zeros((db, 1), F32)
        for blk in range(past // LANES):
            pc = _dot(flags[:, blk * LANES:(blk + 1) * LANES].astype(BF16), tri) + off
            out.append(pc)
            off = pc[:, LANES - 1:LANES]
        return jnp.concatenate(out, axis=1), off

    eq_rank, eq_total = cumsum(jnp.where(eq, 1.0, 0.0))
    sel = jnp.where(jnp.logical_or(gt, jnp.logical_and(eq, eq_rank <= need)), 1.0, 0.0)
    new_sel = jnp.logical_or(key_new > thr, jnp.logical_and(key_new == thr, eq_total < need))
    new_ref[...] = jnp.broadcast_to(jnp.where(new_sel, 1.0, 0.0), new_ref.shape)
    rank, n_sel = cumsum(sel)
    rank = jnp.where(sel > 0.0, rank, 0.0)
    for b in range(db):
        rank_s[b] = rank[b:b + 1, :]
        nsel_s[b] = jnp.broadcast_to(n_sel[b:b + 1, :], (1, LANES))
    ch = min(8 * LANES, past)
    slot = lax.broadcasted_iota(I32, (topk, ch), 0).astype(F32) + 1.0
    lane = lax.broadcasted_iota(I32, (topk, ch), 1).astype(F32)
    slot_col = lax.broadcasted_iota(I32, (topk, 1), 0).astype(F32)

    def extract(b, carry):
        r = rank_s[b]
        pos = jnp.zeros((topk, 1), F32)
        for c in range(past // ch):
            hit = r[:, c * ch:(c + 1) * ch] == slot
            pos = pos + jnp.sum(jnp.where(hit, lane + float(c * ch), 0.0), axis=1, keepdims=True)
        pos = jnp.where(slot_col < nsel_s[b][:, 0:1], pos, -1.0)
        pos_ref[b] = jnp.broadcast_to(pos, (topk, LANES)).astype(I32)
        return carry

    lax.fori_loop(0, db, extract, 0)


def _dsa_sample_select(scores, snew, *, topk):
    db = scores.shape[0]
    tri = _prefix_matrix()
    return pl.pallas_call(
        functools.partial(_dsa_sample_select_kernel, topk=topk),
        out_shape=[jax.ShapeDtypeStruct((db, topk, LANES), I32), jax.ShapeDtypeStruct((db, LANES), F32)],
        scratch_shapes=[pltpu.VMEM((db, 1, scores.shape[1]), F32), pltpu.VMEM((db, 1, LANES), F32)],
        compiler_params=pltpu.CompilerParams(vmem_limit_bytes=VMEM_LIMIT_BYTES),
        name="dsa_sample_select",
    )(scores, snew, tri)


def _dsa_sample_attn_kernel(pos_s, pt_s, q_ref, knew_ref, vnew_ref, posrow_ref, newsel_ref, rbt_ref,
                            ck_hbm, cv_hbm, o_ref, kbuf, vbuf, sem_k, sem_v, *, layer, past, page):
    b = pl.program_id(0)
    topk = kbuf.shape[1]
    slot = lax.rem(b, 2)
    page_shift = page.bit_length() - 1
    unroll = 8

    def row_copies(bb, j, sl):
        p = jnp.maximum(pos_s[bb * topk + j], 0)
        pg = pt_s[bb, jnp.right_shift(p, page_shift)]
        off = p & (page - 1)
        return (pltpu.make_async_copy(ck_hbm.at[layer, pg, off], kbuf.at[sl, j], sem_k.at[sl]),
                pltpu.make_async_copy(cv_hbm.at[layer, pg, off], vbuf.at[sl, j], sem_v.at[sl]))

    def issue(bb, sl):
        def body(i, carry):
            for u in range(unroll):
                ck, cv = row_copies(bb, i * unroll + u, sl)
                ck.start()
                cv.start()
            return carry
        lax.fori_loop(0, topk // unroll, body, 0)

    def wait(sl):
        def body(i, carry):
            for u in range(unroll):
                j = i * unroll + u
                pltpu.make_async_copy(ck_hbm.at[layer, 0, 0], kbuf.at[sl, j], sem_k.at[sl]).wait()
                pltpu.make_async_copy(cv_hbm.at[layer, 0, 0], vbuf.at[sl, j], sem_v.at[sl]).wait()
            return carry
        lax.fori_loop(0, topk // unroll, body, 0)

    @pl.when(b == 0)
    def _():
        issue(0, 0)

    @pl.when(b + 1 < pl.num_programs(0))
    def _():
        issue(b + 1, 1 - slot)

    wait(slot)

    q = q_ref[0].astype(BF16)
    posr = posrow_ref[0]
    valid = posr >= 0
    bucket = _t5_bucket(past - posr)
    bias = jnp.zeros((N_HEADS, topk), F32)
    for bk in range(NUM_BUCKETS):
        bias = jnp.where(bucket == bk, rbt_ref[:, bk:bk + 1], bias)
    shift = GQA.bit_length() - 1
    kv_of_row = jnp.right_shift(lax.broadcasted_iota(I32, (N_HEADS, topk), 0), shift)
    kv_of_row_d = jnp.right_shift(lax.broadcasted_iota(I32, (N_HEADS, HEAD_DIM), 0), shift)
    scale = HEAD_DIM ** -0.5
    qf = q.astype(F32)
    knew = knew_ref[0].astype(BF16).astype(F32)
    vnew = vnew_ref[0].astype(BF16).astype(F32)
    lg = jnp.zeros((N_HEADS, topk), F32)
    lg_new = jnp.zeros((N_HEADS, 1), F32)
    v_new_rows = jnp.zeros((N_HEADS, HEAD_DIM), F32)
    for kh in range(N_KV_HEADS):
        hs = slice(kh * HEAD_DIM, (kh + 1) * HEAD_DIM)
        lg = jnp.where(kv_of_row == kh, _dot_nt(q, kbuf[slot, :, kh, :].astype(BF16)), lg)
        lg_new = jnp.where(kv_of_row[:, 0:1] == kh, jnp.sum(qf * knew[:, hs], axis=1, keepdims=True), lg_new)
        v_new_rows = jnp.where(kv_of_row_d == kh, jnp.broadcast_to(vnew[:, hs], (N_HEADS, HEAD_DIM)), v_new_rows)
    lg = jnp.where(valid, lg * scale + bias, NEG)
    new_sel = newsel_ref[0][:, 0:1] > 0.5
    lg_new = jnp.where(new_sel, lg_new * scale + rbt_ref[:, 0:1], NEG)
    m = jnp.maximum(jnp.max(lg, axis=1, keepdims=True), lg_new)
    p = jnp.where(valid, jnp.exp(lg - m), 0.0)
    p_new = jnp.where(new_sel, jnp.exp(lg_new - m), 0.0)
    denom = jnp.sum(p, axis=1, keepdims=True) + p_new
    pb = (p / denom).astype(BF16)
    out = (p_new / denom).astype(BF16).astype(F32) * v_new_rows
    for kh in range(N_KV_HEADS):
        out = out + jnp.where(kv_of_row_d == kh, _dot(pb, vbuf[slot, :, kh, :].astype(BF16)), 0.0)
    o_ref[0] = out


def _dsa_sample_attn(pos, page_table, q3, k_new, v_new, new_sel, rb_t, cache_k, cache_v, *, layer):
    db, topk = pos.shape
    n_pages = page_table.shape[1]
    page, n_kv, hd = cache_k.shape[2:]
    assert page & (page - 1) == 0
    kvw = n_kv * hd
    row = lambda b, ps, pt: (b, 0, 0)
    grid_spec = pltpu.PrefetchScalarGridSpec(
        num_scalar_prefetch=2,
        grid=(db,),
        in_specs=[
            pl.BlockSpec((1, N_HEADS, HEAD_DIM), row),
            pl.BlockSpec((1, 1, kvw), row),
            pl.BlockSpec((1, 1, kvw), row),
            pl.BlockSpec((1, 1, topk), row),
            pl.BlockSpec((1, 1, LANES), row),
            pl.BlockSpec((N_HEADS, NUM_BUCKETS), lambda b, ps, pt: (0, 0)),
            pl.BlockSpec(memory_space=pl.ANY),
            pl.BlockSpec(memory_space=pl.ANY),
        ],
        out_specs=pl.BlockSpec((1, N_HEADS, HEAD_DIM), row),
        scratch_shapes=[pltpu.VMEM((2, topk, n_kv, hd), F32), pltpu.VMEM((2, topk, n_kv, hd), F32),
                        pltpu.SemaphoreType.DMA((2,)), pltpu.SemaphoreType.DMA((2,))],
    )
    return pl.pallas_call(
        functools.partial(_dsa_sample_attn_kernel, layer=layer, past=n_pages * page, page=page),
        grid_spec=grid_spec,
        out_shape=jax.ShapeDtypeStruct((db, N_HEADS, HEAD_DIM), F32),
        compiler_params=_params("arbitrary"),
        name="dsa_sample_attn",
    )(pos.reshape(-1), page_table, q3, k_new, v_new, pos.reshape(db, 1, topk),
      new_sel.reshape(db, 1, LANES), rb_t, cache_k, cache_v)


def _gate_kernel(att_ref, zu_ref, zv_ref, ga_ref, gb_ref, gs_ref, ws_ref, bs_ref, m_ref, vn_ref, *, first_rows_only):
    aw = att_ref.shape[1]
    tm = zu_ref.shape[0]
    v = jax.nn.gelu(zv_ref[...])
    vn = v * lax.rsqrt(jnp.mean(v * v, axis=-1, keepdims=True) + EPS) * gs_ref[...]
    vn_ref[...] = vn
    m_ref[:, :aw] = (jax.nn.sigmoid(ga_ref[...]) * att_ref[...]).astype(BF16)
    if first_rows_only:
        sgu = jax.nn.gelu(zu_ref[...]) * (vn * ws_ref[...] + bs_ref[...])
        m_ref[:, aw:] = (jax.nn.sigmoid(gb_ref[...]) * sgu).astype(BF16)
    else:
        row = lax.broadcasted_iota(I32, (CHUNK, CHUNK), 0)
        col = lax.broadcasted_iota(I32, (CHUNK, CHUNK), 1)
        for g in range(GMLP_GROUPS):
            cs = slice(g * LANES, (g + 1) * LANES)
            wt = jnp.where(col <= row, ws_ref[g], 0.0).astype(BF16)
            bcol = bs_ref[:, g:g + 1]
            for c in range(tm // CHUNK):
                rs = slice(c * CHUNK, (c + 1) * CHUNK)
                mixed = _dot(wt, vn_ref[rs, cs].astype(BF16)) + bcol
                sgu = jax.nn.gelu(zu_ref[rs, cs]) * mixed
                m_ref[rs, aw + g * LANES:aw + (g + 1) * LANES] = (
                    jax.nn.sigmoid(gb_ref[rs, cs]) * sgu).astype(BF16)


def _gate(att, z, g_sgu, ws, bs, *, first_rows_only, tm, name):
    m, aw = att.shape
    gw = g_sgu.shape[0]
    tm = min(tm, m)
    zb = lambda c: pl.BlockSpec((tm, gw), lambda i: (i, c))
    full = lambda a: pl.BlockSpec(a.shape, lambda i: (0,) * a.ndim)
    first = z.shape[1] // gw - 4
    return pl.pallas_call(
        functools.partial(_gate_kernel, first_rows_only=first_rows_only),
        grid=(m // tm,),
        in_specs=[pl.BlockSpec((tm, aw), lambda i: (i, 0)),
                  zb(first), zb(first + 1), zb(first + 2), zb(first + 3),
                  pl.BlockSpec((1, gw), lambda i: (0, 0)), full(ws), full(bs)],
        out_specs=[pl.BlockSpec((tm, aw + gw), lambda i: (i, 0)), pl.BlockSpec((tm, gw), lambda i: (i, 0))],
        out_shape=[jax.ShapeDtypeStruct((m, aw + gw), BF16), jax.ShapeDtypeStruct((m, gw), F32)],
        compiler_params=_params("arbitrary"),
        name=name,
    )(att, z, z, z, z, g_sgu.reshape(1, gw), ws, bs)


def _matmul_residual_kernel(x_ref, w_ref, h_ref, o_ref):
    o_ref[...] = h_ref[...] + _dot(x_ref[...], w_ref[...])


def _matmul_residual(x, w, h, *, tm, tn, name):
    m, k = x.shape
    n = w.shape[1]
    tm, tn = min(tm, m), min(tn, n)
    return pl.pallas_call(
        _matmul_residual_kernel,
        grid=(m // tm, n // tn),
        in_specs=[pl.BlockSpec((tm, k), lambda i, j: (i, 0)),
                  pl.BlockSpec((k, tn), lambda i, j: (0, j)),
                  pl.BlockSpec((tm, tn), lambda i, j: (i, j))],
        out_specs=pl.BlockSpec((tm, tn), lambda i, j: (i, j)),
        out_shape=jax.ShapeDtypeStruct((m, n), F32),
        compiler_params=_params("arbitrary", "arbitrary"),
        name=name,
    )(x, w, h)


def _cross_kernel(q_ref, mk_ref, mv_ref, h_ref, wo_ref, o_ref):
    rows = q_ref.shape[1]
    q = q_ref[0]
    if rows < SUBLANES:
        q = jnp.broadcast_to(q, (SUBLANES, q.shape[1]))
    qb = q.astype(BF16)
    mk = mk_ref[0].astype(BF16)
    mv = mv_ref[0].astype(BF16)
    hd = mk.shape[1] // X_HEADS
    outs = []
    for hh in range(X_HEADS):
        sl = slice(hh * hd, (hh + 1) * hd)
        lg = _dot_nt(qb[:, sl], mk[:, sl]) * hd ** -0.5
        e = jnp.exp(lg - jnp.max(lg, axis=1, keepdims=True))
        p = e / jnp.sum(e, axis=1, keepdims=True)
        outs.append(_dot(p.astype(BF16), mv[:, sl]))
    y = _dot(jnp.concatenate(outs, axis=1).astype(BF16), wo_ref[...])
    o_ref[0] = h_ref[0] + y[:rows]


def _cross(q, mk, mv, h, w_xo, *, tq, name):
    b, t, xw = q.shape
    d = h.shape[2]
    mlen = mk.shape[1]
    return pl.pallas_call(
        _cross_kernel,
        grid=(b, t // tq),
        in_specs=[pl.BlockSpec((1, tq, xw), lambda bb, i: (bb, i, 0)),
                  pl.BlockSpec((1, mlen, xw), lambda bb, i: (bb, 0, 0)),
                  pl.BlockSpec((1, mlen, xw), lambda bb, i: (bb, 0, 0)),
                  pl.BlockSpec((1, tq, d), lambda bb, i: (bb, i, 0)),
                  pl.BlockSpec((xw, d), lambda bb, i: (0, 0))],
        out_specs=pl.BlockSpec((1, tq, d), lambda bb, i: (bb, i, 0)),
        out_shape=jax.ShapeDtypeStruct((b, t, d), F32),
        compiler_params=_params("arbitrary", "arbitrary"),
        name=name,
    )(q, mk, mv, h, w_xo)


def _top_rows(s, k, payload=None):
    r = s.shape[0]
    rid = lax.broadcasted_iota(I32, s.shape, 0).astype(F32)
    vals, picks = [], []
    for _ in range(k):
        m = jnp.max(s, axis=0, keepdims=True)
        am = jnp.min(jnp.where(s == m, rid, float(r)), axis=0, keepdims=True)
        hit = rid == am
        vals.append(m)
        picks.append(am if payload is None else jnp.sum(jnp.where(hit, payload, 0.0), axis=0, keepdims=True))
        s = jnp.where(hit, -jnp.inf, s)
    return jnp.concatenate(vals, axis=0), jnp.concatenate(picks, axis=0)


def _peer_route_kernel(h_ref, g_ref, w_ref, sk_ref, ids_ref, gate_ref, xn_ref):
    @pl.when(pl.program_id(1) == 0)
    def _():
        x = h_ref[...]
        xn_ref[...] = (x * lax.rsqrt(jnp.mean(x * x, axis=-1, keepdims=True) + EPS) * g_ref[...]).astype(BF16)

    half = sk_ref.shape[2]
    qt = _dot_nt(w_ref[...], xn_ref[...]).astype(BF16)
    v0, i0 = _top_rows(_dot(sk_ref[0], qt[:half]), PEER_TOPK)
    v1, i1 = _top_rows(_dot(sk_ref[1], qt[half:]), PEER_TOPK)
    widths = [PEER_TOPK // (a + 1) for a in range(PEER_TOPK)]
    pad = -sum(widths) % SUBLANES
    cand = jnp.concatenate([v0[a:a + 1] + v1[:n] for a, n in enumerate(widths)]
                           + [jnp.full((pad, v0.shape[1]), -jnp.inf, F32)], axis=0)
    eid = jnp.concatenate([i0[a:a + 1] * float(N_KEYS) + i1[:n] for a, n in enumerate(widths)]
                          + [jnp.zeros((pad, v0.shape[1]), F32)], axis=0)
    sc, e = _top_rows(cand, PEER_TOPK, payload=eid)
    ex = jnp.exp(sc - sc[0:1])
    gate_ref[...] = ex / jnp.sum(ex, axis=0, keepdims=True)
    ids_ref[...] = e.astype(I32)


def _peer_route(h, g_ffn, w_pq_t, sub_keys, *, tb):
    m, d = h.shape
    qd = w_pq_t.shape[0] // PEER_HEADS
    tb = min(tb, m)
    return pl.pallas_call(
        _peer_route_kernel,
        grid=(m // tb, PEER_HEADS),
        in_specs=[pl.BlockSpec((tb, d), lambda t, hh: (t, 0)),
                  pl.BlockSpec((1, d), lambda t, hh: (0, 0)),
                  pl.BlockSpec((qd, d), lambda t, hh: (hh, 0)),
                  pl.BlockSpec(sub_keys.shape, lambda t, hh: (0, 0, 0))],
        out_specs=[pl.BlockSpec((PEER_TOPK, tb), lambda t, hh: (hh, t)),
                   pl.BlockSpec((PEER_TOPK, tb), lambda t, hh: (hh, t))],
        out_shape=[jax.ShapeDtypeStruct((PEER_HEADS * PEER_TOPK, m), I32),
                   jax.ShapeDtypeStruct((PEER_HEADS * PEER_TOPK, m), F32)],
        scratch_shapes=[pltpu.VMEM((tb, d), BF16)],
        compiler_params=_params("arbitrary", "arbitrary"),
        name="peer_route",
    )(h, g_ffn.reshape(1, d), w_pq_t, sub_keys)


def _pack_tables_kernel(u_ref, v_ref, o_ref):
    nr = o_ref.shape[1] // 2

    def words(x):
        bits = pltpu.bitcast(x.astype(BF16).astype(F32), I32)
        half = bits.shape[1] // 2
        return (bits[:, half:] & jnp.int32(-65536)) | lax.shift_right_logical(bits[:, :half], 16)

    wu = words(u_ref[...])
    wv = words(v_ref[...])
    for r in range(nr):
        o_ref[:, r, :] = wu[:, r * LANES:(r + 1) * LANES]
        o_ref[:, nr + r, :] = wv[:, r * LANES:(r + 1) * LANES]


def _pack_tables(u_tab, v_tab, *, eb):
    n, d = u_tab.shape
    nr = d // (2 * LANES)
    return pl.pallas_call(
        _pack_tables_kernel,
        grid=(n // eb,),
        in_specs=[pl.BlockSpec((eb, d), lambda i: (i, 0)), pl.BlockSpec((eb, d), lambda i: (i, 0))],
        out_specs=pl.BlockSpec((eb, 2 * nr, LANES), lambda i: (i, 0, 0)),
        out_shape=jax.ShapeDtypeStruct((n, 2 * nr, LANES), I32),
        compiler_params=_params("arbitrary"),
        name="peer_pack_tables",
    )(u_tab, v_tab)


def _unpack_words(w):
    return pltpu.bitcast(jnp.left_shift(w, 16), F32), pltpu.bitcast(w & jnp.int32(-65536), F32)


def _peer_gather_kernel(ids_hbm, gate_ref, h_ref, gffn_ref, gfin_ref, tab_hbm, o_ref,
                        ids_s, buf, part_s, sem_ids, sem_rows, *, n_steps):
    s = pl.program_id(0)
    g_tok = h_ref.shape[0]
    nr = buf.shape[1] // 2
    n_e = buf.shape[2]
    n_buf = buf.shape[0]
    ahead = n_buf - 1
    assert g_tok % n_buf == 0 and ahead <= g_tok
    cur = lax.rem(s, 2)
    nxt = 1 - cur

    def ids_copy(step, slot):
        return pltpu.make_async_copy(ids_hbm.at[step], ids_s.at[slot], sem_ids.at[slot])

    def row_copy(e, k, slot):
        return pltpu.make_async_copy(tab_hbm.at[e], buf.at[slot, :, k, :], sem_rows.at[slot])

    def issue(ids_slot, t, slot):
        for k in range(n_e):
            row_copy(ids_s[ids_slot, t, k], k, slot).start(priority=k % 2)

    def wait(slot):
        for k in range(n_e):
            row_copy(0, k, slot).wait()

    @pl.when(s == 0)
    def _():
        ids_copy(0, 0).start()
        ids_copy(0, 0).wait()
        for t in range(ahead):
            issue(0, t, t % n_buf)

    @pl.when(s + 1 < n_steps)
    def _():
        ids_copy(s + 1, nxt).start()

    h = h_ref[...]
    xn = h * lax.rsqrt(jnp.mean(h * h, axis=-1, keepdims=True) + EPS) * gffn_ref[...]
    eye = lax.broadcasted_iota(I32, (n_e, n_e), 0) == lax.broadcasted_iota(I32, (n_e, n_e), 1)

    for t in range(g_tok):
        slot = t % n_buf
        ta = t + ahead
        if ta < g_tok:
            issue(cur, ta, ta % n_buf)
        else:
            @pl.when(s + 1 < n_steps)
            def _():
                if ta == g_tok:
                    ids_copy(s + 1, nxt).wait()
                issue(nxt, ta - g_tok, ta % n_buf)
        wait(slot)
        x = xn[t:t + 1, :]
        hacc = jnp.zeros((n_e, LANES), F32)
        for r in range(nr):
            lo, hi = _unpack_words(buf[slot, r])
            hacc = hacc + lo * x[:, r * LANES:(r + 1) * LANES] + hi * x[:, (nr + r) * LANES:(nr + r + 1) * LANES]
        act = jax.nn.gelu(jnp.sum(hacc, axis=-1, keepdims=True))
        gcol = jnp.sum(jnp.where(eye, jnp.broadcast_to(gate_ref[t:t + 1, :], (n_e, n_e)), 0.0),
                       axis=-1, keepdims=True)
        w = jnp.broadcast_to(gcol * act, (n_e, LANES))
        for r in range(nr):
            lo, hi = _unpack_words(buf[slot, nr + r])
            part_s[t, :, r * LANES:(r + 1) * LANES] = jnp.sum(
                (lo * w).reshape(n_e // SUBLANES, SUBLANES, LANES), axis=0)
            part_s[t, :, (nr + r) * LANES:(nr + r + 1) * LANES] = jnp.sum(
                (hi * w).reshape(n_e // SUBLANES, SUBLANES, LANES), axis=0)

    y = h + jnp.sum(part_s[...], axis=1)
    o_ref[...] = y * lax.rsqrt(jnp.mean(y * y, axis=-1, keepdims=True) + EPS) * gfin_ref[...]


PEER_GATHER_BUFFERS = 4


def _peer_gather(ids, gates, h, g_ffn, g_final, packed_tab, *, g_tok):
    m, d = h.shape
    n_e = ids.shape[1]
    n_steps = m // g_tok
    n_rows = packed_tab.shape[1]
    ids3 = ids.reshape(n_steps, g_tok, n_e)
    return pl.pallas_call(
        functools.partial(_peer_gather_kernel, n_steps=n_steps),
        grid=(n_steps,),
        in_specs=[pl.BlockSpec(memory_space=pl.ANY),
                  pl.BlockSpec((g_tok, n_e), lambda s: (s, 0)),
                  pl.BlockSpec((g_tok, d), lambda s: (s, 0)),
                  pl.BlockSpec((1, d), lambda s: (0, 0)),
                  pl.BlockSpec((1, d), lambda s: (0, 0)),
                  pl.BlockSpec(memory_space=pl.ANY)],
        out_specs=pl.BlockSpec((g_tok, d), lambda s: (s, 0)),
        out_shape=jax.ShapeDtypeStruct((m, d), F32),
        scratch_shapes=[pltpu.SMEM((2, g_tok, n_e), I32),
                        pltpu.VMEM((PEER_GATHER_BUFFERS, n_rows, n_e, LANES), I32),
                        pltpu.VMEM((g_tok, SUBLANES, d), F32),
                        pltpu.SemaphoreType.DMA((2,)), pltpu.SemaphoreType.DMA((PEER_GATHER_BUFFERS,))],
        compiler_params=_params("arbitrary"),
        name="peer_gather",
    )(ids3, gates, h, g_ffn.reshape(1, d), g_final.reshape(1, d), packed_tab)


def kernel(x_prompt, x_sample, cache_k, cache_v, cache_kidx, cache_mem_k, cache_mem_v, page_table,
           mem_prompt, rel_bias, g_in, w_in, g_sgu, w_s, b_s, w_out, g_x, w_xq, w_xk, w_xv, w_xo,
           g_ffn, w_pq, sub_keys, peer_u, peer_v, g_final):
    batch, seq, d = x_prompt.shape
    db, dseq, _ = x_sample.shape
    depth = w_in.shape[0]
    assert dseq == 1, "sample group is one new token per sequence"
    n_pages = page_table.shape[1]
    page = cache_k.shape[2]
    past = n_pages * page
    aw, kvw, qiw, gw = N_HEADS * HEAD_DIM, N_KV_HEADS * HEAD_DIM, IDX_HEADS * IDX_DIM, GMLP_GROUPS * LANES
    xw = w_xq.shape[2]
    mlen = mem_prompt.shape[1]
    small_lo = aw + 2 * kvw + qiw
    small_hi = small_lo + IDX_DIM + IDX_HEADS
    assert NUM_BUCKETS // 2 + int(math.log((LANES + 1) / (NUM_BUCKETS // 2)) / math.log(
        MAX_DISTANCE / (NUM_BUCKETS // 2)) * (NUM_BUCKETS // 2)) >= NUM_BUCKETS - 1

    hp = x_prompt.reshape(batch * seq, d)
    hs = x_sample.reshape(db, d)
    btiles = _bias_tiles(rel_bias)
    rb_t = rel_bias.T
    outs = {n: [] for n in ("kp", "vp", "ip", "mkp", "mvp", "ks", "vs", "is", "sv")}

    for l in range(depth):
        w_in_t = jnp.swapaxes(w_in[l], 0, 1).astype(BF16)
        w_small_t = jnp.pad(w_in_t[small_lo:small_hi], ((0, LANES - (small_hi - small_lo)), (0, 0)))
        w_out_b = w_out[l].astype(BF16)
        w_xq_b, w_xk_b, w_xv_b, w_xo_b = (w.astype(BF16) for w in (w_xq[l], w_xk[l], w_xv[l], w_xo[l]))
        w_pq_t = w_pq[l].T.astype(BF16)
        sk_b = sub_keys[l].astype(BF16)
        ws_first = jnp.repeat(w_s[l][:, 0, 0], LANES).reshape(1, gw)
        bs_first = jnp.repeat(b_s[l][:, 0], LANES).reshape(1, gw)
        ones_d = jnp.ones((d,), F32)

        z, zs = _proj_wt(hp, g_in[l], w_in_t, w_small_t, skip_lo=small_lo, skip_hi=small_hi, tm=512, tn=1024,
                         name="in_proj_prompt")
        att = _dsa_prompt(z.reshape(batch, seq, -1), zs.reshape(batch, seq, LANES), btiles,
                          batch=batch, seq=seq, topk=min(TOPK_MAX, seq // 4))
        m_p, _ = _gate(att.reshape(batch * seq, aw), z, g_sgu[l], w_s[l], b_s[l].T,
                       first_rows_only=False, tm=256, name="gate_prompt")
        hp1 = _matmul_residual(m_p, w_out_b, hp, tm=1024, tn=1024, name="out_proj_prompt")
        mem = mem_prompt.reshape(batch * mlen, d)
        mk = _proj(mem, ones_d, w_xk_b, norm=False, tm=256, tn=xw, name="mem_k_proj")
        mv = _proj(mem, ones_d, w_xv_b, norm=False, tm=256, tn=xw, name="mem_v_proj")
        qx = _proj(hp1, g_x[l], w_xq_b, norm=True, tm=512, tn=xw, name="xq_proj_prompt")
        hp2 = _cross(qx.reshape(batch, seq, xw), mk.reshape(batch, mlen, xw), mv.reshape(batch, mlen, xw),
                     hp1.reshape(batch, seq, d), w_xo_b, tq=256, name="cross_prompt").reshape(batch * seq, d)
        outs["kp"].append(z[:, aw:aw + kvw].reshape(batch, seq, N_KV_HEADS, HEAD_DIM))
        outs["vp"].append(z[:, aw + kvw:aw + 2 * kvw].reshape(batch, seq, N_KV_HEADS, HEAD_DIM))
        outs["ip"].append(zs[:, :IDX_DIM].reshape(batch, seq, IDX_DIM))
        outs["mkp"].append(mk.reshape(batch, mlen, X_HEADS, xw // X_HEADS))
        outs["mvp"].append(mv.reshape(batch, mlen, X_HEADS, xw // X_HEADS))

        z_s, zs_s = _proj_wt(hs, g_in[l], w_in_t, w_small_t, skip_lo=small_lo, skip_hi=small_hi, tm=db, tn=512,
                             name="in_proj_sample")
        qi3 = z_s[:, aw + 2 * kvw:aw + 2 * kvw + qiw].reshape(db, IDX_HEADS, IDX_DIM)
        wi3 = zs_s[:, IDX_DIM:IDX_DIM + IDX_HEADS].reshape(db, IDX_HEADS, 1)
        ki_new = zs_s[:, :IDX_DIM].reshape(db, 1, IDX_DIM)
        k_new = z_s[:, aw:aw + kvw].reshape(db, 1, kvw)
        v_new = z_s[:, aw + kvw:aw + 2 * kvw].reshape(db, 1, kvw)
        scores3, snew3 = _dsa_sample_scores(page_table, qi3, wi3, ki_new.reshape(db, IDX_DIM, 1),
                                            jnp.swapaxes(cache_kidx[l], 1, 2))
        pos3, new_sel = _dsa_sample_select(scores3.reshape(db, past), snew3.reshape(db, LANES),
                                           topk=min(TOPK_MAX, (past + dseq) // 4))
        att_s = _dsa_sample_attn(pos3[:, :, 0], page_table, z_s[:, :aw].reshape(db, N_HEADS, HEAD_DIM),
                                 k_new, v_new, new_sel, rb_t, cache_k, cache_v, layer=l)
        m_s, vn_s = _gate(att_s.reshape(db, aw), z_s, g_sgu[l], ws_first, bs_first,
                          first_rows_only=True, tm=db, name="gate_sample")
        hs1 = _matmul_residual(m_s, w_out_b, hs, tm=db, tn=1024, name="out_proj_sample")
        qx_s = _proj(hs1, g_x[l], w_xq_b, norm=True, tm=db, tn=xw, name="xq_proj_sample")
        hs2 = _cross(qx_s.reshape(db, 1, xw), cache_mem_k[l].reshape(db, mlen, xw),
                     cache_mem_v[l].reshape(db, mlen, xw), hs1.reshape(db, 1, d), w_xo_b,
                     tq=1, name="cross_sample").reshape(db, d)
        outs["ks"].append(k_new.reshape(db, dseq, N_KV_HEADS, HEAD_DIM))
        outs["vs"].append(v_new.reshape(db, dseq, N_KV_HEADS, HEAD_DIM))
        outs["is"].append(ki_new.reshape(db, dseq, IDX_DIM))
        outs["sv"].append(vn_s.reshape(db, dseq, gw))

        assert depth == 1
        packed = _pack_tables(peer_u[l], peer_v[l], eb=256)
        ids_p, gates_p = _peer_route(hp2, g_ffn[l], w_pq_t, sk_b, tb=512)
        y_p = _peer_gather(ids_p.T, gates_p.T, hp2, g_ffn[l], g_final, packed, g_tok=8)
        hs2_pad = jnp.pad(hs2, ((0, -db % LANES), (0, 0)))
        ids_s, gates_s = _peer_route(hs2_pad, g_ffn[l], w_pq_t, sk_b, tb=LANES)
        y_s = _peer_gather(ids_s.T[:db], gates_s.T[:db], hs2, g_ffn[l], g_final, packed, g_tok=8)

    y_prompt = y_p.reshape(batch, seq, d)
    y_sample = y_s.reshape(db, dseq, d)
    st = lambda n: jnp.stack(outs[n])
    return (y_prompt, y_sample, st("kp"), st("vp"), st("ip"), st("mkp"), st("mvp"),
            st("ks"), st("vs"), st("is"), st("sv"))
```

```python
import functools
import math

import jax
import jax.numpy as jnp
from jax import lax
from jax.experimental import pallas as pl
from jax.experimental.pallas import tpu as pltpu

F32 = jnp.float32
BF16 = jnp.bfloat16
I32 = jnp.int32

N_HEADS = 16
HEAD_DIM = 128
N_KV_HEADS = 4
GQA = N_HEADS // N_KV_HEADS
IDX_HEADS = 16
IDX_DIM = 64
TOPK_MAX = 256
NUM_BUCKETS = 32
MAX_DISTANCE = 128
CHUNK = 128
GMLP_GROUPS = 16
X_HEADS = 4
N_KEYS = 128
PEER_HEADS = 8
PEER_TOPK = 16
EPS = 1e-6

LANES = 128
SUBLANES = 8
VMEM_LIMIT_BYTES = 56 * 1024 * 1024

NEG = -1e30
INT_MIN = -(2 ** 31)


def _dot_nt(a, b):
    return lax.dot_general(a, b, (((1,), (1,)), ((), ())), preferred_element_type=F32)


def _dot(a, b):
    return jnp.dot(a, b, preferred_element_type=F32)


def _params(*sem):
    return pltpu.CompilerParams(dimension_semantics=sem, vmem_limit_bytes=VMEM_LIMIT_BYTES)


def _sortable(x):
    bits = pltpu.bitcast(x, I32)
    return bits ^ (jnp.right_shift(bits, 31) & 0x7FFFFFFF)


def _t5_bucket(dist):
    n = jnp.maximum(dist, 0)
    max_exact = NUM_BUCKETS // 2
    nf = jnp.maximum(n, 1).astype(F32)
    large = max_exact + (jnp.log(nf / max_exact) / math.log(MAX_DISTANCE / max_exact)
                         * (NUM_BUCKETS - max_exact)).astype(I32)
    large = jnp.minimum(large, NUM_BUCKETS - 1)
    return jnp.where(n < max_exact, n, large)


def _kth_largest_key(count_ge, shape, k):
    def body(it, t):
        cand = t | jnp.left_shift(jnp.int32(1), 31 - it)
        cnt = count_ge(cand ^ INT_MIN)
        return jnp.where(cnt >= k, cand, t)

    t = lax.fori_loop(0, 32, body, jnp.zeros(shape, I32))
    return t ^ INT_MIN


def _prefix_matrix():
    return (lax.broadcasted_iota(I32, (LANES, LANES), 0) <= lax.broadcasted_iota(I32, (LANES, LANES), 1)).astype(BF16)


def _proj_kernel(x_ref, g_ref, w_ref, o_ref, xn_ref, *, norm):
    @pl.when(pl.program_id(1) == 0)
    def _():
        x = x_ref[...]
        if norm:
            x = x * lax.rsqrt(jnp.mean(x * x, axis=-1, keepdims=True) + EPS) * g_ref[...]
        xn_ref[...] = x.astype(BF16)

    o_ref[...] = _dot(xn_ref[...], w_ref[...])


def _proj(x, g, w, *, norm, tm, tn, name):
    m, k = x.shape
    n = w.shape[1]
    tm, tn = min(tm, m), min(tn, n)
    return pl.pallas_call(
        functools.partial(_proj_kernel, norm=norm),
        grid=(m // tm, n // tn),
        in_specs=[pl.BlockSpec((tm, k), lambda i, j: (i, 0)),
                  pl.BlockSpec((1, k), lambda i, j: (0, 0)),
                  pl.BlockSpec((k, tn), lambda i, j: (0, j))],
        out_specs=pl.BlockSpec((tm, tn), lambda i, j: (i, j)),
        out_shape=jax.ShapeDtypeStruct((m, n), F32),
        scratch_shapes=[pltpu.VMEM((tm, k), BF16)],
        compiler_params=_params("arbitrary", "arbitrary"),
        name=name,
    )(x, g.reshape(1, k), w)


def _proj_wt_kernel(x_ref, g_ref, wt_ref, w2t_ref, o_ref, o2_ref, xn_ref):
    @pl.when(pl.program_id(1) == 0)
    def _():
        x = x_ref[...]
        xn_ref[...] = (x * lax.rsqrt(jnp.mean(x * x, axis=-1, keepdims=True) + EPS) * g_ref[...]).astype(BF16)
        o2_ref[...] = _dot_nt(xn_ref[...], w2t_ref[...])

    o_ref[...] = _dot_nt(xn_ref[...], wt_ref[...])


def _proj_wt(x, g, w_t, w_side_t, *, skip_lo, skip_hi, tm, tn, name):
    m, k = x.shape
    gap = skip_hi - skip_lo
    n = w_t.shape[0] - gap
    n2 = w_side_t.shape[0]
    tm, tn = min(tm, m), min(tn, n)
    assert skip_lo % tn == 0 and n % tn == 0

    def w_rows(i, j):
        start = j * tn
        return (pl.multiple_of(jnp.where(start < skip_lo, start, start + gap), math.gcd(tn, gap)), 0)

    return pl.pallas_call(
        _proj_wt_kernel,
        grid=(m // tm, n // tn),
        in_specs=[pl.BlockSpec((tm, k), lambda i, j: (i, 0)),
                  pl.BlockSpec((1, k), lambda i, j: (0, 0)),
                  pl.BlockSpec((pl.Element(tn), pl.Element(k)), w_rows),
                  pl.BlockSpec((n2, k), lambda i, j: (0, 0))],
        out_specs=[pl.BlockSpec((tm, tn), lambda i, j: (i, j)), pl.BlockSpec((tm, n2), lambda i, j: (i, 0))],
        out_shape=[jax.ShapeDtypeStruct((m, n), F32), jax.ShapeDtypeStruct((m, n2), F32)],
        scratch_shapes=[pltpu.VMEM((tm, k), BF16)],
        compiler_params=_params("arbitrary", "arbitrary"),
        name=name,
    )(x, g.reshape(1, k), w_t, w_side_t)


def _bias_tiles_kernel(rb_ref, o_ref):
    tq = lax.broadcasted_iota(I32, (LANES, LANES), 0)
    c = lax.broadcasted_iota(I32, (LANES, LANES), 1)
    for part, off in enumerate((2 * LANES, LANES, 0)):
        dist = tq - c + off
        bucket = _t5_bucket(dist)
        for h in range(N_HEADS):
            tile = lax.fori_loop(
                0, NUM_BUCKETS, lambda b, acc: jnp.where(bucket == b, rb_ref[b, h], acc),
                jnp.zeros((LANES, LANES), F32))
            if off == 0:
                tile = jnp.where(dist < 0, NEG, tile)
            o_ref[part, h] = tile


def _bias_tiles(rel_bias):
    return pl.pallas_call(
        _bias_tiles_kernel,
        in_specs=[pl.BlockSpec(memory_space=pltpu.SMEM)],
        out_specs=pl.BlockSpec(memory_space=pltpu.VMEM),
        out_shape=jax.ShapeDtypeStruct((3, N_HEADS, LANES, LANES), F32),
        name="t5_bias_tiles",
    )(rel_bias)


def _dsa_prompt_kernel(q_ref, k_ref, v_ref, qi_ref, zsk_ref, zsq_ref, bt_ref, tri_ref, o_ref,
                       kbf, vbf, kilo, kihi, key3, qs, m_s, l_s, acc_s, *, topk):
    i = pl.program_id(1)
    tq = q_ref.shape[1]
    s = k_ref.shape[1]
    nkb = s // LANES
    cw = 2 * LANES

    @pl.when(i == 0)
    def _():
        kbf[...] = k_ref[0].astype(BF16)
        vbf[...] = v_ref[0].astype(BF16)
        zs = zsk_ref[0]
        lane = lax.broadcasted_iota(I32, zs.shape, 1)
        kilo[...] = jnp.where(lane < IDX_DIM, zs, 0.0).astype(BF16)
        kihi[...] = jnp.where(lane >= IDX_DIM, pltpu.roll(zs, IDX_DIM, axis=1), 0.0).astype(BF16)

    qi = qi_ref[0].astype(BF16)
    coef = zsq_ref[0][:, IDX_DIM:IDX_DIM + IDX_HEADS] * (IDX_HEADS ** -0.5 * IDX_DIM ** -0.5)
    t_pos = i * tq + lax.broadcasted_iota(I32, (tq, cw), 0)
    t_last = i * tq + tq - 1
    for c in range(s // cw):
        @pl.when(c * cw <= t_last)
        def _():
            klo = kilo[c * cw:(c + 1) * cw, :]
            khi = kihi[c * cw:(c + 1) * cw, :]
            acc = jnp.zeros((tq, cw), F32)
            for p in range(IDX_HEADS // 2):
                a = qi[:, p * LANES:(p + 1) * LANES]
                acc = acc + coef[:, 2 * p:2 * p + 1] * jnp.maximum(_dot_nt(a, klo), 0.0)
                acc = acc + coef[:, 2 * p + 1:2 * p + 2] * jnp.maximum(_dot_nt(a, khi), 0.0)
            s_pos = c * cw + lax.broadcasted_iota(I32, (tq, cw), 1)
            key = _sortable(jnp.where(s_pos <= t_pos, acc, -jnp.inf))
            for u in range(cw // LANES):
                key3[c * (cw // LANES) + u] = key[:, u * LANES:(u + 1) * LANES]

        @pl.when(c * cw > t_last)
        def _():
            for u in range(cw // LANES):
                key3[c * (cw // LANES) + u] = jnp.full((tq, LANES), INT_MIN, I32)

    def count(pred):
        return jnp.sum(jnp.sum(jnp.where(pred, 1.0, 0.0), axis=0), axis=1, keepdims=True)

    thr = _kth_largest_key(lambda cs: count(key3[...] >= cs[None]), (tq, 1), float(topk))

    n_ge = count(key3[...] >= thr[None])

    @pl.when(jnp.max(n_ge) > float(topk))
    def _():
        need = float(topk) - count(key3[...] > thr[None])
        before = jnp.zeros((tq, 1), F32)
        for j in range(nkb):
            kj = key3[j]
            eq = kj == thr
            rank = _dot(jnp.where(eq, 1.0, 0.0).astype(BF16), tri_ref[...]) + before
            key3[j] = jnp.where(jnp.logical_and(eq, rank > need), INT_MIN, kj)
            before = rank[:, LANES - 1:LANES]

    thr4 = jnp.concatenate([thr] * GQA, axis=0)

    qb = q_ref[0].astype(BF16)
    for kh in range(N_KV_HEADS):
        qs[kh] = jnp.concatenate(
            [qb[:, (GQA * kh + g) * HEAD_DIM:(GQA * kh + g + 1) * HEAD_DIM] for g in range(GQA)], axis=0)
    m_s[...] = jnp.full(m_s.shape, NEG, F32)
    l_s[...] = jnp.zeros(l_s.shape, F32)
    acc_s[...] = jnp.zeros(acc_s.shape, F32)
    scale = HEAD_DIM ** -0.5

    def fbody(jj, carry):
        ja = 2 * jj
        jb = ja + 1
        part_a = jnp.clip(ja - i + 2, 0, 2)
        part_b = jnp.clip(jb - i + 2, 0, 2)
        key_b = jnp.where(jb <= i, key3[jb], INT_MIN)
        key8 = jnp.concatenate([key3[ja], key_b], axis=1)
        mask = jnp.concatenate([key8] * GQA, axis=0) >= thr4
        r0 = pl.multiple_of(ja * LANES, 2 * LANES)
        for kh in range(N_KV_HEADS):
            kj = kbf[pl.ds(r0, 2 * LANES), kh * HEAD_DIM:(kh + 1) * HEAD_DIM]
            vj = vbf[pl.ds(r0, 2 * LANES), kh * HEAD_DIM:(kh + 1) * HEAD_DIM]
            bias = jnp.concatenate(
                [bt_ref[part_a, GQA * kh:GQA * (kh + 1)].reshape(GQA * tq, LANES),
                 bt_ref[part_b, GQA * kh:GQA * (kh + 1)].reshape(GQA * tq, LANES)], axis=1)
            lg = jnp.where(mask, _dot_nt(qs[kh], kj) * scale + bias, NEG)
            m_old = m_s[kh]
            m_new = jnp.maximum(m_old, jnp.max(lg, axis=1, keepdims=True))
            alpha = jnp.exp(m_old - m_new)
            p = jnp.where(mask, jnp.exp(lg - jnp.concatenate([m_new, m_new], axis=1)), 0.0)
            l_s[kh] = alpha * l_s[kh] + jnp.sum(p, axis=1, keepdims=True)
            acc_s[kh] = alpha * acc_s[kh] + _dot(p.astype(BF16), vj)
            m_s[kh] = m_new
        return carry

    assert nkb % 2 == 0
    lax.fori_loop(0, jnp.right_shift(i + 2, 1), fbody, 0)
    for kh in range(N_KV_HEADS):
        out = acc_s[kh] / l_s[kh]
        for g in range(GQA):
            h = GQA * kh + g
            o_ref[0, :, h * HEAD_DIM:(h + 1) * HEAD_DIM] = out[g * tq:(g + 1) * tq]


def _dsa_prompt(z, zs, btiles, *, batch, seq, topk):
    tq = LANES
    aw = N_HEADS * HEAD_DIM
    kvw = N_KV_HEADS * HEAD_DIM
    qiw = IDX_HEADS * IDX_DIM
    return pl.pallas_call(
        functools.partial(_dsa_prompt_kernel, topk=topk),
        grid=(batch, seq // tq),
        in_specs=[
            pl.BlockSpec((1, tq, aw), lambda b, i: (b, i, 0)),
            pl.BlockSpec((1, seq, kvw), lambda b, i: (b, 0, aw // kvw)),
            pl.BlockSpec((1, seq, kvw), lambda b, i: (b, 0, aw // kvw + 1)),
            pl.BlockSpec((1, tq, qiw), lambda b, i: (b, i, (aw + 2 * kvw) // qiw)),
            pl.BlockSpec((1, seq, LANES), lambda b, i: (b, 0, 0)),
            pl.BlockSpec((1, tq, LANES), lambda b, i: (b, i, 0)),
            pl.BlockSpec((3, N_HEADS, LANES, LANES), lambda b, i: (0, 0, 0, 0)),
            pl.BlockSpec((LANES, LANES), lambda b, i: (0, 0)),
        ],
        out_specs=pl.BlockSpec((1, tq, aw), lambda b, i: (b, i, 0)),
        out_shape=jax.ShapeDtypeStruct((batch, seq, aw), F32),
        scratch_shapes=[
            pltpu.VMEM((seq, kvw), BF16), pltpu.VMEM((seq, kvw), BF16),
            pltpu.VMEM((seq, LANES), BF16), pltpu.VMEM((seq, LANES), BF16),
            pltpu.VMEM((seq // LANES, tq, LANES), I32),
            pltpu.VMEM((N_KV_HEADS, GQA * tq, HEAD_DIM), BF16),
            pltpu.VMEM((N_KV_HEADS, GQA * tq, LANES), F32),
            pltpu.VMEM((N_KV_HEADS, GQA * tq, LANES), F32),
            pltpu.VMEM((N_KV_HEADS, GQA * tq, HEAD_DIM), F32),
        ],
        compiler_params=_params("arbitrary", "arbitrary"),
        name="dsa_prompt",
    )(z, z, z, z, zs, zs, btiles, _prefix_matrix())


SAMPLE_PAGES_PER_STEP = 32


def _dsa_sample_score_kernel(pt_ref, qi_ref, wi_ref, kinew_ref, *rest):
    kidx_refs, (o_ref, onew_ref) = rest[:-2], rest[-2:]
    qi = qi_ref[0].astype(BF16)
    coef = wi_ref[0] * (IDX_HEADS ** -0.5 * IDX_DIM ** -0.5)

    def score(ki_t):
        return jnp.sum(jnp.maximum(_dot(qi, ki_t.astype(BF16)), 0.0) * coef, axis=0, keepdims=True)

    o_ref[0] = jnp.concatenate([score(r[0]) for r in kidx_refs], axis=1)

    @pl.when(pl.program_id(1) == pl.num_programs(1) - 1)
    def _():
        onew_ref[0] = score(jnp.broadcast_to(kinew_ref[0], (kinew_ref.shape[1], LANES)))


def _dsa_sample_scores(page_table, qi3, wi3, ki_new_t, cache_kidx_t):
    db, n_pages = page_table.shape
    page = cache_kidx_t.shape[2]
    pg = math.gcd(SAMPLE_PAGES_PER_STEP, n_pages)
    row = lambda b, j, pt: (b, 0, 0)
    grid_spec = pltpu.PrefetchScalarGridSpec(
        num_scalar_prefetch=1,
        grid=(db, n_pages // pg),
        in_specs=[pl.BlockSpec((1, IDX_HEADS, IDX_DIM), row),
                  pl.BlockSpec((1, IDX_HEADS, 1), row),
                  pl.BlockSpec((1, IDX_DIM, 1), row)]
        + [pl.BlockSpec((1, IDX_DIM, page), functools.partial(lambda b, j, pt, u: (pt[b, j * pg + u], 0, 0), u=u))
           for u in range(pg)],
        out_specs=[pl.BlockSpec((1, 1, pg * page), lambda b, j, pt: (b, 0, j)),
                   pl.BlockSpec((1, 1, LANES), row)],
    )
    return pl.pallas_call(
        _dsa_sample_score_kernel,
        grid_spec=grid_spec,
        out_shape=[jax.ShapeDtypeStruct((db, 1, n_pages * page), F32),
                   jax.ShapeDtypeStruct((db, 1, LANES), F32)],
        compiler_params=_params("arbitrary", "arbitrary"),
        name="dsa_sample_scores",
    )(page_table, qi3, wi3, ki_new_t, *([cache_kidx_t] * pg))


def _dsa_sample_select_kernel(s_ref, snew_ref, tri_ref, pos_ref, new_ref, rank_s, nsel_s, *, topk):
    db, past = s_ref.shape
    k = float(topk)
    key = _sortable(s_ref[...])
    key_new = _sortable(snew_ref[:, 0:1])

    def count_ge(cs):
        return (jnp.sum(jnp.where(key >= cs, 1.0, 0.0), axis=1, keepdims=True)
                + jnp.where(key_new >= cs, 1.0, 0.0))

    thr = _kth_largest_key(count_ge, (db, 1), k)
    gt = key > thr
    eq = key == thr
    need = k - (jnp.sum(jnp.where(gt, 1.0, 0.0), axis=1, keepdims=True) + jnp.where(key_new > thr, 1.0, 0.0))

    tri = tri_ref[...]

    def cumsum(flags):
        out, off = [], jnp.zeros((db, 1), F32)
        for blk in range(past // LANES):
            pc = _dot(flags[:, blk * LANES:(blk + 1) * LANES].astype(BF16), tri) + off
            out.append(pc)
            off = pc[:, LANES - 1:LANES]
        return jnp.concatenate(out, axis=1), off

    eq_rank, eq_total = cumsum(jnp.where(eq, 1.0, 0.0))
    sel = jnp.where(jnp.logical_or(gt, jnp.logical_and(eq, eq_rank <= need)), 1.0, 0.0)
    new_sel = jnp.logical_or(key_new > thr, jnp.logical_and(key_new == thr, eq_total < need))
    new_ref[...] = jnp.broadcast_to(jnp.where(new_sel, 1.0, 0.0), new_ref.shape)
    rank, n_sel = cumsum(sel)
    rank = jnp.where(sel > 0.0, rank, 0.0)
    for b in range(db):
        rank_s[b] = rank[b:b + 1, :]
        nsel_s[b] = jnp.broadcast_to(n_sel[b:b + 1, :], (1, LANES))
    ch = min(8 * LANES, past)
    slot = lax.broadcasted_iota(I32, (topk, ch), 0).astype(F32) + 1.0
    lane = lax.broadcasted_iota(I32, (topk, ch), 1).astype(F32)
    slot_col = lax.broadcasted_iota(I32, (topk, 1), 0).astype(F32)

    def extract(b, carry):
        r = rank_s[b]
        pos = jnp.zeros((topk, 1), F32)
        for c in range(past // ch):
            hit = r[:, c * ch:(c + 1) * ch] == slot
            pos = pos + jnp.sum(jnp.where(hit, lane + float(c * ch), 0.0), axis=1, keepdims=True)
        pos = jnp.where(slot_col < nsel_s[b][:, 0:1], pos, -1.0)
        pos_ref[b] = jnp.broadcast_to(pos, (topk, LANES)).astype(I32)
        return carry

    lax.fori_loop(0, db, extract, 0)


def _dsa_sample_select(scores, snew, *, topk):
    db = scores.shape[0]
    tri = _prefix_matrix()
    return pl.pallas_call(
        functools.partial(_dsa_sample_select_kernel, topk=topk),
        out_shape=[jax.ShapeDtypeStruct((db, topk, LANES), I32), jax.ShapeDtypeStruct((db, LANES), F32)],
        scratch_shapes=[pltpu.VMEM((db, 1, scores.shape[1]), F32), pltpu.VMEM((db, 1, LANES), F32)],
        compiler_params=pltpu.CompilerParams(vmem_limit_bytes=VMEM_LIMIT_BYTES),
        name="dsa_sample_select",
    )(scores, snew, tri)


def _dsa_sample_attn_kernel(pos_s, pt_s, q_ref, knew_ref, vnew_ref, posrow_ref, newsel_ref, rbt_ref,
                            ck_hbm, cv_hbm, o_ref, kbuf, vbuf, sem_k, sem_v, *, layer, past, page):
    b = pl.program_id(0)
    topk = kbuf.shape[1]
    slot = lax.rem(b, 2)
    page_shift = page.bit_length() - 1
    unroll = 8

    def row_copies(bb, j, sl):
        p = jnp.maximum(pos_s[bb * topk + j], 0)
        pg = pt_s[bb, jnp.right_shift(p, page_shift)]
        off = p & (page - 1)
        return (pltpu.make_async_copy(ck_hbm.at[layer, pg, off], kbuf.at[sl, j], sem_k.at[sl]),
                pltpu.make_async_copy(cv_hbm.at[layer, pg, off], vbuf.at[sl, j], sem_v.at[sl]))

    def issue(bb, sl):
        def body(i, carry):
            for u in range(unroll):
                ck, cv = row_copies(bb, i * unroll + u, sl)
                ck.start()
                cv.start()
            return carry
        lax.fori_loop(0, topk // unroll, body, 0)

    def wait(sl):
        def body(i, carry):
            for u in range(unroll):
                j = i * unroll + u
                pltpu.make_async_copy(ck_hbm.at[layer, 0, 0], kbuf.at[sl, j], sem_k.at[sl]).wait()
                pltpu.make_async_copy(cv_hbm.at[layer, 0, 0], vbuf.at[sl, j], sem_v.at[sl]).wait()
            return carry
        lax.fori_loop(0, topk // unroll, body, 0)

    @pl.when(b == 0)
    def _():
        issue(0, 0)

    @pl.when(b + 1 < pl.num_programs(0))
    def _():
        issue(b + 1, 1 - slot)

    wait(slot)

    q = q_ref[0].astype(BF16)
    posr = posrow_ref[0]
    valid = posr >= 0
    bucket = _t5_bucket(past - posr)
    bias = jnp.zeros((N_HEADS, topk), F32)
    for bk in range(NUM_BUCKETS):
        bias = jnp.where(bucket == bk, rbt_ref[:, bk:bk + 1], bias)
    shift = GQA.bit_length() - 1
    kv_of_row = jnp.right_shift(lax.broadcasted_iota(I32, (N_HEADS, topk), 0), shift)
    kv_of_row_d = jnp.right_shift(lax.broadcasted_iota(I32, (N_HEADS, HEAD_DIM), 0), shift)
    scale = HEAD_DIM ** -0.5
    qf = q.astype(F32)
    knew = knew_ref[0].astype(BF16).astype(F32)
    vnew = vnew_ref[0].astype(BF16).astype(F32)
    lg = jnp.zeros((N_HEADS, topk), F32)
    lg_new = jnp.zeros((N_HEADS, 1), F32)
    v_new_rows = jnp.zeros((N_HEADS, HEAD_DIM), F32)
    for kh in range(N_KV_HEADS):
        hs = slice(kh * HEAD_DIM, (kh + 1) * HEAD_DIM)
        lg = jnp.where(kv_of_row == kh, _dot_nt(q, kbuf[slot, :, kh, :].astype(BF16)), lg)
        lg_new = jnp.where(kv_of_row[:, 0:1] == kh, jnp.sum(qf * knew[:, hs], axis=1, keepdims=True), lg_new)
        v_new_rows = jnp.where(kv_of_row_d == kh, jnp.broadcast_to(vnew[:, hs], (N_HEADS, HEAD_DIM)), v_new_rows)
    lg = jnp.where(valid, lg * scale + bias, NEG)
    new_sel = newsel_ref[0][:, 0:1] > 0.5
    lg_new = jnp.where(new_sel, lg_new * scale + rbt_ref[:, 0:1], NEG)
    m = jnp.maximum(jnp.max(lg, axis=1, keepdims=True), lg_new)
    p = jnp.where(valid, jnp.exp(lg - m), 0.0)
    p_new = jnp.where(new_sel, jnp.exp(lg_new - m), 0.0)
    denom = jnp.sum(p, axis=1, keepdims=True) + p_new
    pb = (p / denom).astype(BF16)
    out = (p_new / denom).astype(BF16).astype(F32) * v_new_rows
    for kh in range(N_KV_HEADS):
        out = out + jnp.where(kv_of_row_d == kh, _dot(pb, vbuf[slot, :, kh, :].astype(BF16)), 0.0)
    o_ref[0] = out


def _dsa_sample_attn(pos, page_table, q3, k_new, v_new, new_sel, rb_t, cache_k, cache_v, *, layer):
    db, topk = pos.shape
    n_pages = page_table.shape[1]
    page, n_kv, hd = cache_k.shape[2:]
    assert page & (page - 1) == 0
    kvw = n_kv * hd
    row = lambda b, ps, pt: (b, 0, 0)
    grid_spec = pltpu.PrefetchScalarGridSpec(
        num_scalar_prefetch=2,
        grid=(db,),
        in_specs=[
            pl.BlockSpec((1, N_HEADS, HEAD_DIM), row),
            pl.BlockSpec((1, 1, kvw), row),
            pl.BlockSpec((1, 1, kvw), row),
            pl.BlockSpec((1, 1, topk), row),
            pl.BlockSpec((1, 1, LANES), row),
            pl.BlockSpec((N_HEADS, NUM_BUCKETS), lambda b, ps, pt: (0, 0)),
            pl.BlockSpec(memory_space=pl.ANY),
            pl.BlockSpec(memory_space=pl.ANY),
        ],
        out_specs=pl.BlockSpec((1, N_HEADS, HEAD_DIM), row),
        scratch_shapes=[pltpu.VMEM((2, topk, n_kv, hd), F32), pltpu.VMEM((2, topk, n_kv, hd), F32),
                        pltpu.SemaphoreType.DMA((2,)), pltpu.SemaphoreType.DMA((2,))],
    )
    return pl.pallas_call(
        functools.partial(_dsa_sample_attn_kernel, layer=layer, past=n_pages * page, page=page),
        grid_spec=grid_spec,
        out_shape=jax.ShapeDtypeStruct((db, N_HEADS, HEAD_DIM), F32),
        compiler_params=_params("arbitrary"),
        name="dsa_sample_attn",
    )(pos.reshape(-1), page_table, q3, k_new, v_new, pos.reshape(db, 1, topk),
      new_sel.reshape(db, 1, LANES), rb_t, cache_k, cache_v)


def _gate_kernel(att_ref, zu_ref, zv_ref, ga_ref, gb_ref, gs_ref, ws_ref, bs_ref, m_ref, vn_ref, *, first_rows_only):
    aw = att_ref.shape[1]
    tm = zu_ref.shape[0]
    v = jax.nn.gelu(zv_ref[...])
    vn = v * lax.rsqrt(jnp.mean(v * v, axis=-1, keepdims=True) + EPS) * gs_ref[...]
    vn_ref[...] = vn
    m_ref[:, :aw] = (jax.nn.sigmoid(ga_ref[...]) * att_ref[...]).astype(BF16)
    if first_rows_only:
        sgu = jax.nn.gelu(zu_ref[...]) * (vn * ws_ref[...] + bs_ref[...])
        m_ref[:, aw:] = (jax.nn.sigmoid(gb_ref[...]) * sgu).astype(BF16)
    else:
        row = lax.broadcasted_iota(I32, (CHUNK, CHUNK), 0)
        col = lax.broadcasted_iota(I32, (CHUNK, CHUNK), 1)
        for g in range(GMLP_GROUPS):
            cs = slice(g * LANES, (g + 1) * LANES)
            wt = jnp.where(col <= row, ws_ref[g], 0.0).astype(BF16)
            bcol = bs_ref[:, g:g + 1]
            for c in range(tm // CHUNK):
                rs = slice(c * CHUNK, (c + 1) * CHUNK)
                mixed = _dot(wt, vn_ref[rs, cs].astype(BF16)) + bcol
                sgu = jax.nn.gelu(zu_ref[rs, cs]) * mixed
                m_ref[rs, aw + g * LANES:aw + (g + 1) * LANES] = (
                    jax.nn.sigmoid(gb_ref[rs, cs]) * sgu).astype(BF16)


def _gate(att, z, g_sgu, ws, bs, *, first_rows_only, tm, name):
    m, aw = att.shape
    gw = g_sgu.shape[0]
    tm = min(tm, m)
    zb = lambda c: pl.BlockSpec((tm, gw), lambda i: (i, c))
    full = lambda a: pl.BlockSpec(a.shape, lambda i: (0,) * a.ndim)
    first = z.shape[1] // gw - 4
    return pl.pallas_call(
        functools.partial(_gate_kernel, first_rows_only=first_rows_only),
        grid=(m // tm,),
        in_specs=[pl.BlockSpec((tm, aw), lambda i: (i, 0)),
                  zb(first), zb(first + 1), zb(first + 2), zb(first + 3),
                  pl.BlockSpec((1, gw), lambda i: (0, 0)), full(ws), full(bs)],
        out_specs=[pl.BlockSpec((tm, aw + gw), lambda i: (i, 0)), pl.BlockSpec((tm, gw), lambda i: (i, 0))],
        out_shape=[jax.ShapeDtypeStruct((m, aw + gw), BF16), jax.ShapeDtypeStruct((m, gw), F32)],
        compiler_params=_params("arbitrary"),
        name=name,
    )(att, z, z, z, z, g_sgu.reshape(1, gw), ws, bs)


def _matmul_residual_kernel(x_ref, w_ref, h_ref, o_ref):
    o_ref[...] = h_ref[...] + _dot(x_ref[...], w_ref[...])


def _matmul_residual(x, w, h, *, tm, tn, name):
    m, k = x.shape
    n = w.shape[1]
    tm, tn = min(tm, m), min(tn, n)
    return pl.pallas_call(
        _matmul_residual_kernel,
        grid=(m // tm, n // tn),
        in_specs=[pl.BlockSpec((tm, k), lambda i, j: (i, 0)),
                  pl.BlockSpec((k, tn), lambda i, j: (0, j)),
                  pl.BlockSpec((tm, tn), lambda i, j: (i, j))],
        out_specs=pl.BlockSpec((tm, tn), lambda i, j: (i, j)),
        out_shape=jax.ShapeDtypeStruct((m, n), F32),
        compiler_params=_params("arbitrary", "arbitrary"),
        name=name,
    )(x, w, h)


def _cross_kernel(q_ref, mk_ref, mv_ref, h_ref, wo_ref, o_ref):
    rows = q_ref.shape[1]
    q = q_ref[0]
    if rows < SUBLANES:
        q = jnp.broadcast_to(q, (SUBLANES, q.shape[1]))
    qb = q.astype(BF16)
    mk = mk_ref[0].astype(BF16)
    mv = mv_ref[0].astype(BF16)
    hd = mk.shape[1] // X_HEADS
    outs = []
    for hh in range(X_HEADS):
        sl = slice(hh * hd, (hh + 1) * hd)
        lg = _dot_nt(qb[:, sl], mk[:, sl]) * hd ** -0.5
        e = jnp.exp(lg - jnp.max(lg, axis=1, keepdims=True))
        p = e / jnp.sum(e, axis=1, keepdims=True)
        outs.append(_dot(p.astype(BF16), mv[:, sl]))
    y = _dot(jnp.concatenate(outs, axis=1).astype(BF16), wo_ref[...])
    o_ref[0] = h_ref[0] + y[:rows]


def _cross(q, mk, mv, h, w_xo, *, tq, name):
    b, t, xw = q.shape
    d = h.shape[2]
    mlen = mk.shape[1]
    return pl.pallas_call(
        _cross_kernel,
        grid=(b, t // tq),
        in_specs=[pl.BlockSpec((1, tq, xw), lambda bb, i: (bb, i, 0)),
                  pl.BlockSpec((1, mlen, xw), lambda bb, i: (bb, 0, 0)),
                  pl.BlockSpec((1, mlen, xw), lambda bb, i: (bb, 0, 0)),
                  pl.BlockSpec((1, tq, d), lambda bb, i: (bb, i, 0)),
                  pl.BlockSpec((xw, d), lambda bb, i: (0, 0))],
        out_specs=pl.BlockSpec((1, tq, d), lambda bb, i: (bb, i, 0)),
        out_shape=jax.ShapeDtypeStruct((b, t, d), F32),
        compiler_params=_params("arbitrary", "arbitrary"),
        name=name,
    )(q, mk, mv, h, w_xo)


def _top_rows(s, k, payload=None):
    r = s.shape[0]
    rid = lax.broadcasted_iota(I32, s.shape, 0).astype(F32)
    vals, picks = [], []
    for _ in range(k):
        m = jnp.max(s, axis=0, keepdims=True)
        am = jnp.min(jnp.where(s == m, rid, float(r)), axis=0, keepdims=True)
        hit = rid == am
        vals.append(m)
        picks.append(am if payload is None else jnp.sum(jnp.where(hit, payload, 0.0), axis=0, keepdims=True))
        s = jnp.where(hit, -jnp.inf, s)
    return jnp.concatenate(vals, axis=0), jnp.concatenate(picks, axis=0)


def _peer_route_kernel(h_ref, g_ref, w_ref, sk_ref, ids_ref, gate_ref, xn_ref):
    @pl.when(pl.program_id(1) == 0)
    def _():
        x = h_ref[...]
        xn_ref[...] = (x * lax.rsqrt(jnp.mean(x * x, axis=-1, keepdims=True) + EPS) * g_ref[...]).astype(BF16)

    half = sk_ref.shape[2]
    qt = _dot_nt(w_ref[...], xn_ref[...]).astype(BF16)
    v0, i0 = _top_rows(_dot(sk_ref[0], qt[:half]), PEER_TOPK)
    v1, i1 = _top_rows(_dot(sk_ref[1], qt[half:]), PEER_TOPK)
    widths = [PEER_TOPK // (a + 1) for a in range(PEER_TOPK)]
    pad = -sum(widths) % SUBLANES
    cand = jnp.concatenate([v0[a:a + 1] + v1[:n] for a, n in enumerate(widths)]
                           + [jnp.full((pad, v0.shape[1]), -jnp.inf, F32)], axis=0)
    eid = jnp.concatenate([i0[a:a + 1] * float(N_KEYS) + i1[:n] for a, n in enumerate(widths)]
                          + [jnp.zeros((pad, v0.shape[1]), F32)], axis=0)
    sc, e = _top_rows(cand, PEER_TOPK, payload=eid)
    ex = jnp.exp(sc - sc[0:1])
    gate_ref[...] = ex / jnp.sum(ex, axis=0, keepdims=True)
    ids_ref[...] = e.astype(I32)


def _peer_route(h, g_ffn, w_pq_t, sub_keys, *, tb):
    m, d = h.shape
    qd = w_pq_t.shape[0] // PEER_HEADS
    tb = min(tb, m)
    return pl.pallas_call(
        _peer_route_kernel,
        grid=(m // tb, PEER_HEADS),
        in_specs=[pl.BlockSpec((tb, d), lambda t, hh: (t, 0)),
                  pl.BlockSpec((1, d), lambda t, hh: (0, 0)),
                  pl.BlockSpec((qd, d), lambda t, hh: (hh, 0)),
                  pl.BlockSpec(sub_keys.shape, lambda t, hh: (0, 0, 0))],
        out_specs=[pl.BlockSpec((PEER_TOPK, tb), lambda t, hh: (hh, t)),
                   pl.BlockSpec((PEER_TOPK, tb), lambda t, hh: (hh, t))],
        out_shape=[jax.ShapeDtypeStruct((PEER_HEADS * PEER_TOPK, m), I32),
                   jax.ShapeDtypeStruct((PEER_HEADS * PEER_TOPK, m), F32)],
        scratch_shapes=[pltpu.VMEM((tb, d), BF16)],
        compiler_params=_params("arbitrary", "arbitrary"),
        name="peer_route",
    )(h, g_ffn.reshape(1, d), w_pq_t, sub_keys)


def _pack_tables_kernel(u_ref, v_ref, o_ref):
    nr = o_ref.shape[1] // 2

    def words(x):
        bits = pltpu.bitcast(x.astype(BF16).astype(F32), I32)
        half = bits.shape[1] // 2
        return (bits[:, half:] & jnp.int32(-65536)) | lax.shift_right_logical(bits[:, :half], 16)

    wu = words(u_ref[...])
    wv = words(v_ref[...])
    for r in range(nr):
        o_ref[:, r, :] = wu[:, r * LANES:(r + 1) * LANES]
        o_ref[:, nr + r, :] = wv[:, r * LANES:(r + 1) * LANES]


def _pack_tables(u_tab, v_tab, *, eb):
    n, d = u_tab.shape
    nr = d // (2 * LANES)
    return pl.pallas_call(
        _pack_tables_kernel,
        grid=(n // eb,),
        in_specs=[pl.BlockSpec((eb, d), lambda i: (i, 0)), pl.BlockSpec((eb, d), lambda i: (i, 0))],
        out_specs=pl.BlockSpec((eb, 2 * nr, LANES), lambda i: (i, 0, 0)),
        out_shape=jax.ShapeDtypeStruct((n, 2 * nr, LANES), I32),
        compiler_params=_params("arbitrary"),
        name="peer_pack_tables",
    )(u_tab, v_tab)


def _unpack_words(w):
    return pltpu.bitcast(jnp.left_shift(w, 16), F32), pltpu.bitcast(w & jnp.int32(-65536), F32)


def _peer_gather_kernel(ids_hbm, gate_ref, h_ref, gffn_ref, gfin_ref, tab_hbm, o_ref,
                        ids_s, buf, part_s, sem_ids, sem_rows, *, n_steps):
    s = pl.program_id(0)
    g_tok = h_ref.shape[0]
    nr = buf.shape[1] // 2
    n_e = buf.shape[2]
    n_buf = buf.shape[0]
    ahead = n_buf - 1
    assert g_tok % n_buf == 0 and ahead <= g_tok
    cur = lax.rem(s, 2)
    nxt = 1 - cur

    def ids_copy(step, slot):
        return pltpu.make_async_copy(ids_hbm.at[step], ids_s.at[slot], sem_ids.at[slot])

    def row_copy(e, k, slot):
        return pltpu.make_async_copy(tab_hbm.at[e], buf.at[slot, :, k, :], sem_rows.at[slot])

    def issue(ids_slot, t, slot):
        for k in range(n_e):
            row_copy(ids_s[ids_slot, t, k], k, slot).start(priority=k % 2)

    def wait(slot):
        for k in range(n_e):
            row_copy(0, k, slot).wait()

    @pl.when(s == 0)
    def _():
        ids_copy(0, 0).start()
        ids_copy(0, 0).wait()
        for t in range(ahead):
            issue(0, t, t % n_buf)

    @pl.when(s + 1 < n_steps)
    def _():
        ids_copy(s + 1, nxt).start()

    h = h_ref[...]
    xn = h * lax.rsqrt(jnp.mean(h * h, axis=-1, keepdims=True) + EPS) * gffn_ref[...]
    eye = lax.broadcasted_iota(I32, (n_e, n_e), 0) == lax.broadcasted_iota(I32, (n_e, n_e), 1)

    for t in range(g_tok):
        slot = t % n_buf
        ta = t + ahead
        if ta < g_tok:
            issue(cur, ta, ta % n_buf)
        else:
            @pl.when(s + 1 < n_steps)
            def _():
                if ta == g_tok:
                    ids_copy(s + 1, nxt).wait()
                issue(nxt, ta - g_tok, ta % n_buf)
        wait(slot)
        x = xn[t:t + 1, :]
        hacc = jnp.zeros((n_e, LANES), F32)
        for r in range(nr):
            lo, hi = _unpack_words(buf[slot, r])
            hacc = hacc + lo * x[:, r * LANES:(r + 1) * LANES] + hi * x[:, (nr + r) * LANES:(nr + r + 1) * LANES]
        act = jax.nn.gelu(jnp.sum(hacc, axis=-1, keepdims=True))
        gcol = jnp.sum(jnp.where(eye, jnp.broadcast_to(gate_ref[t:t + 1, :], (n_e, n_e)), 0.0),
                       axis=-1, keepdims=True)
        w = jnp.broadcast_to(gcol * act, (n_e, LANES))
        for r in range(nr):
            lo, hi = _unpack_words(buf[slot, nr + r])
            part_s[t, :, r * LANES:(r + 1) * LANES] = jnp.sum(
                (lo * w).reshape(n_e // SUBLANES, SUBLANES, LANES), axis=0)
            part_s[t, :, (nr + r) * LANES:(nr + r + 1) * LANES] = jnp.sum(
                (hi * w).reshape(n_e // SUBLANES, SUBLANES, LANES), axis=0)

    y = h + jnp.sum(part_s[...], axis=1)
    o_ref[...] = y * lax.rsqrt(jnp.mean(y * y, axis=-1, keepdims=True) + EPS) * gfin_ref[...]


PEER_GATHER_BUFFERS = 4


def _peer_gather(ids, gates, h, g_ffn, g_final, packed_tab, *, g_tok):
    m, d = h.shape
    n_e = ids.shape[1]
    n_steps = m // g_tok
    n_rows = packed_tab.shape[1]
    ids3 = ids.reshape(n_steps, g_tok, n_e)
    return pl.pallas_call(
        functools.partial(_peer_gather_kernel, n_steps=n_steps),
        grid=(n_steps,),
        in_specs=[pl.BlockSpec(memory_space=pl.ANY),
                  pl.BlockSpec((g_tok, n_e), lambda s: (s, 0)),
                  pl.BlockSpec((g_tok, d), lambda s: (s, 0)),
                  pl.BlockSpec((1, d), lambda s: (0, 0)),
                  pl.BlockSpec((1, d), lambda s: (0, 0)),
                  pl.BlockSpec(memory_space=pl.ANY)],
        out_specs=pl.BlockSpec((g_tok, d), lambda s: (s, 0)),
        out_shape=jax.ShapeDtypeStruct((m, d), F32),
        scratch_shapes=[pltpu.SMEM((2, g_tok, n_e), I32),
                        pltpu.VMEM((PEER_GATHER_BUFFERS, n_rows, n_e, LANES), I32),
                        pltpu.VMEM((g_tok, SUBLANES, d), F32),
                        pltpu.SemaphoreType.DMA((2,)), pltpu.SemaphoreType.DMA((PEER_GATHER_BUFFERS,))],
        compiler_params=_params("arbitrary"),
        name="peer_gather",
    )(ids3, gates, h, g_ffn.reshape(1, d), g_final.reshape(1, d), packed_tab)


def kernel(x_prompt, x_sample, cache_k, cache_v, cache_kidx, cache_mem_k, cache_mem_v, page_table,
           mem_prompt, rel_bias, g_in, w_in, g_sgu, w_s, b_s, w_out, g_x, w_xq, w_xk, w_xv, w_xo,
           g_ffn, w_pq, sub_keys, peer_u, peer_v, g_final):
    batch, seq, d = x_prompt.shape
    db, dseq, _ = x_sample.shape
    depth = w_in.shape[0]
    assert dseq == 1, "sample group is one new token per sequence"
    n_pages = page_table.shape[1]
    page = cache_k.shape[2]
    past = n_pages * page
    aw, kvw, qiw, gw = N_HEADS * HEAD_DIM, N_KV_HEADS * HEAD_DIM, IDX_HEADS * IDX_DIM, GMLP_GROUPS * LANES
    xw = w_xq.shape[2]
    mlen = mem_prompt.shape[1]
    small_lo = aw + 2 * kvw + qiw
    small_hi = small_lo + IDX_DIM + IDX_HEADS
    assert NUM_BUCKETS // 2 + int(math.log((LANES + 1) / (NUM_BUCKETS // 2)) / math.log(
        MAX_DISTANCE / (NUM_BUCKETS // 2)) * (NUM_BUCKETS // 2)) >= NUM_BUCKETS - 1

    hp = x_prompt.reshape(batch * seq, d)
    hs = x_sample.reshape(db, d)
    btiles = _bias_tiles(rel_bias)
    rb_t = rel_bias.T
    outs = {n: [] for n in ("kp", "vp", "ip", "mkp", "mvp", "ks", "vs", "is", "sv")}

    for l in range(depth):
        w_in_t = jnp.swapaxes(w_in[l], 0, 1).astype(BF16)
        w_small_t = jnp.pad(w_in_t[small_lo:small_hi], ((0, LANES - (small_hi - small_lo)), (0, 0)))
        w_out_b = w_out[l].astype(BF16)
        w_xq_b, w_xk_b, w_xv_b, w_xo_b = (w.astype(BF16) for w in (w_xq[l], w_xk[l], w_xv[l], w_xo[l]))
        w_pq_t = w_pq[l].T.astype(BF16)
        sk_b = sub_keys[l].astype(BF16)
        ws_first = jnp.repeat(w_s[l][:, 0, 0], LANES).reshape(1, gw)
        bs_first = jnp.repeat(b_s[l][:, 0], LANES).reshape(1, gw)
        ones_d = jnp.ones((d,), F32)

        z, zs = _proj_wt(hp, g_in[l], w_in_t, w_small_t, skip_lo=small_lo, skip_hi=small_hi, tm=512, tn=1024,
                         name="in_proj_prompt")
        att = _dsa_prompt(z.reshape(batch, seq, -1), zs.reshape(batch, seq, LANES), btiles,
                          batch=batch, seq=seq, topk=min(TOPK_MAX, seq // 4))
        m_p, _ = _gate(att.reshape(batch * seq, aw), z, g_sgu[l], w_s[l], b_s[l].T,
                       first_rows_only=False, tm=256, name="gate_prompt")
        hp1 = _matmul_residual(m_p, w_out_b, hp, tm=1024, tn=1024, name="out_proj_prompt")
        mem = mem_prompt.reshape(batch * mlen, d)
        mk = _proj(mem, ones_d, w_xk_b, norm=False, tm=256, tn=xw, name="mem_k_proj")
        mv = _proj(mem, ones_d, w_xv_b, norm=False, tm=256, tn=xw, name="mem_v_proj")
        qx = _proj(hp1, g_x[l], w_xq_b, norm=True, tm=512, tn=xw, name="xq_proj_prompt")
        hp2 = _cross(qx.reshape(batch, seq, xw), mk.reshape(batch, mlen, xw), mv.reshape(batch, mlen, xw),
                     hp1.reshape(batch, seq, d), w_xo_b, tq=256, name="cross_prompt").reshape(batch * seq, d)
        outs["kp"].append(z[:, aw:aw + kvw].reshape(batch, seq, N_KV_HEADS, HEAD_DIM))
        outs["vp"].append(z[:, aw + kvw:aw + 2 * kvw].reshape(batch, seq, N_KV_HEADS, HEAD_DIM))
        outs["ip"].append(zs[:, :IDX_DIM].reshape(batch, seq, IDX_DIM))
        outs["mkp"].append(mk.reshape(batch, mlen, X_HEADS, xw // X_HEADS))
        outs["mvp"].append(mv.reshape(batch, mlen, X_HEADS, xw // X_HEADS))

        z_s, zs_s = _proj_wt(hs, g_in[l], w_in_t, w_small_t, skip_lo=small_lo, skip_hi=small_hi, tm=db, tn=512,
                             name="in_proj_sample")
        qi3 = z_s[:, aw + 2 * kvw:aw + 2 * kvw + qiw].reshape(db, IDX_HEADS, IDX_DIM)
        wi3 = zs_s[:, IDX_DIM:IDX_DIM + IDX_HEADS].reshape(db, IDX_HEADS, 1)
        ki_new = zs_s[:, :IDX_DIM].reshape(db, 1, IDX_DIM)
        k_new = z_s[:, aw:aw + kvw].reshape(db, 1, kvw)
        v_new = z_s[:, aw + kvw:aw + 2 * kvw].reshape(db, 1, kvw)
        scores3, snew3 = _dsa_sample_scores(page_table, qi3, wi3, ki_new.reshape(db, IDX_DIM, 1),
                                            jnp.swapaxes(cache_kidx[l], 1, 2))
        pos3, new_sel = _dsa_sample_select(scores3.reshape(db, past), snew3.reshape(db, LANES),
                                           topk=min(TOPK_MAX, (past + dseq) // 4))
        att_s = _dsa_sample_attn(pos3[:, :, 0], page_table, z_s[:, :aw].reshape(db, N_HEADS, HEAD_DIM),
                                 k_new, v_new, new_sel, rb_t, cache_k, cache_v, layer=l)
        m_s, vn_s = _gate(att_s.reshape(db, aw), z_s, g_sgu[l], ws_first, bs_first,
                          first_rows_only=True, tm=db, name="gate_sample")
        hs1 = _matmul_residual(m_s, w_out_b, hs, tm=db, tn=1024, name="out_proj_sample")
        qx_s = _proj(hs1, g_x[l], w_xq_b, norm=True, tm=db, tn=xw, name="xq_proj_sample")
        hs2 = _cross(qx_s.reshape(db, 1, xw), cache_mem_k[l].reshape(db, mlen, xw),
                     cache_mem_v[l].reshape(db, mlen, xw), hs1.reshape(db, 1, d), w_xo_b,
                     tq=1, name="cross_sample").reshape(db, d)
        outs["ks"].append(k_new.reshape(db, dseq, N_KV_HEADS, HEAD_DIM))
        outs["vs"].append(v_new.reshape(db, dseq, N_KV_HEADS, HEAD_DIM))
        outs["is"].append(ki_new.reshape(db, dseq, IDX_DIM))
        outs["sv"].append(vn_s.reshape(db, dseq, gw))

        assert depth == 1
        packed = _pack_tables(peer_u[l], peer_v[l], eb=256)
        ids_p, gates_p = _peer_route(hp2, g_ffn[l], w_pq_t, sk_b, tb=512)
        y_p = _peer_gather(ids_p.T, gates_p.T, hp2, g_ffn[l], g_final, packed, g_tok=8)
        hs2_pad = jnp.pad(hs2, ((0, -db % LANES), (0, 0)))
        ids_s, gates_s = _peer_route(hs2_pad, g_ffn[l], w_pq_t, sk_b, tb=LANES)
        y_s = _peer_gather(ids_s.T[:db], gates_s.T[:db], hs2, g_ffn[l], g_final, packed, g_tok=8)

    y_prompt = y_p.reshape(batch, seq, d)
    y_sample = y_s.reshape(db, dseq, d)
    st = lambda n: jnp.stack(outs[n])
    return (y_prompt, y_sample, st("kp"), st("vp"), st("ip"), st("mkp"), st("mvp"),
            st("ks"), st("vs"), st("is"), st("sv"))
```

```python
import functools
import math

import jax
import jax.numpy as jnp
from jax import lax
from jax.experimental import pallas as pl
from jax.experimental.pallas import tpu as pltpu

F32 = jnp.float32
BF16 = jnp.bfloat16
I32 = jnp.int32

N_HEADS = 16
HEAD_DIM = 128
N_KV_HEADS = 4
GQA = N_HEADS // N_KV_HEADS
IDX_HEADS = 16
IDX_DIM = 64
TOPK_MAX = 256
NUM_BUCKETS = 32
MAX_DISTANCE = 128
CHUNK = 128
GMLP_GROUPS = 16
X_HEADS = 4
N_KEYS = 128
PEER_HEADS = 8
PEER_TOPK = 16
EPS = 1e-6

LANES = 128
SUBLANES = 8
VMEM_LIMIT_BYTES = 56 * 1024 * 1024

NEG = -1e30
INT_MIN = -(2 ** 31)


def _dot_nt(a, b):
    return lax.dot_general(a, b, (((1,), (1,)), ((), ())), preferred_element_type=F32)


def _dot(a, b):
    return jnp.dot(a, b, preferred_element_type=F32)


def _params(*sem):
    return pltpu.CompilerParams(dimension_semantics=sem, vmem_limit_bytes=VMEM_LIMIT_BYTES)


def _sortable(x):
    bits = pltpu.bitcast(x, I32)
    return bits ^ (jnp.right_shift(bits, 31) & 0x7FFFFFFF)


def _t5_bucket(dist):
    n = jnp.maximum(dist, 0)
    max_exact = NUM_BUCKETS // 2
    nf = jnp.maximum(n, 1).astype(F32)
    large = max_exact + (jnp.log(nf / max_exact) / math.log(MAX_DISTANCE / max_exact)
                         * (NUM_BUCKETS - max_exact)).astype(I32)
    large = jnp.minimum(large, NUM_BUCKETS - 1)
    return jnp.where(n < max_exact, n, large)


def _kth_largest_key(count_ge, shape, k):
    def body(it, t):
        cand = t | jnp.left_shift(jnp.int32(1), 31 - it)
        cnt = count_ge(cand ^ INT_MIN)
        return jnp.where(cnt >= k, cand, t)

    t = lax.fori_loop(0, 32, body, jnp.zeros(shape, I32))
    return t ^ INT_MIN


def _prefix_matrix():
    return (lax.broadcasted_iota(I32, (LANES, LANES), 0) <= lax.broadcasted_iota(I32, (LANES, LANES), 1)).astype(BF16)


def _proj_kernel(x_ref, g_ref, w_ref, o_ref, xn_ref, *, norm):
    @pl.when(pl.program_id(1) == 0)
    def _():
        x = x_ref[...]
        if norm:
            x = x * lax.rsqrt(jnp.mean(x * x, axis=-1, keepdims=True) + EPS) * g_ref[...]
        xn_ref[...] = x.astype(BF16)

    o_ref[...] = _dot(xn_ref[...], w_ref[...])


def _proj(x, g, w, *, norm, tm, tn, name):
    m, k = x.shape
    n = w.shape[1]
    tm, tn = min(tm, m), min(tn, n)
    return pl.pallas_call(
        functools.partial(_proj_kernel, norm=norm),
        grid=(m // tm, n // tn),
        in_specs=[pl.BlockSpec((tm, k), lambda i, j: (i, 0)),
                  pl.BlockSpec((1, k), lambda i, j: (0, 0)),
                  pl.BlockSpec((k, tn), lambda i, j: (0, j))],
        out_specs=pl.BlockSpec((tm, tn), lambda i, j: (i, j)),
        out_shape=jax.ShapeDtypeStruct((m, n), F32),
        scratch_shapes=[pltpu.VMEM((tm, k), BF16)],
        compiler_params=_params("arbitrary", "arbitrary"),
        name=name,
    )(x, g.reshape(1, k), w)


IN_PROJ_WEIGHT_BUFFERS = 3


def _proj_wt_kernel(x_ref, g_ref, wt_hbm, w2t_ref, o_ref, o2_ref, xn_ref, wbuf, sem, *, tn, skip_lo, gap):
    nj = pl.num_programs(1)
    total = pl.num_programs(0) * nj
    s = pl.program_id(0) * nj + pl.program_id(1)
    n_buf = wbuf.shape[0]

    def w_copy(step, slot):
        start = lax.rem(step, nj) * tn
        row = pl.multiple_of(jnp.where(start < skip_lo, start, start + gap), math.gcd(tn, gap))
        return pltpu.make_async_copy(wt_hbm.at[pl.ds(row, tn)], wbuf.at[slot], sem.at[slot])

    @pl.when(s == 0)
    def _():
        for a in range(n_buf - 1):
            w_copy(a, a).start()

    @pl.when(s + n_buf - 1 < total)
    def _():
        w_copy(s + n_buf - 1, lax.rem(s + n_buf - 1, n_buf)).start()

    @pl.when(pl.program_id(1) == 0)
    def _():
        x = x_ref[...]
        xn_ref[...] = (x * lax.rsqrt(jnp.mean(x * x, axis=-1, keepdims=True) + EPS) * g_ref[...]).astype(BF16)
        o2_ref[...] = _dot_nt(xn_ref[...], w2t_ref[...])

    slot = lax.rem(s, n_buf)
    w_copy(s, slot).wait()
    o_ref[...] = _dot_nt(xn_ref[...], wbuf[slot])


def _proj_wt(x, g, w_t, w_side_t, *, skip_lo, skip_hi, tm, tn, name):
    m, k = x.shape
    gap = skip_hi - skip_lo
    n = w_t.shape[0] - gap
    n2 = w_side_t.shape[0]
    tm, tn = min(tm, m), min(tn, n)
    assert skip_lo % tn == 0 and n % tn == 0
    assert (m // tm) * (n // tn) >= IN_PROJ_WEIGHT_BUFFERS - 1
    return pl.pallas_call(
        functools.partial(_proj_wt_kernel, tn=tn, skip_lo=skip_lo, gap=gap),
        grid=(m // tm, n // tn),
        in_specs=[pl.BlockSpec((tm, k), lambda i, j: (i, 0)),
                  pl.BlockSpec((1, k), lambda i, j: (0, 0)),
                  pl.BlockSpec(memory_space=pl.ANY),
                  pl.BlockSpec((n2, k), lambda i, j: (0, 0))],
        out_specs=[pl.BlockSpec((tm, tn), lambda i, j: (i, j)), pl.BlockSpec((tm, n2), lambda i, j: (i, 0))],
        out_shape=[jax.ShapeDtypeStruct((m, n), F32), jax.ShapeDtypeStruct((m, n2), F32)],
        scratch_shapes=[pltpu.VMEM((tm, k), BF16), pltpu.VMEM((IN_PROJ_WEIGHT_BUFFERS, tn, k), BF16),
                        pltpu.SemaphoreType.DMA((IN_PROJ_WEIGHT_BUFFERS,))],
        compiler_params=_params("arbitrary", "arbitrary"),
        name=name,
    )(x, g.reshape(1, k), w_t, w_side_t)


def _bias_tiles_kernel(rb_ref, o_ref):
    tq = lax.broadcasted_iota(I32, (LANES, LANES), 0)
    c = lax.broadcasted_iota(I32, (LANES, LANES), 1)
    for part, off in enumerate((2 * LANES, LANES, 0)):
        dist = tq - c + off
        bucket = _t5_bucket(dist)
        for h in range(N_HEADS):
            tile = lax.fori_loop(
                0, NUM_BUCKETS, lambda b, acc: jnp.where(bucket == b, rb_ref[b, h], acc),
                jnp.zeros((LANES, LANES), F32))
            if off == 0:
                tile = jnp.where(dist < 0, NEG, tile)
            o_ref[part, h] = tile


def _bias_tiles(rel_bias):
    return pl.pallas_call(
        _bias_tiles_kernel,
        in_specs=[pl.BlockSpec(memory_space=pltpu.SMEM)],
        out_specs=pl.BlockSpec(memory_space=pltpu.VMEM),
        out_shape=jax.ShapeDtypeStruct((3, N_HEADS, LANES, LANES), F32),
        name="t5_bias_tiles",
    )(rel_bias)


def _dsa_prompt_kernel(q_ref, k_ref, v_ref, qi_ref, zsk_ref, zsq_ref, bt_ref, tri_ref, o_ref,
                       kbf, vbf, kilo, kihi, key3, qs, m_s, l_s, acc_s, *, topk):
    i = pl.program_id(1)
    tq = q_ref.shape[1]
    s = k_ref.shape[1]
    nkb = s // LANES
    cw = 2 * LANES

    @pl.when(i == 0)
    def _():
        kbf[...] = k_ref[0].astype(BF16)
        vbf[...] = v_ref[0].astype(BF16)
        zs = zsk_ref[0]
        lane = lax.broadcasted_iota(I32, zs.shape, 1)
        kilo[...] = jnp.where(lane < IDX_DIM, zs, 0.0).astype(BF16)
        kihi[...] = jnp.where(lane >= IDX_DIM, pltpu.roll(zs, IDX_DIM, axis=1), 0.0).astype(BF16)

    qi = qi_ref[0].astype(BF16)
    coef = zsq_ref[0][:, IDX_DIM:IDX_DIM + IDX_HEADS] * (IDX_HEADS ** -0.5 * IDX_DIM ** -0.5)
    t_pos = i * tq + lax.broadcasted_iota(I32, (tq, cw), 0)
    t_last = i * tq + tq - 1
    for c in range(s // cw):
        @pl.when(c * cw <= t_last)
        def _():
            klo = kilo[c * cw:(c + 1) * cw, :]
            khi = kihi[c * cw:(c + 1) * cw, :]
            acc = jnp.zeros((tq, cw), F32)
            for p in range(IDX_HEADS // 2):
                a = qi[:, p * LANES:(p + 1) * LANES]
                acc = acc + coef[:, 2 * p:2 * p + 1] * jnp.maximum(_dot_nt(a, klo), 0.0)
                acc = acc + coef[:, 2 * p + 1:2 * p + 2] * jnp.maximum(_dot_nt(a, khi), 0.0)
            s_pos = c * cw + lax.broadcasted_iota(I32, (tq, cw), 1)
            key = _sortable(jnp.where(s_pos <= t_pos, acc, -jnp.inf))
            for u in range(cw // LANES):
                key3[c * (cw // LANES) + u] = key[:, u * LANES:(u + 1) * LANES]

        @pl.when(c * cw > t_last)
        def _():
            for u in range(cw // LANES):
                key3[c * (cw // LANES) + u] = jnp.full((tq, LANES), INT_MIN, I32)

    def count(pred):
        return jnp.sum(jnp.sum(jnp.where(pred, 1.0, 0.0), axis=0), axis=1, keepdims=True)

    thr = _kth_largest_key(lambda cs: count(key3[...] >= cs[None]), (tq, 1), float(topk))

    n_ge = count(key3[...] >= thr[None])

    @pl.when(jnp.max(n_ge) > float(topk))
    def _():
        need = float(topk) - count(key3[...] > thr[None])
        before = jnp.zeros((tq, 1), F32)
        for j in range(nkb):
            kj = key3[j]
            eq = kj == thr
            rank = _dot(jnp.where(eq, 1.0, 0.0).astype(BF16), tri_ref[...]) + before
            key3[j] = jnp.where(jnp.logical_and(eq, rank > need), INT_MIN, kj)
            before = rank[:, LANES - 1:LANES]

    thr4 = jnp.concatenate([thr] * GQA, axis=0)

    qb = q_ref[0].astype(BF16)
    for kh in range(N_KV_HEADS):
        qs[kh] = jnp.concatenate(
            [qb[:, (GQA * kh + g) * HEAD_DIM:(GQA * kh + g + 1) * HEAD_DIM] for g in range(GQA)], axis=0)
    m_s[...] = jnp.full(m_s.shape, NEG, F32)
    l_s[...] = jnp.zeros(l_s.shape, F32)
    acc_s[...] = jnp.zeros(acc_s.shape, F32)
    scale = HEAD_DIM ** -0.5

    def fbody(jj, carry):
        ja = 2 * jj
        jb = ja + 1
        part_a = jnp.clip(ja - i + 2, 0, 2)
        part_b = jnp.clip(jb - i + 2, 0, 2)
        key_b = jnp.where(jb <= i, key3[jb], INT_MIN)
        key8 = jnp.concatenate([key3[ja], key_b], axis=1)
        mask = jnp.concatenate([key8] * GQA, axis=0) >= thr4
        r0 = pl.multiple_of(ja * LANES, 2 * LANES)
        for kh in range(N_KV_HEADS):
            kj = kbf[pl.ds(r0, 2 * LANES), kh * HEAD_DIM:(kh + 1) * HEAD_DIM]
            vj = vbf[pl.ds(r0, 2 * LANES), kh * HEAD_DIM:(kh + 1) * HEAD_DIM]
            bias = jnp.concatenate(
                [bt_ref[part_a, GQA * kh:GQA * (kh + 1)].reshape(GQA * tq, LANES),
                 bt_ref[part_b, GQA * kh:GQA * (kh + 1)].reshape(GQA * tq, LANES)], axis=1)
            lg = jnp.where(mask, _dot_nt(qs[kh], kj) * scale + bias, NEG)
            m_old = m_s[kh]
            m_new = jnp.maximum(m_old, jnp.max(lg, axis=1, keepdims=True))
            alpha = jnp.exp(m_old - m_new)
            p = jnp.where(mask, jnp.exp(lg - jnp.concatenate([m_new, m_new], axis=1)), 0.0)
            l_s[kh] = alpha * l_s[kh] + jnp.sum(p, axis=1, keepdims=True)
            acc_s[kh] = alpha * acc_s[kh] + _dot(p.astype(BF16), vj)
            m_s[kh] = m_new
        return carry

    assert nkb % 2 == 0
    lax.fori_loop(0, jnp.right_shift(i + 2, 1), fbody, 0)
    for kh in range(N_KV_HEADS):
        out = acc_s[kh] / l_s[kh]
        for g in range(GQA):
            h = GQA * kh + g
            o_ref[0, :, h * HEAD_DIM:(h + 1) * HEAD_DIM] = out[g * tq:(g + 1) * tq]


def _dsa_prompt(z, zs, btiles, *, batch, seq, topk):
    tq = LANES
    aw = N_HEADS * HEAD_DIM
    kvw = N_KV_HEADS * HEAD_DIM
    qiw = IDX_HEADS * IDX_DIM
    return pl.pallas_call(
        functools.partial(_dsa_prompt_kernel, topk=topk),
        grid=(batch, seq // tq),
        in_specs=[
            pl.BlockSpec((1, tq, aw), lambda b, i: (b, i, 0)),
            pl.BlockSpec((1, seq, kvw), lambda b, i: (b, 0, aw // kvw)),
            pl.BlockSpec((1, seq, kvw), lambda b, i: (b, 0, aw // kvw + 1)),
            pl.BlockSpec((1, tq, qiw), lambda b, i: (b, i, (aw + 2 * kvw) // qiw)),
            pl.BlockSpec((1, seq, LANES), lambda b, i: (b, 0, 0)),
            pl.BlockSpec((1, tq, LANES), lambda b, i: (b, i, 0)),
            pl.BlockSpec((3, N_HEADS, LANES, LANES), lambda b, i: (0, 0, 0, 0)),
            pl.BlockSpec((LANES, LANES), lambda b, i: (0, 0)),
        ],
        out_specs=pl.BlockSpec((1, tq, aw), lambda b, i: (b, i, 0)),
        out_shape=jax.ShapeDtypeStruct((batch, seq, aw), F32),
        scratch_shapes=[
            pltpu.VMEM((seq, kvw), BF16), pltpu.VMEM((seq, kvw), BF16),
            pltpu.VMEM((seq, LANES), BF16), pltpu.VMEM((seq, LANES), BF16),
            pltpu.VMEM((seq // LANES, tq, LANES), I32),
            pltpu.VMEM((N_KV_HEADS, GQA * tq, HEAD_DIM), BF16),
            pltpu.VMEM((N_KV_HEADS, GQA * tq, LANES), F32),
            pltpu.VMEM((N_KV_HEADS, GQA * tq, LANES), F32),
            pltpu.VMEM((N_KV_HEADS, GQA * tq, HEAD_DIM), F32),
        ],
        compiler_params=_params("arbitrary", "arbitrary"),
        name="dsa_prompt",
    )(z, z, z, z, zs, zs, btiles, _prefix_matrix())


SAMPLE_PAGES_PER_STEP = 32


def _dsa_sample_score_kernel(pt_ref, qi_ref, wi_ref, kinew_ref, *rest):
    kidx_refs, (o_ref, onew_ref) = rest[:-2], rest[-2:]
    qi = qi_ref[0].astype(BF16)
    coef = wi_ref[0] * (IDX_HEADS ** -0.5 * IDX_DIM ** -0.5)

    def score(ki_t):
        return jnp.sum(jnp.maximum(_dot(qi, ki_t.astype(BF16)), 0.0) * coef, axis=0, keepdims=True)

    o_ref[0] = jnp.concatenate([score(r[0]) for r in kidx_refs], axis=1)

    @pl.when(pl.program_id(1) == pl.num_programs(1) - 1)
    def _():
        onew_ref[0] = score(jnp.broadcast_to(kinew_ref[0], (kinew_ref.shape[1], LANES)))


def _dsa_sample_scores(page_table, qi3, wi3, ki_new_t, cache_kidx_t):
    db, n_pages = page_table.shape
    page = cache_kidx_t.shape[2]
    pg = math.gcd(SAMPLE_PAGES_PER_STEP, n_pages)
    row = lambda b, j, pt: (b, 0, 0)
    grid_spec = pltpu.PrefetchScalarGridSpec(
        num_scalar_prefetch=1,
        grid=(db, n_pages // pg),
        in_specs=[pl.BlockSpec((1, IDX_HEADS, IDX_DIM), row),
                  pl.BlockSpec((1, IDX_HEADS, 1), row),
                  pl.BlockSpec((1, IDX_DIM, 1), row)]
        + [pl.BlockSpec((1, IDX_DIM, page), functools.partial(lambda b, j, pt, u: (pt[b, j * pg + u], 0, 0), u=u))
           for u in range(pg)],
        out_specs=[pl.BlockSpec((1, 1, pg * page), lambda b, j, pt: (b, 0, j)),
                   pl.BlockSpec((1, 1, LANES), row)],
    )
    return pl.pallas_call(
        _dsa_sample_score_kernel,
        grid_spec=grid_spec,
        out_shape=[jax.ShapeDtypeStruct((db, 1, n_pages * page), F32),
                   jax.ShapeDtypeStruct((db, 1, LANES), F32)],
        compiler_params=_params("arbitrary", "arbitrary"),
        name="dsa_sample_scores",
    )(page_table, qi3, wi3, ki_new_t, *([cache_kidx_t] * pg))


def _dsa_sample_select_kernel(s_ref, snew_ref, tri_ref, pos_ref, new_ref, rank_s, nsel_s, *, topk):
    db, past = s_ref.shape
    k = float(topk)
    key = _sortable(s_ref[...])
    key_new = _sortable(snew_ref[:, 0:1])

    def count_ge(cs):
        return (jnp.sum(jnp.where(key >= cs, 1.0, 0.0), axis=1, keepdims=True)
                + jnp.where(key_new >= cs, 1.0, 0.0))

    thr = _kth_largest_key(count_ge, (db, 1), k)
    gt = key > thr
    eq = key == thr
    need = k - (jnp.sum(jnp.where(gt, 1.0, 0.0), axis=1, keepdims=True) + jnp.where(key_new > thr, 1.0, 0.0))

    tri = tri_ref[...]

    def cumsum(flags):
        out, off = [], jnp.zeros((db, 1), F32)
        for blk in range(past // LANES):
            pc = _dot(flags[:, blk * LANES:(blk + 1) * LANES].astype(BF16), tri) + off
            out.append(pc)
            off = pc[:, LANES - 1:LANES]
        return jnp.concatenate(out, axis=1), off

    eq_rank, eq_total = cumsum(jnp.where(eq, 1.0, 0.0))
    sel = jnp.where(jnp.logical_or(gt, jnp.logical_and(eq, eq_rank <= need)), 1.0, 0.0)
    new_sel = jnp.logical_or(key_new > thr, jnp.logical_and(key_new == thr, eq_total < need))
    new_ref[...] = jnp.broadcast_to(jnp.where(new_sel, 1.0, 0.0), new_ref.shape)
    rank, n_sel = cumsum(sel)
    rank = jnp.where(sel > 0.0, rank, 0.0)
    for b in range(db):
        rank_s[b] = rank[b:b + 1, :]
        nsel_s[b] = jnp.broadcast_to(n_sel[b:b + 1, :], (1, LANES))
    ch = min(8 * LANES, past)
    slot = lax.broadcasted_iota(I32, (topk, ch), 0).astype(F32) + 1.0
    lane = lax.broadcasted_iota(I32, (topk, ch), 1).astype(F32)
    slot_col = lax.broadcasted_iota(I32, (topk, 1), 0).astype(F32)

    def extract(b, carry):
        r = rank_s[b]
        pos = jnp.zeros((topk, 1), F32)
        for c in range(past // ch):
            hit = r[:, c * ch:(c + 1) * ch] == slot
            pos = pos + jnp.sum(jnp.where(hit, lane + float(c * ch), 0.0), axis=1, keepdims=True)
        pos = jnp.where(slot_col < nsel_s[b][:, 0:1], pos, -1.0)
        pos_ref[b] = jnp.broadcast_to(pos, (topk, LANES)).astype(I32)
        return carry

    lax.fori_loop(0, db, extract, 0)


def _dsa_sample_select(scores, snew, *, topk):
    db = scores.shape[0]
    tri = _prefix_matrix()
    return pl.pallas_call(
        functools.partial(_dsa_sample_select_kernel, topk=topk),
        out_shape=[jax.ShapeDtypeStruct((db, topk, LANES), I32), jax.ShapeDtypeStruct((db, LANES), F32)],
        scratch_shapes=[pltpu.VMEM((db, 1, scores.shape[1]), F32), pltpu.VMEM((db, 1, LANES), F32)],
        compiler_params=pltpu.CompilerParams(vmem_limit_bytes=VMEM_LIMIT_BYTES),
        name="dsa_sample_select",
    )(scores, snew, tri)


def _dsa_sample_attn_kernel(pos_s, pt_s, q_ref, knew_ref, vnew_ref, posrow_ref, newsel_ref, rbt_ref,
                            ck_hbm, cv_hbm, o_ref, kbuf, vbuf, sem_k, sem_v, *, layer, past, page):
    b = pl.program_id(0)
    topk = kbuf.shape[1]
    slot = lax.rem(b, 2)
    page_shift = page.bit_length() - 1
    unroll = 8

    def row_copies(bb, j, sl):
        p = jnp.maximum(pos_s[bb * topk + j], 0)
        pg = pt_s[bb, jnp.right_shift(p, page_shift)]
        off = p & (page - 1)
        return (pltpu.make_async_copy(ck_hbm.at[layer, pg, off], kbuf.at[sl, j], sem_k.at[sl]),
                pltpu.make_async_copy(cv_hbm.at[layer, pg, off], vbuf.at[sl, j], sem_v.at[sl]))

    def issue(bb, sl):
        def body(i, carry):
            for u in range(unroll):
                ck, cv = row_copies(bb, i * unroll + u, sl)
                ck.start()
                cv.start()
            return carry
        lax.fori_loop(0, topk // unroll, body, 0)

    def wait(sl):
        def body(i, carry):
            for u in range(unroll):
                j = i * unroll + u
                pltpu.make_async_copy(ck_hbm.at[layer, 0, 0], kbuf.at[sl, j], sem_k.at[sl]).wait()
                pltpu.make_async_copy(cv_hbm.at[layer, 0, 0], vbuf.at[sl, j], sem_v.at[sl]).wait()
            return carry
        lax.fori_loop(0, topk // unroll, body, 0)

    @pl.when(b == 0)
    def _():
        issue(0, 0)

    @pl.when(b + 1 < pl.num_programs(0))
    def _():
        issue(b + 1, 1 - slot)

    wait(slot)

    q = q_ref[0].astype(BF16)
    posr = posrow_ref[0]
    valid = posr >= 0
    bucket = _t5_bucket(past - posr)
    bias = jnp.zeros((N_HEADS, topk), F32)
    for bk in range(NUM_BUCKETS):
        bias = jnp.where(bucket == bk, rbt_ref[:, bk:bk + 1], bias)
    shift = GQA.bit_length() - 1
    kv_of_row = jnp.right_shift(lax.broadcasted_iota(I32, (N_HEADS, topk), 0), shift)
    kv_of_row_d = jnp.right_shift(lax.broadcasted_iota(I32, (N_HEADS, HEAD_DIM), 0), shift)
    scale = HEAD_DIM ** -0.5
    qf = q.astype(F32)
    knew = knew_ref[0].astype(BF16).astype(F32)
    vnew = vnew_ref[0].astype(BF16).astype(F32)
    lg = jnp.zeros((N_HEADS, topk), F32)
    lg_new = jnp.zeros((N_HEADS, 1), F32)
    v_new_rows = jnp.zeros((N_HEADS, HEAD_DIM), F32)
    for kh in range(N_KV_HEADS):
        hs = slice(kh * HEAD_DIM, (kh + 1) * HEAD_DIM)
        lg = jnp.where(kv_of_row == kh, _dot_nt(q, kbuf[slot, :, kh, :].astype(BF16)), lg)
        lg_new = jnp.where(kv_of_row[:, 0:1] == kh, jnp.sum(qf * knew[:, hs], axis=1, keepdims=True), lg_new)
        v_new_rows = jnp.where(kv_of_row_d == kh, jnp.broadcast_to(vnew[:, hs], (N_HEADS, HEAD_DIM)), v_new_rows)
    lg = jnp.where(valid, lg * scale + bias, NEG)
    new_sel = newsel_ref[0][:, 0:1] > 0.5
    lg_new = jnp.where(new_sel, lg_new * scale + rbt_ref[:, 0:1], NEG)
    m = jnp.maximum(jnp.max(lg, axis=1, keepdims=True), lg_new)
    p = jnp.where(valid, jnp.exp(lg - m), 0.0)
    p_new = jnp.where(new_sel, jnp.exp(lg_new - m), 0.0)
    denom = jnp.sum(p, axis=1, keepdims=True) + p_new
    pb = (p / denom).astype(BF16)
    out = (p_new / denom).astype(BF16).astype(F32) * v_new_rows
    for kh in range(N_KV_HEADS):
        out = out + jnp.where(kv_of_row_d == kh, _dot(pb, vbuf[slot, :, kh, :].astype(BF16)), 0.0)
    o_ref[0] = out


def _dsa_sample_attn(pos, page_table, q3, k_new, v_new, new_sel, rb_t, cache_k, cache_v, *, layer):
    db, topk = pos.shape
    n_pages = page_table.shape[1]
    page, n_kv, hd = cache_k.shape[2:]
    assert page & (page - 1) == 0
    kvw = n_kv * hd
    row = lambda b, ps, pt: (b, 0, 0)
    grid_spec = pltpu.PrefetchScalarGridSpec(
        num_scalar_prefetch=2,
        grid=(db,),
        in_specs=[
            pl.BlockSpec((1, N_HEADS, HEAD_DIM), row),
            pl.BlockSpec((1, 1, kvw), row),
            pl.BlockSpec((1, 1, kvw), row),
            pl.BlockSpec((1, 1, topk), row),
            pl.BlockSpec((1, 1, LANES), row),
            pl.BlockSpec((N_HEADS, NUM_BUCKETS), lambda b, ps, pt: (0, 0)),
            pl.BlockSpec(memory_space=pl.ANY),
            pl.BlockSpec(memory_space=pl.ANY),
        ],
        out_specs=pl.BlockSpec((1, N_HEADS, HEAD_DIM), row),
        scratch_shapes=[pltpu.VMEM((2, topk, n_kv, hd), F32), pltpu.VMEM((2, topk, n_kv, hd), F32),
                        pltpu.SemaphoreType.DMA((2,)), pltpu.SemaphoreType.DMA((2,))],
    )
    return pl.pallas_call(
        functools.partial(_dsa_sample_attn_kernel, layer=layer, past=n_pages * page, page=page),
        grid_spec=grid_spec,
        out_shape=jax.ShapeDtypeStruct((db, N_HEADS, HEAD_DIM), F32),
        compiler_params=_params("arbitrary"),
        name="dsa_sample_attn",
    )(pos.reshape(-1), page_table, q3, k_new, v_new, pos.reshape(db, 1, topk),
      new_sel.reshape(db, 1, LANES), rb_t, cache_k, cache_v)


def _gate_kernel(att_ref, zu_ref, zv_ref, ga_ref, gb_ref, gs_ref, ws_ref, bs_ref, m_ref, vn_ref, *, first_rows_only):
    aw = att_ref.shape[1]
    tm = zu_ref.shape[0]
    v = jax.nn.gelu(zv_ref[...])
    vn = v * lax.rsqrt(jnp.mean(v * v, axis=-1, keepdims=True) + EPS) * gs_ref[...]
    vn_ref[...] = vn
    m_ref[:, :aw] = (jax.nn.sigmoid(ga_ref[...]) * att_ref[...]).astype(BF16)
    if first_rows_only:
        sgu = jax.nn.gelu(zu_ref[...]) * (vn * ws_ref[...] + bs_ref[...])
        m_ref[:, aw:] = (jax.nn.sigmoid(gb_ref[...]) * sgu).astype(BF16)
    else:
        row = lax.broadcasted_iota(I32, (CHUNK, CHUNK), 0)
        col = lax.broadcasted_iota(I32, (CHUNK, CHUNK), 1)
        for g in range(GMLP_GROUPS):
            cs = slice(g * LANES, (g + 1) * LANES)
            wt = jnp.where(col <= row, ws_ref[g], 0.0).astype(BF16)
            bcol = bs_ref[:, g:g + 1]
            for c in range(tm // CHUNK):
                rs = slice(c * CHUNK, (c + 1) * CHUNK)
                mixed = _dot(wt, vn_ref[rs, cs].astype(BF16)) + bcol
                sgu = jax.nn.gelu(zu_ref[rs, cs]) * mixed
                m_ref[rs, aw + g * LANES:aw + (g + 1) * LANES] = (
                    jax.nn.sigmoid(gb_ref[rs, cs]) * sgu).astype(BF16)


def _gate(att, z, g_sgu, ws, bs, *, first_rows_only, tm, name):
    m, aw = att.shape
    gw = g_sgu.shape[0]
    tm = min(tm, m)
    zb = lambda c: pl.BlockSpec((tm, gw), lambda i: (i, c))
    full = lambda a: pl.BlockSpec(a.shape, lambda i: (0,) * a.ndim)
    first = z.shape[1] // gw - 4
    return pl.pallas_call(
        functools.partial(_gate_kernel, first_rows_only=first_rows_only),
        grid=(m // tm,),
        in_specs=[pl.BlockSpec((tm, aw), lambda i: (i, 0)),
                  zb(first), zb(first + 1), zb(first + 2), zb(first + 3),
                  pl.BlockSpec((1, gw), lambda i: (0, 0)), full(ws), full(bs)],
        out_specs=[pl.BlockSpec((tm, aw + gw), lambda i: (i, 0)), pl.BlockSpec((tm, gw), lambda i: (i, 0))],
        out_shape=[jax.ShapeDtypeStruct((m, aw + gw), BF16), jax.ShapeDtypeStruct((m, gw), F32)],
        compiler_params=_params("arbitrary"),
        name=name,
    )(att, z, z, z, z, g_sgu.reshape(1, gw), ws, bs)


def _matmul_residual_kernel(x_ref, w_ref, h_ref, o_ref):
    o_ref[...] = h_ref[...] + _dot(x_ref[...], w_ref[...])


def _matmul_residual(x, w, h, *, tm, tn, name):
    m, k = x.shape
    n = w.shape[1]
    tm, tn = min(tm, m), min(tn, n)
    return pl.pallas_call(
        _matmul_residual_kernel,
        grid=(m // tm, n // tn),
        in_specs=[pl.BlockSpec((tm, k), lambda i, j: (i, 0)),
                  pl.BlockSpec((k, tn), lambda i, j: (0, j)),
                  pl.BlockSpec((tm, tn), lambda i, j: (i, j))],
        out_specs=pl.BlockSpec((tm, tn), lambda i, j: (i, j)),
        out_shape=jax.ShapeDtypeStruct((m, n), F32),
        compiler_params=_params("arbitrary", "arbitrary"),
        name=name,
    )(x, w, h)


def _cross_kernel(q_ref, mk_ref, mv_ref, h_ref, wo_ref, o_ref):
    rows = q_ref.shape[1]
    q = q_ref[0]
    if rows < SUBLANES:
        q = jnp.broadcast_to(q, (SUBLANES, q.shape[1]))
    qb = q.astype(BF16)
    mk = mk_ref[0].astype(BF16)
    mv = mv_ref[0].astype(BF16)
    hd = mk.shape[1] // X_HEADS
    outs = []
    for hh in range(X_HEADS):
        sl = slice(hh * hd, (hh + 1) * hd)
        lg = _dot_nt(qb[:, sl], mk[:, sl]) * hd ** -0.5
        e = jnp.exp(lg - jnp.max(lg, axis=1, keepdims=True))
        p = e / jnp.sum(e, axis=1, keepdims=True)
        outs.append(_dot(p.astype(BF16), mv[:, sl]))
    y = _dot(jnp.concatenate(outs, axis=1).astype(BF16), wo_ref[...])
    o_ref[0] = h_ref[0] + y[:rows]


def _cross(q, mk, mv, h, w_xo, *, tq, name):
    b, t, xw = q.shape
    d = h.shape[2]
    mlen = mk.shape[1]
    return pl.pallas_call(
        _cross_kernel,
        grid=(b, t // tq),
        in_specs=[pl.BlockSpec((1, tq, xw), lambda bb, i: (bb, i, 0)),
                  pl.BlockSpec((1, mlen, xw), lambda bb, i: (bb, 0, 0)),
                  pl.BlockSpec((1, mlen, xw), lambda bb, i: (bb, 0, 0)),
                  pl.BlockSpec((1, tq, d), lambda bb, i: (bb, i, 0)),
                  pl.BlockSpec((xw, d), lambda bb, i: (0, 0))],
        out_specs=pl.BlockSpec((1, tq, d), lambda bb, i: (bb, i, 0)),
        out_shape=jax.ShapeDtypeStruct((b, t, d), F32),
        compiler_params=_params("arbitrary", "arbitrary"),
        name=name,
    )(q, mk, mv, h, w_xo)


def _top_rows(s, k, payload=None):
    r = s.shape[0]
    rid = lax.broadcasted_iota(I32, s.shape, 0).astype(F32)
    vals, picks = [], []
    for _ in range(k):
        m = jnp.max(s, axis=0, keepdims=True)
        am = jnp.min(jnp.where(s == m, rid, float(r)), axis=0, keepdims=True)
        hit = rid == am
        vals.append(m)
        picks.append(am if payload is None else jnp.sum(jnp.where(hit, payload, 0.0), axis=0, keepdims=True))
        s = jnp.where(hit, -jnp.inf, s)
    return jnp.concatenate(vals, axis=0), jnp.concatenate(picks, axis=0)


def _peer_route_kernel(h_ref, g_ref, w_ref, sk_ref, ids_ref, gate_ref, xn_ref):
    @pl.when(pl.program_id(1) == 0)
    def _():
        x = h_ref[...]
        xn_ref[...] = (x * lax.rsqrt(jnp.mean(x * x, axis=-1, keepdims=True) + EPS) * g_ref[...]).astype(BF16)

    half = sk_ref.shape[2]
    qt = _dot_nt(w_ref[...], xn_ref[...]).astype(BF16)
    v0, i0 = _top_rows(_dot(sk_ref[0], qt[:half]), PEER_TOPK)
    v1, i1 = _top_rows(_dot(sk_ref[1], qt[half:]), PEER_TOPK)
    widths = [PEER_TOPK // (a + 1) for a in range(PEER_TOPK)]
    pad = -sum(widths) % SUBLANES
    cand = jnp.concatenate([v0[a:a + 1] + v1[:n] for a, n in enumerate(widths)]
                           + [jnp.full((pad, v0.shape[1]), -jnp.inf, F32)], axis=0)
    eid = jnp.concatenate([i0[a:a + 1] * float(N_KEYS) + i1[:n] for a, n in enumerate(widths)]
                          + [jnp.zeros((pad, v0.shape[1]), F32)], axis=0)
    sc, e = _top_rows(cand, PEER_TOPK, payload=eid)
    ex = jnp.exp(sc - sc[0:1])
    gate_ref[...] = ex / jnp.sum(ex, axis=0, keepdims=True)
    ids_ref[...] = e.astype(I32)


def _peer_route(h, g_ffn, w_pq_t, sub_keys, *, tb):
    m, d = h.shape
    qd = w_pq_t.shape[0] // PEER_HEADS
    tb = min(tb, m)
    return pl.pallas_call(
        _peer_route_kernel,
        grid=(m // tb, PEER_HEADS),
        in_specs=[pl.BlockSpec((tb, d), lambda t, hh: (t, 0)),
                  pl.BlockSpec((1, d), lambda t, hh: (0, 0)),
                  pl.BlockSpec((qd, d), lambda t, hh: (hh, 0)),
                  pl.BlockSpec(sub_keys.shape, lambda t, hh: (0, 0, 0))],
        out_specs=[pl.BlockSpec((PEER_TOPK, tb), lambda t, hh: (hh, t)),
                   pl.BlockSpec((PEER_TOPK, tb), lambda t, hh: (hh, t))],
        out_shape=[jax.ShapeDtypeStruct((PEER_HEADS * PEER_TOPK, m), I32),
                   jax.ShapeDtypeStruct((PEER_HEADS * PEER_TOPK, m), F32)],
        scratch_shapes=[pltpu.VMEM((tb, d), BF16)],
        compiler_params=_params("arbitrary", "arbitrary"),
        name="peer_route",
    )(h, g_ffn.reshape(1, d), w_pq_t, sub_keys)


def _pack_tables_kernel(u_ref, v_ref, o_ref):
    nr = o_ref.shape[1] // 2

    def words(x):
        bits = pltpu.bitcast(x.astype(BF16).astype(F32), I32)
        half = bits.shape[1] // 2
        return (bits[:, half:] & jnp.int32(-65536)) | lax.shift_right_logical(bits[:, :half], 16)

    wu = words(u_ref[...])
    wv = words(v_ref[...])
    for r in range(nr):
        o_ref[:, r, :] = wu[:, r * LANES:(r + 1) * LANES]
        o_ref[:, nr + r, :] = wv[:, r * LANES:(r + 1) * LANES]


def _pack_tables(u_tab, v_tab, *, eb):
    n, d = u_tab.shape
    nr = d // (2 * LANES)
    return pl.pallas_call(
        _pack_tables_kernel,
        grid=(n // eb,),
        in_specs=[pl.BlockSpec((eb, d), lambda i: (i, 0)), pl.BlockSpec((eb, d), lambda i: (i, 0))],
        out_specs=pl.BlockSpec((eb, 2 * nr, LANES), lambda i: (i, 0, 0)),
        out_shape=jax.ShapeDtypeStruct((n, 2 * nr, LANES), I32),
        compiler_params=_params("arbitrary"),
        name="peer_pack_tables",
    )(u_tab, v_tab)


def _unpack_words(w):
    return pltpu.bitcast(jnp.left_shift(w, 16), F32), pltpu.bitcast(w & jnp.int32(-65536), F32)


def _peer_gather_kernel(ids_hbm, gate_ref, h_ref, gffn_ref, gfin_ref, tab_hbm, o_ref,
                        ids_s, buf, part_s, sem_ids, sem_rows, *, n_steps):
    s = pl.program_id(0)
    g_tok = h_ref.shape[0]
    nr = buf.shape[1] // 2
    n_e = buf.shape[2]
    n_buf = buf.shape[0]
    ahead = n_buf - 1
    assert g_tok % n_buf == 0 and ahead <= g_tok
    cur = lax.rem(s, 2)
    nxt = 1 - cur

    def ids_copy(step, slot):
        return pltpu.make_async_copy(ids_hbm.at[step], ids_s.at[slot], sem_ids.at[slot])

    def row_copy(e, k, slot):
        return pltpu.make_async_copy(tab_hbm.at[e], buf.at[slot, :, k, :], sem_rows.at[slot])

    def issue(ids_slot, t, slot):
        for k in range(n_e):
            row_copy(ids_s[ids_slot, t, k], k, slot).start(priority=k % 2)

    def wait(slot):
        for k in range(n_e):
            row_copy(0, k, slot).wait()

    @pl.when(s == 0)
    def _():
        ids_copy(0, 0).start()
        ids_copy(0, 0).wait()
        for t in range(ahead):
            issue(0, t, t % n_buf)

    @pl.when(s + 1 < n_steps)
    def _():
        ids_copy(s + 1, nxt).start()

    h = h_ref[...]
    xn = h * lax.rsqrt(jnp.mean(h * h, axis=-1, keepdims=True) + EPS) * gffn_ref[...]
    eye = lax.broadcasted_iota(I32, (n_e, n_e), 0) == lax.broadcasted_iota(I32, (n_e, n_e), 1)

    for t in range(g_tok):
        slot = t % n_buf
        ta = t + ahead
        if ta < g_tok:
            issue(cur, ta, ta % n_buf)
        else:
            @pl.when(s + 1 < n_steps)
            def _():
                if ta == g_tok:
                    ids_copy(s + 1, nxt).wait()
                issue(nxt, ta - g_tok, ta % n_buf)
        wait(slot)
        x = xn[t:t + 1, :]
        hacc = jnp.zeros((n_e, LANES), F32)
        for r in range(nr):
            lo, hi = _unpack_words(buf[slot, r])
            hacc = hacc + lo * x[:, r * LANES:(r + 1) * LANES] + hi * x[:, (nr + r) * LANES:(nr + r + 1) * LANES]
        act = jax.nn.gelu(jnp.sum(hacc, axis=-1, keepdims=True))
        gcol = jnp.sum(jnp.where(eye, jnp.broadcast_to(gate_ref[t:t + 1, :], (n_e, n_e)), 0.0),
                       axis=-1, keepdims=True)
        w = jnp.broadcast_to(gcol * act, (n_e, LANES))
        for r in range(nr):
            lo, hi = _unpack_words(buf[slot, nr + r])
            part_s[t, :, r * LANES:(r + 1) * LANES] = jnp.sum(
                (lo * w).reshape(n_e // SUBLANES, SUBLANES, LANES), axis=0)
            part_s[t, :, (nr + r) * LANES:(nr + r + 1) * LANES] = jnp.sum(
                (hi * w).reshape(n_e // SUBLANES, SUBLANES, LANES), axis=0)

    y = h + jnp.sum(part_s[...], axis=1)
    o_ref[...] = y * lax.rsqrt(jnp.mean(y * y, axis=-1, keepdims=True) + EPS) * gfin_ref[...]


PEER_GATHER_BUFFERS = 4


def _peer_gather(ids, gates, h, g_ffn, g_final, packed_tab, *, g_tok):
    m, d = h.shape
    n_e = ids.shape[1]
    n_steps = m // g_tok
    n_rows = packed_tab.shape[1]
    ids3 = ids.reshape(n_steps, g_tok, n_e)
    return pl.pallas_call(
        functools.partial(_peer_gather_kernel, n_steps=n_steps),
        grid=(n_steps,),
        in_specs=[pl.BlockSpec(memory_space=pl.ANY),
                  pl.BlockSpec((g_tok, n_e), lambda s: (s, 0)),
                  pl.BlockSpec((g_tok, d), lambda s: (s, 0)),
                  pl.BlockSpec((1, d), lambda s: (0, 0)),
                  pl.BlockSpec((1, d), lambda s: (0, 0)),
                  pl.BlockSpec(memory_space=pl.ANY)],
        out_specs=pl.BlockSpec((g_tok, d), lambda s: (s, 0)),
        out_shape=jax.ShapeDtypeStruct((m, d), F32),
        scratch_shapes=[pltpu.SMEM((2, g_tok, n_e), I32),
                        pltpu.VMEM((PEER_GATHER_BUFFERS, n_rows, n_e, LANES), I32),
                        pltpu.VMEM((g_tok, SUBLANES, d), F32),
                        pltpu.SemaphoreType.DMA((2,)), pltpu.SemaphoreType.DMA((PEER_GATHER_BUFFERS,))],
        compiler_params=_params("arbitrary"),
        name="peer_gather",
    )(ids3, gates, h, g_ffn.reshape(1, d), g_final.reshape(1, d), packed_tab)


def kernel(x_prompt, x_sample, cache_k, cache_v, cache_kidx, cache_mem_k, cache_mem_v, page_table,
           mem_prompt, rel_bias, g_in, w_in, g_sgu, w_s, b_s, w_out, g_x, w_xq, w_xk, w_xv, w_xo,
           g_ffn, w_pq, sub_keys, peer_u, peer_v, g_final):
    batch, seq, d = x_prompt.shape
    db, dseq, _ = x_sample.shape
    depth = w_in.shape[0]
    assert dseq == 1, "sample group is one new token per sequence"
    n_pages = page_table.shape[1]
    page = cache_k.shape[2]
    past = n_pages * page
    aw, kvw, qiw, gw = N_HEADS * HEAD_DIM, N_KV_HEADS * HEAD_DIM, IDX_HEADS * IDX_DIM, GMLP_GROUPS * LANES
    xw = w_xq.shape[2]
    mlen = mem_prompt.shape[1]
    small_lo = aw + 2 * kvw + qiw
    small_hi = small_lo + IDX_DIM + IDX_HEADS
    assert NUM_BUCKETS // 2 + int(math.log((LANES + 1) / (NUM_BUCKETS // 2)) / math.log(
        MAX_DISTANCE / (NUM_BUCKETS // 2)) * (NUM_BUCKETS // 2)) >= NUM_BUCKETS - 1

    hp = x_prompt.reshape(batch * seq, d)
    hs = x_sample.reshape(db, d)
    btiles = _bias_tiles(rel_bias)
    rb_t = rel_bias.T
    outs = {n: [] for n in ("kp", "vp", "ip", "mkp", "mvp", "ks", "vs", "is", "sv")}

    for l in range(depth):
        w_in_t = jnp.swapaxes(w_in[l], 0, 1).astype(BF16)
        w_small_t = jnp.pad(w_in_t[small_lo:small_hi], ((0, LANES - (small_hi - small_lo)), (0, 0)))
        w_out_b = w_out[l].astype(BF16)
        w_xq_b, w_xk_b, w_xv_b, w_xo_b = (w.astype(BF16) for w in (w_xq[l], w_xk[l], w_xv[l], w_xo[l]))
        w_pq_t = w_pq[l].T.astype(BF16)
        sk_b = sub_keys[l].astype(BF16)
        ws_first = jnp.repeat(w_s[l][:, 0, 0], LANES).reshape(1, gw)
        bs_first = jnp.repeat(b_s[l][:, 0], LANES).reshape(1, gw)
        ones_d = jnp.ones((d,), F32)

        z, zs = _proj_wt(hp, g_in[l], w_in_t, w_small_t, skip_lo=small_lo, skip_hi=small_hi, tm=512, tn=1024,
                         name="in_proj_prompt")
        att = _dsa_prompt(z.reshape(batch, seq, -1), zs.reshape(batch, seq, LANES), btiles,
                          batch=batch, seq=seq, topk=min(TOPK_MAX, seq // 4))
        m_p, _ = _gate(att.reshape(batch * seq, aw), z, g_sgu[l], w_s[l], b_s[l].T,
                       first_rows_only=False, tm=256, name="gate_prompt")
        hp1 = _matmul_residual(m_p, w_out_b, hp, tm=1024, tn=1024, name="out_proj_prompt")
        mem = mem_prompt.reshape(batch * mlen, d)
        mk = _proj(mem, ones_d, w_xk_b, norm=False, tm=256, tn=xw, name="mem_k_proj")
        mv = _proj(mem, ones_d, w_xv_b, norm=False, tm=256, tn=xw, name="mem_v_proj")
        qx = _proj(hp1, g_x[l], w_xq_b, norm=True, tm=512, tn=xw, name="xq_proj_prompt")
        hp2 = _cross(qx.reshape(batch, seq, xw), mk.reshape(batch, mlen, xw), mv.reshape(batch, mlen, xw),
                     hp1.reshape(batch, seq, d), w_xo_b, tq=256, name="cross_prompt").reshape(batch * seq, d)
        outs["kp"].append(z[:, aw:aw + kvw].reshape(batch, seq, N_KV_HEADS, HEAD_DIM))
        outs["vp"].append(z[:, aw + kvw:aw + 2 * kvw].reshape(batch, seq, N_KV_HEADS, HEAD_DIM))
        outs["ip"].append(zs[:, :IDX_DIM].reshape(batch, seq, IDX_DIM))
        outs["mkp"].append(mk.reshape(batch, mlen, X_HEADS, xw // X_HEADS))
        outs["mvp"].append(mv.reshape(batch, mlen, X_HEADS, xw // X_HEADS))

        z_s, zs_s = _proj_wt(hs, g_in[l], w_in_t, w_small_t, skip_lo=small_lo, skip_hi=small_hi, tm=db, tn=512,
                             name="in_proj_sample")
        qi3 = z_s[:, aw + 2 * kvw:aw + 2 * kvw + qiw].reshape(db, IDX_HEADS, IDX_DIM)
        wi3 = zs_s[:, IDX_DIM:IDX_DIM + IDX_HEADS].reshape(db, IDX_HEADS, 1)
        ki_new = zs_s[:, :IDX_DIM].reshape(db, 1, IDX_DIM)
        k_new = z_s[:, aw:aw + kvw].reshape(db, 1, kvw)
        v_new = z_s[:, aw + kvw:aw + 2 * kvw].reshape(db, 1, kvw)
        scores3, snew3 = _dsa_sample_scores(page_table, qi3, wi3, ki_new.reshape(db, IDX_DIM, 1),
                                            jnp.swapaxes(cache_kidx[l], 1, 2))
        pos3, new_sel = _dsa_sample_select(scores3.reshape(db, past), snew3.reshape(db, LANES),
                                           topk=min(TOPK_MAX, (past + dseq) // 4))
        att_s = _dsa_sample_attn(pos3[:, :, 0], page_table, z_s[:, :aw].reshape(db, N_HEADS, HEAD_DIM),
                                 k_new, v_new, new_sel, rb_t, cache_k, cache_v, layer=l)
        m_s, vn_s = _gate(att_s.reshape(db, aw), z_s, g_sgu[l], ws_first, bs_first,
                          first_rows_only=True, tm=db, name="gate_sample")
        hs1 = _matmul_residual(m_s, w_out_b, hs, tm=db, tn=1024, name="out_proj_sample")
        qx_s = _proj(hs1, g_x[l], w_xq_b, norm=True, tm=db, tn=xw, name="xq_proj_sample")
        hs2 = _cross(qx_s.reshape(db, 1, xw), cache_mem_k[l].reshape(db, mlen, xw),
                     cache_mem_v[l].reshape(db, mlen, xw), hs1.reshape(db, 1, d), w_xo_b,
                     tq=1, name="cross_sample").reshape(db, d)
        outs["ks"].append(k_new.reshape(db, dseq, N_KV_HEADS, HEAD_DIM))
        outs["vs"].append(v_new.reshape(db, dseq, N_KV_HEADS, HEAD_DIM))
        outs["is"].append(ki_new.reshape(db, dseq, IDX_DIM))
        outs["sv"].append(vn_s.reshape(db, dseq, gw))

        assert depth == 1
        packed = _pack_tables(peer_u[l], peer_v[l], eb=256)
        ids_p, gates_p = _peer_route(hp2, g_ffn[l], w_pq_t, sk_b, tb=512)
        y_p = _peer_gather(ids_p.T, gates_p.T, hp2, g_ffn[l], g_final, packed, g_tok=8)
        hs2_pad = jnp.pad(hs2, ((0, -db % LANES), (0, 0)))
        ids_s, gates_s = _peer_route(hs2_pad, g_ffn[l], w_pq_t, sk_b, tb=LANES)
        y_s = _peer_gather(ids_s.T[:db], gates_s.T[:db], hs2, g_ffn[l], g_final, packed, g_tok=8)

    y_prompt = y_p.reshape(batch, seq, d)
    y_sample = y_s.reshape(db, dseq, d)
    st = lambda n: jnp.stack(outs[n])
    return (y_prompt, y_sample, st("kp"), st("vp"), st("ip"), st("mkp"), st("mvp"),
            st("ks"), st("vs"), st("is"), st("sv"))
```

```python
import functools
import math

import jax
import jax.numpy as jnp
from jax import lax
from jax.experimental import pallas as pl
from jax.experimental.pallas import tpu as pltpu

F32 = jnp.float32
BF16 = jnp.bfloat16
I32 = jnp.int32

N_HEADS = 16
HEAD_DIM = 128
N_KV_HEADS = 4
GQA = N_HEADS // N_KV_HEADS
IDX_HEADS = 16
IDX_DIM = 64
TOPK_MAX = 256
NUM_BUCKETS = 32
MAX_DISTANCE = 128
CHUNK = 128
GMLP_GROUPS = 16
X_HEADS = 4
N_KEYS = 128
PEER_HEADS = 8
PEER_TOPK = 16
EPS = 1e-6

LANES = 128
SUBLANES = 8
VMEM_LIMIT_BYTES = 56 * 1024 * 1024

NEG = -1e30
INT_MIN = -(2 ** 31)


def _dot_nt(a, b):
    return lax.dot_general(a, b, (((1,), (1,)), ((), ())), preferred_element_type=F32)


def _dot(a, b):
    return jnp.dot(a, b, preferred_element_type=F32)


SMALL_CALL_ROWS = 256
SMALL_VMEM_LIMIT_BYTES = 24 * 1024 * 1024


def _params(*sem, rows=None):
    small = rows is not None and rows <= SMALL_CALL_ROWS
    return pltpu.CompilerParams(dimension_semantics=sem,
                                vmem_limit_bytes=SMALL_VMEM_LIMIT_BYTES if small else VMEM_LIMIT_BYTES)


def _sortable(x):
    bits = pltpu.bitcast(x, I32)
    return bits ^ (jnp.right_shift(bits, 31) & 0x7FFFFFFF)


def _t5_bucket(dist):
    n = jnp.maximum(dist, 0)
    max_exact = NUM_BUCKETS // 2
    nf = jnp.maximum(n, 1).astype(F32)
    large = max_exact + (jnp.log(nf / max_exact) / math.log(MAX_DISTANCE / max_exact)
                         * (NUM_BUCKETS - max_exact)).astype(I32)
    large = jnp.minimum(large, NUM_BUCKETS - 1)
    return jnp.where(n < max_exact, n, large)


def _kth_largest_key(count_ge, shape, k):
    def body(it, t):
        cand = t | jnp.left_shift(jnp.int32(1), 31 - it)
        cnt = count_ge(cand ^ INT_MIN)
        return jnp.where(cnt >= k, cand, t)

    t = lax.fori_loop(0, 32, body, jnp.zeros(shape, I32))
    return t ^ INT_MIN


def _prefix_matrix():
    return (lax.broadcasted_iota(I32, (LANES, LANES), 0) <= lax.broadcasted_iota(I32, (LANES, LANES), 1)).astype(BF16)


def _proj_kernel(x_ref, g_ref, w_ref, o_ref, xn_ref, *, norm):
    @pl.when(pl.program_id(1) == 0)
    def _():
        x = x_ref[...]
        if norm:
            x = x * lax.rsqrt(jnp.mean(x * x, axis=-1, keepdims=True) + EPS) * g_ref[...]
        xn_ref[...] = x.astype(BF16)

    o_ref[...] = _dot(xn_ref[...], w_ref[...])


def _proj(x, g, w, *, norm, tm, tn, name):
    m, k = x.shape
    n = w.shape[1]
    tm, tn = min(tm, m), min(tn, n)
    return pl.pallas_call(
        functools.partial(_proj_kernel, norm=norm),
        grid=(m // tm, n // tn),
        in_specs=[pl.BlockSpec((tm, k), lambda i, j: (i, 0)),
                  pl.BlockSpec((1, k), lambda i, j: (0, 0)),
                  pl.BlockSpec((k, tn), lambda i, j: (0, j))],
        out_specs=pl.BlockSpec((tm, tn), lambda i, j: (i, j)),
        out_shape=jax.ShapeDtypeStruct((m, n), F32),
        scratch_shapes=[pltpu.VMEM((tm, k), BF16)],
        compiler_params=_params("arbitrary", "arbitrary", rows=m),
        name=name,
    )(x, g.reshape(1, k), w)


IN_PROJ_WEIGHT_BUFFERS = 3


def _proj_wt_kernel(x_ref, g_ref, wt_hbm, w2t_ref, o_ref, o2_ref, xn_ref, wbuf, sem, *, tn, skip_lo, gap):
    nj = pl.num_programs(1)
    total = pl.num_programs(0) * nj
    s = pl.program_id(0) * nj + pl.program_id(1)
    n_buf = wbuf.shape[0]

    def w_copy(step, slot):
        start = lax.rem(step, nj) * tn
        row = pl.multiple_of(jnp.where(start < skip_lo, start, start + gap), math.gcd(tn, gap))
        return pltpu.make_async_copy(wt_hbm.at[pl.ds(row, tn)], wbuf.at[slot], sem.at[slot])

    @pl.when(s == 0)
    def _():
        for a in range(n_buf - 1):
            w_copy(a, a).start()

    @pl.when(s + n_buf - 1 < total)
    def _():
        w_copy(s + n_buf - 1, lax.rem(s + n_buf - 1, n_buf)).start()

    @pl.when(pl.program_id(1) == 0)
    def _():
        x = x_ref[...]
        xn_ref[...] = (x * lax.rsqrt(jnp.mean(x * x, axis=-1, keepdims=True) + EPS) * g_ref[...]).astype(BF16)
        o2_ref[...] = _dot_nt(xn_ref[...], w2t_ref[...])

    slot = lax.rem(s, n_buf)
    w_copy(s, slot).wait()
    o_ref[...] = _dot_nt(xn_ref[...], wbuf[slot])


def _proj_wt(x, g, w_t, w_side_t, *, skip_lo, skip_hi, tm, tn, name):
    m, k = x.shape
    gap = skip_hi - skip_lo
    n = w_t.shape[0] - gap
    n2 = w_side_t.shape[0]
    tm, tn = min(tm, m), min(tn, n)
    assert skip_lo % tn == 0 and n % tn == 0
    assert (m // tm) * (n // tn) >= IN_PROJ_WEIGHT_BUFFERS - 1
    return pl.pallas_call(
        functools.partial(_proj_wt_kernel, tn=tn, skip_lo=skip_lo, gap=gap),
        grid=(m // tm, n // tn),
        in_specs=[pl.BlockSpec((tm, k), lambda i, j: (i, 0)),
                  pl.BlockSpec((1, k), lambda i, j: (0, 0)),
                  pl.BlockSpec(memory_space=pl.ANY),
                  pl.BlockSpec((n2, k), lambda i, j: (0, 0))],
        out_specs=[pl.BlockSpec((tm, tn), lambda i, j: (i, j)), pl.BlockSpec((tm, n2), lambda i, j: (i, 0))],
        out_shape=[jax.ShapeDtypeStruct((m, n), F32), jax.ShapeDtypeStruct((m, n2), F32)],
        scratch_shapes=[pltpu.VMEM((tm, k), BF16), pltpu.VMEM((IN_PROJ_WEIGHT_BUFFERS, tn, k), BF16),
                        pltpu.SemaphoreType.DMA((IN_PROJ_WEIGHT_BUFFERS,))],
        compiler_params=_params("arbitrary", "arbitrary", rows=m),
        name=name,
    )(x, g.reshape(1, k), w_t, w_side_t)


def _bias_tiles_kernel(rb_ref, o_ref):
    tq = lax.broadcasted_iota(I32, (LANES, LANES), 0)
    c = lax.broadcasted_iota(I32, (LANES, LANES), 1)
    for part, off in enumerate((2 * LANES, LANES, 0)):
        dist = tq - c + off
        bucket = _t5_bucket(dist)
        for h in range(N_HEADS):
            tile = lax.fori_loop(
                0, NUM_BUCKETS, lambda b, acc: jnp.where(bucket == b, rb_ref[b, h], acc),
                jnp.zeros((LANES, LANES), F32))
            if off == 0:
                tile = jnp.where(dist < 0, NEG, tile)
            o_ref[part, h] = tile


def _bias_tiles(rel_bias):
    return pl.pallas_call(
        _bias_tiles_kernel,
        in_specs=[pl.BlockSpec(memory_space=pltpu.SMEM)],
        out_specs=pl.BlockSpec(memory_space=pltpu.VMEM),
        out_shape=jax.ShapeDtypeStruct((3, N_HEADS, LANES, LANES), F32),
        name="t5_bias_tiles",
    )(rel_bias)


def _dsa_prompt_kernel(q_ref, k_ref, v_ref, qi_ref, zsk_ref, zsq_ref, bt_ref, tri_ref, o_ref,
                       kbf, vbf, kilo, kihi, key3, qs, m_s, l_s, acc_s, *, topk):
    i = pl.program_id(1)
    tq = q_ref.shape[1]
    s = k_ref.shape[1]
    nkb = s // LANES
    cw = 2 * LANES

    @pl.when(i == 0)
    def _():
        kbf[...] = k_ref[0].astype(BF16)
        vbf[...] = v_ref[0].astype(BF16)
        zs = zsk_ref[0]
        lane = lax.broadcasted_iota(I32, zs.shape, 1)
        kilo[...] = jnp.where(lane < IDX_DIM, zs, 0.0).astype(BF16)
        kihi[...] = jnp.where(lane >= IDX_DIM, pltpu.roll(zs, IDX_DIM, axis=1), 0.0).astype(BF16)

    qi = qi_ref[0].astype(BF16)
    coef = zsq_ref[0][:, IDX_DIM:IDX_DIM + IDX_HEADS] * (IDX_HEADS ** -0.5 * IDX_DIM ** -0.5)
    t_pos = i * tq + lax.broadcasted_iota(I32, (tq, cw), 0)
    t_last = i * tq + tq - 1
    for c in range(s // cw):
        @pl.when(c * cw <= t_last)
        def _():
            klo = kilo[c * cw:(c + 1) * cw, :]
            khi = kihi[c * cw:(c + 1) * cw, :]
            acc = jnp.zeros((tq, cw), F32)
            for p in range(IDX_HEADS // 2):
                a = qi[:, p * LANES:(p + 1) * LANES]
                acc = acc + coef[:, 2 * p:2 * p + 1] * jnp.maximum(_dot_nt(a, klo), 0.0)
                acc = acc + coef[:, 2 * p + 1:2 * p + 2] * jnp.maximum(_dot_nt(a, khi), 0.0)
            s_pos = c * cw + lax.broadcasted_iota(I32, (tq, cw), 1)
            key = _sortable(jnp.where(s_pos <= t_pos, acc, -jnp.inf))
            for u in range(cw // LANES):
                key3[c * (cw // LANES) + u] = key[:, u * LANES:(u + 1) * LANES]

        @pl.when(c * cw > t_last)
        def _():
            for u in range(cw // LANES):
                key3[c * (cw // LANES) + u] = jnp.full((tq, LANES), INT_MIN, I32)

    def count(pred):
        return jnp.sum(jnp.sum(jnp.where(pred, 1.0, 0.0), axis=0), axis=1, keepdims=True)

    thr = _kth_largest_key(lambda cs: count(key3[...] >= cs[None]), (tq, 1), float(topk))

    n_ge = count(key3[...] >= thr[None])

    @pl.when(jnp.max(n_ge) > float(topk))
    def _():
        need = float(topk) - count(key3[...] > thr[None])
        before = jnp.zeros((tq, 1), F32)
        for j in range(nkb):
            kj = key3[j]
            eq = kj == thr
            rank = _dot(jnp.where(eq, 1.0, 0.0).astype(BF16), tri_ref[...]) + before
            key3[j] = jnp.where(jnp.logical_and(eq, rank > need), INT_MIN, kj)
            before = rank[:, LANES - 1:LANES]

    thr4 = jnp.concatenate([thr] * GQA, axis=0)

    qb = q_ref[0].astype(BF16)
    for kh in range(N_KV_HEADS):
        qs[kh] = jnp.concatenate(
            [qb[:, (GQA * kh + g) * HEAD_DIM:(GQA * kh + g + 1) * HEAD_DIM] for g in range(GQA)], axis=0)
    m_s[...] = jnp.full(m_s.shape, NEG, F32)
    l_s[...] = jnp.zeros(l_s.shape, F32)
    acc_s[...] = jnp.zeros(acc_s.shape, F32)
    scale = HEAD_DIM ** -0.5

    def fbody(jj, carry):
        ja = 2 * jj
        jb = ja + 1
        part_a = jnp.clip(ja - i + 2, 0, 2)
        part_b = jnp.clip(jb - i + 2, 0, 2)
        key_b = jnp.where(jb <= i, key3[jb], INT_MIN)
        key8 = jnp.concatenate([key3[ja], key_b], axis=1)
        mask = jnp.concatenate([key8] * GQA, axis=0) >= thr4
        r0 = pl.multiple_of(ja * LANES, 2 * LANES)
        for kh in range(N_KV_HEADS):
            kj = kbf[pl.ds(r0, 2 * LANES), kh * HEAD_DIM:(kh + 1) * HEAD_DIM]
            vj = vbf[pl.ds(r0, 2 * LANES), kh * HEAD_DIM:(kh + 1) * HEAD_DIM]
            bias = jnp.concatenate(
                [bt_ref[part_a, GQA * kh:GQA * (kh + 1)].reshape(GQA * tq, LANES),
                 bt_ref[part_b, GQA * kh:GQA * (kh + 1)].reshape(GQA * tq, LANES)], axis=1)
            lg = jnp.where(mask, _dot_nt(qs[kh], kj) * scale + bias, NEG)
            m_old = m_s[kh]
            m_new = jnp.maximum(m_old, jnp.max(lg, axis=1, keepdims=True))
            alpha = jnp.exp(m_old - m_new)
            p = jnp.where(mask, jnp.exp(lg - jnp.concatenate([m_new, m_new], axis=1)), 0.0)
            l_s[kh] = alpha * l_s[kh] + jnp.sum(p, axis=1, keepdims=True)
            acc_s[kh] = alpha * acc_s[kh] + _dot(p.astype(BF16), vj)
            m_s[kh] = m_new
        return carry

    assert nkb % 2 == 0
    lax.fori_loop(0, jnp.right_shift(i + 2, 1), fbody, 0)
    for kh in range(N_KV_HEADS):
        out = acc_s[kh] / l_s[kh]
        for g in range(GQA):
            h = GQA * kh + g
            o_ref[0, :, h * HEAD_DIM:(h + 1) * HEAD_DIM] = out[g * tq:(g + 1) * tq]


def _dsa_prompt(z, zs, btiles, *, batch, seq, topk):
    tq = LANES
    aw = N_HEADS * HEAD_DIM
    kvw = N_KV_HEADS * HEAD_DIM
    qiw = IDX_HEADS * IDX_DIM
    return pl.pallas_call(
        functools.partial(_dsa_prompt_kernel, topk=topk),
        grid=(batch, seq // tq),
        in_specs=[
            pl.BlockSpec((1, tq, aw), lambda b, i: (b, i, 0)),
            pl.BlockSpec((1, seq, kvw), lambda b, i: (b, 0, aw // kvw)),
            pl.BlockSpec((1, seq, kvw), lambda b, i: (b, 0, aw // kvw + 1)),
            pl.BlockSpec((1, tq, qiw), lambda b, i: (b, i, (aw + 2 * kvw) // qiw)),
            pl.BlockSpec((1, seq, LANES), lambda b, i: (b, 0, 0)),
            pl.BlockSpec((1, tq, LANES), lambda b, i: (b, i, 0)),
            pl.BlockSpec((3, N_HEADS, LANES, LANES), lambda b, i: (0, 0, 0, 0)),
            pl.BlockSpec((LANES, LANES), lambda b, i: (0, 0)),
        ],
        out_specs=pl.BlockSpec((1, tq, aw), lambda b, i: (b, i, 0)),
        out_shape=jax.ShapeDtypeStruct((batch, seq, aw), F32),
        scratch_shapes=[
            pltpu.VMEM((seq, kvw), BF16), pltpu.VMEM((seq, kvw), BF16),
            pltpu.VMEM((seq, LANES), BF16), pltpu.VMEM((seq, LANES), BF16),
            pltpu.VMEM((seq // LANES, tq, LANES), I32),
            pltpu.VMEM((N_KV_HEADS, GQA * tq, HEAD_DIM), BF16),
            pltpu.VMEM((N_KV_HEADS, GQA * tq, LANES), F32),
            pltpu.VMEM((N_KV_HEADS, GQA * tq, LANES), F32),
            pltpu.VMEM((N_KV_HEADS, GQA * tq, HEAD_DIM), F32),
        ],
        compiler_params=_params("arbitrary", "arbitrary"),
        name="dsa_prompt",
    )(z, z, z, z, zs, zs, btiles, _prefix_matrix())


SAMPLE_PAGES_PER_STEP = 32


def _dsa_sample_score_kernel(pt_ref, qi_ref, wi_ref, kinew_ref, *rest):
    kidx_refs, (o_ref, onew_ref) = rest[:-2], rest[-2:]
    qi = qi_ref[0].astype(BF16)
    coef = wi_ref[0] * (IDX_HEADS ** -0.5 * IDX_DIM ** -0.5)

    def score(ki_t):
        return jnp.sum(jnp.maximum(_dot(qi, ki_t.astype(BF16)), 0.0) * coef, axis=0, keepdims=True)

    o_ref[0] = jnp.concatenate([score(r[0]) for r in kidx_refs], axis=1)

    @pl.when(pl.program_id(1) == pl.num_programs(1) - 1)
    def _():
        onew_ref[0] = score(jnp.broadcast_to(kinew_ref[0], (kinew_ref.shape[1], LANES)))


def _dsa_sample_scores(page_table, qi3, wi3, ki_new_t, cache_kidx_t):
    db, n_pages = page_table.shape
    page = cache_kidx_t.shape[2]
    pg = math.gcd(SAMPLE_PAGES_PER_STEP, n_pages)
    row = lambda b, j, pt: (b, 0, 0)
    grid_spec = pltpu.PrefetchScalarGridSpec(
        num_scalar_prefetch=1,
        grid=(db, n_pages // pg),
        in_specs=[pl.BlockSpec((1, IDX_HEADS, IDX_DIM), row),
                  pl.BlockSpec((1, IDX_HEADS, 1), row),
                  pl.BlockSpec((1, IDX_DIM, 1), row)]
        + [pl.BlockSpec((1, IDX_DIM, page), functools.partial(lambda b, j, pt, u: (pt[b, j * pg + u], 0, 0), u=u))
           for u in range(pg)],
        out_specs=[pl.BlockSpec((1, 1, pg * page), lambda b, j, pt: (b, 0, j)),
                   pl.BlockSpec((1, 1, LANES), row)],
    )
    return pl.pallas_call(
        _dsa_sample_score_kernel,
        grid_spec=grid_spec,
        out_shape=[jax.ShapeDtypeStruct((db, 1, n_pages * page), F32),
                   jax.ShapeDtypeStruct((db, 1, LANES), F32)],
        compiler_params=_params("arbitrary", "arbitrary", rows=db),
        name="dsa_sample_scores",
    )(page_table, qi3, wi3, ki_new_t, *([cache_kidx_t] * pg))


def _dsa_sample_select_kernel(s_ref, snew_ref, tri_ref, pos_ref, new_ref, rank_s, nsel_s, *, topk):
    db, past = s_ref.shape
    k = float(topk)
    key = _sortable(s_ref[...])
    key_new = _sortable(snew_ref[:, 0:1])

    def count_ge(cs):
        return (jnp.sum(jnp.where(key >= cs, 1.0, 0.0), axis=1, keepdims=True)
                + jnp.where(key_new >= cs, 1.0, 0.0))

    thr = _kth_largest_key(count_ge, (db, 1), k)
    gt = key > thr
    eq = key == thr
    need = k - (jnp.sum(jnp.where(gt, 1.0, 0.0), axis=1, keepdims=True) + jnp.where(key_new > thr, 1.0, 0.0))

    tri = tri_ref[...]

    def cumsum(flags):
        out, off = [], jnp.zeros((db, 1), F32)
        for blk in range(past // LANES):
            pc = _dot(flags[:, blk * LANES:(blk + 1) * LANES].astype(BF16), tri) + off
            out.append(pc)
            off = pc[:, LANES - 1:LANES]
        return jnp.concatenate(out, axis=1), off

    eq_rank, eq_total = cumsum(jnp.where(eq, 1.0, 0.0))
    sel = jnp.where(jnp.logical_or(gt, jnp.logical_and(eq, eq_rank <= need)), 1.0, 0.0)
    new_sel = jnp.logical_or(key_new > thr, jnp.logical_and(key_new == thr, eq_total < need))
    new_ref[...] = jnp.broadcast_to(jnp.where(new_sel, 1.0, 0.0), new_ref.shape)
    rank, n_sel = cumsum(sel)
    rank = jnp.where(sel > 0.0, rank, 0.0)
    for b in range(db):
        rank_s[b] = rank[b:b + 1, :]
        nsel_s[b] = jnp.broadcast_to(n_sel[b:b + 1, :], (1, LANES))
    ch = min(8 * LANES, past)
    slot = lax.broadcasted_iota(I32, (topk, ch), 0).astype(F32) + 1.0
    lane = lax.broadcasted_iota(I32, (topk, ch), 1).astype(F32)
    slot_col = lax.broadcasted_iota(I32, (topk, 1), 0).astype(F32)

    def extract(b, carry):
        r = rank_s[b]
        pos = jnp.zeros((topk, 1), F32)
        for c in range(past // ch):
            hit = r[:, c * ch:(c + 1) * ch] == slot
            pos = pos + jnp.sum(jnp.where(hit, lane + float(c * ch), 0.0), axis=1, keepdims=True)
        pos = jnp.where(slot_col < nsel_s[b][:, 0:1], pos, -1.0)
        pos_ref[b] = jnp.broadcast_to(pos, (topk, LANES)).astype(I32)
        return carry

    lax.fori_loop(0, db, extract, 0)


def _dsa_sample_select(scores, snew, *, topk):
    db = scores.shape[0]
    tri = _prefix_matrix()
    return pl.pallas_call(
        functools.partial(_dsa_sample_select_kernel, topk=topk),
        out_shape=[jax.ShapeDtypeStruct((db, topk, LANES), I32), jax.ShapeDtypeStruct((db, LANES), F32)],
        scratch_shapes=[pltpu.VMEM((db, 1, scores.shape[1]), F32), pltpu.VMEM((db, 1, LANES), F32)],
        compiler_params=pltpu.CompilerParams(vmem_limit_bytes=VMEM_LIMIT_BYTES),
        name="dsa_sample_select",
    )(scores, snew, tri)


def _dsa_sample_attn_kernel(pos_s, pt_s, q_ref, knew_ref, vnew_ref, posrow_ref, newsel_ref, rbt_ref,
                            ck_hbm, cv_hbm, o_ref, kbuf, vbuf, sem_k, sem_v, *, layer, past, page):
    b = pl.program_id(0)
    topk = kbuf.shape[1]
    slot = lax.rem(b, 2)
    page_shift = page.bit_length() - 1
    unroll = 8

    def row_copies(bb, j, sl):
        p = jnp.maximum(pos_s[bb * topk + j], 0)
        pg = pt_s[bb, jnp.right_shift(p, page_shift)]
        off = p & (page - 1)
        return (pltpu.make_async_copy(ck_hbm.at[layer, pg, off], kbuf.at[sl, j], sem_k.at[sl]),
                pltpu.make_async_copy(cv_hbm.at[layer, pg, off], vbuf.at[sl, j], sem_v.at[sl]))

    def issue(bb, sl):
        def body(i, carry):
            for u in range(unroll):
                ck, cv = row_copies(bb, i * unroll + u, sl)
                ck.start()
                cv.start()
            return carry
        lax.fori_loop(0, topk // unroll, body, 0)

    def wait(sl):
        def body(i, carry):
            for u in range(unroll):
                j = i * unroll + u
                pltpu.make_async_copy(ck_hbm.at[layer, 0, 0], kbuf.at[sl, j], sem_k.at[sl]).wait()
                pltpu.make_async_copy(cv_hbm.at[layer, 0, 0], vbuf.at[sl, j], sem_v.at[sl]).wait()
            return carry
        lax.fori_loop(0, topk // unroll, body, 0)

    @pl.when(b == 0)
    def _():
        issue(0, 0)

    @pl.when(b + 1 < pl.num_programs(0))
    def _():
        issue(b + 1, 1 - slot)

    wait(slot)

    q = q_ref[0].astype(BF16)
    posr = posrow_ref[0]
    valid = posr >= 0
    bucket = _t5_bucket(past - posr)
    bias = jnp.zeros((N_HEADS, topk), F32)
    for bk in range(NUM_BUCKETS):
        bias = jnp.where(bucket == bk, rbt_ref[:, bk:bk + 1], bias)
    shift = GQA.bit_length() - 1
    kv_of_row = jnp.right_shift(lax.broadcasted_iota(I32, (N_HEADS, topk), 0), shift)
    kv_of_row_d = jnp.right_shift(lax.broadcasted_iota(I32, (N_HEADS, HEAD_DIM), 0), shift)
    scale = HEAD_DIM ** -0.5
    qf = q.astype(F32)
    knew = knew_ref[0].astype(BF16).astype(F32)
    vnew = vnew_ref[0].astype(BF16).astype(F32)
    lg = jnp.zeros((N_HEADS, topk), F32)
    lg_new = jnp.zeros((N_HEADS, 1), F32)
    v_new_rows = jnp.zeros((N_HEADS, HEAD_DIM), F32)
    for kh in range(N_KV_HEADS):
        hs = slice(kh * HEAD_DIM, (kh + 1) * HEAD_DIM)
        lg = jnp.where(kv_of_row == kh, _dot_nt(q, kbuf[slot, :, kh, :].astype(BF16)), lg)
        lg_new = jnp.where(kv_of_row[:, 0:1] == kh, jnp.sum(qf * knew[:, hs], axis=1, keepdims=True), lg_new)
        v_new_rows = jnp.where(kv_of_row_d == kh, jnp.broadcast_to(vnew[:, hs], (N_HEADS, HEAD_DIM)), v_new_rows)
    lg = jnp.where(valid, lg * scale + bias, NEG)
    new_sel = newsel_ref[0][:, 0:1] > 0.5
    lg_new = jnp.where(new_sel, lg_new * scale + rbt_ref[:, 0:1], NEG)
    m = jnp.maximum(jnp.max(lg, axis=1, keepdims=True), lg_new)
    p = jnp.where(valid, jnp.exp(lg - m), 0.0)
    p_new = jnp.where(new_sel, jnp.exp(lg_new - m), 0.0)
    denom = jnp.sum(p, axis=1, keepdims=True) + p_new
    pb = (p / denom).astype(BF16)
    out = (p_new / denom).astype(BF16).astype(F32) * v_new_rows
    for kh in range(N_KV_HEADS):
        out = out + jnp.where(kv_of_row_d == kh, _dot(pb, vbuf[slot, :, kh, :].astype(BF16)), 0.0)
    o_ref[0] = out


def _dsa_sample_attn(pos, page_table, q3, k_new, v_new, new_sel, rb_t, cache_k, cache_v, *, layer):
    db, topk = pos.shape
    n_pages = page_table.shape[1]
    page, n_kv, hd = cache_k.shape[2:]
    assert page & (page - 1) == 0
    kvw = n_kv * hd
    row = lambda b, ps, pt: (b, 0, 0)
    grid_spec = pltpu.PrefetchScalarGridSpec(
        num_scalar_prefetch=2,
        grid=(db,),
        in_specs=[
            pl.BlockSpec((1, N_HEADS, HEAD_DIM), row),
            pl.BlockSpec((1, 1, kvw), row),
            pl.BlockSpec((1, 1, kvw), row),
            pl.BlockSpec((1, 1, topk), row),
            pl.BlockSpec((1, 1, LANES), row),
            pl.BlockSpec((N_HEADS, NUM_BUCKETS), lambda b, ps, pt: (0, 0)),
            pl.BlockSpec(memory_space=pl.ANY),
            pl.BlockSpec(memory_space=pl.ANY),
        ],
        out_specs=pl.BlockSpec((1, N_HEADS, HEAD_DIM), row),
        scratch_shapes=[pltpu.VMEM((2, topk, n_kv, hd), F32), pltpu.VMEM((2, topk, n_kv, hd), F32),
                        pltpu.SemaphoreType.DMA((2,)), pltpu.SemaphoreType.DMA((2,))],
    )
    return pl.pallas_call(
        functools.partial(_dsa_sample_attn_kernel, layer=layer, past=n_pages * page, page=page),
        grid_spec=grid_spec,
        out_shape=jax.ShapeDtypeStruct((db, N_HEADS, HEAD_DIM), F32),
        compiler_params=_params("arbitrary", rows=db),
        name="dsa_sample_attn",
    )(pos.reshape(-1), page_table, q3, k_new, v_new, pos.reshape(db, 1, topk),
      new_sel.reshape(db, 1, LANES), rb_t, cache_k, cache_v)


def _gate_kernel(att_ref, zu_ref, zv_ref, ga_ref, gb_ref, gs_ref, ws_ref, bs_ref, m_ref, vn_ref, *, first_rows_only):
    aw = att_ref.shape[1]
    tm = zu_ref.shape[0]
    v = jax.nn.gelu(zv_ref[...])
    vn = v * lax.rsqrt(jnp.mean(v * v, axis=-1, keepdims=True) + EPS) * gs_ref[...]
    vn_ref[...] = vn
    m_ref[:, :aw] = (jax.nn.sigmoid(ga_ref[...]) * att_ref[...]).astype(BF16)
    if first_rows_only:
        sgu = jax.nn.gelu(zu_ref[...]) * (vn * ws_ref[...] + bs_ref[...])
        m_ref[:, aw:] = (jax.nn.sigmoid(gb_ref[...]) * sgu).astype(BF16)
    else:
        row = lax.broadcasted_iota(I32, (CHUNK, CHUNK), 0)
        col = lax.broadcasted_iota(I32, (CHUNK, CHUNK), 1)
        for g in range(GMLP_GROUPS):
            cs = slice(g * LANES, (g + 1) * LANES)
            wt = jnp.where(col <= row, ws_ref[g], 0.0).astype(BF16)
            bcol = bs_ref[:, g:g + 1]
            for c in range(tm // CHUNK):
                rs = slice(c * CHUNK, (c + 1) * CHUNK)
                mixed = _dot(wt, vn_ref[rs, cs].astype(BF16)) + bcol
                sgu = jax.nn.gelu(zu_ref[rs, cs]) * mixed
                m_ref[rs, aw + g * LANES:aw + (g + 1) * LANES] = (
                    jax.nn.sigmoid(gb_ref[rs, cs]) * sgu).astype(BF16)


def _gate(att, z, g_sgu, ws, bs, *, first_rows_only, tm, name):
    m, aw = att.shape
    gw = g_sgu.shape[0]
    tm = min(tm, m)
    zb = lambda c: pl.BlockSpec((tm, gw), lambda i: (i, c))
    full = lambda a: pl.BlockSpec(a.shape, lambda i: (0,) * a.ndim)
    first = z.shape[1] // gw - 4
    return pl.pallas_call(
        functools.partial(_gate_kernel, first_rows_only=first_rows_only),
        grid=(m // tm,),
        in_specs=[pl.BlockSpec((tm, aw), lambda i: (i, 0)),
                  zb(first), zb(first + 1), zb(first + 2), zb(first + 3),
                  pl.BlockSpec((1, gw), lambda i: (0, 0)), full(ws), full(bs)],
        out_specs=[pl.BlockSpec((tm, aw + gw), lambda i: (i, 0)), pl.BlockSpec((tm, gw), lambda i: (i, 0))],
        out_shape=[jax.ShapeDtypeStruct((m, aw + gw), BF16), jax.ShapeDtypeStruct((m, gw), F32)],
        compiler_params=_params("arbitrary", rows=m),
        name=name,
    )(att, z, z, z, z, g_sgu.reshape(1, gw), ws, bs)


def _matmul_residual_kernel(x_ref, w_ref, h_ref, o_ref):
    o_ref[...] = h_ref[...] + _dot(x_ref[...], w_ref[...])


def _matmul_residual(x, w, h, *, tm, tn, name):
    m, k = x.shape
    n = w.shape[1]
    tm, tn = min(tm, m), min(tn, n)
    return pl.pallas_call(
        _matmul_residual_kernel,
        grid=(m // tm, n // tn),
        in_specs=[pl.BlockSpec((tm, k), lambda i, j: (i, 0)),
                  pl.BlockSpec((k, tn), lambda i, j: (0, j)),
                  pl.BlockSpec((tm, tn), lambda i, j: (i, j))],
        out_specs=pl.BlockSpec((tm, tn), lambda i, j: (i, j)),
        out_shape=jax.ShapeDtypeStruct((m, n), F32),
        compiler_params=_params("arbitrary", "arbitrary", rows=m),
        name=name,
    )(x, w, h)


def _cross_kernel(q_ref, mk_ref, mv_ref, h_ref, wo_ref, o_ref):
    rows = q_ref.shape[1]
    q = q_ref[0]
    if rows < SUBLANES:
        q = jnp.broadcast_to(q, (SUBLANES, q.shape[1]))
    qb = q.astype(BF16)
    mk = mk_ref[0].astype(BF16)
    mv = mv_ref[0].astype(BF16)
    hd = mk.shape[1] // X_HEADS
    outs = []
    for hh in range(X_HEADS):
        sl = slice(hh * hd, (hh + 1) * hd)
        lg = _dot_nt(qb[:, sl], mk[:, sl]) * hd ** -0.5
        e = jnp.exp(lg - jnp.max(lg, axis=1, keepdims=True))
        p = e / jnp.sum(e, axis=1, keepdims=True)
        outs.append(_dot(p.astype(BF16), mv[:, sl]))
    y = _dot(jnp.concatenate(outs, axis=1).astype(BF16), wo_ref[...])
    o_ref[0] = h_ref[0] + y[:rows]


def _cross(q, mk, mv, h, w_xo, *, tq, name):
    b, t, xw = q.shape
    m = b * t
    d = h.shape[2]
    mlen = mk.shape[1]
    return pl.pallas_call(
        _cross_kernel,
        grid=(b, t // tq),
        in_specs=[pl.BlockSpec((1, tq, xw), lambda bb, i: (bb, i, 0)),
                  pl.BlockSpec((1, mlen, xw), lambda bb, i: (bb, 0, 0)),
                  pl.BlockSpec((1, mlen, xw), lambda bb, i: (bb, 0, 0)),
                  pl.BlockSpec((1, tq, d), lambda bb, i: (bb, i, 0)),
                  pl.BlockSpec((xw, d), lambda bb, i: (0, 0))],
        out_specs=pl.BlockSpec((1, tq, d), lambda bb, i: (bb, i, 0)),
        out_shape=jax.ShapeDtypeStruct((b, t, d), F32),
        compiler_params=_params("arbitrary", "arbitrary", rows=m),
        name=name,
    )(q, mk, mv, h, w_xo)


def _top_rows(s, k, payload=None):
    r = s.shape[0]
    rid = lax.broadcasted_iota(I32, s.shape, 0).astype(F32)
    vals, picks = [], []
    for _ in range(k):
        m = jnp.max(s, axis=0, keepdims=True)
        am = jnp.min(jnp.where(s == m, rid, float(r)), axis=0, keepdims=True)
        hit = rid == am
        vals.append(m)
        picks.append(am if payload is None else jnp.sum(jnp.where(hit, payload, 0.0), axis=0, keepdims=True))
        s = jnp.where(hit, -jnp.inf, s)
    return jnp.concatenate(vals, axis=0), jnp.concatenate(picks, axis=0)


def _peer_route_kernel(h_ref, g_ref, w_ref, sk_ref, ids_ref, gate_ref, xn_ref):
    @pl.when(pl.program_id(1) == 0)
    def _():
        x = h_ref[...]
        xn_ref[...] = (x * lax.rsqrt(jnp.mean(x * x, axis=-1, keepdims=True) + EPS) * g_ref[...]).astype(BF16)

    half = sk_ref.shape[2]
    qt = _dot_nt(w_ref[...], xn_ref[...]).astype(BF16)
    v0, i0 = _top_rows(_dot(sk_ref[0], qt[:half]), PEER_TOPK)
    v1, i1 = _top_rows(_dot(sk_ref[1], qt[half:]), PEER_TOPK)
    widths = [PEER_TOPK // (a + 1) for a in range(PEER_TOPK)]
    pad = -sum(widths) % SUBLANES
    cand = jnp.concatenate([v0[a:a + 1] + v1[:n] for a, n in enumerate(widths)]
                           + [jnp.full((pad, v0.shape[1]), -jnp.inf, F32)], axis=0)
    eid = jnp.concatenate([i0[a:a + 1] * float(N_KEYS) + i1[:n] for a, n in enumerate(widths)]
                          + [jnp.zeros((pad, v0.shape[1]), F32)], axis=0)
    sc, e = _top_rows(cand, PEER_TOPK, payload=eid)
    ex = jnp.exp(sc - sc[0:1])
    gate_ref[...] = ex / jnp.sum(ex, axis=0, keepdims=True)
    ids_ref[...] = e.astype(I32)


def _peer_route(h, g_ffn, w_pq_t, sub_keys, *, tb):
    m, d = h.shape
    qd = w_pq_t.shape[0] // PEER_HEADS
    tb = min(tb, m)
    return pl.pallas_call(
        _peer_route_kernel,
        grid=(m // tb, PEER_HEADS),
        in_specs=[pl.BlockSpec((tb, d), lambda t, hh: (t, 0)),
                  pl.BlockSpec((1, d), lambda t, hh: (0, 0)),
                  pl.BlockSpec((qd, d), lambda t, hh: (hh, 0)),
                  pl.BlockSpec(sub_keys.shape, lambda t, hh: (0, 0, 0))],
        out_specs=[pl.BlockSpec((PEER_TOPK, tb), lambda t, hh: (hh, t)),
                   pl.BlockSpec((PEER_TOPK, tb), lambda t, hh: (hh, t))],
        out_shape=[jax.ShapeDtypeStruct((PEER_HEADS * PEER_TOPK, m), I32),
                   jax.ShapeDtypeStruct((PEER_HEADS * PEER_TOPK, m), F32)],
        scratch_shapes=[pltpu.VMEM((tb, d), BF16)],
        compiler_params=_params("arbitrary", "arbitrary", rows=m),
        name="peer_route",
    )(h, g_ffn.reshape(1, d), w_pq_t, sub_keys)


def _pack_tables_kernel(u_ref, v_ref, o_ref):
    nr = o_ref.shape[1] // 2

    def words(x):
        bits = pltpu.bitcast(x.astype(BF16).astype(F32), I32)
        half = bits.shape[1] // 2
        return (bits[:, half:] & jnp.int32(-65536)) | lax.shift_right_logical(bits[:, :half], 16)

    wu = words(u_ref[...])
    wv = words(v_ref[...])
    for r in range(nr):
        o_ref[:, r, :] = wu[:, r * LANES:(r + 1) * LANES]
        o_ref[:, nr + r, :] = wv[:, r * LANES:(r + 1) * LANES]


def _pack_tables(u_tab, v_tab, *, eb):
    n, d = u_tab.shape
    nr = d // (2 * LANES)
    return pl.pallas_call(
        _pack_tables_kernel,
        grid=(n // eb,),
        in_specs=[pl.BlockSpec((eb, d), lambda i: (i, 0)), pl.BlockSpec((eb, d), lambda i: (i, 0))],
        out_specs=pl.BlockSpec((eb, 2 * nr, LANES), lambda i: (i, 0, 0)),
        out_shape=jax.ShapeDtypeStruct((n, 2 * nr, LANES), I32),
        compiler_params=_params("arbitrary"),
        name="peer_pack_tables",
    )(u_tab, v_tab)


def _unpack_words(w):
    return pltpu.bitcast(jnp.left_shift(w, 16), F32), pltpu.bitcast(w & jnp.int32(-65536), F32)


def _peer_gather_kernel(ids_hbm, gate_ref, h_ref, gffn_ref, gfin_ref, tab_hbm, o_ref,
                        ids_s, buf, part_s, sem_ids, sem_rows, *, n_steps):
    s = pl.program_id(0)
    g_tok = h_ref.shape[0]
    nr = buf.shape[1] // 2
    n_e = buf.shape[2]
    n_buf = buf.shape[0]
    ahead = n_buf - 1
    assert g_tok % n_buf == 0 and ahead <= g_tok
    cur = lax.rem(s, 2)
    nxt = 1 - cur

    def ids_copy(step, slot):
        return pltpu.make_async_copy(ids_hbm.at[step], ids_s.at[slot], sem_ids.at[slot])

    def row_copy(e, k, slot):
        return pltpu.make_async_copy(tab_hbm.at[e], buf.at[slot, :, k, :], sem_rows.at[slot])

    def issue(ids_slot, t, slot):
        for k in range(n_e):
            row_copy(ids_s[ids_slot, t, k], k, slot).start(priority=k % 2)

    def wait(slot):
        for k in range(n_e):
            row_copy(0, k, slot).wait()

    @pl.when(s == 0)
    def _():
        ids_copy(0, 0).start()
        ids_copy(0, 0).wait()
        for t in range(ahead):
            issue(0, t, t % n_buf)

    @pl.when(s + 1 < n_steps)
    def _():
        ids_copy(s + 1, nxt).start()

    h = h_ref[...]
    xn = h * lax.rsqrt(jnp.mean(h * h, axis=-1, keepdims=True) + EPS) * gffn_ref[...]
    eye = lax.broadcasted_iota(I32, (n_e, n_e), 0) == lax.broadcasted_iota(I32, (n_e, n_e), 1)

    for t in range(g_tok):
        slot = t % n_buf
        ta = t + ahead
        if ta < g_tok:
            issue(cur, ta, ta % n_buf)
        else:
            @pl.when(s + 1 < n_steps)
            def _():
                if ta == g_tok:
                    ids_copy(s + 1, nxt).wait()
                issue(nxt, ta - g_tok, ta % n_buf)
        wait(slot)
        x = xn[t:t + 1, :]
        hacc = jnp.zeros((n_e, LANES), F32)
        for r in range(nr):
            lo, hi = _unpack_words(buf[slot, r])
            hacc = hacc + lo * x[:, r * LANES:(r + 1) * LANES] + hi * x[:, (nr + r) * LANES:(nr + r + 1) * LANES]
        act = jax.nn.gelu(jnp.sum(hacc, axis=-1, keepdims=True))
        gcol = jnp.sum(jnp.where(eye, jnp.broadcast_to(gate_ref[t:t + 1, :], (n_e, n_e)), 0.0),
                       axis=-1, keepdims=True)
        w = jnp.broadcast_to(gcol * act, (n_e, LANES))
        for r in range(nr):
            lo, hi = _unpack_words(buf[slot, nr + r])
            part_s[t, :, r * LANES:(r + 1) * LANES] = jnp.sum(
                (lo * w).reshape(n_e // SUBLANES, SUBLANES, LANES), axis=0)
            part_s[t, :, (nr + r) * LANES:(nr + r + 1) * LANES] = jnp.sum(
                (hi * w).reshape(n_e // SUBLANES, SUBLANES, LANES), axis=0)

    y = h + jnp.sum(part_s[...], axis=1)
    o_ref[...] = y * lax.rsqrt(jnp.mean(y * y, axis=-1, keepdims=True) + EPS) * gfin_ref[...]


PEER_GATHER_BUFFERS = 4


def _peer_gather(ids, gates, h, g_ffn, g_final, packed_tab, *, g_tok):
    m, d = h.shape
    n_e = ids.shape[1]
    n_steps = m // g_tok
    n_rows = packed_tab.shape[1]
    ids3 = ids.reshape(n_steps, g_tok, n_e)
    return pl.pallas_call(
        functools.partial(_peer_gather_kernel, n_steps=n_steps),
        grid=(n_steps,),
        in_specs=[pl.BlockSpec(memory_space=pl.ANY),
                  pl.BlockSpec((g_tok, n_e), lambda s: (s, 0)),
                  pl.BlockSpec((g_tok, d), lambda s: (s, 0)),
                  pl.BlockSpec((1, d), lambda s: (0, 0)),
                  pl.BlockSpec((1, d), lambda s: (0, 0)),
                  pl.BlockSpec(memory_space=pl.ANY)],
        out_specs=pl.BlockSpec((g_tok, d), lambda s: (s, 0)),
        out_shape=jax.ShapeDtypeStruct((m, d), F32),
        scratch_shapes=[pltpu.SMEM((2, g_tok, n_e), I32),
                        pltpu.VMEM((PEER_GATHER_BUFFERS, n_rows, n_e, LANES), I32),
                        pltpu.VMEM((g_tok, SUBLANES, d), F32),
                        pltpu.SemaphoreType.DMA((2,)), pltpu.SemaphoreType.DMA((PEER_GATHER_BUFFERS,))],
        compiler_params=_params("arbitrary", rows=m),
        name="peer_gather",
    )(ids3, gates, h, g_ffn.reshape(1, d), g_final.reshape(1, d), packed_tab)


def kernel(x_prompt, x_sample, cache_k, cache_v, cache_kidx, cache_mem_k, cache_mem_v, page_table,
           mem_prompt, rel_bias, g_in, w_in, g_sgu, w_s, b_s, w_out, g_x, w_xq, w_xk, w_xv, w_xo,
           g_ffn, w_pq, sub_keys, peer_u, peer_v, g_final):
    batch, seq, d = x_prompt.shape
    db, dseq, _ = x_sample.shape
    depth = w_in.shape[0]
    assert dseq == 1, "sample group is one new token per sequence"
    n_pages = page_table.shape[1]
    page = cache_k.shape[2]
    past = n_pages * page
    aw, kvw, qiw, gw = N_HEADS * HEAD_DIM, N_KV_HEADS * HEAD_DIM, IDX_HEADS * IDX_DIM, GMLP_GROUPS * LANES
    xw = w_xq.shape[2]
    mlen = mem_prompt.shape[1]
    small_lo = aw + 2 * kvw + qiw
    small_hi = small_lo + IDX_DIM + IDX_HEADS
    assert NUM_BUCKETS // 2 + int(math.log((LANES + 1) / (NUM_BUCKETS // 2)) / math.log(
        MAX_DISTANCE / (NUM_BUCKETS // 2)) * (NUM_BUCKETS // 2)) >= NUM_BUCKETS - 1

    hp = x_prompt.reshape(batch * seq, d)
    hs = x_sample.reshape(db, d)
    btiles = _bias_tiles(rel_bias)
    rb_t = rel_bias.T
    outs = {n: [] for n in ("kp", "vp", "ip", "mkp", "mvp", "ks", "vs", "is", "sv")}

    for l in range(depth):
        w_in_t = jnp.swapaxes(w_in[l], 0, 1).astype(BF16)
        w_small_t = jnp.pad(w_in_t[small_lo:small_hi], ((0, LANES - (small_hi - small_lo)), (0, 0)))
        w_out_b = w_out[l].astype(BF16)
        w_xq_b, w_xk_b, w_xv_b, w_xo_b = (w.astype(BF16) for w in (w_xq[l], w_xk[l], w_xv[l], w_xo[l]))
        w_pq_t = w_pq[l].T.astype(BF16)
        sk_b = sub_keys[l].astype(BF16)
        ws_first = jnp.repeat(w_s[l][:, 0, 0], LANES).reshape(1, gw)
        bs_first = jnp.repeat(b_s[l][:, 0], LANES).reshape(1, gw)
        ones_d = jnp.ones((d,), F32)

        z, zs = _proj_wt(hp, g_in[l], w_in_t, w_small_t, skip_lo=small_lo, skip_hi=small_hi, tm=512, tn=1024,
                         name="in_proj_prompt")
        att = _dsa_prompt(z.reshape(batch, seq, -1), zs.reshape(batch, seq, LANES), btiles,
                          batch=batch, seq=seq, topk=min(TOPK_MAX, seq // 4))
        m_p, _ = _gate(att.reshape(batch * seq, aw), z, g_sgu[l], w_s[l], b_s[l].T,
                       first_rows_only=False, tm=256, name="gate_prompt")
        hp1 = _matmul_residual(m_p, w_out_b, hp, tm=1024, tn=1024, name="out_proj_prompt")
        mem = mem_prompt.reshape(batch * mlen, d)
        mk = _proj(mem, ones_d, w_xk_b, norm=False, tm=256, tn=xw, name="mem_k_proj")
        mv = _proj(mem, ones_d, w_xv_b, norm=False, tm=256, tn=xw, name="mem_v_proj")
        qx = _proj(hp1, g_x[l], w_xq_b, norm=True, tm=512, tn=xw, name="xq_proj_prompt")
        hp2 = _cross(qx.reshape(batch, seq, xw), mk.reshape(batch, mlen, xw), mv.reshape(batch, mlen, xw),
                     hp1.reshape(batch, seq, d), w_xo_b, tq=256, name="cross_prompt").reshape(batch * seq, d)
        outs["kp"].append(z[:, aw:aw + kvw].reshape(batch, seq, N_KV_HEADS, HEAD_DIM))
        outs["vp"].append(z[:, aw + kvw:aw + 2 * kvw].reshape(batch, seq, N_KV_HEADS, HEAD_DIM))
        outs["ip"].append(zs[:, :IDX_DIM].reshape(batch, seq, IDX_DIM))
        outs["mkp"].append(mk.reshape(batch, mlen, X_HEADS, xw // X_HEADS))
        outs["mvp"].append(mv.reshape(batch, mlen, X_HEADS, xw // X_HEADS))

        z_s, zs_s = _proj_wt(hs, g_in[l], w_in_t, w_small_t, skip_lo=small_lo, skip_hi=small_hi, tm=db, tn=512,
                             name="in_proj_sample")
        qi3 = z_s[:, aw + 2 * kvw:aw + 2 * kvw + qiw].reshape(db, IDX_HEADS, IDX_DIM)
        wi3 = zs_s[:, IDX_DIM:IDX_DIM + IDX_HEADS].reshape(db, IDX_HEADS, 1)
        ki_new = zs_s[:, :IDX_DIM].reshape(db, 1, IDX_DIM)
        k_new = z_s[:, aw:aw + kvw].reshape(db, 1, kvw)
        v_new = z_s[:, aw + kvw:aw + 2 * kvw].reshape(db, 1, kvw)
        scores3, snew3 = _dsa_sample_scores(page_table, qi3, wi3, ki_new.reshape(db, IDX_DIM, 1),
                                            jnp.swapaxes(cache_kidx[l], 1, 2))
        pos3, new_sel = _dsa_sample_select(scores3.reshape(db, past), snew3.reshape(db, LANES),
                                           topk=min(TOPK_MAX, (past + dseq) // 4))
        att_s = _dsa_sample_attn(pos3[:, :, 0], page_table, z_s[:, :aw].reshape(db, N_HEADS, HEAD_DIM),
                                 k_new, v_new, new_sel, rb_t, cache_k, cache_v, layer=l)
        m_s, vn_s = _gate(att_s.reshape(db, aw), z_s, g_sgu[l], ws_first, bs_first,
                          first_rows_only=True, tm=db, name="gate_sample")
        hs1 = _matmul_residual(m_s, w_out_b, hs, tm=db, tn=1024, name="out_proj_sample")
        qx_s = _proj(hs1, g_x[l], w_xq_b, norm=True, tm=db, tn=xw, name="xq_proj_sample")
        hs2 = _cross(qx_s.reshape(db, 1, xw), cache_mem_k[l].reshape(db, mlen, xw),
                     cache_mem_v[l].reshape(db, mlen, xw), hs1.reshape(db, 1, d), w_xo_b,
                     tq=1, name="cross_sample").reshape(db, d)
        outs["ks"].append(k_new.reshape(db, dseq, N_KV_HEADS, HEAD_DIM))
        outs["vs"].append(v_new.reshape(db, dseq, N_KV_HEADS, HEAD_DIM))
        outs["is"].append(ki_new.reshape(db, dseq, IDX_DIM))
        outs["sv"].append(vn_s.reshape(db, dseq, gw))

        assert depth == 1
        packed = _pack_tables(peer_u[l], peer_v[l], eb=256)
        ids_p, gates_p = _peer_route(hp2, g_ffn[l], w_pq_t, sk_b, tb=512)
        y_p = _peer_gather(ids_p.T, gates_p.T, hp2, g_ffn[l], g_final, packed, g_tok=8)
        hs2_pad = jnp.pad(hs2, ((0, -db % LANES), (0, 0)))
        ids_s, gates_s = _peer_route(hs2_pad, g_ffn[l], w_pq_t, sk_b, tb=LANES)
        y_s = _peer_gather(ids_s.T[:db], gates_s.T[:db], hs2, g_ffn[l], g_final, packed, g_tok=8)

    y_prompt = y_p.reshape(batch, seq, d)
    y_sample = y_s.reshape(db, dseq, d)
    st = lambda n: jnp.stack(outs[n])
    return (y_prompt, y_sample, st("kp"), st("vp"), st("ip"), st("mkp"), st("mvp"),
            st("ks"), st("vs"), st("is"), st("sv"))
```
